```python
import jax
import jax.numpy as jnp
from jax import lax
import numpy as np


D_MODEL = 1024
BATCH = 8
SEQ = 2048
DEPTH = 2

RMS_EPS = 1e-6
SSD_WIDTH = D_MODEL
SSD_HEAD_DIM = 64
SSD_HEADS = SSD_WIDTH // SSD_HEAD_DIM
SSD_GROUPS = 2
SSD_HEADS_PER_GROUP = SSD_HEADS // SSD_GROUPS
SSD_STATE = 64
SSD_CONV = 4
SSD_CHUNK = 128
SSD_CONV_DIM = SSD_WIDTH + 2 * SSD_GROUPS * SSD_STATE
GLA_HEADS = 8
GLA_VALUE_WIDTH = D_MODEL
GLA_KEY_WIDTH = D_MODEL // 2
GLA_HEAD_K = GLA_KEY_WIDTH // GLA_HEADS
GLA_HEAD_V = GLA_VALUE_WIDTH // GLA_HEADS
GLA_GATE_RANK = 16
GLA_GATE_NORMALIZER = 16.0
GLA_CHUNK = 64
EVEN_SPLITS = (SSD_WIDTH, SSD_CONV_DIM, SSD_HEADS, GLA_KEY_WIDTH, GLA_KEY_WIDTH, GLA_VALUE_WIDTH, GLA_GATE_RANK, GLA_VALUE_WIDTH)
EVEN_IN_WIDTH = sum(EVEN_SPLITS)
EVEN_MIX_WIDTH = SSD_WIDTH + GLA_VALUE_WIDTH
MOBA_HEADS = 16
MOBA_HEAD_DIM = 64
MOBA_WIDTH = MOBA_HEADS * MOBA_HEAD_DIM
MOBA_BLOCK = 256
MOBA_TOPK = 3
MOBA_Q_CHUNK = 32
ODD_IN_WIDTH = 4 * MOBA_WIDTH
N_EVEN = (DEPTH + 1) // 2
N_ODD = DEPTH // 2

kernel_name = 'hybrid_ssd_gla_moba_trunk'


def rms_norm(x, w):
    x32 = x.astype(jnp.float32)
    y = x32 * lax.rsqrt(jnp.mean(x32 * x32, axis=-1, keepdims=True) + RMS_EPS)
    return y * w.astype(jnp.float32)


def split_cols(u, sizes):
    cuts = [int(c) for c in np.cumsum(sizes)[:-1]]
    return jnp.split(u, cuts, axis=-1)


def causal_depthwise_conv(u, w, b):
    k_width = w.shape[0]
    y = lax.conv_general_dilated(u, w.astype(u.dtype)[:, None, :], window_strides=(1,), padding=[(k_width - 1, 0)], dimension_numbers=('NWC', 'WIO', 'NWC'), feature_group_count=u.shape[-1])
    return y + b.astype(u.dtype)


def ssd_chunked_scan(xs, dt, a, bm, cm, d_skip):
    b, L, G, E, P = xs.shape
    N = bm.shape[-1]
    Q = SSD_CHUNK
    nc = L // Q
    x_c = xs.reshape(b, nc, Q, G, E, P)
    dt_c = dt.reshape(b, nc, Q, G, E)
    xdt = x_c * dt_c[..., None]
    b_c = bm.reshape(b, nc, Q, G, N)
    c_c = cm.reshape(b, nc, Q, G, N)
    a_cs = jnp.cumsum(dt_c * a, axis=2)
    a_t = jnp.moveaxis(a_cs, 2, -1)
    seg = a_t[..., :, None] - a_t[..., None, :]
    causal = jnp.tril(jnp.ones((Q, Q), dtype=bool))
    decay_ij = jnp.exp(jnp.where(causal, seg, -jnp.inf))
    cb = jnp.einsum('bcign,bcjgn->bcgij', c_c, b_c)
    y_diag = jnp.einsum('bcgeij,bcjgep->bcigep', cb[:, :, :, None] * decay_ij, xdt)
    a_last = a_cs[:, :, -1]
    decay_to_end = jnp.exp(a_last[:, :, None] - a_cs)
    chunk_states = jnp.einsum('bcjgn,bcjge,bcjgep->bcgepn', b_c, decay_to_end, xdt)

    def carry_state(h, inp):
        chunk_decay, s = inp
        return chunk_decay[..., None, None] * h + s, h

    _, prev_states = lax.scan(carry_state, jnp.zeros_like(chunk_states[:, 0]), (jnp.moveaxis(jnp.exp(a_last), 1, 0), jnp.moveaxis(chunk_states, 1, 0)))
    prev_states = jnp.moveaxis(prev_states, 0, 1)
    y_off = jnp.einsum('bcign,bcgepn->bcigep', c_c, prev_states) * jnp.exp(a_cs)[..., None]
    y = y_diag + y_off + d_skip[:, :, None] * x_c
    return y.reshape(b, L, G * E * P)


def gla_chunked(q, k, v, log_decay):
    b, L, H, dk = q.shape
    dv = v.shape[-1]
    C = GLA_CHUNK
    n = L // C
    q = q.reshape(b, n, C, H, dk)
    k = k.reshape(b, n, C, H, dk)
    v = v.reshape(b, n, C, H, dv)
    gcs = jnp.cumsum(log_decay.reshape(b, n, C, H, dk), axis=2)
    ref = gcs[:, :, C // 2:C // 2 + 1]
    scores = jnp.einsum('bnihd,bnjhd->bnhij', q * jnp.exp(gcs - ref), k * jnp.exp(ref - gcs))
    causal = jnp.tril(jnp.ones((C, C), dtype=bool))
    scores = jnp.where(causal, scores, 0.0)
    o_intra = jnp.einsum('bnhij,bnjhv->bnihv', scores, v)
    g_last = gcs[:, :, -1]
    chunk_kv = jnp.einsum('bnjhd,bnjhv->bnhdv', k * jnp.exp(g_last[:, :, None] - gcs), v)

    def carry_state(s, inp):
        decay, kv = inp
        return decay[..., None] * s + kv, s

    _, prev_states = lax.scan(carry_state, jnp.zeros_like(chunk_kv[:, 0]), (jnp.moveaxis(jnp.exp(g_last), 1, 0), jnp.moveaxis(chunk_kv, 1, 0)))
    prev_states = jnp.moveaxis(prev_states, 0, 1)
    o_inter = jnp.einsum('bnihd,bnhdv->bnihv', q * jnp.exp(gcs), prev_states)
    return (o_intra + o_inter).reshape(b, L, H, dv)


def ssd_gla_layer(x, norm_w, w_in, conv_w, conv_b, a_log, dt_bias, d_skip, ssd_norm_w, gate_w2, gate_b, gla_norm_w, w_out):
    bsz, L, _ = x.shape
    h = rms_norm(x, norm_w)
    u = h @ w_in.astype(jnp.float32)
    z_a, xbc, dt_raw, q_b, k_b, v_b, gate_lr, g_b = split_cols(u, EVEN_SPLITS)
    xbc = jax.nn.silu(causal_depthwise_conv(xbc, conv_w, conv_b))
    xs, bm, cm = split_cols(xbc, (SSD_WIDTH, SSD_GROUPS * SSD_STATE, SSD_GROUPS * SSD_STATE))
    xs = xs.reshape(bsz, L, SSD_GROUPS, SSD_HEADS_PER_GROUP, SSD_HEAD_DIM)
    bm = bm.reshape(bsz, L, SSD_GROUPS, SSD_STATE)
    cm = cm.reshape(bsz, L, SSD_GROUPS, SSD_STATE)
    dt = jax.nn.softplus(dt_raw + dt_bias.astype(jnp.float32)).reshape(bsz, L, SSD_GROUPS, SSD_HEADS_PER_GROUP)
    a = -jnp.exp(a_log.astype(jnp.float32)).reshape(SSD_GROUPS, SSD_HEADS_PER_GROUP)
    d = d_skip.astype(jnp.float32).reshape(SSD_GROUPS, SSD_HEADS_PER_GROUP)
    y_a = ssd_chunked_scan(xs, dt, a, bm, cm, d)
    y_a = rms_norm(y_a * jax.nn.silu(z_a), ssd_norm_w)
    q_b = q_b.reshape(bsz, L, GLA_HEADS, GLA_HEAD_K) * (GLA_HEAD_K ** -0.5)
    k_b = k_b.reshape(bsz, L, GLA_HEADS, GLA_HEAD_K)
    v_b = v_b.reshape(bsz, L, GLA_HEADS, GLA_HEAD_V)
    log_decay = jax.nn.log_sigmoid(gate_lr @ gate_w2.astype(jnp.float32) + gate_b.astype(jnp.float32)) / GLA_GATE_NORMALIZER
    log_decay = log_decay.reshape(bsz, L, GLA_HEADS, GLA_HEAD_K)
    o_b = gla_chunked(q_b, k_b, v_b, log_decay)
    o_b = rms_norm(o_b, gla_norm_w).reshape(bsz, L, GLA_VALUE_WIDTH) * jax.nn.silu(g_b)
    out = jnp.concatenate([y_a, o_b], axis=-1) @ w_out.astype(jnp.float32)
    return x + out.astype(x.dtype)


def moba_attention(q, k, v):
    b, H, L, d = q.shape
    nb = -(-L // MOBA_BLOCK)
    lp = nb * MOBA_BLOCK
    pad = ((0, 0), (0, 0), (0, lp - L), (0, 0))
    q = jnp.pad(q * (d ** -0.5), pad)
    k = jnp.pad(k, pad)
    v = jnp.pad(v, pad)
    kb = k.reshape(b, H, nb, MOBA_BLOCK, d)
    vb = v.reshape(b, H, nb, MOBA_BLOCK, d)
    k_mean = jnp.mean(kb, axis=3)
    gate = jnp.einsum('bhqd,bhnd->bhqn', q, k_mean).astype(jnp.float32)
    n_past_all = jnp.arange(lp) // MOBA_BLOCK
    fully_past = jnp.arange(nb)[None, :] < n_past_all[:, None]
    gate = jnp.where(fully_past, gate, -jnp.inf)
    k_sel = min(MOBA_TOPK, nb)
    _, top_idx = lax.top_k(gate, k_sel)
    bi = jnp.arange(b)[:, None, None]
    hi = jnp.arange(H)[None, :, None]

    def attend_chunk(start):
        qc = lax.dynamic_slice_in_dim(q, start, MOBA_Q_CHUNK, axis=2)
        idx_c = lax.dynamic_slice_in_dim(top_idx, start, MOBA_Q_CHUNK, axis=2)
        pos = start + jnp.arange(MOBA_Q_CHUNK)
        n_past = pos // MOBA_BLOCK
        logits = []
        for s in range(k_sel):
            ks = kb[bi, hi, idx_c[..., s]]
            ls = jnp.einsum('bhqd,bhqkd->bhqk', qc, ks).astype(jnp.float32)
            logits.append(jnp.where((s < n_past)[:, None], ls, -jnp.inf))
        own = start // MOBA_BLOCK
        ko = lax.dynamic_index_in_dim(kb, own, axis=2, keepdims=False)
        vo = lax.dynamic_index_in_dim(vb, own, axis=2, keepdims=False)
        lo = jnp.einsum('bhqd,bhkd->bhqk', qc, ko).astype(jnp.float32)
        key_pos = own * MOBA_BLOCK + jnp.arange(MOBA_BLOCK)
        logits.append(jnp.where(key_pos[None, :] <= pos[:, None], lo, -jnp.inf))
        p = jax.nn.softmax(jnp.concatenate(logits, axis=-1), axis=-1)
        p_parts = jnp.split(p, k_sel + 1, axis=-1)
        out = jnp.einsum('bhqk,bhkd->bhqd', p_parts[-1].astype(vo.dtype), vo)
        for s in range(k_sel):
            vs = vb[bi, hi, idx_c[..., s]]
            out = out + jnp.einsum('bhqk,bhqkd->bhqd', p_parts[s].astype(vs.dtype), vs)
        return out

    starts = jnp.arange(lp // MOBA_Q_CHUNK) * MOBA_Q_CHUNK
    outs = lax.map(attend_chunk, starts)
    out = jnp.moveaxis(outs, 0, 2).reshape(b, H, lp, d)
    return out[:, :, :L]


def moba_layer(x, norm_w, w_in, w_out):
    bsz, L, _ = x.shape
    h = rms_norm(x, norm_w)
    u = h @ w_in.astype(jnp.float32)
    q, k, v, z = split_cols(u, (MOBA_WIDTH, MOBA_WIDTH, MOBA_WIDTH, MOBA_WIDTH))
    to_heads = lambda t: t.reshape(bsz, L, MOBA_HEADS, MOBA_HEAD_DIM).transpose(0, 2, 1, 3)
    o = moba_attention(to_heads(q), to_heads(k), to_heads(v))
    o = o.transpose(0, 2, 1, 3).reshape(bsz, L, MOBA_WIDTH) * jax.nn.silu(z)
    return x + (o @ w_out.astype(jnp.float32)).astype(x.dtype)


def setup_inputs(seed: int = 0) -> dict:
    key = jax.random.key(seed)
    ks = jax.random.split(key, 20)
    nrm = jax.random.normal
    f32 = jnp.float32
    dt_init = jnp.exp(jax.random.uniform(ks[6], (N_EVEN, SSD_HEADS), f32, np.log(1e-3), np.log(1e-1)))
    return {
        'x': nrm(ks[0], (BATCH, SEQ, D_MODEL), f32),
        'even_norm': 1.0 + 0.02 * nrm(ks[1], (N_EVEN, D_MODEL), f32),
        'even_w_in': nrm(ks[2], (N_EVEN, D_MODEL, EVEN_IN_WIDTH), f32) * D_MODEL ** -0.5,
        'even_conv_w': nrm(ks[3], (N_EVEN, SSD_CONV, SSD_CONV_DIM), f32) * SSD_CONV ** -0.5,
        'even_conv_b': 0.02 * nrm(ks[4], (N_EVEN, SSD_CONV_DIM), f32),
        'even_a_log': jnp.log(jax.random.uniform(ks[5], (N_EVEN, SSD_HEADS), f32, 1.0, 16.0)),
        'even_dt_bias': dt_init + jnp.log(-jnp.expm1(-dt_init)),
        'even_d_skip': 1.0 + 0.1 * nrm(ks[7], (N_EVEN, SSD_HEADS), f32),
        'even_ssd_norm': 1.0 + 0.02 * nrm(ks[8], (N_EVEN, SSD_WIDTH), f32),
        'even_gate_w2': nrm(ks[9], (N_EVEN, GLA_GATE_RANK, GLA_KEY_WIDTH), f32) * GLA_GATE_RANK ** -0.5,
        'even_gate_b': 0.1 * nrm(ks[10], (N_EVEN, GLA_KEY_WIDTH), f32),
        'even_gla_norm': 1.0 + 0.02 * nrm(ks[11], (N_EVEN, GLA_HEAD_V), f32),
        'even_w_out': nrm(ks[12], (N_EVEN, EVEN_MIX_WIDTH, D_MODEL), f32) * EVEN_MIX_WIDTH ** -0.5,
        'odd_norm': 1.0 + 0.02 * nrm(ks[13], (N_ODD, D_MODEL), f32),
        'odd_w_in': nrm(ks[14], (N_ODD, D_MODEL, ODD_IN_WIDTH), f32) * D_MODEL ** -0.5,
        'odd_w_out': nrm(ks[15], (N_ODD, MOBA_WIDTH, D_MODEL), f32) * MOBA_WIDTH ** -0.5,
        'final_norm': 1.0 + 0.02 * nrm(ks[16], (D_MODEL,), f32),
    }


def reference(x, even_norm, even_w_in, even_conv_w, even_conv_b, even_a_log, even_dt_bias, even_d_skip, even_ssd_norm, even_gate_w2, even_gate_b, even_gla_norm, even_w_out, odd_norm, odd_w_in, odd_w_out, final_norm):
    for layer in range(DEPTH):
        i = layer // 2
        if layer % 2 == 0:
            x = ssd_gla_layer(x, even_norm[i], even_w_in[i], even_conv_w[i], even_conv_b[i], even_a_log[i], even_dt_bias[i], even_d_skip[i], even_ssd_norm[i], even_gate_w2[i], even_gate_b[i], even_gla_norm[i], even_w_out[i])
        else:
            x = moba_layer(x, odd_norm[i], odd_w_in[i], odd_w_out[i])
    return rms_norm(x, final_norm).astype(x.dtype)
```

```python
import functools

import jax
import jax.numpy as jnp
from jax import lax
from jax.experimental import pallas as pl
from jax.experimental.pallas import tpu as pltpu

F32 = jnp.float32
BF16 = jnp.bfloat16
HIGHEST = lax.Precision.HIGHEST

LANES = 128
SUBLANES = 8
VMEM_LIMIT_BYTES = 56 * 1024 * 1024

RMS_EPS = 1e-6
SSD_HEAD_DIM = 64
SSD_GROUPS = 2
SSD_STATE = 64
SSD_CONV = 4
SSD_CHUNK = 128
GLA_HEADS = 8
GLA_GATE_NORMALIZER = 16.0
GLA_CHUNK = 64
MOBA_HEAD_DIM = 64
MOBA_BLOCK = 256
MOBA_TOPK = 3

PROJ_ROWS = 256
GLA_ROWS = 128
HALF = LANES // 2


def _params(*sem):
    return pltpu.CompilerParams(dimension_semantics=sem, vmem_limit_bytes=VMEM_LIMIT_BYTES)


def _rms(x, w):
    return x * lax.rsqrt(jnp.mean(x * x, axis=-1, keepdims=True) + RMS_EPS) * w


def _silu(x):
    return x * jax.nn.sigmoid(x)


def _softplus(x):
    return jnp.maximum(x, 0.0) + jnp.log1p(jnp.exp(-jnp.abs(x)))


def _log_sigmoid(x):
    return jnp.minimum(x, 0.0) - jnp.log1p(jnp.exp(-jnp.abs(x)))


def _dot(a, b):
    return jnp.dot(a.astype(BF16), b.astype(BF16), preferred_element_type=F32)


def _dot_nt(a, b):
    return lax.dot_general(a.astype(BF16), b.astype(BF16), (((1,), (1,)), ((), ())), preferred_element_type=F32)


def _dot_tn(a, b):
    return lax.dot_general(a.astype(BF16), b.astype(BF16), (((0,), (0,)), ((), ())), preferred_element_type=F32)


def _dot_f32(a, b):
    return jnp.dot(a, b, preferred_element_type=F32, precision=HIGHEST)


def _resident(shape):
    return pl.BlockSpec(shape, lambda *_: (0,) * len(shape), pipeline_mode=pl.Buffered(1))


def _norm_proj_body(x_ref, nw_ref, *refs):
    n = len(refs) // 2
    h = _rms(x_ref[...], nw_ref[...]).astype(BF16)
    for w_ref, o_ref in zip(refs[:n], refs[n:]):
        o_ref[...] = jnp.dot(h, w_ref[...], preferred_element_type=F32)


def _norm_proj(x2d, norm_w, weights):
    m, d = x2d.shape
    tm = PROJ_ROWS
    return pl.pallas_call(
        _norm_proj_body,
        grid=(m // tm,),
        in_specs=[pl.BlockSpec((tm, d), lambda i: (i, 0)), _resident((1, d))]
        + [_resident(w.shape) for w in weights],
        out_specs=[pl.BlockSpec((tm, w.shape[1]), lambda i: (i, 0)) for w in weights],
        out_shape=[jax.ShapeDtypeStruct((m, w.shape[1]), F32) for w in weights],
        compiler_params=_params("parallel"),
    )(x2d, norm_w.reshape(1, d), *weights)


def _ssd_body(xbc_ref, sm_ref, z_ref, cw_ref, cb_ref, alog_ref, dtb_ref, dskip_ref, nw_ref,
              y_ref, tail_ref, h_ref, xpad_ref, xw_ref, ycat_ref):
    q = SSD_CHUNK
    width = y_ref.shape[-1]
    n_pairs = width // LANES
    gstate = SSD_GROUPS * SSD_STATE
    gwidth = width // SSD_GROUPS

    @pl.when(pl.program_id(1) == 0)
    def _():
        tail_ref[...] = jnp.zeros_like(tail_ref)
        h_ref[...] = jnp.zeros_like(h_ref)

    xbc = xbc_ref[0]
    xpad_ref[0:SUBLANES, :] = tail_ref[...]
    xpad_ref[SUBLANES:SUBLANES + q, :] = xbc
    tail_ref[...] = xbc[q - SUBLANES:, :]
    conv = cb_ref[...] + cw_ref[SSD_CONV - 1:SSD_CONV, :] * xbc
    for k in range(SSD_CONV - 1):
        off = SUBLANES - (SSD_CONV - 1) + k
        conv = conv + cw_ref[k:k + 1, :] * xpad_ref[off:off + q, :]
    act = _silu(conv)
    xs = act[:, :width]
    bm = act[:, width:width + gstate]
    cm = act[:, width + gstate:]

    dt = _softplus(sm_ref[0] + dtb_ref[...])
    dta = dt * (-jnp.exp(alog_ref[...]))
    ri = lax.broadcasted_iota(jnp.int32, (q, q), 0)
    ci = lax.broadcasted_iota(jnp.int32, (q, q), 1)
    causal = ri >= ci
    a_cs = _dot_f32(causal.astype(F32), dta)
    a_cs_t = a_cs.T
    a_last = a_cs[q - 1:q, :]
    bm_t = bm.T

    lane = lax.broadcasted_iota(jnp.int32, (q, LANES), 1)
    lo = lane < HALF
    lo_row = lo[0:1, :]
    cm_g = [jnp.where(lo, cm, 0.0), jnp.where(lo, 0.0, cm)]
    cb = [_dot_nt(c, bm) for c in cm_g]
    h_prev = h_ref[...]
    y_off = [_dot(c, h_prev) for c in cm_g]

    a_last_pairs = []
    for p in range(n_pairs):
        e0, e1 = 2 * p, 2 * p + 1
        g = (p * LANES) // gwidth
        col = slice(p * LANES, (p + 1) * LANES)
        gcol = slice(p * LANES - g * gwidth, (p + 1) * LANES - g * gwidth)
        acs_pair = jnp.where(lo, a_cs[:, e0:e0 + 1], a_cs[:, e1:e1 + 1])
        dt_pair = jnp.where(lo, dt[:, e0:e0 + 1], dt[:, e1:e1 + 1])
        al_pair = jnp.where(lo_row, a_last[:, e0:e0 + 1], a_last[:, e1:e1 + 1])
        a_last_pairs.append(al_pair)
        xs2 = xs[:, col]
        xdt = xs2 * dt_pair
        xdt_b = xdt.astype(BF16)
        yd = []
        for e in (e0, e1):
            seg = a_cs[:, e:e + 1] - a_cs_t[e:e + 1, :]
            decay = jnp.exp(jnp.where(causal, seg, -jnp.inf))
            yd.append(_dot(cb[g] * decay, xdt_b))
        y2 = jnp.where(lo, yd[0], yd[1])
        y2 = y2 + y_off[g][:, gcol] * jnp.exp(acs_pair) + dskip_ref[:, col] * xs2
        ycat_ref[:, col] = y2
        xw_ref[:, col] = (xdt * jnp.exp(al_pair - acs_pair)).astype(BF16)

    for g in range(SSD_GROUPS):
        rows = slice(g * SSD_STATE, (g + 1) * SSD_STATE)
        ppg = n_pairs // SSD_GROUPS
        dec = jnp.exp(jnp.concatenate(a_last_pairs[g * ppg:(g + 1) * ppg], axis=1))
        s_g = _dot(bm_t[rows, :], xw_ref[:, g * gwidth:(g + 1) * gwidth])
        h_ref[rows, :] = h_prev[rows, :] * dec + s_g

    y_ref[0] = _rms(ycat_ref[...] * _silu(z_ref[0]), nw_ref[...])


def _ssd(xbc, small, z, conv_w, conv_b, a_log, dt_bias, d_skip, norm_w):
    b, l, cdim = xbc.shape
    width = z.shape[-1]
    q = SSD_CHUNK
    n_heads = a_log.shape[0]
    pad = lambda v: jnp.pad(v, (0, LANES - n_heads)).reshape(1, LANES)
    gwidth = width // SSD_GROUPS
    blk = lambda w: pl.BlockSpec((1, q, w), lambda i, c: (i, c, 0))
    return pl.pallas_call(
        _ssd_body,
        grid=(b, l // q),
        in_specs=[blk(cdim), blk(LANES), blk(width),
                  _resident((SSD_CONV, cdim)), _resident((1, cdim)), _resident((1, LANES)),
                  _resident((1, LANES)), _resident((1, width)), _resident((1, width))],
        out_specs=blk(width),
        out_shape=jax.ShapeDtypeStruct((b, l, width), F32),
        scratch_shapes=[pltpu.VMEM((SUBLANES, cdim), F32),
                        pltpu.VMEM((SSD_GROUPS * SSD_STATE, gwidth), F32),
                        pltpu.VMEM((q + SUBLANES, cdim), F32),
                        pltpu.VMEM((q, width), BF16),
                        pltpu.VMEM((q, width), F32)],
        compiler_params=_params("parallel", "arbitrary"),
    )(xbc, small, z, conv_w, conv_b.reshape(1, cdim), pad(a_log), pad(dt_bias),
      jnp.repeat(d_skip, SSD_HEAD_DIM).reshape(1, width), norm_w.reshape(1, width))


def _gla_body(q_ref, k_ref, v_ref, sm_ref, g_ref, w2_ref, gb_ref, nw_ref, o_ref, st_ref):
    c = GLA_CHUNK
    kw = q_ref.shape[-1]
    dk = kw // GLA_HEADS
    dv = v_ref.shape[-1] // GLA_HEADS
    n_pairs = kw // LANES

    @pl.when(pl.program_id(1) == 0)
    def _():
        st_ref[...] = jnp.zeros_like(st_ref)

    ri = lax.broadcasted_iota(jnp.int32, (c, c), 0)
    ci = lax.broadcasted_iota(jnp.int32, (c, c), 1)
    causal = ri >= ci
    tri = causal.astype(F32)
    lo = lax.broadcasted_iota(jnp.int32, (c, LANES), 1) < HALF
    lo_sq = lax.broadcasted_iota(jnp.int32, (dv, LANES), 1) < HALF

    for ch in range(q_ref.shape[1] // c):
        rows = slice(ch * c, (ch + 1) * c)
        pre = _dot_f32(sm_ref[0, rows, :], w2_ref[...]) + gb_ref[...]
        gcs = _dot_f32(tri, _log_sigmoid(pre) * (1.0 / GLA_GATE_NORMALIZER))
        g_mid = gcs[c // 2:c // 2 + 1, :]
        g_last = gcs[c - 1:c, :]
        qs = q_ref[0, rows, :] * (dk ** -0.5)
        ks = k_ref[0, rows, :]
        q_in = qs * jnp.exp(gcs - g_mid)
        k_in = ks * jnp.exp(g_mid - gcs)
        q_st = qs * jnp.exp(gcs)
        k_st = ks * jnp.exp(g_last - gcs)
        for p in range(n_pairs):
            col = slice(p * LANES, (p + 1) * LANES)
            st = st_ref[:, col]
            kv_t = []
            for half in range(2):
                hd = 2 * p + half
                vcol = slice(hd * dv, (hd + 1) * dv)
                mask = lo if half == 0 else jnp.logical_not(lo)
                v_h = v_ref[0, rows, vcol]
                s = _dot_nt(jnp.where(mask, q_in[:, col], 0.0), k_in[:, col])
                s = jnp.where(causal, s, 0.0)
                o = _dot(s, v_h) + _dot_nt(jnp.where(mask, q_st[:, col], 0.0), st)
                kv_t.append(_dot_tn(v_h, k_st[:, col]))
                g_h = g_ref[0, rows, vcol]
                o_ref[0, rows, vcol] = _rms(o, nw_ref[...]) * _silu(g_h)
            st_ref[:, col] = st * jnp.exp(g_last[:, col]) + jnp.where(lo_sq, kv_t[0], kv_t[1])


def _gla(q, k, v, small, g, w2_pad, gate_b, norm_w):
    b, l, kw = q.shape
    vw = v.shape[-1]
    t = GLA_ROWS
    blk = lambda w: pl.BlockSpec((1, t, w), lambda i, c: (i, c, 0))
    return pl.pallas_call(
        _gla_body,
        grid=(b, l // t),
        in_specs=[blk(kw), blk(kw), blk(vw), blk(LANES), blk(vw),
                  _resident((LANES, kw)), _resident((1, kw)), _resident((1, vw // GLA_HEADS))],
        out_specs=blk(vw),
        out_shape=jax.ShapeDtypeStruct((b, l, vw), F32),
        scratch_shapes=[pltpu.VMEM((vw // GLA_HEADS, kw), F32)],
        compiler_params=_params("parallel", "arbitrary"),
    )(q, k, v, small, g, w2_pad, gate_b.reshape(1, kw), norm_w.reshape(1, -1))


def _out_proj_body(*refs, n_in, final_norm):
    a_refs, x_ref = refs[:n_in], refs[n_in]
    w_refs = refs[n_in + 1:2 * n_in + 1]
    rest = refs[2 * n_in + 1:]
    acc = x_ref[...]
    for a_ref, w_ref in zip(a_refs, w_refs):
        acc = acc + jnp.dot(a_ref[...].astype(BF16), w_ref[...], preferred_element_type=F32)
    if final_norm:
        fw_ref, o_ref = rest
        o_ref[...] = _rms(acc, fw_ref[...])
    else:
        (o_ref,) = rest
        o_ref[...] = acc


def _out_proj(acts, x2d, weights, final_w=None):
    m, d = x2d.shape
    tm = PROJ_ROWS
    n_in = len(acts)
    row = lambda w: pl.BlockSpec((tm, w), lambda i: (i, 0))
    in_specs = [row(a.shape[1]) for a in acts] + [row(d)] + [_resident(w.shape) for w in weights]
    args = list(acts) + [x2d] + list(weights)
    if final_w is not None:
        in_specs.append(_resident((1, d)))
        args.append(final_w.reshape(1, d))
    return pl.pallas_call(
        functools.partial(_out_proj_body, n_in=n_in, final_norm=final_w is not None),
        grid=(m // tm,),
        in_specs=in_specs,
        out_specs=row(d),
        out_shape=jax.ShapeDtypeStruct((m, d), F32),
        compiler_params=_params("parallel"),
    )(*args)


def _moba_body(q_ref, k_ref, v_ref, z_ref, o_ref, vt_ref, kmean_ref, sel_ref):
    blk = MOBA_BLOCK
    nb = k_ref.shape[1] // blk
    i = pl.program_id(2)

    @pl.when(i == 0)
    def _():
        for n in range(nb):
            rows = slice(n * blk, (n + 1) * blk)
            vt_ref[n] = v_ref[0, rows, :].T.astype(BF16)
            kmean_ref[n:n + 1, :] = jnp.mean(k_ref[0, rows, :], axis=0, keepdims=True)

    q2 = q_ref[0] * (MOBA_HEAD_DIM ** -0.5)
    lo = lax.broadcasted_iota(jnp.int32, (blk, LANES), 1) < HALF
    q_heads = [jnp.where(lo, q2, 0.0), jnp.where(lo, 0.0, q2)]
    key_i = lax.broadcasted_iota(jnp.int32, (blk, blk), 0)
    qry_i = lax.broadcasted_iota(jnp.int32, (blk, blk), 1)
    causal_t = key_i <= qry_i

    outs = []
    for half in range(2):
        qh = q_heads[half]
        gate = lax.dot_general(kmean_ref[...], qh, (((1,), (1,)), ((), ())),
                               preferred_element_type=F32, precision=HIGHEST)
        g_rows = [gate[n:n + 1, :] for n in range(nb)]
        for n in range(nb):
            rank = jnp.zeros((1, blk), F32)
            for m in range(nb):
                if m == n:
                    continue
                ahead = (g_rows[m] >= g_rows[n]) if m < n else (g_rows[m] > g_rows[n])
                rank = rank + jnp.where(ahead, jnp.where(m < i, 1.0, 0.0), 0.0)
            sel_ref[half, n:n + 1, :] = jnp.where(rank < MOBA_TOPK, 1.0, 0.0)

        qb = qh.astype(BF16)
        own = pl.multiple_of(i * blk, blk)
        s = _dot_nt(k_ref[0, pl.ds(own, blk), :], qb)
        s = jnp.where(causal_t, s, -jnp.inf)
        m0 = jnp.max(s, axis=0, keepdims=True)
        p0 = jnp.exp(s - m0)
        l0 = jnp.sum(p0, axis=0, keepdims=True)
        acc0 = jnp.dot(vt_ref[i], p0.astype(BF16), preferred_element_type=F32)

        def past_block(j, carry, qb=qb, half=half):
            m_run, l_run, acc = carry
            start = pl.multiple_of(j * blk, blk)
            sj = _dot_nt(k_ref[0, pl.ds(start, blk), :], qb)
            sj = jnp.where(sel_ref[half, pl.ds(j, 1), :] > 0.0, sj, -jnp.inf)
            m_new = jnp.maximum(m_run, jnp.max(sj, axis=0, keepdims=True))
            pj = jnp.exp(sj - m_new)
            alpha = jnp.exp(m_run - m_new)
            l_new = alpha * l_run + jnp.sum(pj, axis=0, keepdims=True)
            acc = alpha * acc + jnp.dot(vt_ref[j], pj.astype(BF16), preferred_element_type=F32)
            return m_new, l_new, acc

        _, l_fin, acc = lax.fori_loop(0, i, past_block, (m0, l0, acc0))
        rows = slice(half * HALF, (half + 1) * HALF)
        outs.append(acc[rows, :] / l_fin)
    o = jnp.concatenate(outs, axis=0).T
    o_ref[0] = o * _silu(z_ref[0])


def _moba(q, k, v, z):
    b, l, w = q.shape
    blk = MOBA_BLOCK
    nb = l // blk
    qspec = pl.BlockSpec((1, blk, LANES), lambda bi, hp, i: (bi, i, hp))
    kvspec = pl.BlockSpec((1, l, LANES), lambda bi, hp, i: (bi, 0, hp))
    return pl.pallas_call(
        _moba_body,
        grid=(b, w // LANES, nb),
        in_specs=[qspec, kvspec, kvspec, qspec],
        out_specs=qspec,
        out_shape=jax.ShapeDtypeStruct((b, l, w), F32),
        scratch_shapes=[pltpu.VMEM((nb, LANES, blk), BF16),
                        pltpu.VMEM((nb, LANES), F32),
                        pltpu.VMEM((2, nb, blk), F32)],
        compiler_params=_params("parallel", "parallel", "arbitrary"),
    )(q, k, v, z)


def _even_layer(x, norm_w, w_in, conv_w, conv_b, a_log, dt_bias, d_skip, ssd_norm_w, gate_w2, gate_b,
                gla_norm_w, w_out):
    b, l, d = x.shape
    width = ssd_norm_w.shape[0]
    cdim = conv_w.shape[1]
    n_heads = a_log.shape[0]
    rank, kw = gate_w2.shape
    vw = gla_norm_w.shape[0] * GLA_HEADS
    cuts = [0]
    for s in (width, cdim, n_heads, kw, kw, vw, rank, vw):
        cuts.append(cuts[-1] + s)
    seg = lambda j: w_in[:, cuts[j]:cuts[j + 1]]
    w_small = jnp.concatenate([seg(2), seg(6), jnp.zeros((d, LANES - n_heads - rank), w_in.dtype)], axis=1)
    weights = [seg(0), seg(1), w_small, seg(3), seg(4), seg(5), seg(7)]
    x2d = x.reshape(b * l, d)
    z_a, xbc, small, q_b, k_b, v_b, g_b = [
        u.reshape(b, l, -1) for u in _norm_proj(x2d, norm_w, [w.astype(BF16) for w in weights])]
    y_a = _ssd(xbc, small, z_a, conv_w, conv_b, a_log, dt_bias, d_skip, ssd_norm_w)
    w2_pad = jnp.zeros((LANES, kw), F32).at[n_heads:n_heads + rank, :].set(gate_w2)
    o_b = _gla(q_b, k_b, v_b, small, g_b, w2_pad, gate_b, gla_norm_w)
    return [y_a.reshape(b * l, width), o_b.reshape(b * l, vw)], [w_out[:width].astype(BF16), w_out[width:].astype(BF16)]


def _odd_layer(x, norm_w, w_in, w_out):
    b, l, d = x.shape
    w = w_out.shape[0]
    weights = [w_in[:, j * w:(j + 1) * w].astype(BF16) for j in range(4)]
    q, k, v, z = [u.reshape(b, l, w) for u in _norm_proj(x.reshape(b * l, d), norm_w, weights)]
    o = _moba(q, k, v, z)
    return [o.reshape(b * l, w)], [w_out.astype(BF16)]


def kernel(x, even_norm, even_w_in, even_conv_w, even_conv_b, even_a_log, even_dt_bias, even_d_skip, even_ssd_norm,
           even_gate_w2, even_gate_b, even_gla_norm, even_w_out, odd_norm, odd_w_in, odd_w_out, final_norm):
    b, l, d = x.shape
    depth = even_norm.shape[0] + odd_norm.shape[0]
    for layer in range(depth):
        i = layer // 2
        if layer % 2 == 0:
            acts, weights = _even_layer(x, even_norm[i], even_w_in[i], even_conv_w[i], even_conv_b[i], even_a_log[i],
                                        even_dt_bias[i], even_d_skip[i], even_ssd_norm[i], even_gate_w2[i],
                                        even_gate_b[i], even_gla_norm[i], even_w_out[i])
        else:
            acts, weights = _odd_layer(x, odd_norm[i], odd_w_in[i], odd_w_out[i])
        last = layer == depth - 1
        x = _out_proj(acts, x.reshape(b * l, d), weights, final_norm if last else None).reshape(b, l, d)
    return x
```

```python
import functools

import jax
import jax.numpy as jnp
from jax import lax
from jax.experimental import pallas as pl
from jax.experimental.pallas import tpu as pltpu

F32 = jnp.float32
BF16 = jnp.bfloat16
HIGHEST = lax.Precision.HIGHEST

LANES = 128
SUBLANES = 8
VMEM_LIMIT_BYTES = 56 * 1024 * 1024

RMS_EPS = 1e-6
SSD_HEAD_DIM = 64
SSD_GROUPS = 2
SSD_STATE = 64
SSD_CONV = 4
SSD_CHUNK = 128
GLA_HEADS = 8
GLA_GATE_NORMALIZER = 16.0
GLA_CHUNK = 64
MOBA_HEAD_DIM = 64
MOBA_BLOCK = 256
MOBA_TOPK = 3

PROJ_ROWS = 256
GLA_ROWS = 128
HALF = LANES // 2


def _params(*sem):
    return pltpu.CompilerParams(dimension_semantics=sem, vmem_limit_bytes=VMEM_LIMIT_BYTES)


def _rms(x, w):
    return x * lax.rsqrt(jnp.mean(x * x, axis=-1, keepdims=True) + RMS_EPS) * w


def _silu(x):
    return x * jax.nn.sigmoid(x)


def _softplus(x):
    return jnp.maximum(x, 0.0) + jnp.log1p(jnp.exp(-jnp.abs(x)))


def _log_sigmoid(x):
    return jnp.minimum(x, 0.0) - jnp.log1p(jnp.exp(-jnp.abs(x)))


def _dot(a, b):
    return jnp.dot(a.astype(BF16), b.astype(BF16), preferred_element_type=F32)


def _dot_nt(a, b):
    return lax.dot_general(a.astype(BF16), b.astype(BF16), (((1,), (1,)), ((), ())), preferred_element_type=F32)


def _dot_tn(a, b):
    return lax.dot_general(a.astype(BF16), b.astype(BF16), (((0,), (0,)), ((), ())), preferred_element_type=F32)


def _dot_f32(a, b):
    return jnp.dot(a, b, preferred_element_type=F32, precision=HIGHEST)


def _resident(shape):
    return pl.BlockSpec(shape, lambda *_: (0,) * len(shape), pipeline_mode=pl.Buffered(1))


def _norm_proj_body(x_ref, nw_ref, *refs):
    n = len(refs) // 2
    h = _rms(x_ref[...], nw_ref[...]).astype(BF16)
    for w_ref, o_ref in zip(refs[:n], refs[n:]):
        o_ref[...] = jnp.dot(h, w_ref[...], preferred_element_type=F32)


def _norm_proj(x2d, norm_w, weights):
    m, d = x2d.shape
    tm = PROJ_ROWS
    return pl.pallas_call(
        _norm_proj_body,
        name="norm_proj",
        grid=(m // tm,),
        in_specs=[pl.BlockSpec((tm, d), lambda i: (i, 0)), _resident((1, d))]
        + [_resident(w.shape) for w in weights],
        out_specs=[pl.BlockSpec((tm, w.shape[1]), lambda i: (i, 0)) for w in weights],
        out_shape=[jax.ShapeDtypeStruct((m, w.shape[1]), F32) for w in weights],
        compiler_params=_params("parallel"),
    )(x2d, norm_w.reshape(1, d), *weights)


def _ssd_body(xbc_ref, sm_ref, z_ref, cw_ref, cb_ref, alog_ref, dtb_ref, dskip_ref, nw_ref,
              y_ref, tail_ref, h_ref, xpad_ref, xw_ref, ycat_ref):
    q = SSD_CHUNK
    width = y_ref.shape[-1]
    n_pairs = width // LANES
    gstate = SSD_GROUPS * SSD_STATE
    gwidth = width // SSD_GROUPS

    @pl.when(pl.program_id(1) == 0)
    def _():
        tail_ref[...] = jnp.zeros_like(tail_ref)
        h_ref[...] = jnp.zeros_like(h_ref)

    xbc = xbc_ref[0]
    xpad_ref[0:SUBLANES, :] = tail_ref[...]
    xpad_ref[SUBLANES:SUBLANES + q, :] = xbc
    tail_ref[...] = xbc[q - SUBLANES:, :]
    conv = cb_ref[...] + cw_ref[SSD_CONV - 1:SSD_CONV, :] * xbc
    for k in range(SSD_CONV - 1):
        off = SUBLANES - (SSD_CONV - 1) + k
        conv = conv + cw_ref[k:k + 1, :] * xpad_ref[off:off + q, :]
    act = _silu(conv)
    xs = act[:, :width]
    bm = act[:, width:width + gstate]
    cm = act[:, width + gstate:]

    dt = _softplus(sm_ref[0] + dtb_ref[...])
    dta = dt * (-jnp.exp(alog_ref[...]))
    ri = lax.broadcasted_iota(jnp.int32, (q, q), 0)
    ci = lax.broadcasted_iota(jnp.int32, (q, q), 1)
    causal = ri >= ci
    a_cs = _dot_f32(causal.astype(F32), dta)
    a_cs_t = a_cs.T
    a_last = a_cs[q - 1:q, :]
    bm_t = bm.T

    lane = lax.broadcasted_iota(jnp.int32, (q, LANES), 1)
    lo = lane < HALF
    lo_row = lo[0:1, :]
    cm_g = [jnp.where(lo, cm, 0.0), jnp.where(lo, 0.0, cm)]
    cb = [_dot_nt(c, bm) for c in cm_g]
    h_prev = h_ref[...]
    y_off = [_dot(c, h_prev) for c in cm_g]

    a_last_pairs = []
    for p in range(n_pairs):
        e0, e1 = 2 * p, 2 * p + 1
        g = (p * LANES) // gwidth
        col = slice(p * LANES, (p + 1) * LANES)
        gcol = slice(p * LANES - g * gwidth, (p + 1) * LANES - g * gwidth)
        acs_pair = jnp.where(lo, a_cs[:, e0:e0 + 1], a_cs[:, e1:e1 + 1])
        dt_pair = jnp.where(lo, dt[:, e0:e0 + 1], dt[:, e1:e1 + 1])
        al_pair = jnp.where(lo_row, a_last[:, e0:e0 + 1], a_last[:, e1:e1 + 1])
        a_last_pairs.append(al_pair)
        xs2 = xs[:, col]
        xdt = xs2 * dt_pair
        xdt_b = xdt.astype(BF16)
        yd = []
        for e in (e0, e1):
            seg = a_cs[:, e:e + 1] - a_cs_t[e:e + 1, :]
            decay = jnp.exp(jnp.where(causal, seg, -jnp.inf))
            yd.append(_dot(cb[g] * decay, xdt_b))
        y2 = jnp.where(lo, yd[0], yd[1])
        y2 = y2 + y_off[g][:, gcol] * jnp.exp(acs_pair) + dskip_ref[:, col] * xs2
        ycat_ref[:, col] = y2
        xw_ref[:, col] = (xdt * jnp.exp(al_pair - acs_pair)).astype(BF16)

    for g in range(SSD_GROUPS):
        rows = slice(g * SSD_STATE, (g + 1) * SSD_STATE)
        ppg = n_pairs // SSD_GROUPS
        dec = jnp.exp(jnp.concatenate(a_last_pairs[g * ppg:(g + 1) * ppg], axis=1))
        s_g = _dot(bm_t[rows, :], xw_ref[:, g * gwidth:(g + 1) * gwidth])
        h_ref[rows, :] = h_prev[rows, :] * dec + s_g

    y_ref[0] = _rms(ycat_ref[...] * _silu(z_ref[0]), nw_ref[...])


def _ssd(xbc, small, z, conv_w, conv_b, a_log, dt_bias, d_skip, norm_w):
    b, l, cdim = xbc.shape
    width = z.shape[-1]
    q = SSD_CHUNK
    n_heads = a_log.shape[0]
    pad = lambda v: jnp.pad(v, (0, LANES - n_heads)).reshape(1, LANES)
    gwidth = width // SSD_GROUPS
    blk = lambda w: pl.BlockSpec((1, q, w), lambda i, c: (i, c, 0))
    return pl.pallas_call(
        _ssd_body,
        name="ssd",
        grid=(b, l // q),
        in_specs=[blk(cdim), blk(LANES), blk(width),
                  _resident((SSD_CONV, cdim)), _resident((1, cdim)), _resident((1, LANES)),
                  _resident((1, LANES)), _resident((1, width)), _resident((1, width))],
        out_specs=blk(width),
        out_shape=jax.ShapeDtypeStruct((b, l, width), F32),
        scratch_shapes=[pltpu.VMEM((SUBLANES, cdim), F32),
                        pltpu.VMEM((SSD_GROUPS * SSD_STATE, gwidth), F32),
                        pltpu.VMEM((q + SUBLANES, cdim), F32),
                        pltpu.VMEM((q, width), BF16),
                        pltpu.VMEM((q, width), F32)],
        compiler_params=_params("parallel", "arbitrary"),
    )(xbc, small, z, conv_w, conv_b.reshape(1, cdim), pad(a_log), pad(dt_bias),
      jnp.repeat(d_skip, SSD_HEAD_DIM).reshape(1, width), norm_w.reshape(1, width))


def _gla_body(q_ref, k_ref, v_ref, sm_ref, g_ref, w2_ref, gb_ref, nw_ref, o_ref, st_ref):
    c = GLA_CHUNK
    kw = q_ref.shape[-1]
    dk = kw // GLA_HEADS
    dv = v_ref.shape[-1] // GLA_HEADS
    n_pairs = kw // LANES

    @pl.when(pl.program_id(1) == 0)
    def _():
        st_ref[...] = jnp.zeros_like(st_ref)

    ri = lax.broadcasted_iota(jnp.int32, (c, c), 0)
    ci = lax.broadcasted_iota(jnp.int32, (c, c), 1)
    causal = ri >= ci
    tri = causal.astype(F32)
    lo = lax.broadcasted_iota(jnp.int32, (c, LANES), 1) < HALF
    lo_sq = lax.broadcasted_iota(jnp.int32, (dv, LANES), 1) < HALF

    for ch in range(q_ref.shape[1] // c):
        rows = slice(ch * c, (ch + 1) * c)
        pre = _dot_f32(sm_ref[0, rows, :], w2_ref[...]) + gb_ref[...]
        gcs = _dot_f32(tri, _log_sigmoid(pre) * (1.0 / GLA_GATE_NORMALIZER))
        g_mid = gcs[c // 2:c // 2 + 1, :]
        g_last = gcs[c - 1:c, :]
        qs = q_ref[0, rows, :] * (dk ** -0.5)
        ks = k_ref[0, rows, :]
        q_in = qs * jnp.exp(gcs - g_mid)
        k_in = ks * jnp.exp(g_mid - gcs)
        q_st = qs * jnp.exp(gcs)
        k_st = ks * jnp.exp(g_last - gcs)
        for p in range(n_pairs):
            col = slice(p * LANES, (p + 1) * LANES)
            st = st_ref[:, col]
            kv_t = []
            for half in range(2):
                hd = 2 * p + half
                vcol = slice(hd * dv, (hd + 1) * dv)
                mask = lo if half == 0 else jnp.logical_not(lo)
                v_h = v_ref[0, rows, vcol]
                s = _dot_nt(jnp.where(mask, q_in[:, col], 0.0), k_in[:, col])
                s = jnp.where(causal, s, 0.0)
                o = _dot(s, v_h) + _dot_nt(jnp.where(mask, q_st[:, col], 0.0), st)
                kv_t.append(_dot_tn(v_h, k_st[:, col]))
                g_h = g_ref[0, rows, vcol]
                o_ref[0, rows, vcol] = _rms(o, nw_ref[...]) * _silu(g_h)
            st_ref[:, col] = st * jnp.exp(g_last[:, col]) + jnp.where(lo_sq, kv_t[0], kv_t[1])


def _gla(q, k, v, small, g, w2_pad, gate_b, norm_w):
    b, l, kw = q.shape
    vw = v.shape[-1]
    t = GLA_ROWS
    blk = lambda w: pl.BlockSpec((1, t, w), lambda i, c: (i, c, 0))
    return pl.pallas_call(
        _gla_body,
        name="gla",
        grid=(b, l // t),
        in_specs=[blk(kw), blk(kw), blk(vw), blk(LANES), blk(vw),
                  _resident((LANES, kw)), _resident((1, kw)), _resident((1, vw // GLA_HEADS))],
        out_specs=blk(vw),
        out_shape=jax.ShapeDtypeStruct((b, l, vw), F32),
        scratch_shapes=[pltpu.VMEM((vw // GLA_HEADS, kw), F32)],
        compiler_params=_params("parallel", "arbitrary"),
    )(q, k, v, small, g, w2_pad, gate_b.reshape(1, kw), norm_w.reshape(1, -1))


def _out_proj_body(*refs, n_in, final_norm):
    a_refs, x_ref = refs[:n_in], refs[n_in]
    w_refs = refs[n_in + 1:2 * n_in + 1]
    rest = refs[2 * n_in + 1:]
    acc = x_ref[...]
    for a_ref, w_ref in zip(a_refs, w_refs):
        acc = acc + jnp.dot(a_ref[...].astype(BF16), w_ref[...], preferred_element_type=F32)
    if final_norm:
        fw_ref, o_ref = rest
        o_ref[...] = _rms(acc, fw_ref[...])
    else:
        (o_ref,) = rest
        o_ref[...] = acc


def _out_proj(acts, x2d, weights, final_w=None):
    m, d = x2d.shape
    tm = PROJ_ROWS
    n_in = len(acts)
    row = lambda w: pl.BlockSpec((tm, w), lambda i: (i, 0))
    in_specs = [row(a.shape[1]) for a in acts] + [row(d)] + [_resident(w.shape) for w in weights]
    args = list(acts) + [x2d] + list(weights)
    if final_w is not None:
        in_specs.append(_resident((1, d)))
        args.append(final_w.reshape(1, d))
    return pl.pallas_call(
        functools.partial(_out_proj_body, n_in=n_in, final_norm=final_w is not None),
        name="out_proj",
        grid=(m // tm,),
        in_specs=in_specs,
        out_specs=row(d),
        out_shape=jax.ShapeDtypeStruct((m, d), F32),
        compiler_params=_params("parallel"),
    )(*args)


def _moba_query_block(n_past, q_ref, k_ref, z_ref, o_ref, vt_ref, kmean_ref, s_ref, p_ref):
    blk = MOBA_BLOCK
    nk = (n_past + 1) * blk
    q2 = q_ref[0] * (MOBA_HEAD_DIM ** -0.5)
    lo = lax.broadcasted_iota(jnp.int32, (blk, LANES), 1) < HALF
    key_i = lax.broadcasted_iota(jnp.int32, (blk, blk), 0)
    qry_i = lax.broadcasted_iota(jnp.int32, (blk, blk), 1)
    causal_t = key_i <= qry_i

    outs = []
    for half in range(2):
        qh = jnp.where(lo, q2, 0.0) if half == 0 else jnp.where(lo, 0.0, q2)
        masks = []
        if n_past:
            gate = lax.dot_general(kmean_ref[...], qh, (((1,), (1,)), ((), ())),
                                   preferred_element_type=F32, precision=HIGHEST)
            g_rows = [gate[n:n + 1, :] for n in range(n_past)]
            for n in range(n_past):
                rank = jnp.zeros((1, blk), F32)
                for m in range(n_past):
                    if m != n:
                        ahead = (g_rows[m] >= g_rows[n]) if m < n else (g_rows[m] > g_rows[n])
                        rank = rank + jnp.where(ahead, 1.0, 0.0)
                masks.append(rank < MOBA_TOPK)
        masks.append(causal_t)

        s_ref[0:nk, :] = _dot_nt(k_ref[0, 0:nk, :], qh)
        m_run = None
        for j in range(n_past + 1):
            rows = slice(j * blk, (j + 1) * blk)
            sj = jnp.where(masks[j], s_ref[rows, :], -jnp.inf)
            s_ref[rows, :] = sj
            mj = jnp.max(sj, axis=0, keepdims=True)
            m_run = mj if m_run is None else jnp.maximum(m_run, mj)
        l_run = jnp.zeros((1, blk), F32)
        for j in range(n_past + 1):
            rows = slice(j * blk, (j + 1) * blk)
            pj = jnp.exp(s_ref[rows, :] - m_run)
            l_run = l_run + jnp.sum(pj, axis=0, keepdims=True)
            p_ref[rows, :] = pj.astype(BF16)
        acc = jnp.dot(vt_ref[:, 0:nk], p_ref[0:nk, :], preferred_element_type=F32)
        outs.append(acc[half * HALF:(half + 1) * HALF, :] / l_run)
    o = jnp.concatenate(outs, axis=0).T
    o_ref[0] = o * _silu(z_ref[0])


def _moba_body(q_ref, k_ref, v_ref, z_ref, o_ref, vt_ref, kmean_ref, s_ref, p_ref):
    blk = MOBA_BLOCK
    nb = k_ref.shape[1] // blk
    i = pl.program_id(2)

    @pl.when(i == 0)
    def _():
        for n in range(nb):
            rows = slice(n * blk, (n + 1) * blk)
            vt_ref[:, rows] = v_ref[0, rows, :].T.astype(BF16)
            kmean_ref[n:n + 1, :] = jnp.mean(k_ref[0, rows, :], axis=0, keepdims=True)

    for n_past in range(nb):
        pl.when(i == n_past)(functools.partial(
            _moba_query_block, n_past, q_ref, k_ref, z_ref, o_ref, vt_ref, kmean_ref, s_ref, p_ref))


def _moba(q, k, v, z):
    b, l, w = q.shape
    blk = MOBA_BLOCK
    nb = l // blk
    qspec = pl.BlockSpec((1, blk, LANES), lambda bi, hp, i: (bi, i, hp))
    kvspec = pl.BlockSpec((1, l, LANES), lambda bi, hp, i: (bi, 0, hp))
    return pl.pallas_call(
        _moba_body,
        name="moba",
        grid=(b, w // LANES, nb),
        in_specs=[qspec, kvspec, kvspec, qspec],
        out_specs=qspec,
        out_shape=jax.ShapeDtypeStruct((b, l, w), F32),
        scratch_shapes=[pltpu.VMEM((LANES, l), BF16),
                        pltpu.VMEM((nb, LANES), F32),
                        pltpu.VMEM((l, blk), F32),
                        pltpu.VMEM((l, blk), BF16)],
        compiler_params=_params("parallel", "parallel", "arbitrary"),
    )(q, k, v, z)


def _even_layer(x, norm_w, w_in, conv_w, conv_b, a_log, dt_bias, d_skip, ssd_norm_w, gate_w2, gate_b,
                gla_norm_w, w_out):
    b, l, d = x.shape
    width = ssd_norm_w.shape[0]
    cdim = conv_w.shape[1]
    n_heads = a_log.shape[0]
    rank, kw = gate_w2.shape
    vw = gla_norm_w.shape[0] * GLA_HEADS
    cuts = [0]
    for s in (width, cdim, n_heads, kw, kw, vw, rank, vw):
        cuts.append(cuts[-1] + s)
    seg = lambda j: w_in[:, cuts[j]:cuts[j + 1]]
    w_small = jnp.concatenate([seg(2), seg(6), jnp.zeros((d, LANES - n_heads - rank), w_in.dtype)], axis=1)
    weights = [seg(0), seg(1), w_small, seg(3), seg(4), seg(5), seg(7)]
    x2d = x.reshape(b * l, d)
    z_a, xbc, small, q_b, k_b, v_b, g_b = [
        u.reshape(b, l, -1) for u in _norm_proj(x2d, norm_w, [w.astype(BF16) for w in weights])]
    y_a = _ssd(xbc, small, z_a, conv_w, conv_b, a_log, dt_bias, d_skip, ssd_norm_w)
    w2_pad = jnp.zeros((LANES, kw), F32).at[n_heads:n_heads + rank, :].set(gate_w2)
    o_b = _gla(q_b, k_b, v_b, small, g_b, w2_pad, gate_b, gla_norm_w)
    return [y_a.reshape(b * l, width), o_b.reshape(b * l, vw)], [w_out[:width].astype(BF16), w_out[width:].astype(BF16)]


def _odd_layer(x, norm_w, w_in, w_out):
    b, l, d = x.shape
    w = w_out.shape[0]
    weights = [w_in[:, j * w:(j + 1) * w].astype(BF16) for j in range(4)]
    q, k, v, z = [u.reshape(b, l, w) for u in _norm_proj(x.reshape(b * l, d), norm_w, weights)]
    o = _moba(q, k, v, z)
    return [o.reshape(b * l, w)], [w_out.astype(BF16)]


def kernel(x, even_norm, even_w_in, even_conv_w, even_conv_b, even_a_log, even_dt_bias, even_d_skip, even_ssd_norm,
           even_gate_w2, even_gate_b, even_gla_norm, even_w_out, odd_norm, odd_w_in, odd_w_out, final_norm):
    b, l, d = x.shape
    depth = even_norm.shape[0] + odd_norm.shape[0]
    for layer in range(depth):
        i = layer // 2
        if layer % 2 == 0:
            acts, weights = _even_layer(x, even_norm[i], even_w_in[i], even_conv_w[i], even_conv_b[i], even_a_log[i],
                                        even_dt_bias[i], even_d_skip[i], even_ssd_norm[i], even_gate_w2[i],
                                        even_gate_b[i], even_gla_norm[i], even_w_out[i])
        else:
            acts, weights = _odd_layer(x, odd_norm[i], odd_w_in[i], odd_w_out[i])
        last = layer == depth - 1
        x = _out_proj(acts, x.reshape(b * l, d), weights, final_norm if last else None).reshape(b, l, d)
    return x
```

```python
import functools

import jax
import jax.numpy as jnp
from jax import lax
from jax.experimental import pallas as pl
from jax.experimental.pallas import tpu as pltpu

F32 = jnp.float32
BF16 = jnp.bfloat16
HIGHEST = lax.Precision.HIGHEST

LANES = 128
SUBLANES = 8
BF16_SUBLANES = 16
VMEM_LIMIT_BYTES = 56 * 1024 * 1024

RMS_EPS = 1e-6
SSD_HEAD_DIM = 64
SSD_GROUPS = 2
SSD_STATE = 64
SSD_CONV = 4
SSD_CHUNK = 128
GLA_HEADS = 8
GLA_GATE_NORMALIZER = 16.0
GLA_CHUNK = 64
MOBA_HEAD_DIM = 64
MOBA_BLOCK = 256
MOBA_TOPK = 3
LOG2_E = 1.4426950408889634

PROJ_ROWS = 256
GLA_ROWS = 128
HALF = LANES // 2


def _params(*sem):
    return pltpu.CompilerParams(dimension_semantics=sem, vmem_limit_bytes=VMEM_LIMIT_BYTES)


def _rms(x, w):
    return x * lax.rsqrt(jnp.mean(x * x, axis=-1, keepdims=True) + RMS_EPS) * w


def _silu(x):
    return x * jax.nn.sigmoid(x)


def _softplus(x):
    return jnp.maximum(x, 0.0) + jnp.log1p(jnp.exp(-jnp.abs(x)))


def _log_sigmoid(x):
    return jnp.minimum(x, 0.0) - jnp.log1p(jnp.exp(-jnp.abs(x)))


def _dot(a, b):
    return jnp.dot(a.astype(BF16), b.astype(BF16), preferred_element_type=F32)


def _dot_nt(a, b):
    return lax.dot_general(a.astype(BF16), b.astype(BF16), (((1,), (1,)), ((), ())), preferred_element_type=F32)


def _dot_tn(a, b):
    return lax.dot_general(a.astype(BF16), b.astype(BF16), (((0,), (0,)), ((), ())), preferred_element_type=F32)


def _dot_f32(a, b):
    return jnp.dot(a, b, preferred_element_type=F32, precision=HIGHEST)


def _resident(shape):
    return pl.BlockSpec(shape, lambda *_: (0,) * len(shape), pipeline_mode=pl.Buffered(1))


def _norm_proj_body(x_ref, nw_ref, *refs):
    n = len(refs) // 2
    h = _rms(x_ref[...], nw_ref[...]).astype(BF16)
    for w_ref, o_ref in zip(refs[:n], refs[n:]):
        o_ref[...] = jnp.dot(h, w_ref[...], preferred_element_type=F32)


def _norm_proj(x2d, norm_w, weights):
    m, d = x2d.shape
    tm = PROJ_ROWS
    return pl.pallas_call(
        _norm_proj_body,
        name="norm_proj",
        grid=(m // tm,),
        in_specs=[pl.BlockSpec((tm, d), lambda i: (i, 0)), _resident((1, d))]
        + [_resident(w.shape) for w in weights],
        out_specs=[pl.BlockSpec((tm, w.shape[1]), lambda i: (i, 0)) for w in weights],
        out_shape=[jax.ShapeDtypeStruct((m, w.shape[1]), F32) for w in weights],
        compiler_params=_params("parallel"),
    )(x2d, norm_w.reshape(1, d), *weights)


def _ssd_body(xbc_ref, sm_ref, z_ref, cw_ref, cb_ref, alog_ref, dtb_ref, dskip_ref, nw_ref,
              y_ref, tail_ref, h_ref, xpad_ref, xw_ref, ycat_ref):
    q = SSD_CHUNK
    width = y_ref.shape[-1]
    n_pairs = width // LANES
    gstate = SSD_GROUPS * SSD_STATE
    gwidth = width // SSD_GROUPS

    @pl.when(pl.program_id(1) == 0)
    def _():
        tail_ref[...] = jnp.zeros_like(tail_ref)
        h_ref[...] = jnp.zeros_like(h_ref)

    xbc = xbc_ref[0]
    xpad_ref[0:SUBLANES, :] = tail_ref[...]
    xpad_ref[SUBLANES:SUBLANES + q, :] = xbc
    tail_ref[...] = xbc[q - SUBLANES:, :]
    conv = cb_ref[...] + cw_ref[SSD_CONV - 1:SSD_CONV, :] * xbc
    for k in range(SSD_CONV - 1):
        off = SUBLANES - (SSD_CONV - 1) + k
        conv = conv + cw_ref[k:k + 1, :] * xpad_ref[off:off + q, :]
    act = _silu(conv)
    xs = act[:, :width]
    bm = act[:, width:width + gstate]
    cm = act[:, width + gstate:]

    dt = _softplus(sm_ref[0] + dtb_ref[...])
    dta = dt * (-jnp.exp(alog_ref[...]))
    ri = lax.broadcasted_iota(jnp.int32, (q, q), 0)
    ci = lax.broadcasted_iota(jnp.int32, (q, q), 1)
    causal = ri >= ci
    a_cs = _dot_f32(causal.astype(F32), dta)
    a_cs_t = a_cs.T
    a_last = a_cs[q - 1:q, :]
    bm_t = bm.T

    lane = lax.broadcasted_iota(jnp.int32, (q, LANES), 1)
    lo = lane < HALF
    lo_row = lo[0:1, :]
    cm_g = [jnp.where(lo, cm, 0.0), jnp.where(lo, 0.0, cm)]
    cb = [_dot_nt(c, bm) for c in cm_g]
    h_prev = h_ref[...]
    y_off = [_dot(c, h_prev) for c in cm_g]

    a_last_pairs = []
    for p in range(n_pairs):
        e0, e1 = 2 * p, 2 * p + 1
        g = (p * LANES) // gwidth
        col = slice(p * LANES, (p + 1) * LANES)
        gcol = slice(p * LANES - g * gwidth, (p + 1) * LANES - g * gwidth)
        acs_pair = jnp.where(lo, a_cs[:, e0:e0 + 1], a_cs[:, e1:e1 + 1])
        dt_pair = jnp.where(lo, dt[:, e0:e0 + 1], dt[:, e1:e1 + 1])
        al_pair = jnp.where(lo_row, a_last[:, e0:e0 + 1], a_last[:, e1:e1 + 1])
        a_last_pairs.append(al_pair)
        xs2 = xs[:, col]
        xdt = xs2 * dt_pair
        xdt_b = xdt.astype(BF16)
        yd = []
        for e in (e0, e1):
            seg = a_cs[:, e:e + 1] - a_cs_t[e:e + 1, :]
            decay = jnp.exp(jnp.where(causal, seg, -jnp.inf))
            yd.append(_dot(cb[g] * decay, xdt_b))
        y2 = jnp.where(lo, yd[0], yd[1])
        y2 = y2 + y_off[g][:, gcol] * jnp.exp(acs_pair) + dskip_ref[:, col] * xs2
        ycat_ref[:, col] = y2
        xw_ref[:, col] = (xdt * jnp.exp(al_pair - acs_pair)).astype(BF16)

    for g in range(SSD_GROUPS):
        rows = slice(g * SSD_STATE, (g + 1) * SSD_STATE)
        ppg = n_pairs // SSD_GROUPS
        dec = jnp.exp(jnp.concatenate(a_last_pairs[g * ppg:(g + 1) * ppg], axis=1))
        s_g = _dot(bm_t[rows, :], xw_ref[:, g * gwidth:(g + 1) * gwidth])
        h_ref[rows, :] = h_prev[rows, :] * dec + s_g

    y_ref[0] = _rms(ycat_ref[...] * _silu(z_ref[0]), nw_ref[...])


def _ssd(xbc, small, z, conv_w, conv_b, a_log, dt_bias, d_skip, norm_w):
    b, l, cdim = xbc.shape
    width = z.shape[-1]
    q = SSD_CHUNK
    n_heads = a_log.shape[0]
    pad = lambda v: jnp.pad(v, (0, LANES - n_heads)).reshape(1, LANES)
    gwidth = width // SSD_GROUPS
    blk = lambda w: pl.BlockSpec((1, q, w), lambda i, c: (i, c, 0))
    return pl.pallas_call(
        _ssd_body,
        name="ssd",
        grid=(b, l // q),
        in_specs=[blk(cdim), blk(LANES), blk(width),
                  _resident((SSD_CONV, cdim)), _resident((1, cdim)), _resident((1, LANES)),
                  _resident((1, LANES)), _resident((1, width)), _resident((1, width))],
        out_specs=blk(width),
        out_shape=jax.ShapeDtypeStruct((b, l, width), F32),
        scratch_shapes=[pltpu.VMEM((SUBLANES, cdim), F32),
                        pltpu.VMEM((SSD_GROUPS * SSD_STATE, gwidth), F32),
                        pltpu.VMEM((q + SUBLANES, cdim), F32),
                        pltpu.VMEM((q, width), BF16),
                        pltpu.VMEM((q, width), F32)],
        compiler_params=_params("parallel", "arbitrary"),
    )(xbc, small, z, conv_w, conv_b.reshape(1, cdim), pad(a_log), pad(dt_bias),
      jnp.repeat(d_skip, SSD_HEAD_DIM).reshape(1, width), norm_w.reshape(1, width))


def _gla_body(q_ref, k_ref, v_ref, sm_ref, g_ref, w2_ref, gb_ref, nw_ref, o_ref, st_ref):
    c = GLA_CHUNK
    kw = q_ref.shape[-1]
    dk = kw // GLA_HEADS
    dv = v_ref.shape[-1] // GLA_HEADS
    n_pairs = kw // LANES

    @pl.when(pl.program_id(1) == 0)
    def _():
        st_ref[...] = jnp.zeros_like(st_ref)

    ri = lax.broadcasted_iota(jnp.int32, (c, c), 0)
    ci = lax.broadcasted_iota(jnp.int32, (c, c), 1)
    causal = ri >= ci
    tri = causal.astype(F32)
    lo = lax.broadcasted_iota(jnp.int32, (c, LANES), 1) < HALF
    lo_sq = lax.broadcasted_iota(jnp.int32, (dv, LANES), 1) < HALF

    for ch in range(q_ref.shape[1] // c):
        rows = slice(ch * c, (ch + 1) * c)
        pre = _dot_f32(sm_ref[0, rows, :], w2_ref[...]) + gb_ref[...]
        gcs = _dot_f32(tri, _log_sigmoid(pre) * (1.0 / GLA_GATE_NORMALIZER))
        g_mid = gcs[c // 2:c // 2 + 1, :]
        g_last = gcs[c - 1:c, :]
        qs = q_ref[0, rows, :] * (dk ** -0.5)
        ks = k_ref[0, rows, :]
        q_in = qs * jnp.exp(gcs - g_mid)
        k_in = ks * jnp.exp(g_mid - gcs)
        q_st = qs * jnp.exp(gcs)
        k_st = ks * jnp.exp(g_last - gcs)
        for p in range(n_pairs):
            col = slice(p * LANES, (p + 1) * LANES)
            st = st_ref[:, col]
            kv_t = []
            for half in range(2):
                hd = 2 * p + half
                vcol = slice(hd * dv, (hd + 1) * dv)
                mask = lo if half == 0 else jnp.logical_not(lo)
                v_h = v_ref[0, rows, vcol]
                s = _dot_nt(jnp.where(mask, q_in[:, col], 0.0), k_in[:, col])
                s = jnp.where(causal, s, 0.0)
                o = _dot(s, v_h) + _dot_nt(jnp.where(mask, q_st[:, col], 0.0), st)
                kv_t.append(_dot_tn(v_h, k_st[:, col]))
                g_h = g_ref[0, rows, vcol]
                o_ref[0, rows, vcol] = _rms(o, nw_ref[...]) * _silu(g_h)
            st_ref[:, col] = st * jnp.exp(g_last[:, col]) + jnp.where(lo_sq, kv_t[0], kv_t[1])


def _gla(q, k, v, small, g, w2_pad, gate_b, norm_w):
    b, l, kw = q.shape
    vw = v.shape[-1]
    t = GLA_ROWS
    blk = lambda w: pl.BlockSpec((1, t, w), lambda i, c: (i, c, 0))
    return pl.pallas_call(
        _gla_body,
        name="gla",
        grid=(b, l // t),
        in_specs=[blk(kw), blk(kw), blk(vw), blk(LANES), blk(vw),
                  _resident((LANES, kw)), _resident((1, kw)), _resident((1, vw // GLA_HEADS))],
        out_specs=blk(vw),
        out_shape=jax.ShapeDtypeStruct((b, l, vw), F32),
        scratch_shapes=[pltpu.VMEM((vw // GLA_HEADS, kw), F32)],
        compiler_params=_params("parallel", "arbitrary"),
    )(q, k, v, small, g, w2_pad, gate_b.reshape(1, kw), norm_w.reshape(1, -1))


def _out_proj_body(*refs, n_in, final_norm):
    a_refs, x_ref = refs[:n_in], refs[n_in]
    w_refs = refs[n_in + 1:2 * n_in + 1]
    rest = refs[2 * n_in + 1:]
    acc = x_ref[...]
    for a_ref, w_ref in zip(a_refs, w_refs):
        acc = acc + jnp.dot(a_ref[...].astype(BF16), w_ref[...], preferred_element_type=F32)
    if final_norm:
        fw_ref, o_ref = rest
        o_ref[...] = _rms(acc, fw_ref[...])
    else:
        (o_ref,) = rest
        o_ref[...] = acc


def _out_proj(acts, x2d, weights, final_w=None):
    m, d = x2d.shape
    tm = PROJ_ROWS
    n_in = len(acts)
    row = lambda w: pl.BlockSpec((tm, w), lambda i: (i, 0))
    in_specs = [row(a.shape[1]) for a in acts] + [row(d)] + [_resident(w.shape) for w in weights]
    args = list(acts) + [x2d] + list(weights)
    if final_w is not None:
        in_specs.append(_resident((1, d)))
        args.append(final_w.reshape(1, d))
    return pl.pallas_call(
        functools.partial(_out_proj_body, n_in=n_in, final_norm=final_w is not None),
        name="out_proj",
        grid=(m // tm,),
        in_specs=in_specs,
        out_specs=row(d),
        out_shape=jax.ShapeDtypeStruct((m, d), F32),
        compiler_params=_params("parallel"),
    )(*args)


def _moba_query_block(n_past, q_ref, z_ref, o_ref, kb_ref, vt_ref, kmean_ref, s_ref):
    blk = MOBA_BLOCK
    nk = (n_past + 1) * blk
    q2 = q_ref[0]
    lo = lax.broadcasted_iota(jnp.int32, (blk, LANES), 1) < HALF
    q_cat = jnp.concatenate([jnp.where(lo, q2, 0.0), jnp.where(lo, 0.0, q2)], axis=0)
    scale = MOBA_HEAD_DIM ** -0.5
    key_i = lax.broadcasted_iota(jnp.int32, (blk, 2 * blk), 0)
    qry_i = lax.broadcasted_iota(jnp.int32, (blk, 2 * blk), 1)
    causal_t = key_i <= jnp.where(qry_i < blk, qry_i, qry_i - blk)

    masks = []
    if n_past:
        gate = lax.dot_general(kmean_ref[...], q_cat * scale, (((1,), (1,)), ((), ())),
                               preferred_element_type=F32, precision=HIGHEST)
        g_rows = [gate[n:n + 1, :] for n in range(n_past)]
        for n in range(n_past):
            rank = jnp.zeros((1, 2 * blk), F32)
            for m in range(n_past):
                if m != n:
                    ahead = (g_rows[m] >= g_rows[n]) if m < n else (g_rows[m] > g_rows[n])
                    rank = rank + jnp.where(ahead, 1.0, 0.0)
            masks.append(rank < MOBA_TOPK)
    masks.append(causal_t)

    q_s = (q_cat * (scale * LOG2_E)).astype(BF16)
    m_run = None
    for j in range(n_past + 1):
        rows = slice(j * blk, (j + 1) * blk)
        sj = jnp.where(masks[j], _dot_nt(kb_ref[rows, :], q_s), -jnp.inf)
        s_ref[rows, :] = sj
        mj = jnp.max(sj, axis=0, keepdims=True)
        m_run = mj if m_run is None else jnp.maximum(m_run, mj)
    acc = None
    for j in range(n_past + 1):
        rows = slice(j * blk, (j + 1) * blk)
        pj = jnp.exp2(s_ref[rows, :] - m_run).astype(BF16)
        d = jnp.dot(vt_ref[:, rows], pj, preferred_element_type=F32)
        acc = d if acc is None else d + acc
    acc = acc[0:LANES, :] / acc[LANES:LANES + 1, :]
    o_t = jnp.concatenate([acc[0:HALF, 0:blk], acc[HALF:LANES, blk:2 * blk]], axis=0)
    o_ref[0] = o_t.T * _silu(z_ref[0])


def _moba_body(q_ref, k_ref, v_ref, z_ref, o_ref, kb_ref, vt_ref, kmean_ref, s_ref):
    blk = MOBA_BLOCK
    nb = k_ref.shape[1] // blk
    i = pl.program_id(2)

    @pl.when(i == 0)
    def _():
        for n in range(nb):
            rows = slice(n * blk, (n + 1) * blk)
            kn = k_ref[0, rows, :]
            kb_ref[rows, :] = kn.astype(BF16)
            kmean_ref[n:n + 1, :] = jnp.mean(kn, axis=0, keepdims=True)
            vt_ref[0:LANES, rows] = v_ref[0, rows, :].T.astype(BF16)
        vt_ref[LANES:, :] = jnp.ones((vt_ref.shape[0] - LANES, vt_ref.shape[1]), BF16)

    for n_past in range(nb):
        pl.when(i == n_past)(functools.partial(
            _moba_query_block, n_past, q_ref, z_ref, o_ref, kb_ref, vt_ref, kmean_ref, s_ref))


def _moba(q, k, v, z):
    b, l, w = q.shape
    blk = MOBA_BLOCK
    nb = l // blk
    qspec = pl.BlockSpec((1, blk, LANES), lambda bi, hp, i: (bi, i, hp))
    kvspec = pl.BlockSpec((1, l, LANES), lambda bi, hp, i: (bi, 0, hp))
    return pl.pallas_call(
        _moba_body,
        name="moba",
        grid=(b, w // LANES, nb),
        in_specs=[qspec, kvspec, kvspec, qspec],
        out_specs=qspec,
        out_shape=jax.ShapeDtypeStruct((b, l, w), F32),
        scratch_shapes=[pltpu.VMEM((l, LANES), BF16),
                        pltpu.VMEM((LANES + BF16_SUBLANES, l), BF16),
                        pltpu.VMEM((nb, LANES), F32),
                        pltpu.VMEM((l, 2 * blk), F32)],
        compiler_params=_params("parallel", "parallel", "arbitrary"),
    )(q, k, v, z)


def _even_layer(x, norm_w, w_in, conv_w, conv_b, a_log, dt_bias, d_skip, ssd_norm_w, gate_w2, gate_b,
                gla_norm_w, w_out):
    b, l, d = x.shape
    width = ssd_norm_w.shape[0]
    cdim = conv_w.shape[1]
    n_heads = a_log.shape[0]
    rank, kw = gate_w2.shape
    vw = gla_norm_w.shape[0] * GLA_HEADS
    cuts = [0]
    for s in (width, cdim, n_heads, kw, kw, vw, rank, vw):
        cuts.append(cuts[-1] + s)
    seg = lambda j: w_in[:, cuts[j]:cuts[j + 1]]
    w_small = jnp.concatenate([seg(2), seg(6), jnp.zeros((d, LANES - n_heads - rank), w_in.dtype)], axis=1)
    weights = [seg(0), seg(1), w_small, seg(3), seg(4), seg(5), seg(7)]
    x2d = x.reshape(b * l, d)
    z_a, xbc, small, q_b, k_b, v_b, g_b = [
        u.reshape(b, l, -1) for u in _norm_proj(x2d, norm_w, [w.astype(BF16) for w in weights])]
    y_a = _ssd(xbc, small, z_a, conv_w, conv_b, a_log, dt_bias, d_skip, ssd_norm_w)
    w2_pad = jnp.zeros((LANES, kw), F32).at[n_heads:n_heads + rank, :].set(gate_w2)
    o_b = _gla(q_b, k_b, v_b, small, g_b, w2_pad, gate_b, gla_norm_w)
    return [y_a.reshape(b * l, width), o_b.reshape(b * l, vw)], [w_out[:width].astype(BF16), w_out[width:].astype(BF16)]


def _odd_layer(x, norm_w, w_in, w_out):
    b, l, d = x.shape
    w = w_out.shape[0]
    weights = [w_in[:, j * w:(j + 1) * w].astype(BF16) for j in range(4)]
    q, k, v, z = [u.reshape(b, l, w) for u in _norm_proj(x.reshape(b * l, d), norm_w, weights)]
    o = _moba(q, k, v, z)
    return [o.reshape(b * l, w)], [w_out.astype(BF16)]


def kernel(x, even_norm, even_w_in, even_conv_w, even_conv_b, even_a_log, even_dt_bias, even_d_skip, even_ssd_norm,
           even_gate_w2, even_gate_b, even_gla_norm, even_w_out, odd_norm, odd_w_in, odd_w_out, final_norm):
    b, l, d = x.shape
    depth = even_norm.shape[0] + odd_norm.shape[0]
    for layer in range(depth):
        i = layer // 2
        if layer % 2 == 0:
            acts, weights = _even_layer(x, even_norm[i], even_w_in[i], even_conv_w[i], even_conv_b[i], even_a_log[i],
                                        even_dt_bias[i], even_d_skip[i], even_ssd_norm[i], even_gate_w2[i],
                                        even_gate_b[i], even_gla_norm[i], even_w_out[i])
        else:
            acts, weights = _odd_layer(x, odd_norm[i], odd_w_in[i], odd_w_out[i])
        last = layer == depth - 1
        x = _out_proj(acts, x.reshape(b * l, d), weights, final_norm if last else None).reshape(b, l, d)
    return x
```

```python
import functools

import jax
import jax.numpy as jnp
from jax import lax
from jax.experimental import pallas as pl
from jax.experimental.pallas import tpu as pltpu

F32 = jnp.float32
BF16 = jnp.bfloat16
HIGHEST = lax.Precision.HIGHEST

LANES = 128
SUBLANES = 8
BF16_SUBLANES = 16
VMEM_LIMIT_BYTES = 56 * 1024 * 1024

RMS_EPS = 1e-6
SSD_HEAD_DIM = 64
SSD_GROUPS = 2
SSD_STATE = 64
SSD_CONV = 4
SSD_CHUNK = 128
GLA_HEADS = 8
GLA_GATE_NORMALIZER = 16.0
GLA_CHUNK = 64
MOBA_HEAD_DIM = 64
MOBA_BLOCK = 256
MOBA_TOPK = 3
LOG2_E = 1.4426950408889634
CUMSUM_TERMS = 3
GATE_COPIES = 3

PROJ_ROWS = 256
GLA_ROWS = 256
HALF = LANES // 2


def _params(*sem):
    return pltpu.CompilerParams(dimension_semantics=sem, vmem_limit_bytes=VMEM_LIMIT_BYTES)


def _rms(x, w):
    return x * lax.rsqrt(jnp.mean(x * x, axis=-1, keepdims=True) + RMS_EPS) * w


def _silu(x):
    return x * jax.nn.sigmoid(x)


def _softplus(x):
    return jnp.maximum(x, 0.0) + jnp.log1p(jnp.exp(-jnp.abs(x)))


def _log_sigmoid(x):
    return jnp.minimum(x, 0.0) - jnp.log1p(jnp.exp(-jnp.abs(x)))


def _dot(a, b):
    return jnp.dot(a.astype(BF16), b.astype(BF16), preferred_element_type=F32)


def _dot_nt(a, b):
    return lax.dot_general(a.astype(BF16), b.astype(BF16), (((1,), (1,)), ((), ())), preferred_element_type=F32)


def _dot_tn(a, b):
    return lax.dot_general(a.astype(BF16), b.astype(BF16), (((0,), (0,)), ((), ())), preferred_element_type=F32)


def _split_bf16(x, terms):
    parts = []
    for _ in range(terms):
        p = x.astype(BF16)
        parts.append(p)
        x = x - p.astype(F32)
    return parts


def _chunk_cumsum(x, chunk):
    rows = x.shape[0]
    ri = lax.broadcasted_iota(jnp.int32, (chunk, CUMSUM_TERMS * chunk), 0)
    ci = lax.broadcasted_iota(jnp.int32, (chunk, CUMSUM_TERMS * chunk), 1)
    tri = jnp.where((ci & (chunk - 1)) <= ri, 1.0, 0.0).astype(BF16)
    out = []
    for r0 in range(0, rows, chunk):
        stacked = jnp.concatenate(_split_bf16(x[r0:r0 + chunk, :], CUMSUM_TERMS), axis=0)
        out.append(jnp.dot(tri, stacked, preferred_element_type=F32))
    return out[0] if len(out) == 1 else jnp.concatenate(out, axis=0)


def _resident(shape):
    return pl.BlockSpec(shape, lambda *_: (0,) * len(shape), pipeline_mode=pl.Buffered(1))


def _norm_proj_body(x_ref, nw_ref, *refs):
    n = len(refs) // 2
    h = _rms(x_ref[...], nw_ref[...]).astype(BF16)
    for w_ref, o_ref in zip(refs[:n], refs[n:]):
        o_ref[...] = jnp.dot(h, w_ref[...], preferred_element_type=F32)


def _norm_proj(x2d, norm_w, weights):
    m, d = x2d.shape
    tm = PROJ_ROWS
    return pl.pallas_call(
        _norm_proj_body,
        name="norm_proj",
        grid=(m // tm,),
        in_specs=[pl.BlockSpec((tm, d), lambda i: (i, 0)), _resident((1, d))]
        + [_resident(w.shape) for w in weights],
        out_specs=[pl.BlockSpec((tm, w.shape[1]), lambda i: (i, 0)) for w in weights],
        out_shape=[jax.ShapeDtypeStruct((m, w.shape[1]), F32) for w in weights],
        compiler_params=_params("parallel"),
    )(x2d, norm_w.reshape(1, d), *weights)


def _ssd_body(xbc_ref, sm_ref, z_ref, cw_ref, cb_ref, alog_ref, dtb_ref, dskip_ref, nw_ref,
              y_ref, tail_ref, h_ref, xpad_ref, xw_ref, ycat_ref):
    q = SSD_CHUNK
    width = y_ref.shape[-1]
    n_pairs = width // LANES
    gstate = SSD_GROUPS * SSD_STATE
    gwidth = width // SSD_GROUPS

    @pl.when(pl.program_id(1) == 0)
    def _():
        tail_ref[...] = jnp.zeros_like(tail_ref)
        h_ref[...] = jnp.zeros_like(h_ref)

    xbc = xbc_ref[0]
    xpad_ref[0:SUBLANES, :] = tail_ref[...]
    xpad_ref[SUBLANES:SUBLANES + q, :] = xbc
    tail_ref[...] = xbc[q - SUBLANES:, :]
    conv = cb_ref[...] + cw_ref[SSD_CONV - 1:SSD_CONV, :] * xbc
    for k in range(SSD_CONV - 1):
        off = SUBLANES - (SSD_CONV - 1) + k
        conv = conv + cw_ref[k:k + 1, :] * xpad_ref[off:off + q, :]
    act = _silu(conv)
    xs = act[:, :width]
    bm = act[:, width:width + gstate]
    cm = act[:, width + gstate:]

    dt = _softplus(sm_ref[0] + dtb_ref[...])
    dta = dt * (-jnp.exp(alog_ref[...]))
    ri = lax.broadcasted_iota(jnp.int32, (q, q), 0)
    ci = lax.broadcasted_iota(jnp.int32, (q, q), 1)
    causal = ri >= ci
    a_cs = _chunk_cumsum(dta, q)
    a_cs_t = a_cs.T
    a_last = a_cs[q - 1:q, :]
    bm_t = bm.T

    lane = lax.broadcasted_iota(jnp.int32, (q, LANES), 1)
    lo = lane < HALF
    lo_row = lo[0:1, :]
    cm_g = [jnp.where(lo, cm, 0.0), jnp.where(lo, 0.0, cm)]
    cb = [_dot_nt(c, bm) for c in cm_g]
    h_prev = h_ref[...]
    y_off = [_dot(c, h_prev) for c in cm_g]

    a_last_pairs = []
    for p in range(n_pairs):
        e0, e1 = 2 * p, 2 * p + 1
        g = (p * LANES) // gwidth
        col = slice(p * LANES, (p + 1) * LANES)
        gcol = slice(p * LANES - g * gwidth, (p + 1) * LANES - g * gwidth)
        acs_pair = jnp.where(lo, a_cs[:, e0:e0 + 1], a_cs[:, e1:e1 + 1])
        dt_pair = jnp.where(lo, dt[:, e0:e0 + 1], dt[:, e1:e1 + 1])
        al_pair = jnp.where(lo_row, a_last[:, e0:e0 + 1], a_last[:, e1:e1 + 1])
        a_last_pairs.append(al_pair)
        xs2 = xs[:, col]
        xdt = xs2 * dt_pair
        xdt_b = xdt.astype(BF16)
        yd = []
        for e in (e0, e1):
            seg = a_cs[:, e:e + 1] - a_cs_t[e:e + 1, :]
            decay = jnp.exp(jnp.where(causal, seg, -jnp.inf))
            yd.append(_dot(cb[g] * decay, xdt_b))
        y2 = jnp.where(lo, yd[0], yd[1])
        y2 = y2 + y_off[g][:, gcol] * jnp.exp(acs_pair) + dskip_ref[:, col] * xs2
        ycat_ref[:, col] = y2
        xw_ref[:, col] = (xdt * jnp.exp(al_pair - acs_pair)).astype(BF16)

    for g in range(SSD_GROUPS):
        rows = slice(g * SSD_STATE, (g + 1) * SSD_STATE)
        ppg = n_pairs // SSD_GROUPS
        dec = jnp.exp(jnp.concatenate(a_last_pairs[g * ppg:(g + 1) * ppg], axis=1))
        s_g = _dot(bm_t[rows, :], xw_ref[:, g * gwidth:(g + 1) * gwidth])
        h_ref[rows, :] = h_prev[rows, :] * dec + s_g

    y_ref[0] = _rms(ycat_ref[...] * _silu(z_ref[0]), nw_ref[...])


def _ssd(xbc, small, z, conv_w, conv_b, a_log, dt_bias, d_skip, norm_w):
    b, l, cdim = xbc.shape
    width = z.shape[-1]
    q = SSD_CHUNK
    n_heads = a_log.shape[0]
    pad = lambda v: jnp.pad(v, (0, LANES - n_heads)).reshape(1, LANES)
    gwidth = width // SSD_GROUPS
    blk = lambda w: pl.BlockSpec((1, q, w), lambda i, c: (i, c, 0))
    return pl.pallas_call(
        _ssd_body,
        name="ssd",
        grid=(b, l // q),
        in_specs=[blk(cdim), blk(LANES), blk(width),
                  _resident((SSD_CONV, cdim)), _resident((1, cdim)), _resident((1, LANES)),
                  _resident((1, LANES)), _resident((1, width)), _resident((1, width))],
        out_specs=blk(width),
        out_shape=jax.ShapeDtypeStruct((b, l, width), F32),
        scratch_shapes=[pltpu.VMEM((SUBLANES, cdim), F32),
                        pltpu.VMEM((SSD_GROUPS * SSD_STATE, gwidth), F32),
                        pltpu.VMEM((q + SUBLANES, cdim), F32),
                        pltpu.VMEM((q, width), BF16),
                        pltpu.VMEM((q, width), F32)],
        compiler_params=_params("parallel", "arbitrary"),
    )(xbc, small, z, conv_w, conv_b.reshape(1, cdim), pad(a_log), pad(dt_bias),
      jnp.repeat(d_skip, SSD_HEAD_DIM).reshape(1, width), norm_w.reshape(1, width))


def _gla_body(q_ref, k_ref, v_ref, sm_ref, g_ref, w2_ref, gb_ref, nw_ref, o_ref, st_ref, *, mid_lane):
    c = GLA_CHUNK
    kw = q_ref.shape[-1]
    dk = kw // GLA_HEADS
    dv = v_ref.shape[-1] // GLA_HEADS
    n_pairs = kw // LANES

    @pl.when(pl.program_id(1) == 0)
    def _():
        st_ref[...] = jnp.zeros_like(st_ref)

    ri = lax.broadcasted_iota(jnp.int32, (c, c), 0)
    ci = lax.broadcasted_iota(jnp.int32, (c, c), 1)
    causal = ri >= ci
    lo = lax.broadcasted_iota(jnp.int32, (c, LANES), 1) < HALF
    lo_sq = lax.broadcasted_iota(jnp.int32, (dv, LANES), 1) < HALF

    sm_hi, sm_mid = _split_bf16(sm_ref[0], 2)
    sm_lane = lax.broadcasted_iota(jnp.int32, sm_hi.shape, 1)
    pre = jnp.dot(jnp.where(sm_lane < mid_lane, sm_hi, sm_mid), w2_ref[...],
                  preferred_element_type=F32) + gb_ref[...]
    gcs_all = _chunk_cumsum(_log_sigmoid(pre) * (1.0 / GLA_GATE_NORMALIZER), c)

    chunks = [slice(ch * c, (ch + 1) * c) for ch in range(q_ref.shape[1] // c)]
    pairs = [slice(p * LANES, (p + 1) * LANES) for p in range(n_pairs)]
    causal2 = jnp.concatenate([causal, causal], axis=0)

    def by_head(x):
        return jnp.concatenate([jnp.where(lo, x, 0.0), jnp.where(lo, 0.0, x)], axis=0).astype(BF16)

    q_in, k_in, q_st, k_st, g_last = [], [], [], [], []
    for rows in chunks:
        gcs = gcs_all[rows, :]
        g_mid = gcs[c // 2:c // 2 + 1, :]
        g_last.append(gcs[c - 1:c, :])
        qs = q_ref[0, rows, :] * (dk ** -0.5)
        ks = k_ref[0, rows, :]
        q_in.append(qs * jnp.exp(gcs - g_mid))
        k_in.append((ks * jnp.exp(g_mid - gcs)).astype(BF16))
        q_st.append(qs * jnp.exp(gcs))
        k_st.append((ks * jnp.exp(g_last[-1] - gcs)).astype(BF16))

    scores = [[_dot_nt(by_head(q_in[ch][:, col]), k_in[ch][:, col]) for col in pairs]
              for ch in range(len(chunks))]
    scores = [[jnp.where(causal2, s, 0.0).astype(BF16) for s in row] for row in scores]

    o_intra, kv_t = [], []
    for ch, rows in enumerate(chunks):
        o_row, kv_row = [], []
        for p, col in enumerate(pairs):
            v_a = v_ref[0, rows, (2 * p) * dv:(2 * p + 1) * dv].astype(BF16)
            v_b = v_ref[0, rows, (2 * p + 1) * dv:(2 * p + 2) * dv].astype(BF16)
            o_row.append((jnp.dot(scores[ch][p][0:c, :], v_a, preferred_element_type=F32),
                          jnp.dot(scores[ch][p][c:2 * c, :], v_b, preferred_element_type=F32)))
            kv_row.append(jnp.where(lo_sq, _dot_tn(v_a, k_st[ch][:, col]), _dot_tn(v_b, k_st[ch][:, col])))
        o_intra.append(o_row)
        kv_t.append(kv_row)

    o_inter = [[None] * n_pairs for _ in chunks]
    for p, col in enumerate(pairs):
        st = st_ref[:, col]
        for ch in range(len(chunks)):
            o_inter[ch][p] = _dot_nt(by_head(q_st[ch][:, col]), st)
            st = st * jnp.exp(g_last[ch][:, col]) + kv_t[ch][p]
        st_ref[:, col] = st

    for ch, rows in enumerate(chunks):
        for p in range(n_pairs):
            for half in range(2):
                vcol = slice((2 * p + half) * dv, (2 * p + half + 1) * dv)
                o = o_intra[ch][p][half] + o_inter[ch][p][half * c:(half + 1) * c, :]
                o_ref[0, rows, vcol] = _rms(o, nw_ref[...]) * _silu(g_ref[0, rows, vcol])


def _gla(q, k, v, small, g, gate_w2, gate_b, norm_w, lr_lane):
    b, l, kw = q.shape
    vw = v.shape[-1]
    rank = gate_w2.shape[0]
    t = GLA_ROWS
    w2_hi, w2_mid = _split_bf16(gate_w2, 2)
    w2_pad = jnp.zeros((LANES, kw), BF16)
    for copy, part in enumerate((w2_hi, w2_mid, w2_hi)):
        w2_pad = w2_pad.at[lr_lane + copy * rank:lr_lane + (copy + 1) * rank, :].set(part)
    blk = lambda w: pl.BlockSpec((1, t, w), lambda i, c: (i, c, 0))
    return pl.pallas_call(
        functools.partial(_gla_body, mid_lane=lr_lane + 2 * rank),
        name="gla",
        grid=(b, l // t),
        in_specs=[blk(kw), blk(kw), blk(vw), blk(LANES), blk(vw),
                  _resident((LANES, kw)), _resident((1, kw)), _resident((1, vw // GLA_HEADS))],
        out_specs=blk(vw),
        out_shape=jax.ShapeDtypeStruct((b, l, vw), F32),
        scratch_shapes=[pltpu.VMEM((vw // GLA_HEADS, kw), F32)],
        compiler_params=_params("parallel", "arbitrary"),
    )(q, k, v, small, g, w2_pad, gate_b.reshape(1, kw), norm_w.reshape(1, -1))


def _out_proj_body(*refs, n_in, final_norm):
    a_refs, x_ref = refs[:n_in], refs[n_in]
    w_refs = refs[n_in + 1:2 * n_in + 1]
    rest = refs[2 * n_in + 1:]
    acc = x_ref[...]
    for a_ref, w_ref in zip(a_refs, w_refs):
        acc = acc + jnp.dot(a_ref[...].astype(BF16), w_ref[...], preferred_element_type=F32)
    if final_norm:
        fw_ref, o_ref = rest
        o_ref[...] = _rms(acc, fw_ref[...])
    else:
        (o_ref,) = rest
        o_ref[...] = acc


def _out_proj(acts, x2d, weights, final_w=None):
    m, d = x2d.shape
    tm = PROJ_ROWS
    n_in = len(acts)
    row = lambda w: pl.BlockSpec((tm, w), lambda i: (i, 0))
    in_specs = [row(a.shape[1]) for a in acts] + [row(d)] + [_resident(w.shape) for w in weights]
    args = list(acts) + [x2d] + list(weights)
    if final_w is not None:
        in_specs.append(_resident((1, d)))
        args.append(final_w.reshape(1, d))
    return pl.pallas_call(
        functools.partial(_out_proj_body, n_in=n_in, final_norm=final_w is not None),
        name="out_proj",
        grid=(m // tm,),
        in_specs=in_specs,
        out_specs=row(d),
        out_shape=jax.ShapeDtypeStruct((m, d), F32),
        compiler_params=_params("parallel"),
    )(*args)


def _moba_query_block(n_past, q_ref, z_ref, o_ref, kb_ref, vt_ref, kmean_ref, s_ref):
    blk = MOBA_BLOCK
    own = slice(n_past * blk, (n_past + 1) * blk)
    s_ref = s_ref.at[n_past % s_ref.shape[0]]
    q2 = q_ref[0, own, :]
    lo = lax.broadcasted_iota(jnp.int32, (blk, LANES), 1) < HALF
    q_cat = jnp.concatenate([jnp.where(lo, q2, 0.0), jnp.where(lo, 0.0, q2)], axis=0)
    scale = MOBA_HEAD_DIM ** -0.5
    key_i = lax.broadcasted_iota(jnp.int32, (blk, 2 * blk), 0)
    qry_i = lax.broadcasted_iota(jnp.int32, (blk, 2 * blk), 1)
    causal_t = key_i <= jnp.where(qry_i < blk, qry_i, qry_i - blk)

    masks = []
    if n_past:
        gate = lax.dot_general(kmean_ref[...], q_cat * scale, (((1,), (1,)), ((), ())),
                               preferred_element_type=F32, precision=HIGHEST)
        g_rows = [gate[n:n + 1, :] for n in range(n_past)]
        for n in range(n_past):
            rank = jnp.zeros((1, 2 * blk), F32)
            for m in range(n_past):
                if m != n:
                    ahead = (g_rows[m] >= g_rows[n]) if m < n else (g_rows[m] > g_rows[n])
                    rank = rank + jnp.where(ahead, 1.0, 0.0)
            masks.append(rank < MOBA_TOPK)
    masks.append(causal_t)

    q_s = (q_cat * (scale * LOG2_E)).astype(BF16)
    m_run = None
    for j in range(n_past + 1):
        rows = slice(j * blk, (j + 1) * blk)
        sj = jnp.where(masks[j], _dot_nt(kb_ref[rows, :], q_s), -jnp.inf)
        s_ref[rows, :] = sj
        mj = jnp.max(sj, axis=0, keepdims=True)
        m_run = mj if m_run is None else jnp.maximum(m_run, mj)
    acc = None
    for j in range(n_past + 1):
        rows = slice(j * blk, (j + 1) * blk)
        pj = jnp.exp2(s_ref[rows, :] - m_run).astype(BF16)
        d = jnp.dot(vt_ref[:, rows], pj, preferred_element_type=F32)
        acc = d if acc is None else d + acc
    acc = acc[0:LANES, :] / acc[LANES:LANES + 1, :]
    o_t = jnp.concatenate([acc[0:HALF, 0:blk], acc[HALF:LANES, blk:2 * blk]], axis=0)
    o_ref[0, own, :] = o_t.T * _silu(z_ref[0, own, :])


def _moba_body(q_ref, k_ref, v_ref, z_ref, o_ref, kb_ref, vt_ref, kmean_ref, s_ref):
    blk = MOBA_BLOCK
    nb = k_ref.shape[1] // blk
    for n in range(nb):
        rows = slice(n * blk, (n + 1) * blk)
        kn = k_ref[0, rows, :]
        kb_ref[rows, :] = kn.astype(BF16)
        kmean_ref[n:n + 1, :] = jnp.mean(kn, axis=0, keepdims=True)
        vt_ref[0:LANES, rows] = v_ref[0, rows, :].T.astype(BF16)
    vt_ref[LANES:, :] = jnp.ones((vt_ref.shape[0] - LANES, vt_ref.shape[1]), BF16)
    for n_past in range(nb):
        _moba_query_block(n_past, q_ref, z_ref, o_ref, kb_ref, vt_ref, kmean_ref, s_ref)


def _moba(q, k, v, z):
    b, l, w = q.shape
    blk = MOBA_BLOCK
    nb = l // blk
    spec = pl.BlockSpec((1, l, LANES), lambda bi, hp: (bi, 0, hp))
    return pl.pallas_call(
        _moba_body,
        name="moba",
        grid=(b, w // LANES),
        in_specs=[spec, spec, spec, spec],
        out_specs=spec,
        out_shape=jax.ShapeDtypeStruct((b, l, w), F32),
        scratch_shapes=[pltpu.VMEM((l, LANES), BF16),
                        pltpu.VMEM((LANES + BF16_SUBLANES, l), BF16),
                        pltpu.VMEM((nb, LANES), F32),
                        pltpu.VMEM((2, l, 2 * blk), F32)],
        compiler_params=_params("parallel", "parallel"),
    )(q, k, v, z)


def _even_layer(x, norm_w, w_in, conv_w, conv_b, a_log, dt_bias, d_skip, ssd_norm_w, gate_w2, gate_b,
                gla_norm_w, w_out):
    b, l, d = x.shape
    width = ssd_norm_w.shape[0]
    cdim = conv_w.shape[1]
    n_heads = a_log.shape[0]
    rank, kw = gate_w2.shape
    vw = gla_norm_w.shape[0] * GLA_HEADS
    cuts = [0]
    for s in (width, cdim, n_heads, kw, kw, vw, rank, vw):
        cuts.append(cuts[-1] + s)
    seg = lambda j: w_in[:, cuts[j]:cuts[j + 1]]
    w_small = jnp.concatenate([seg(2)] + [seg(6)] * GATE_COPIES
                              + [jnp.zeros((d, LANES - n_heads - GATE_COPIES * rank), w_in.dtype)], axis=1)
    weights = [seg(0), seg(1), w_small, seg(3), seg(4), seg(5), seg(7)]
    x2d = x.reshape(b * l, d)
    z_a, xbc, small, q_b, k_b, v_b, g_b = [
        u.reshape(b, l, -1) for u in _norm_proj(x2d, norm_w, [w.astype(BF16) for w in weights])]
    y_a = _ssd(xbc, small, z_a, conv_w, conv_b, a_log, dt_bias, d_skip, ssd_norm_w)
    o_b = _gla(q_b, k_b, v_b, small, g_b, gate_w2, gate_b, gla_norm_w, lr_lane=n_heads)
    return [y_a.reshape(b * l, width), o_b.reshape(b * l, vw)], [w_out[:width].astype(BF16), w_out[width:].astype(BF16)]


def _odd_layer(x, norm_w, w_in, w_out):
    b, l, d = x.shape
    w = w_out.shape[0]
    weights = [w_in[:, j * w:(j + 1) * w].astype(BF16) for j in range(4)]
    q, k, v, z = [u.reshape(b, l, w) for u in _norm_proj(x.reshape(b * l, d), norm_w, weights)]
    o = _moba(q, k, v, z)
    return [o.reshape(b * l, w)], [w_out.astype(BF16)]


def kernel(x, even_norm, even_w_in, even_conv_w, even_conv_b, even_a_log, even_dt_bias, even_d_skip, even_ssd_norm,
           even_gate_w2, even_gate_b, even_gla_norm, even_w_out, odd_norm, odd_w_in, odd_w_out, final_norm):
    b, l, d = x.shape
    depth = even_norm.shape[0] + odd_norm.shape[0]
    for layer in range(depth):
        i = layer // 2
        if layer % 2 == 0:
            acts, weights = _even_layer(x, even_norm[i], even_w_in[i], even_conv_w[i], even_conv_b[i], even_a_log[i],
                                        even_dt_bias[i], even_d_skip[i], even_ssd_norm[i], even_gate_w2[i],
                                        even_gate_b[i], even_gla_norm[i], even_w_out[i])
        else:
            acts, weights = _odd_layer(x, odd_norm[i], odd_w_in[i], odd_w_out[i])
        last = layer == depth - 1
        x = _out_proj(acts, x.reshape(b * l, d), weights, final_norm if last else None).reshape(b, l, d)
    return x
```

```python
import functools

import jax
import jax.numpy as jnp
from jax import lax
from jax.experimental import pallas as pl
from jax.experimental.pallas import tpu as pltpu

F32 = jnp.float32
BF16 = jnp.bfloat16
HIGHEST = lax.Precision.HIGHEST

LANES = 128
SUBLANES = 8
BF16_SUBLANES = 16
VMEM_LIMIT_BYTES = 56 * 1024 * 1024

RMS_EPS = 1e-6
SSD_HEAD_DIM = 64
SSD_GROUPS = 2
SSD_STATE = 64
SSD_CONV = 4
SSD_CHUNK = 128
GLA_HEADS = 8
GLA_GATE_NORMALIZER = 16.0
GLA_CHUNK = 64
MOBA_HEAD_DIM = 64
MOBA_BLOCK = 256
MOBA_TOPK = 3
LOG2_E = 1.4426950408889634
CUMSUM_TERMS = 3
GATE_COPIES = 3

PROJ_ROWS = 256
GLA_ROWS = 256
HALF = LANES // 2


def _params(*sem):
    return pltpu.CompilerParams(dimension_semantics=sem, vmem_limit_bytes=VMEM_LIMIT_BYTES)


def _rms(x, w):
    return x * lax.rsqrt(jnp.mean(x * x, axis=-1, keepdims=True) + RMS_EPS) * w


def _silu(x):
    return x * jax.nn.sigmoid(x)


def _softplus(x):
    return jnp.maximum(x, 0.0) + jnp.log1p(jnp.exp(-jnp.abs(x)))


def _log_sigmoid(x):
    return jnp.minimum(x, 0.0) - jnp.log1p(jnp.exp(-jnp.abs(x)))


def _dot(a, b):
    return jnp.dot(a.astype(BF16), b.astype(BF16), preferred_element_type=F32)


def _dot_nt(a, b):
    return lax.dot_general(a.astype(BF16), b.astype(BF16), (((1,), (1,)), ((), ())), preferred_element_type=F32)


def _dot_tn(a, b):
    return lax.dot_general(a.astype(BF16), b.astype(BF16), (((0,), (0,)), ((), ())), preferred_element_type=F32)


def _split_bf16(x, terms):
    parts = []
    for _ in range(terms):
        p = x.astype(BF16)
        parts.append(p)
        x = x - p.astype(F32)
    return parts


def _chunk_cumsum(x, chunk):
    rows = x.shape[0]
    ri = lax.broadcasted_iota(jnp.int32, (chunk, CUMSUM_TERMS * chunk), 0)
    ci = lax.broadcasted_iota(jnp.int32, (chunk, CUMSUM_TERMS * chunk), 1)
    tri = jnp.where((ci & (chunk - 1)) <= ri, 1.0, 0.0).astype(BF16)
    out = []
    for r0 in range(0, rows, chunk):
        stacked = jnp.concatenate(_split_bf16(x[r0:r0 + chunk, :], CUMSUM_TERMS), axis=0)
        out.append(jnp.dot(tri, stacked, preferred_element_type=F32))
    return out[0] if len(out) == 1 else jnp.concatenate(out, axis=0)


def _resident(shape):
    return pl.BlockSpec(shape, lambda *_: (0,) * len(shape), pipeline_mode=pl.Buffered(1))


def _norm_proj_body(x_ref, nw_ref, *refs):
    n = len(refs) // 2
    h = _rms(x_ref[...], nw_ref[...]).astype(BF16)
    for w_ref, o_ref in zip(refs[:n], refs[n:]):
        o_ref[...] = jnp.dot(h, w_ref[...], preferred_element_type=F32).astype(o_ref.dtype)


def _norm_proj(x2d, norm_w, weights, out_dtypes):
    m, d = x2d.shape
    tm = PROJ_ROWS
    return pl.pallas_call(
        _norm_proj_body,
        name="norm_proj",
        grid=(m // tm,),
        in_specs=[pl.BlockSpec((tm, d), lambda i: (i, 0)), _resident((1, d))]
        + [_resident(w.shape) for w in weights],
        out_specs=[pl.BlockSpec((tm, w.shape[1]), lambda i: (i, 0)) for w in weights],
        out_shape=[jax.ShapeDtypeStruct((m, w.shape[1]), dt) for w, dt in zip(weights, out_dtypes)],
        compiler_params=_params("parallel"),
    )(x2d, norm_w.reshape(1, d), *weights)


def _ssd_body(xbc_ref, sm_ref, z_ref, cw_ref, cb_ref, alog_ref, dtb_ref, dskip_ref, nw_ref,
              y_ref, tail_ref, h_ref, xpad_ref, xw_ref, ycat_ref):
    q = SSD_CHUNK
    width = y_ref.shape[-1]
    n_pairs = width // LANES
    gstate = SSD_GROUPS * SSD_STATE
    gwidth = width // SSD_GROUPS

    @pl.when(pl.program_id(1) == 0)
    def _():
        tail_ref[...] = jnp.zeros_like(tail_ref)
        h_ref[...] = jnp.zeros_like(h_ref)

    xbc = xbc_ref[0]
    xpad_ref[0:SUBLANES, :] = tail_ref[...]
    xpad_ref[SUBLANES:SUBLANES + q, :] = xbc
    tail_ref[...] = xbc[q - SUBLANES:, :]
    conv = cb_ref[...] + cw_ref[SSD_CONV - 1:SSD_CONV, :] * xbc
    for k in range(SSD_CONV - 1):
        off = SUBLANES - (SSD_CONV - 1) + k
        conv = conv + cw_ref[k:k + 1, :] * xpad_ref[off:off + q, :]
    act = _silu(conv)
    xs = act[:, :width]
    bm = act[:, width:width + gstate]
    cm = act[:, width + gstate:]

    dt = _softplus(sm_ref[0] + dtb_ref[...])
    dta = dt * (-jnp.exp(alog_ref[...]))
    ri = lax.broadcasted_iota(jnp.int32, (q, q), 0)
    ci = lax.broadcasted_iota(jnp.int32, (q, q), 1)
    causal = ri >= ci
    a_cs = _chunk_cumsum(dta, q)
    a_cs_t = a_cs.T
    a_last = a_cs[q - 1:q, :]
    bm_t = bm.T

    lane = lax.broadcasted_iota(jnp.int32, (q, LANES), 1)
    lo = lane < HALF
    lo_row = lo[0:1, :]
    cm_g = [jnp.where(lo, cm, 0.0), jnp.where(lo, 0.0, cm)]
    cb = [_dot_nt(c, bm) for c in cm_g]
    h_prev = h_ref[...]
    y_off = [_dot(c, h_prev) for c in cm_g]

    a_last_pairs = []
    for p in range(n_pairs):
        e0, e1 = 2 * p, 2 * p + 1
        g = (p * LANES) // gwidth
        col = slice(p * LANES, (p + 1) * LANES)
        gcol = slice(p * LANES - g * gwidth, (p + 1) * LANES - g * gwidth)
        acs_pair = jnp.where(lo, a_cs[:, e0:e0 + 1], a_cs[:, e1:e1 + 1])
        dt_pair = jnp.where(lo, dt[:, e0:e0 + 1], dt[:, e1:e1 + 1])
        al_pair = jnp.where(lo_row, a_last[:, e0:e0 + 1], a_last[:, e1:e1 + 1])
        a_last_pairs.append(al_pair)
        xs2 = xs[:, col]
        xdt = xs2 * dt_pair
        xdt_b = xdt.astype(BF16)
        yd = []
        for e in (e0, e1):
            seg = a_cs[:, e:e + 1] - a_cs_t[e:e + 1, :]
            decay = jnp.exp(jnp.where(causal, seg, -jnp.inf))
            yd.append(_dot(cb[g] * decay, xdt_b))
        y2 = jnp.where(lo, yd[0], yd[1])
        y2 = y2 + y_off[g][:, gcol] * jnp.exp(acs_pair) + dskip_ref[:, col] * xs2
        ycat_ref[:, col] = y2
        xw_ref[:, col] = (xdt * jnp.exp(al_pair - acs_pair)).astype(BF16)

    for g in range(SSD_GROUPS):
        rows = slice(g * SSD_STATE, (g + 1) * SSD_STATE)
        ppg = n_pairs // SSD_GROUPS
        dec = jnp.exp(jnp.concatenate(a_last_pairs[g * ppg:(g + 1) * ppg], axis=1))
        s_g = _dot(bm_t[rows, :], xw_ref[:, g * gwidth:(g + 1) * gwidth])
        h_ref[rows, :] = h_prev[rows, :] * dec + s_g

    y_ref[0] = _rms(ycat_ref[...] * _silu(z_ref[0]), nw_ref[...]).astype(y_ref.dtype)


def _ssd(xbc, small, z, conv_w, conv_b, a_log, dt_bias, d_skip, norm_w):
    b, l, cdim = xbc.shape
    width = z.shape[-1]
    q = SSD_CHUNK
    n_heads = a_log.shape[0]
    pad = lambda v: jnp.pad(v, (0, LANES - n_heads)).reshape(1, LANES)
    gwidth = width // SSD_GROUPS
    blk = lambda w: pl.BlockSpec((1, q, w), lambda i, c: (i, c, 0))
    return pl.pallas_call(
        _ssd_body,
        name="ssd",
        grid=(b, l // q),
        in_specs=[blk(cdim), blk(LANES), blk(width),
                  _resident((SSD_CONV, cdim)), _resident((1, cdim)), _resident((1, LANES)),
                  _resident((1, LANES)), _resident((1, width)), _resident((1, width))],
        out_specs=blk(width),
        out_shape=jax.ShapeDtypeStruct((b, l, width), BF16),
        scratch_shapes=[pltpu.VMEM((SUBLANES, cdim), F32),
                        pltpu.VMEM((SSD_GROUPS * SSD_STATE, gwidth), F32),
                        pltpu.VMEM((q + SUBLANES, cdim), F32),
                        pltpu.VMEM((q, width), BF16),
                        pltpu.VMEM((q, width), F32)],
        compiler_params=_params("parallel", "arbitrary"),
    )(xbc, small, z, conv_w, conv_b.reshape(1, cdim), pad(a_log), pad(dt_bias),
      jnp.repeat(d_skip, SSD_HEAD_DIM).reshape(1, width), norm_w.reshape(1, width))


def _gla_body(q_ref, k_ref, v_ref, sm_ref, g_ref, w2_ref, gb_ref, nw_ref, o_ref, st_ref, *, mid_lane):
    c = GLA_CHUNK
    kw = q_ref.shape[-1]
    dk = kw // GLA_HEADS
    dv = v_ref.shape[-1] // GLA_HEADS
    n_pairs = kw // LANES

    @pl.when(pl.program_id(1) == 0)
    def _():
        st_ref[...] = jnp.zeros_like(st_ref)

    ri = lax.broadcasted_iota(jnp.int32, (c, c), 0)
    ci = lax.broadcasted_iota(jnp.int32, (c, c), 1)
    causal = ri >= ci
    lo = lax.broadcasted_iota(jnp.int32, (c, LANES), 1) < HALF
    lo_sq = lax.broadcasted_iota(jnp.int32, (dv, LANES), 1) < HALF

    sm_hi, sm_mid = _split_bf16(sm_ref[0], 2)
    sm_lane = lax.broadcasted_iota(jnp.int32, sm_hi.shape, 1)
    pre = jnp.dot(jnp.where(sm_lane < mid_lane, sm_hi, sm_mid), w2_ref[...],
                  preferred_element_type=F32) + gb_ref[...]
    gcs_all = _chunk_cumsum(_log_sigmoid(pre) * (1.0 / GLA_GATE_NORMALIZER), c)

    chunks = [slice(ch * c, (ch + 1) * c) for ch in range(q_ref.shape[1] // c)]
    pairs = [slice(p * LANES, (p + 1) * LANES) for p in range(n_pairs)]
    causal2 = jnp.concatenate([causal, causal], axis=0)

    def by_head(x):
        return jnp.concatenate([jnp.where(lo, x, 0.0), jnp.where(lo, 0.0, x)], axis=0).astype(BF16)

    q_in, k_in, q_st, k_st, g_last = [], [], [], [], []
    for rows in chunks:
        gcs = gcs_all[rows, :]
        g_mid = gcs[c // 2:c // 2 + 1, :]
        g_last.append(gcs[c - 1:c, :])
        qs = q_ref[0, rows, :] * (dk ** -0.5)
        ks = k_ref[0, rows, :]
        q_in.append(qs * jnp.exp(gcs - g_mid))
        k_in.append((ks * jnp.exp(g_mid - gcs)).astype(BF16))
        q_st.append(qs * jnp.exp(gcs))
        k_st.append((ks * jnp.exp(g_last[-1] - gcs)).astype(BF16))

    scores = [[_dot_nt(by_head(q_in[ch][:, col]), k_in[ch][:, col]) for col in pairs]
              for ch in range(len(chunks))]
    scores = [[jnp.where(causal2, s, 0.0).astype(BF16) for s in row] for row in scores]

    o_intra, kv_t = [], []
    for ch, rows in enumerate(chunks):
        o_row, kv_row = [], []
        for p, col in enumerate(pairs):
            v_a = v_ref[0, rows, (2 * p) * dv:(2 * p + 1) * dv].astype(BF16)
            v_b = v_ref[0, rows, (2 * p + 1) * dv:(2 * p + 2) * dv].astype(BF16)
            o_row.append((jnp.dot(scores[ch][p][0:c, :], v_a, preferred_element_type=F32),
                          jnp.dot(scores[ch][p][c:2 * c, :], v_b, preferred_element_type=F32)))
            kv_row.append(jnp.where(lo_sq, _dot_tn(v_a, k_st[ch][:, col]), _dot_tn(v_b, k_st[ch][:, col])))
        o_intra.append(o_row)
        kv_t.append(kv_row)

    o_inter = [[None] * n_pairs for _ in chunks]
    for p, col in enumerate(pairs):
        st = st_ref[:, col]
        for ch in range(len(chunks)):
            o_inter[ch][p] = _dot_nt(by_head(q_st[ch][:, col]), st)
            st = st * jnp.exp(g_last[ch][:, col]) + kv_t[ch][p]
        st_ref[:, col] = st

    for ch, rows in enumerate(chunks):
        for p in range(n_pairs):
            for half in range(2):
                vcol = slice((2 * p + half) * dv, (2 * p + half + 1) * dv)
                o = o_intra[ch][p][half] + o_inter[ch][p][half * c:(half + 1) * c, :]
                o_ref[0, rows, vcol] = (_rms(o, nw_ref[...]) * _silu(g_ref[0, rows, vcol])).astype(o_ref.dtype)


def _gla(q, k, v, small, g, gate_w2, gate_b, norm_w, lr_lane):
    b, l, kw = q.shape
    vw = v.shape[-1]
    rank = gate_w2.shape[0]
    t = GLA_ROWS
    w2_hi, w2_mid = _split_bf16(gate_w2, 2)
    w2_pad = jnp.zeros((LANES, kw), BF16)
    for copy, part in enumerate((w2_hi, w2_mid, w2_hi)):
        w2_pad = w2_pad.at[lr_lane + copy * rank:lr_lane + (copy + 1) * rank, :].set(part)
    blk = lambda w: pl.BlockSpec((1, t, w), lambda i, c: (i, c, 0))
    return pl.pallas_call(
        functools.partial(_gla_body, mid_lane=lr_lane + 2 * rank),
        name="gla",
        grid=(b, l // t),
        in_specs=[blk(kw), blk(kw), blk(vw), blk(LANES), blk(vw),
                  _resident((LANES, kw)), _resident((1, kw)), _resident((1, vw // GLA_HEADS))],
        out_specs=blk(vw),
        out_shape=jax.ShapeDtypeStruct((b, l, vw), BF16),
        scratch_shapes=[pltpu.VMEM((vw // GLA_HEADS, kw), F32)],
        compiler_params=_params("parallel", "arbitrary"),
    )(q, k, v, small, g, w2_pad, gate_b.reshape(1, kw), norm_w.reshape(1, -1))


def _out_proj_body(*refs, n_in, final_norm):
    a_refs, x_ref = refs[:n_in], refs[n_in]
    w_refs = refs[n_in + 1:2 * n_in + 1]
    rest = refs[2 * n_in + 1:]
    acc = x_ref[...]
    for a_ref, w_ref in zip(a_refs, w_refs):
        acc = acc + jnp.dot(a_ref[...].astype(BF16), w_ref[...], preferred_element_type=F32)
    if final_norm:
        fw_ref, o_ref = rest
        o_ref[...] = _rms(acc, fw_ref[...])
    else:
        (o_ref,) = rest
        o_ref[...] = acc


def _out_proj(acts, x2d, weights, final_w=None):
    m, d = x2d.shape
    tm = PROJ_ROWS
    n_in = len(acts)
    row = lambda w: pl.BlockSpec((tm, w), lambda i: (i, 0))
    in_specs = [row(a.shape[1]) for a in acts] + [row(d)] + [_resident(w.shape) for w in weights]
    args = list(acts) + [x2d] + list(weights)
    if final_w is not None:
        in_specs.append(_resident((1, d)))
        args.append(final_w.reshape(1, d))
    return pl.pallas_call(
        functools.partial(_out_proj_body, n_in=n_in, final_norm=final_w is not None),
        name="out_proj",
        grid=(m // tm,),
        in_specs=in_specs,
        out_specs=row(d),
        out_shape=jax.ShapeDtypeStruct((m, d), F32),
        compiler_params=_params("parallel"),
    )(*args)


class _MobaQueryBlock:
    def __init__(self, n_past, q_ref, z_ref, o_ref, kb_ref, vt_ref, kmean_ref, s_ref):
        blk = MOBA_BLOCK
        self.n_past, self.z_ref, self.o_ref, self.kb_ref, self.vt_ref = n_past, z_ref, o_ref, kb_ref, vt_ref
        self.own = slice(n_past * blk, (n_past + 1) * blk)
        self.s_ref = s_ref.at[n_past % s_ref.shape[0]]
        q2 = q_ref[0, self.own, :]
        lo = lax.broadcasted_iota(jnp.int32, (blk, LANES), 1) < HALF
        q_cat = jnp.concatenate([jnp.where(lo, q2, 0.0), jnp.where(lo, 0.0, q2)], axis=0)
        scale = MOBA_HEAD_DIM ** -0.5
        key_i = lax.broadcasted_iota(jnp.int32, (blk, 2 * blk), 0)
        qry_i = lax.broadcasted_iota(jnp.int32, (blk, 2 * blk), 1)
        causal_t = key_i <= jnp.where(qry_i < blk, qry_i, qry_i - blk)

        self.masks = []
        if n_past:
            gate = lax.dot_general(kmean_ref[...], q_cat * scale, (((1,), (1,)), ((), ())),
                                   preferred_element_type=F32, precision=HIGHEST)
            g_rows = [gate[n:n + 1, :] for n in range(n_past)]
            for n in range(n_past):
                rank = jnp.zeros((1, 2 * blk), F32)
                for m in range(n_past):
                    if m != n:
                        ahead = (g_rows[m] >= g_rows[n]) if m < n else (g_rows[m] > g_rows[n])
                        rank = rank + jnp.where(ahead, 1.0, 0.0)
                self.masks.append(rank < MOBA_TOPK)
        self.masks.append(causal_t)
        self.q_s = (q_cat * (scale * LOG2_E)).astype(BF16)
        self.m_run = None
        self.acc = None

    def logits_step(self, j):
        rows = slice(j * MOBA_BLOCK, (j + 1) * MOBA_BLOCK)
        sj = jnp.where(self.masks[j], _dot_nt(self.kb_ref[rows, :], self.q_s), -jnp.inf)
        self.s_ref[rows, :] = sj
        mj = jnp.max(sj, axis=0, keepdims=True)
        self.m_run = mj if self.m_run is None else jnp.maximum(self.m_run, mj)

    def value_step(self, j):
        rows = slice(j * MOBA_BLOCK, (j + 1) * MOBA_BLOCK)
        pj = jnp.exp2(self.s_ref[rows, :] - self.m_run).astype(BF16)
        d = jnp.dot(self.vt_ref[:, rows], pj, preferred_element_type=F32)
        self.acc = d if self.acc is None else d + self.acc

    def finish(self):
        blk = MOBA_BLOCK
        acc = self.acc[0:LANES, :] / self.acc[LANES:LANES + 1, :]
        o_t = jnp.concatenate([acc[0:HALF, 0:blk], acc[HALF:LANES, blk:2 * blk]], axis=0)
        self.o_ref[0, self.own, :] = (o_t.T * _silu(self.z_ref[0, self.own, :])).astype(self.o_ref.dtype)


def _moba_body(q_ref, k_ref, v_ref, z_ref, o_ref, vt_ref, kmean_ref, s_ref):
    blk = MOBA_BLOCK
    nb = k_ref.shape[1] // blk
    for n in range(nb):
        rows = slice(n * blk, (n + 1) * blk)
        kmean_ref[n:n + 1, :] = jnp.mean(k_ref[0, rows, :].astype(F32), axis=0, keepdims=True)
        vt_ref[0:LANES, rows] = v_ref[0, rows, :].astype(F32).T.astype(BF16)
    vt_ref[LANES:, :] = jnp.ones((vt_ref.shape[0] - LANES, vt_ref.shape[1]), BF16)
    make = functools.partial(_MobaQueryBlock, q_ref=q_ref, z_ref=z_ref, o_ref=o_ref, kb_ref=k_ref.at[0],
                             vt_ref=vt_ref, kmean_ref=kmean_ref, s_ref=s_ref)
    cur = make(0)
    cur.logits_step(0)
    for i in range(nb):
        nxt = make(i + 1) if i + 1 < nb else None
        for j in range(i + 2):
            if nxt is not None:
                nxt.logits_step(j)
            if j <= i:
                cur.value_step(j)
        cur.finish()
        cur = nxt


def _moba(q, k, v, z):
    b, l, w = q.shape
    blk = MOBA_BLOCK
    nb = l // blk
    spec = pl.BlockSpec((1, l, LANES), lambda bi, hp: (bi, 0, hp))
    return pl.pallas_call(
        _moba_body,
        name="moba",
        grid=(b, w // LANES),
        in_specs=[spec, spec, spec, spec],
        out_specs=spec,
        out_shape=jax.ShapeDtypeStruct((b, l, w), BF16),
        scratch_shapes=[pltpu.VMEM((LANES + BF16_SUBLANES, l), BF16),
                        pltpu.VMEM((nb, LANES), F32),
                        pltpu.VMEM((2, l, 2 * blk), F32)],
        compiler_params=_params("parallel", "parallel"),
    )(q, k, v, z)


def _even_layer(x, norm_w, w_in, conv_w, conv_b, a_log, dt_bias, d_skip, ssd_norm_w, gate_w2, gate_b,
                gla_norm_w, w_out):
    b, l, d = x.shape
    width = ssd_norm_w.shape[0]
    cdim = conv_w.shape[1]
    n_heads = a_log.shape[0]
    rank, kw = gate_w2.shape
    vw = gla_norm_w.shape[0] * GLA_HEADS
    cuts = [0]
    for s in (width, cdim, n_heads, kw, kw, vw, rank, vw):
        cuts.append(cuts[-1] + s)
    seg = lambda j: w_in[:, cuts[j]:cuts[j + 1]]
    w_small = jnp.concatenate([seg(2)] + [seg(6)] * GATE_COPIES
                              + [jnp.zeros((d, LANES - n_heads - GATE_COPIES * rank), w_in.dtype)], axis=1)
    weights = [seg(0), seg(1), w_small, seg(3), seg(4), seg(5), seg(7)]
    x2d = x.reshape(b * l, d)
    z_a, xbc, small, q_b, k_b, v_b, g_b = [
        u.reshape(b, l, -1) for u in _norm_proj(x2d, norm_w, [w.astype(BF16) for w in weights],
                                                [F32, F32, F32, F32, F32, BF16, F32])]
    y_a = _ssd(xbc, small, z_a, conv_w, conv_b, a_log, dt_bias, d_skip, ssd_norm_w)
    o_b = _gla(q_b, k_b, v_b, small, g_b, gate_w2, gate_b, gla_norm_w, lr_lane=n_heads)
    return [y_a.reshape(b * l, width), o_b.reshape(b * l, vw)], [w_out[:width].astype(BF16), w_out[width:].astype(BF16)]


def _odd_layer(x, norm_w, w_in, w_out):
    b, l, d = x.shape
    w = w_out.shape[0]
    weights = [w_in[:, j * w:(j + 1) * w].astype(BF16) for j in range(4)]
    q, k, v, z = [u.reshape(b, l, w) for u in _norm_proj(x.reshape(b * l, d), norm_w, weights,
                                                         [F32, BF16, BF16, F32])]
    o = _moba(q, k, v, z)
    return [o.reshape(b * l, w)], [w_out.astype(BF16)]


def kernel(x, even_norm, even_w_in, even_conv_w, even_conv_b, even_a_log, even_dt_bias, even_d_skip, even_ssd_norm,
           even_gate_w2, even_gate_b, even_gla_norm, even_w_out, odd_norm, odd_w_in, odd_w_out, final_norm):
    b, l, d = x.shape
    depth = even_norm.shape[0] + odd_norm.shape[0]
    for layer in range(depth):
        i = layer // 2
        if layer % 2 == 0:
            acts, weights = _even_layer(x, even_norm[i], even_w_in[i], even_conv_w[i], even_conv_b[i], even_a_log[i],
                                        even_dt_bias[i], even_d_skip[i], even_ssd_norm[i], even_gate_w2[i],
                                        even_gate_b[i], even_gla_norm[i], even_w_out[i])
        else:
            acts, weights = _odd_layer(x, odd_norm[i], odd_w_in[i], odd_w_out[i])
        last = layer == depth - 1
        x = _out_proj(acts, x.reshape(b * l, d), weights, final_norm if last else None).reshape(b, l, d)
    return x
```

```python
import functools

import jax
import jax.numpy as jnp
from jax import lax
from jax.experimental import pallas as pl
from jax.experimental.pallas import tpu as pltpu

F32 = jnp.float32
BF16 = jnp.bfloat16
HIGHEST = lax.Precision.HIGHEST

LANES = 128
SUBLANES = 8
BF16_SUBLANES = 16
VMEM_LIMIT_BYTES = 56 * 1024 * 1024

RMS_EPS = 1e-6
SSD_HEAD_DIM = 64
SSD_GROUPS = 2
SSD_STATE = 64
SSD_CONV = 4
SSD_CHUNK = 128
GLA_HEADS = 8
GLA_GATE_NORMALIZER = 16.0
GLA_CHUNK = 64
MOBA_HEAD_DIM = 64
MOBA_BLOCK = 256
MOBA_TOPK = 3
LOG2_E = 1.4426950408889634
CONV_ROW_STRIDE = 4
CUMSUM_TERMS = 3
GATE_COPIES = 3

IN_PROJ_ROWS = 512
OUT_PROJ_ROWS = 1024
GLA_ROWS = 256
HALF = LANES // 2


def _params(*sem):
    return pltpu.CompilerParams(dimension_semantics=sem, vmem_limit_bytes=VMEM_LIMIT_BYTES)


def _rms(x, w):
    return x * lax.rsqrt(jnp.mean(x * x, axis=-1, keepdims=True) + RMS_EPS) * w


def _silu(x):
    h = 0.5 * x
    return h + h * jnp.tanh(h)


def _softplus(x):
    return jnp.maximum(x, 0.0) + jnp.log1p(jnp.exp(-jnp.abs(x)))


def _log_sigmoid(x):
    return jnp.minimum(x, 0.0) - jnp.log1p(jnp.exp(-jnp.abs(x)))


def _dot(a, b):
    return jnp.dot(a.astype(BF16), b.astype(BF16), preferred_element_type=F32)


def _dot_nt(a, b):
    return lax.dot_general(a.astype(BF16), b.astype(BF16), (((1,), (1,)), ((), ())), preferred_element_type=F32)


def _dot_tn(a, b):
    return lax.dot_general(a.astype(BF16), b.astype(BF16), (((0,), (0,)), ((), ())), preferred_element_type=F32)


def _split_bf16(x, terms):
    parts = []
    for _ in range(terms):
        p = x.astype(BF16)
        parts.append(p)
        x = x - p.astype(F32)
    return parts


def _chunk_cumsum(x, chunk):
    rows = x.shape[0]
    ri = lax.broadcasted_iota(jnp.int32, (chunk, CUMSUM_TERMS * chunk), 0)
    ci = lax.broadcasted_iota(jnp.int32, (chunk, CUMSUM_TERMS * chunk), 1)
    tri = jnp.where((ci & (chunk - 1)) <= ri, 1.0, 0.0).astype(BF16)
    out = []
    for r0 in range(0, rows, chunk):
        stacked = jnp.concatenate(_split_bf16(x[r0:r0 + chunk, :], CUMSUM_TERMS), axis=0)
        out.append(jnp.dot(tri, stacked, preferred_element_type=F32))
    return out[0] if len(out) == 1 else jnp.concatenate(out, axis=0)


def _resident(shape):
    return pl.BlockSpec(shape, lambda *_: (0,) * len(shape), pipeline_mode=pl.Buffered(1))


def _norm_proj_body(x_ref, nw_ref, *refs):
    n = len(refs) // 2
    h = _rms(x_ref[...], nw_ref[...]).astype(BF16)
    for w_ref, o_ref in zip(refs[:n], refs[n:]):
        o_ref[...] = jnp.dot(h, w_ref[...], preferred_element_type=F32).astype(o_ref.dtype)


def _norm_proj(x2d, norm_w, weights, out_dtypes):
    m, d = x2d.shape
    tm = IN_PROJ_ROWS
    return pl.pallas_call(
        _norm_proj_body,
        name="norm_proj",
        grid=(m // tm,),
        in_specs=[pl.BlockSpec((tm, d), lambda i: (i, 0)), _resident((1, d))]
        + [_resident(w.shape) for w in weights],
        out_specs=[pl.BlockSpec((tm, w.shape[1]), lambda i: (i, 0)) for w in weights],
        out_shape=[jax.ShapeDtypeStruct((m, w.shape[1]), dt) for w, dt in zip(weights, out_dtypes)],
        compiler_params=_params("parallel"),
    )(x2d, norm_w.reshape(1, d), *weights)


def _ssd_body(xbc_ref, sm_ref, z_ref, cw_ref, cb_ref, alog_ref, dtb_ref, dskip_ref, nw_ref,
              y_ref, h_ref, xpad_ref, act_ref, xw_ref, ycat_ref):
    q = SSD_CHUNK
    width = y_ref.shape[-1]
    n_pairs = width // LANES
    n_slabs = xpad_ref.shape[0]
    gstate = SSD_GROUPS * SSD_STATE
    gwidth = width // SSD_GROUPS
    first = pl.program_id(1) == 0

    @pl.when(first)
    def _():
        xpad_ref[:, 0:SUBLANES, :] = jnp.zeros((n_slabs, SUBLANES, LANES), F32)
        h_ref[...] = jnp.zeros_like(h_ref)

    @pl.when(jnp.logical_not(first))
    def _():
        xpad_ref[:, 0:SUBLANES, :] = xpad_ref[:, q:q + SUBLANES, :]

    for c in range(n_slabs):
        lanes = slice(c * LANES, (c + 1) * LANES)
        xpad_ref[c, SUBLANES:SUBLANES + q, :] = xbc_ref[0, :, lanes]
        taps = [jnp.broadcast_to(cw_ref[k:k + 1, lanes], (SUBLANES, LANES)) for k in range(SSD_CONV)]
        bias = jnp.broadcast_to(cb_ref[:, lanes], (SUBLANES, LANES))
        for t0 in range(0, q, SUBLANES * CONV_ROW_STRIDE):
            for g in range(CONV_ROW_STRIDE):
                conv = bias
                for k in range(SSD_CONV):
                    start = SUBLANES + t0 + g - (SSD_CONV - 1 - k)
                    conv = conv + taps[k] * xpad_ref[c, pl.ds(start, SUBLANES, stride=CONV_ROW_STRIDE), :]
                act_ref[c, pl.ds(t0 + g, SUBLANES, stride=CONV_ROW_STRIDE), :] = _silu(conv)
    bm = act_ref[n_pairs]
    cm = act_ref[n_pairs + 1]

    dt = _softplus(sm_ref[0] + dtb_ref[...])
    dta = dt * (-jnp.exp(alog_ref[...]) * LOG2_E)
    ri = lax.broadcasted_iota(jnp.int32, (q, q), 0)
    ci = lax.broadcasted_iota(jnp.int32, (q, q), 1)
    causal = ri >= ci
    a_cs = _chunk_cumsum(dta, q)
    a_cs_t = a_cs.T
    a_last = a_cs[q - 1:q, :]
    bm_t = bm.T

    lane = lax.broadcasted_iota(jnp.int32, (q, LANES), 1)
    lo = lane < HALF
    lo_row = lo[0:1, :]
    cm_g = [jnp.where(lo, cm, 0.0), jnp.where(lo, 0.0, cm)]
    cb = [_dot_nt(c, bm) for c in cm_g]
    h_prev = h_ref[...]
    y_off = [_dot(c, h_prev) for c in cm_g]

    a_last_pairs = []
    for p in range(n_pairs):
        e0, e1 = 2 * p, 2 * p + 1
        g = (p * LANES) // gwidth
        col = slice(p * LANES, (p + 1) * LANES)
        gcol = slice(p * LANES - g * gwidth, (p + 1) * LANES - g * gwidth)
        acs_pair = jnp.where(lo, a_cs[:, e0:e0 + 1], a_cs[:, e1:e1 + 1])
        dt_pair = jnp.where(lo, dt[:, e0:e0 + 1], dt[:, e1:e1 + 1])
        al_pair = jnp.where(lo_row, a_last[:, e0:e0 + 1], a_last[:, e1:e1 + 1])
        a_last_pairs.append(al_pair)
        xs2 = act_ref[p]
        xdt = xs2 * dt_pair
        xdt_b = xdt.astype(BF16)
        yd = []
        for e in (e0, e1):
            seg = a_cs[:, e:e + 1] - a_cs_t[e:e + 1, :]
            decay = jnp.exp2(jnp.where(causal, seg, -jnp.inf))
            yd.append(_dot(cb[g] * decay, xdt_b))
        y2 = jnp.where(lo, yd[0], yd[1])
        y2 = y2 + y_off[g][:, gcol] * jnp.exp2(acs_pair) + dskip_ref[:, col] * xs2
        ycat_ref[:, col] = y2
        xw_ref[:, col] = (xdt * jnp.exp2(al_pair - acs_pair)).astype(BF16)

    for g in range(SSD_GROUPS):
        rows = slice(g * SSD_STATE, (g + 1) * SSD_STATE)
        ppg = n_pairs // SSD_GROUPS
        dec = jnp.exp2(jnp.concatenate(a_last_pairs[g * ppg:(g + 1) * ppg], axis=1))
        s_g = _dot(bm_t[rows, :], xw_ref[:, g * gwidth:(g + 1) * gwidth])
        h_ref[rows, :] = h_prev[rows, :] * dec + s_g

    y_ref[0] = _rms(ycat_ref[...] * _silu(z_ref[0]), nw_ref[...]).astype(y_ref.dtype)


def _ssd(xbc, small, z, conv_w, conv_b, a_log, dt_bias, d_skip, norm_w):
    b, l, cdim = xbc.shape
    width = z.shape[-1]
    q = SSD_CHUNK
    n_heads = a_log.shape[0]
    pad = lambda v: jnp.pad(v, (0, LANES - n_heads)).reshape(1, LANES)
    gwidth = width // SSD_GROUPS
    blk = lambda w: pl.BlockSpec((1, q, w), lambda i, c: (i, c, 0))
    return pl.pallas_call(
        _ssd_body,
        name="ssd",
        grid=(b, l // q),
        in_specs=[blk(cdim), blk(LANES), blk(width),
                  _resident((SSD_CONV, cdim)), _resident((1, cdim)), _resident((1, LANES)),
                  _resident((1, LANES)), _resident((1, width)), _resident((1, width))],
        out_specs=blk(width),
        out_shape=jax.ShapeDtypeStruct((b, l, width), BF16),
        scratch_shapes=[pltpu.VMEM((SSD_GROUPS * SSD_STATE, gwidth), F32),
                        pltpu.VMEM((cdim // LANES, q + SUBLANES, LANES), F32),
                        pltpu.VMEM((cdim // LANES, q, LANES), F32),
                        pltpu.VMEM((q, width), BF16),
                        pltpu.VMEM((q, width), F32)],
        compiler_params=_params("parallel", "arbitrary"),
    )(xbc, small, z, conv_w, conv_b.reshape(1, cdim), pad(a_log), pad(dt_bias),
      jnp.repeat(d_skip, SSD_HEAD_DIM).reshape(1, width), norm_w.reshape(1, width))


def _gla_body(q_ref, k_ref, v_ref, sm_ref, g_ref, w2_ref, gb_ref, nw_ref, o_ref, st_ref, *, mid_lane):
    c = GLA_CHUNK
    kw = q_ref.shape[-1]
    dk = kw // GLA_HEADS
    dv = v_ref.shape[-1] // GLA_HEADS
    n_pairs = kw // LANES

    @pl.when(pl.program_id(1) == 0)
    def _():
        st_ref[...] = jnp.zeros_like(st_ref)

    ri = lax.broadcasted_iota(jnp.int32, (c, c), 0)
    ci = lax.broadcasted_iota(jnp.int32, (c, c), 1)
    causal = ri >= ci
    lo = lax.broadcasted_iota(jnp.int32, (c, LANES), 1) < HALF
    lo_sq = lax.broadcasted_iota(jnp.int32, (dv, LANES), 1) < HALF

    sm_hi, sm_mid = _split_bf16(sm_ref[0], 2)
    sm_lane = lax.broadcasted_iota(jnp.int32, sm_hi.shape, 1)
    pre = jnp.dot(jnp.where(sm_lane < mid_lane, sm_hi, sm_mid), w2_ref[...],
                  preferred_element_type=F32) + gb_ref[...]
    gcs_all = _chunk_cumsum(_log_sigmoid(pre) * (LOG2_E / GLA_GATE_NORMALIZER), c)

    chunks = [slice(ch * c, (ch + 1) * c) for ch in range(q_ref.shape[1] // c)]
    pairs = [slice(p * LANES, (p + 1) * LANES) for p in range(n_pairs)]
    causal2 = jnp.concatenate([causal, causal], axis=0)

    def by_head(x):
        return jnp.concatenate([jnp.where(lo, x, 0.0), jnp.where(lo, 0.0, x)], axis=0).astype(BF16)

    q_in, k_in, q_st, k_st, g_last = [], [], [], [], []
    for rows in chunks:
        gcs = gcs_all[rows, :]
        g_mid = gcs[c // 2:c // 2 + 1, :]
        g_last.append(gcs[c - 1:c, :])
        qs = q_ref[0, rows, :] * (dk ** -0.5)
        ks = k_ref[0, rows, :]
        q_in.append(qs * jnp.exp2(gcs - g_mid))
        k_in.append((ks * jnp.exp2(g_mid - gcs)).astype(BF16))
        q_st.append(qs * jnp.exp2(gcs))
        k_st.append((ks * jnp.exp2(g_last[-1] - gcs)).astype(BF16))

    scores = [[_dot_nt(by_head(q_in[ch][:, col]), k_in[ch][:, col]) for col in pairs]
              for ch in range(len(chunks))]
    scores = [[jnp.where(causal2, s, 0.0).astype(BF16) for s in row] for row in scores]

    o_intra, kv_t = [], []
    for ch, rows in enumerate(chunks):
        o_row, kv_row = [], []
        for p, col in enumerate(pairs):
            v_a = v_ref[0, rows, (2 * p) * dv:(2 * p + 1) * dv].astype(BF16)
            v_b = v_ref[0, rows, (2 * p + 1) * dv:(2 * p + 2) * dv].astype(BF16)
            o_row.append((jnp.dot(scores[ch][p][0:c, :], v_a, preferred_element_type=F32),
                          jnp.dot(scores[ch][p][c:2 * c, :], v_b, preferred_element_type=F32)))
            kv_row.append(jnp.where(lo_sq, _dot_tn(v_a, k_st[ch][:, col]), _dot_tn(v_b, k_st[ch][:, col])))
        o_intra.append(o_row)
        kv_t.append(kv_row)

    o_inter = [[None] * n_pairs for _ in chunks]
    for p, col in enumerate(pairs):
        st = st_ref[:, col]
        for ch in range(len(chunks)):
            o_inter[ch][p] = _dot_nt(by_head(q_st[ch][:, col]), st)
            st = st * jnp.exp2(g_last[ch][:, col]) + kv_t[ch][p]
        st_ref[:, col] = st

    for ch, rows in enumerate(chunks):
        for p in range(n_pairs):
            for half in range(2):
                vcol = slice((2 * p + half) * dv, (2 * p + half + 1) * dv)
                o = o_intra[ch][p][half] + o_inter[ch][p][half * c:(half + 1) * c, :]
                o_ref[0, rows, vcol] = (_rms(o, nw_ref[...]) * _silu(g_ref[0, rows, vcol])).astype(o_ref.dtype)


def _gla(q, k, v, small, g, gate_w2, gate_b, norm_w, lr_lane):
    b, l, kw = q.shape
    vw = v.shape[-1]
    rank = gate_w2.shape[0]
    t = GLA_ROWS
    w2_hi, w2_mid = _split_bf16(gate_w2, 2)
    w2_pad = jnp.zeros((LANES, kw), BF16)
    for copy, part in enumerate((w2_hi, w2_mid, w2_hi)):
        w2_pad = w2_pad.at[lr_lane + copy * rank:lr_lane + (copy + 1) * rank, :].set(part)
    blk = lambda w: pl.BlockSpec((1, t, w), lambda i, c: (i, c, 0))
    return pl.pallas_call(
        functools.partial(_gla_body, mid_lane=lr_lane + 2 * rank),
        name="gla",
        grid=(b, l // t),
        in_specs=[blk(kw), blk(kw), blk(vw), blk(LANES), blk(vw),
                  _resident((LANES, kw)), _resident((1, kw)), _resident((1, vw // GLA_HEADS))],
        out_specs=blk(vw),
        out_shape=jax.ShapeDtypeStruct((b, l, vw), BF16),
        scratch_shapes=[pltpu.VMEM((vw // GLA_HEADS, kw), F32)],
        compiler_params=_params("parallel", "arbitrary"),
    )(q, k, v, small, g, w2_pad, gate_b.reshape(1, kw), norm_w.reshape(1, -1))


def _out_proj_body(*refs, n_in, final_norm):
    a_refs, x_ref = refs[:n_in], refs[n_in]
    w_refs = refs[n_in + 1:2 * n_in + 1]
    rest = refs[2 * n_in + 1:]
    acc = x_ref[...]
    for a_ref, w_ref in zip(a_refs, w_refs):
        acc = acc + jnp.dot(a_ref[...].astype(BF16), w_ref[...], preferred_element_type=F32)
    if final_norm:
        fw_ref, o_ref = rest
        o_ref[...] = _rms(acc, fw_ref[...])
    else:
        (o_ref,) = rest
        o_ref[...] = acc


def _out_proj(acts, x2d, weights, final_w=None):
    m, d = x2d.shape
    tm = OUT_PROJ_ROWS
    n_in = len(acts)
    row = lambda w: pl.BlockSpec((tm, w), lambda i: (i, 0))
    in_specs = [row(a.shape[1]) for a in acts] + [row(d)] + [_resident(w.shape) for w in weights]
    args = list(acts) + [x2d] + list(weights)
    if final_w is not None:
        in_specs.append(_resident((1, d)))
        args.append(final_w.reshape(1, d))
    return pl.pallas_call(
        functools.partial(_out_proj_body, n_in=n_in, final_norm=final_w is not None),
        name="out_proj",
        grid=(m // tm,),
        in_specs=in_specs,
        out_specs=row(d),
        out_shape=jax.ShapeDtypeStruct((m, d), F32),
        compiler_params=_params("parallel"),
    )(*args)


class _MobaQueryBlock:
    def __init__(self, n_past, q_ref, z_ref, o_ref, kb_ref, vt_ref, kmean_ref, s_ref):
        blk = MOBA_BLOCK
        self.n_past, self.z_ref, self.o_ref, self.kb_ref, self.vt_ref = n_past, z_ref, o_ref, kb_ref, vt_ref
        self.own = slice(n_past * blk, (n_past + 1) * blk)
        self.s_ref = s_ref.at[n_past % s_ref.shape[0]]
        q2 = q_ref[0, self.own, :]
        lo = lax.broadcasted_iota(jnp.int32, (blk, LANES), 1) < HALF
        q_cat = jnp.concatenate([jnp.where(lo, q2, 0.0), jnp.where(lo, 0.0, q2)], axis=0)
        scale = MOBA_HEAD_DIM ** -0.5
        key_i = lax.broadcasted_iota(jnp.int32, (blk, 2 * blk), 0)
        qry_i = lax.broadcasted_iota(jnp.int32, (blk, 2 * blk), 1)
        causal_t = key_i <= jnp.where(qry_i < blk, qry_i, qry_i - blk)

        self.masks = []
        if n_past:
            gate = lax.dot_general(kmean_ref[...], q_cat * scale, (((1,), (1,)), ((), ())),
                                   preferred_element_type=F32, precision=HIGHEST)
            g_rows = [gate[n:n + 1, :] for n in range(n_past)]
            for n in range(n_past):
                rank = jnp.zeros((1, 2 * blk), F32)
                for m in range(n_past):
                    if m != n:
                        ahead = (g_rows[m] >= g_rows[n]) if m < n else (g_rows[m] > g_rows[n])
                        rank = rank + jnp.where(ahead, 1.0, 0.0)
                self.masks.append(rank < MOBA_TOPK)
        self.masks.append(causal_t)
        self.q_s = (q_cat * (scale * LOG2_E)).astype(BF16)
        self.m_run = None
        self.acc = None

    def logits_step(self, j):
        rows = slice(j * MOBA_BLOCK, (j + 1) * MOBA_BLOCK)
        sj = jnp.where(self.masks[j], _dot_nt(self.kb_ref[rows, :], self.q_s), -jnp.inf)
        self.s_ref[rows, :] = sj
        mj = jnp.max(sj, axis=0, keepdims=True)
        self.m_run = mj if self.m_run is None else jnp.maximum(self.m_run, mj)

    def value_step(self, j):
        rows = slice(j * MOBA_BLOCK, (j + 1) * MOBA_BLOCK)
        pj = jnp.exp2(self.s_ref[rows, :] - self.m_run).astype(BF16)
        d = jnp.dot(self.vt_ref[:, rows], pj, preferred_element_type=F32)
        self.acc = d if self.acc is None else d + self.acc

    def finish(self):
        blk = MOBA_BLOCK
        acc = self.acc[0:LANES, :] / self.acc[LANES:LANES + 1, :]
        o_t = jnp.concatenate([acc[0:HALF, 0:blk], acc[HALF:LANES, blk:2 * blk]], axis=0)
        self.o_ref[0, self.own, :] = (o_t.T * _silu(self.z_ref[0, self.own, :])).astype(self.o_ref.dtype)


def _moba_body(q_ref, k_ref, v_ref, z_ref, o_ref, vt_ref, kmean_ref, s_ref):
    blk = MOBA_BLOCK
    nb = k_ref.shape[1] // blk
    for n in range(nb):
        rows = slice(n * blk, (n + 1) * blk)
        kmean_ref[n:n + 1, :] = jnp.mean(k_ref[0, rows, :].astype(F32), axis=0, keepdims=True)
        vt_ref[0:LANES, rows] = v_ref[0, rows, :].astype(F32).T.astype(BF16)
    vt_ref[LANES:, :] = jnp.ones((vt_ref.shape[0] - LANES, vt_ref.shape[1]), BF16)
    make = functools.partial(_MobaQueryBlock, q_ref=q_ref, z_ref=z_ref, o_ref=o_ref, kb_ref=k_ref.at[0],
                             vt_ref=vt_ref, kmean_ref=kmean_ref, s_ref=s_ref)
    cur = make(0)
    cur.logits_step(0)
    for i in range(nb):
        nxt = make(i + 1) if i + 1 < nb else None
        for j in range(i + 2):
            if nxt is not None:
                nxt.logits_step(j)
            if j <= i:
                cur.value_step(j)
        cur.finish()
        cur = nxt


def _moba(q, k, v, z):
    b, l, w = q.shape
    blk = MOBA_BLOCK
    nb = l // blk
    spec = pl.BlockSpec((1, l, LANES), lambda bi, hp: (bi, 0, hp))
    return pl.pallas_call(
        _moba_body,
        name="moba",
        grid=(b, w // LANES),
        in_specs=[spec, spec, spec, spec],
        out_specs=spec,
        out_shape=jax.ShapeDtypeStruct((b, l, w), BF16),
        scratch_shapes=[pltpu.VMEM((LANES + BF16_SUBLANES, l), BF16),
                        pltpu.VMEM((nb, LANES), F32),
                        pltpu.VMEM((2, l, 2 * blk), F32)],
        compiler_params=_params("parallel", "parallel"),
    )(q, k, v, z)


def _even_layer(x, norm_w, w_in, conv_w, conv_b, a_log, dt_bias, d_skip, ssd_norm_w, gate_w2, gate_b,
                gla_norm_w, w_out):
    b, l, d = x.shape
    width = ssd_norm_w.shape[0]
    cdim = conv_w.shape[1]
    n_heads = a_log.shape[0]
    rank, kw = gate_w2.shape
    vw = gla_norm_w.shape[0] * GLA_HEADS
    cuts = [0]
    for s in (width, cdim, n_heads, kw, kw, vw, rank, vw):
        cuts.append(cuts[-1] + s)
    seg = lambda j: w_in[:, cuts[j]:cuts[j + 1]]
    w_small = jnp.concatenate([seg(2)] + [seg(6)] * GATE_COPIES
                              + [jnp.zeros((d, LANES - n_heads - GATE_COPIES * rank), w_in.dtype)], axis=1)
    weights = [seg(0), seg(1), w_small, seg(3), seg(4), seg(5), seg(7)]
    x2d = x.reshape(b * l, d)
    z_a, xbc, small, q_b, k_b, v_b, g_b = [
        u.reshape(b, l, -1) for u in _norm_proj(x2d, norm_w, [w.astype(BF16) for w in weights],
                                                [F32, F32, F32, F32, F32, BF16, F32])]
    y_a = _ssd(xbc, small, z_a, conv_w, conv_b, a_log, dt_bias, d_skip, ssd_norm_w)
    o_b = _gla(q_b, k_b, v_b, small, g_b, gate_w2, gate_b, gla_norm_w, lr_lane=n_heads)
    return [y_a.reshape(b * l, width), o_b.reshape(b * l, vw)], [w_out[:width].astype(BF16), w_out[width:].astype(BF16)]


def _odd_layer(x, norm_w, w_in, w_out):
    b, l, d = x.shape
    w = w_out.shape[0]
    weights = [w_in[:, j * w:(j + 1) * w].astype(BF16) for j in range(4)]
    q, k, v, z = [u.reshape(b, l, w) for u in _norm_proj(x.reshape(b * l, d), norm_w, weights,
                                                         [F32, BF16, BF16, F32])]
    o = _moba(q, k, v, z)
    return [o.reshape(b * l, w)], [w_out.astype(BF16)]


def kernel(x, even_norm, even_w_in, even_conv_w, even_conv_b, even_a_log, even_dt_bias, even_d_skip, even_ssd_norm,
           even_gate_w2, even_gate_b, even_gla_norm, even_w_out, odd_norm, odd_w_in, odd_w_out, final_norm):
    b, l, d = x.shape
    depth = even_norm.shape[0] + odd_norm.shape[0]
    for layer in range(depth):
        i = layer // 2
        if layer % 2 == 0:
            acts, weights = _even_layer(x, even_norm[i], even_w_in[i], even_conv_w[i], even_conv_b[i], even_a_log[i],
                                        even_dt_bias[i], even_d_skip[i], even_ssd_norm[i], even_gate_w2[i],
                                        even_gate_b[i], even_gla_norm[i], even_w_out[i])
        else:
            acts, weights = _odd_layer(x, odd_norm[i], odd_w_in[i], odd_w_out[i])
        last = layer == depth - 1
        x = _out_proj(acts, x.reshape(b * l, d), weights, final_norm if last else None).reshape(b, l, d)
    return x
```

```python
import functools

import jax
import jax.numpy as jnp
from jax import lax
from jax.experimental import pallas as pl
from jax.experimental.pallas import tpu as pltpu

F32 = jnp.float32
BF16 = jnp.bfloat16
HIGHEST = lax.Precision.HIGHEST

LANES = 128
SUBLANES = 8
BF16_SUBLANES = 16
VMEM_LIMIT_BYTES = 56 * 1024 * 1024

RMS_EPS = 1e-6
SSD_HEAD_DIM = 64
SSD_GROUPS = 2
SSD_STATE = 64
SSD_CONV = 4
SSD_CHUNK = 128
GLA_HEADS = 8
GLA_GATE_NORMALIZER = 16.0
GLA_CHUNK = 64
MOBA_HEAD_DIM = 64
MOBA_BLOCK = 256
MOBA_TOPK = 3
LOG2_E = 1.4426950408889634
CONV_ROW_STRIDE = 4
CUMSUM_TERMS = 3
GATE_COPIES = 3

IN_PROJ_ROWS = 512
OUT_PROJ_ROWS = 1024
GLA_ROWS = 256
HALF = LANES // 2
VT_ROWS = HALF + BF16_SUBLANES


def _params(*sem):
    return pltpu.CompilerParams(dimension_semantics=sem, vmem_limit_bytes=VMEM_LIMIT_BYTES)


def _rms(x, w):
    return x * lax.rsqrt(jnp.mean(x * x, axis=-1, keepdims=True) + RMS_EPS) * w


def _silu(x):
    h = 0.5 * x
    return h + h * jnp.tanh(h)


def _softplus(x):
    return jnp.maximum(x, 0.0) + jnp.log1p(jnp.exp(-jnp.abs(x)))


def _log_sigmoid(x):
    return jnp.minimum(x, 0.0) - jnp.log1p(jnp.exp(-jnp.abs(x)))


def _dot(a, b):
    return jnp.dot(a.astype(BF16), b.astype(BF16), preferred_element_type=F32)


def _dot_nt(a, b):
    return lax.dot_general(a.astype(BF16), b.astype(BF16), (((1,), (1,)), ((), ())), preferred_element_type=F32)


def _dot_tn(a, b):
    return lax.dot_general(a.astype(BF16), b.astype(BF16), (((0,), (0,)), ((), ())), preferred_element_type=F32)


def _split_bf16(x, terms):
    parts = []
    for _ in range(terms):
        p = x.astype(BF16)
        parts.append(p)
        x = x - p.astype(F32)
    return parts


def _chunk_cumsum(x, chunk):
    rows = x.shape[0]
    ri = lax.broadcasted_iota(jnp.int32, (chunk, CUMSUM_TERMS * chunk), 0)
    ci = lax.broadcasted_iota(jnp.int32, (chunk, CUMSUM_TERMS * chunk), 1)
    tri = jnp.where((ci & (chunk - 1)) <= ri, 1.0, 0.0).astype(BF16)
    out = []
    for r0 in range(0, rows, chunk):
        stacked = jnp.concatenate(_split_bf16(x[r0:r0 + chunk, :], CUMSUM_TERMS), axis=0)
        out.append(jnp.dot(tri, stacked, preferred_element_type=F32))
    return out[0] if len(out) == 1 else jnp.concatenate(out, axis=0)


def _resident(shape):
    return pl.BlockSpec(shape, lambda *_: (0,) * len(shape), pipeline_mode=pl.Buffered(1))


def _norm_proj_body(x_ref, nw_ref, *refs):
    n = len(refs) // 2
    h = _rms(x_ref[...], nw_ref[...]).astype(BF16)
    for w_ref, o_ref in zip(refs[:n], refs[n:]):
        o_ref[...] = jnp.dot(h, w_ref[...], preferred_element_type=F32).astype(o_ref.dtype)


def _norm_proj(x2d, norm_w, weights, out_dtypes):
    m, d = x2d.shape
    tm = IN_PROJ_ROWS
    return pl.pallas_call(
        _norm_proj_body,
        name="norm_proj",
        grid=(m // tm,),
        in_specs=[pl.BlockSpec((tm, d), lambda i: (i, 0)), _resident((1, d))]
        + [_resident(w.shape) for w in weights],
        out_specs=[pl.BlockSpec((tm, w.shape[1]), lambda i: (i, 0)) for w in weights],
        out_shape=[jax.ShapeDtypeStruct((m, w.shape[1]), dt) for w, dt in zip(weights, out_dtypes)],
        compiler_params=_params("parallel"),
    )(x2d, norm_w.reshape(1, d), *weights)


def _ssd_body(xbc_ref, sm_ref, z_ref, cw_ref, cb_ref, alog_ref, dtb_ref, dskip_ref, nw_ref,
              y_ref, h_ref, xpad_ref, act_ref, xw_ref, ycat_ref):
    q = SSD_CHUNK
    width = y_ref.shape[-1]
    n_pairs = width // LANES
    n_slabs = xpad_ref.shape[0]
    gstate = SSD_GROUPS * SSD_STATE
    gwidth = width // SSD_GROUPS
    first = pl.program_id(1) == 0

    @pl.when(first)
    def _():
        xpad_ref[:, 0:SUBLANES, :] = jnp.zeros((n_slabs, SUBLANES, LANES), F32)
        h_ref[...] = jnp.zeros_like(h_ref)

    @pl.when(jnp.logical_not(first))
    def _():
        xpad_ref[:, 0:SUBLANES, :] = xpad_ref[:, q:q + SUBLANES, :]

    for c in range(n_slabs):
        lanes = slice(c * LANES, (c + 1) * LANES)
        xpad_ref[c, SUBLANES:SUBLANES + q, :] = xbc_ref[0, :, lanes]
        taps = [jnp.broadcast_to(cw_ref[k:k + 1, lanes], (SUBLANES, LANES)) for k in range(SSD_CONV)]
        bias = jnp.broadcast_to(cb_ref[:, lanes], (SUBLANES, LANES))
        for t0 in range(0, q, SUBLANES * CONV_ROW_STRIDE):
            for g in range(CONV_ROW_STRIDE):
                conv = bias
                for k in range(SSD_CONV):
                    start = SUBLANES + t0 + g - (SSD_CONV - 1 - k)
                    conv = conv + taps[k] * xpad_ref[c, pl.ds(start, SUBLANES, stride=CONV_ROW_STRIDE), :]
                act_ref[c, pl.ds(t0 + g, SUBLANES, stride=CONV_ROW_STRIDE), :] = _silu(conv)
    bm = act_ref[n_pairs]
    cm = act_ref[n_pairs + 1]

    dt = _softplus(sm_ref[0] + dtb_ref[...])
    dta = dt * (-jnp.exp(alog_ref[...]) * LOG2_E)
    ri = lax.broadcasted_iota(jnp.int32, (q, q), 0)
    ci = lax.broadcasted_iota(jnp.int32, (q, q), 1)
    causal = ri >= ci
    a_cs = _chunk_cumsum(dta, q)
    a_cs_t = a_cs.T
    a_last = a_cs[q - 1:q, :]
    bm_t = bm.T

    lane = lax.broadcasted_iota(jnp.int32, (q, LANES), 1)
    lo = lane < HALF
    lo_row = lo[0:1, :]
    cm_g = [jnp.where(lo, cm, 0.0), jnp.where(lo, 0.0, cm)]
    cb = [_dot_nt(c, bm) for c in cm_g]
    h_prev = h_ref[...]
    y_off = [_dot(c, h_prev) for c in cm_g]

    a_last_pairs = []
    for p in range(n_pairs):
        e0, e1 = 2 * p, 2 * p + 1
        g = (p * LANES) // gwidth
        col = slice(p * LANES, (p + 1) * LANES)
        gcol = slice(p * LANES - g * gwidth, (p + 1) * LANES - g * gwidth)
        acs_pair = jnp.where(lo, a_cs[:, e0:e0 + 1], a_cs[:, e1:e1 + 1])
        dt_pair = jnp.where(lo, dt[:, e0:e0 + 1], dt[:, e1:e1 + 1])
        al_pair = jnp.where(lo_row, a_last[:, e0:e0 + 1], a_last[:, e1:e1 + 1])
        a_last_pairs.append(al_pair)
        xs2 = act_ref[p]
        xdt = xs2 * dt_pair
        xdt_b = xdt.astype(BF16)
        yd = []
        for e in (e0, e1):
            seg = a_cs[:, e:e + 1] - a_cs_t[e:e + 1, :]
            decay = jnp.exp2(jnp.where(causal, seg, -jnp.inf))
            yd.append(_dot(cb[g] * decay, xdt_b))
        y2 = jnp.where(lo, yd[0], yd[1])
        y2 = y2 + y_off[g][:, gcol] * jnp.exp2(acs_pair) + dskip_ref[:, col] * xs2
        ycat_ref[:, col] = y2
        xw_ref[:, col] = (xdt * jnp.exp2(al_pair - acs_pair)).astype(BF16)

    for g in range(SSD_GROUPS):
        rows = slice(g * SSD_STATE, (g + 1) * SSD_STATE)
        ppg = n_pairs // SSD_GROUPS
        dec = jnp.exp2(jnp.concatenate(a_last_pairs[g * ppg:(g + 1) * ppg], axis=1))
        s_g = _dot(bm_t[rows, :], xw_ref[:, g * gwidth:(g + 1) * gwidth])
        h_ref[rows, :] = h_prev[rows, :] * dec + s_g

    y_ref[0] = _rms(ycat_ref[...] * _silu(z_ref[0]), nw_ref[...]).astype(y_ref.dtype)


def _ssd(xbc, small, z, conv_w, conv_b, a_log, dt_bias, d_skip, norm_w):
    b, l, cdim = xbc.shape
    width = z.shape[-1]
    q = SSD_CHUNK
    n_heads = a_log.shape[0]
    pad = lambda v: jnp.pad(v, (0, LANES - n_heads)).reshape(1, LANES)
    gwidth = width // SSD_GROUPS
    blk = lambda w: pl.BlockSpec((1, q, w), lambda i, c: (i, c, 0))
    return pl.pallas_call(
        _ssd_body,
        name="ssd",
        grid=(b, l // q),
        in_specs=[blk(cdim), blk(LANES), blk(width),
                  _resident((SSD_CONV, cdim)), _resident((1, cdim)), _resident((1, LANES)),
                  _resident((1, LANES)), _resident((1, width)), _resident((1, width))],
        out_specs=blk(width),
        out_shape=jax.ShapeDtypeStruct((b, l, width), BF16),
        scratch_shapes=[pltpu.VMEM((SSD_GROUPS * SSD_STATE, gwidth), F32),
                        pltpu.VMEM((cdim // LANES, q + SUBLANES, LANES), F32),
                        pltpu.VMEM((cdim // LANES, q, LANES), F32),
                        pltpu.VMEM((q, width), BF16),
                        pltpu.VMEM((q, width), F32)],
        compiler_params=_params("parallel", "arbitrary"),
    )(xbc, small, z, conv_w, conv_b.reshape(1, cdim), pad(a_log), pad(dt_bias),
      jnp.repeat(d_skip, SSD_HEAD_DIM).reshape(1, width), norm_w.reshape(1, width))


def _gla_body(q_ref, k_ref, v_ref, sm_ref, g_ref, w2_ref, gb_ref, nw_ref, o_ref, st_ref, *, mid_lane):
    c = GLA_CHUNK
    kw = q_ref.shape[-1]
    dk = kw // GLA_HEADS
    dv = v_ref.shape[-1] // GLA_HEADS
    n_pairs = kw // LANES

    @pl.when(pl.program_id(1) == 0)
    def _():
        st_ref[...] = jnp.zeros_like(st_ref)

    ri = lax.broadcasted_iota(jnp.int32, (c, c), 0)
    ci = lax.broadcasted_iota(jnp.int32, (c, c), 1)
    causal = ri >= ci
    lo = lax.broadcasted_iota(jnp.int32, (c, LANES), 1) < HALF
    lo_sq = lax.broadcasted_iota(jnp.int32, (dv, LANES), 1) < HALF

    sm_hi, sm_mid = _split_bf16(sm_ref[0], 2)
    sm_lane = lax.broadcasted_iota(jnp.int32, sm_hi.shape, 1)
    pre = jnp.dot(jnp.where(sm_lane < mid_lane, sm_hi, sm_mid), w2_ref[...],
                  preferred_element_type=F32) + gb_ref[...]
    gcs_all = _chunk_cumsum(_log_sigmoid(pre) * (LOG2_E / GLA_GATE_NORMALIZER), c)

    chunks = [slice(ch * c, (ch + 1) * c) for ch in range(q_ref.shape[1] // c)]
    pairs = [slice(p * LANES, (p + 1) * LANES) for p in range(n_pairs)]
    causal2 = jnp.concatenate([causal, causal], axis=0)

    def by_head(x):
        return jnp.concatenate([jnp.where(lo, x, 0.0), jnp.where(lo, 0.0, x)], axis=0).astype(BF16)

    q_in, k_in, q_st, k_st, g_last = [], [], [], [], []
    for rows in chunks:
        gcs = gcs_all[rows, :]
        g_mid = gcs[c // 2:c // 2 + 1, :]
        g_last.append(gcs[c - 1:c, :])
        qs = q_ref[0, rows, :] * (dk ** -0.5)
        ks = k_ref[0, rows, :]
        q_in.append(qs * jnp.exp2(gcs - g_mid))
        k_in.append((ks * jnp.exp2(g_mid - gcs)).astype(BF16))
        q_st.append(qs * jnp.exp2(gcs))
        k_st.append((ks * jnp.exp2(g_last[-1] - gcs)).astype(BF16))

    scores = [[_dot_nt(by_head(q_in[ch][:, col]), k_in[ch][:, col]) for col in pairs]
              for ch in range(len(chunks))]
    scores = [[jnp.where(causal2, s, 0.0).astype(BF16) for s in row] for row in scores]

    o_intra, kv_t = [], []
    for ch, rows in enumerate(chunks):
        o_row, kv_row = [], []
        for p, col in enumerate(pairs):
            v_a = v_ref[0, rows, (2 * p) * dv:(2 * p + 1) * dv].astype(BF16)
            v_b = v_ref[0, rows, (2 * p + 1) * dv:(2 * p + 2) * dv].astype(BF16)
            o_row.append((jnp.dot(scores[ch][p][0:c, :], v_a, preferred_element_type=F32),
                          jnp.dot(scores[ch][p][c:2 * c, :], v_b, preferred_element_type=F32)))
            kv_row.append(jnp.where(lo_sq, _dot_tn(v_a, k_st[ch][:, col]), _dot_tn(v_b, k_st[ch][:, col])))
        o_intra.append(o_row)
        kv_t.append(kv_row)

    o_inter = [[None] * n_pairs for _ in chunks]
    for p, col in enumerate(pairs):
        st = st_ref[:, col]
        for ch in range(len(chunks)):
            o_inter[ch][p] = _dot_nt(by_head(q_st[ch][:, col]), st)
            st = st * jnp.exp2(g_last[ch][:, col]) + kv_t[ch][p]
        st_ref[:, col] = st

    for ch, rows in enumerate(chunks):
        for p in range(n_pairs):
            for half in range(2):
                vcol = slice((2 * p + half) * dv, (2 * p + half + 1) * dv)
                o = o_intra[ch][p][half] + o_inter[ch][p][half * c:(half + 1) * c, :]
                o_ref[0, rows, vcol] = (_rms(o, nw_ref[...]) * _silu(g_ref[0, rows, vcol])).astype(o_ref.dtype)


def _gla(q, k, v, small, g, gate_w2, gate_b, norm_w, lr_lane):
    b, l, kw = q.shape
    vw = v.shape[-1]
    rank = gate_w2.shape[0]
    t = GLA_ROWS
    w2_hi, w2_mid = _split_bf16(gate_w2, 2)
    w2_pad = jnp.zeros((LANES, kw), BF16)
    for copy, part in enumerate((w2_hi, w2_mid, w2_hi)):
        w2_pad = w2_pad.at[lr_lane + copy * rank:lr_lane + (copy + 1) * rank, :].set(part)
    blk = lambda w: pl.BlockSpec((1, t, w), lambda i, c: (i, c, 0))
    return pl.pallas_call(
        functools.partial(_gla_body, mid_lane=lr_lane + 2 * rank),
        name="gla",
        grid=(b, l // t),
        in_specs=[blk(kw), blk(kw), blk(vw), blk(LANES), blk(vw),
                  _resident((LANES, kw)), _resident((1, kw)), _resident((1, vw // GLA_HEADS))],
        out_specs=blk(vw),
        out_shape=jax.ShapeDtypeStruct((b, l, vw), BF16),
        scratch_shapes=[pltpu.VMEM((vw // GLA_HEADS, kw), F32)],
        compiler_params=_params("parallel", "arbitrary"),
    )(q, k, v, small, g, w2_pad, gate_b.reshape(1, kw), norm_w.reshape(1, -1))


def _out_proj_body(*refs, n_in, final_norm):
    a_refs, x_ref = refs[:n_in], refs[n_in]
    w_refs = refs[n_in + 1:2 * n_in + 1]
    rest = refs[2 * n_in + 1:]
    acc = x_ref[...]
    for a_ref, w_ref in zip(a_refs, w_refs):
        acc = acc + jnp.dot(a_ref[...].astype(BF16), w_ref[...], preferred_element_type=F32)
    if final_norm:
        fw_ref, o_ref = rest
        o_ref[...] = _rms(acc, fw_ref[...])
    else:
        (o_ref,) = rest
        o_ref[...] = acc


def _out_proj(acts, x2d, weights, final_w=None):
    m, d = x2d.shape
    tm = OUT_PROJ_ROWS
    n_in = len(acts)
    row = lambda w: pl.BlockSpec((tm, w), lambda i: (i, 0))
    in_specs = [row(a.shape[1]) for a in acts] + [row(d)] + [_resident(w.shape) for w in weights]
    args = list(acts) + [x2d] + list(weights)
    if final_w is not None:
        in_specs.append(_resident((1, d)))
        args.append(final_w.reshape(1, d))
    return pl.pallas_call(
        functools.partial(_out_proj_body, n_in=n_in, final_norm=final_w is not None),
        name="out_proj",
        grid=(m // tm,),
        in_specs=in_specs,
        out_specs=row(d),
        out_shape=jax.ShapeDtypeStruct((m, d), F32),
        compiler_params=_params("parallel"),
    )(*args)


class _MobaQueryBlock:
    def __init__(self, n_past, q_ref, z_ref, o_ref, kb_ref, vt_ref, kmean_ref, s_ref):
        blk = MOBA_BLOCK
        self.n_past, self.z_ref, self.o_ref, self.kb_ref, self.vt_ref = n_past, z_ref, o_ref, kb_ref, vt_ref
        self.own = slice(n_past * blk, (n_past + 1) * blk)
        self.s_ref = s_ref.at[n_past % s_ref.shape[0]]
        q2 = q_ref[0, self.own, :]
        lo = lax.broadcasted_iota(jnp.int32, (blk, LANES), 1) < HALF
        q_cat = jnp.concatenate([jnp.where(lo, q2, 0.0), jnp.where(lo, 0.0, q2)], axis=0)
        scale = MOBA_HEAD_DIM ** -0.5
        key_i = lax.broadcasted_iota(jnp.int32, (blk, 2 * blk), 0)
        qry_i = lax.broadcasted_iota(jnp.int32, (blk, 2 * blk), 1)
        causal_t = key_i <= jnp.where(qry_i < blk, qry_i, qry_i - blk)

        self.masks = []
        if n_past:
            nb = kmean_ref.shape[0] // 2
            parts = [_dot_nt(kmean_ref[...], q_part) for q_part in _split_bf16(q_cat * scale, 2)]
            gate = (parts[0][0:nb] + parts[0][nb:]) + (parts[1][0:nb] + parts[1][nb:])
            g_rows = [gate[n:n + 1, :] for n in range(n_past)]
            for n in range(n_past):
                rank = jnp.zeros((1, 2 * blk), F32)
                for m in range(n_past):
                    if m != n:
                        ahead = (g_rows[m] >= g_rows[n]) if m < n else (g_rows[m] > g_rows[n])
                        rank = rank + jnp.where(ahead, 1.0, 0.0)
                self.masks.append(rank < MOBA_TOPK)
        self.masks.append(causal_t)
        self.q_s = (q_cat * (scale * LOG2_E)).astype(BF16)
        self.m_run = None
        self.p = []

    def logits_step(self, j):
        rows = slice(j * MOBA_BLOCK, (j + 1) * MOBA_BLOCK)
        sj = jnp.where(self.masks[j], _dot_nt(self.kb_ref[rows, :], self.q_s), -jnp.inf)
        self.s_ref[rows, :] = sj
        mj = jnp.max(sj, axis=0, keepdims=True)
        self.m_run = mj if self.m_run is None else jnp.maximum(self.m_run, mj)

    def value_step(self, j):
        rows = slice(j * MOBA_BLOCK, (j + 1) * MOBA_BLOCK)
        self.p.append(jnp.exp2(self.s_ref[rows, :] - self.m_run).astype(BF16))

    def finish(self):
        blk = MOBA_BLOCK
        nk = len(self.p) * blk
        p_all = jnp.concatenate(self.p, axis=0)
        outs = []
        for half in range(2):
            vt = self.vt_ref[half * VT_ROWS:(half + 1) * VT_ROWS, 0:nk]
            acc = jnp.dot(vt, p_all[:, half * blk:(half + 1) * blk], preferred_element_type=F32)
            outs.append(acc[0:HALF, :] / acc[HALF:HALF + 1, :])
        o_t = jnp.concatenate(outs, axis=0)
        self.o_ref[0, self.own, :] = (o_t.T * _silu(self.z_ref[0, self.own, :])).astype(self.o_ref.dtype)


def _moba_body(q_ref, k_ref, v_ref, z_ref, o_ref, vt_ref, kmean_ref, s_ref):
    blk = MOBA_BLOCK
    nb = k_ref.shape[1] // blk
    kmean = []
    for n in range(nb):
        rows = slice(n * blk, (n + 1) * blk)
        kmean.append(jnp.mean(k_ref[0, rows, :].astype(F32), axis=0, keepdims=True))
        v_t = v_ref[0, rows, :].astype(F32).T.astype(BF16)
        for half in range(2):
            vt_ref[half * VT_ROWS:half * VT_ROWS + HALF, rows] = v_t[half * HALF:(half + 1) * HALF, :]
    for half in range(2):
        vt_ref[half * VT_ROWS + HALF:(half + 1) * VT_ROWS, :] = jnp.ones((BF16_SUBLANES, vt_ref.shape[1]), BF16)
    kmean_ref[...] = jnp.concatenate(_split_bf16(jnp.concatenate(kmean, axis=0), 2), axis=0)
    make = functools.partial(_MobaQueryBlock, q_ref=q_ref, z_ref=z_ref, o_ref=o_ref, kb_ref=k_ref.at[0],
                             vt_ref=vt_ref, kmean_ref=kmean_ref, s_ref=s_ref)
    blocks = {0: make(0), 1: make(1)}
    blocks[0].logits_step(0)
    for i in range(nb):
        if i + 2 < nb:
            blocks[i + 2] = make(i + 2)
        cur, nxt = blocks.pop(i), blocks.get(i + 1)
        for j in range(i + 2):
            if j <= i:
                cur.value_step(j)
            if nxt is not None:
                nxt.logits_step(j)
        cur.finish()


def _moba(q, k, v, z):
    b, l, w = q.shape
    blk = MOBA_BLOCK
    nb = l // blk
    spec = pl.BlockSpec((1, l, LANES), lambda bi, hp: (bi, 0, hp))
    return pl.pallas_call(
        _moba_body,
        name="moba",
        grid=(b, w // LANES),
        in_specs=[spec, spec, spec, spec],
        out_specs=spec,
        out_shape=jax.ShapeDtypeStruct((b, l, w), BF16),
        scratch_shapes=[pltpu.VMEM((2 * VT_ROWS, l), BF16),
                        pltpu.VMEM((2 * nb, LANES), BF16),
                        pltpu.VMEM((2, l, 2 * blk), F32)],
        compiler_params=_params("parallel", "parallel"),
    )(q, k, v, z)


def _even_layer(x, norm_w, w_in, conv_w, conv_b, a_log, dt_bias, d_skip, ssd_norm_w, gate_w2, gate_b,
                gla_norm_w, w_out):
    b, l, d = x.shape
    width = ssd_norm_w.shape[0]
    cdim = conv_w.shape[1]
    n_heads = a_log.shape[0]
    rank, kw = gate_w2.shape
    vw = gla_norm_w.shape[0] * GLA_HEADS
    cuts = [0]
    for s in (width, cdim, n_heads, kw, kw, vw, rank, vw):
        cuts.append(cuts[-1] + s)
    seg = lambda j: w_in[:, cuts[j]:cuts[j + 1]]
    w_small = jnp.concatenate([seg(2)] + [seg(6)] * GATE_COPIES
                              + [jnp.zeros((d, LANES - n_heads - GATE_COPIES * rank), w_in.dtype)], axis=1)
    weights = [seg(0), seg(1), w_small, seg(3), seg(4), seg(5), seg(7)]
    x2d = x.reshape(b * l, d)
    z_a, xbc, small, q_b, k_b, v_b, g_b = [
        u.reshape(b, l, -1) for u in _norm_proj(x2d, norm_w, [w.astype(BF16) for w in weights],
                                                [F32, F32, F32, F32, F32, BF16, F32])]
    y_a = _ssd(xbc, small, z_a, conv_w, conv_b, a_log, dt_bias, d_skip, ssd_norm_w)
    o_b = _gla(q_b, k_b, v_b, small, g_b, gate_w2, gate_b, gla_norm_w, lr_lane=n_heads)
    return [y_a.reshape(b * l, width), o_b.reshape(b * l, vw)], [w_out[:width].astype(BF16), w_out[width:].astype(BF16)]


def _odd_layer(x, norm_w, w_in, w_out):
    b, l, d = x.shape
    w = w_out.shape[0]
    weights = [w_in[:, j * w:(j + 1) * w].astype(BF16) for j in range(4)]
    q, k, v, z = [u.reshape(b, l, w) for u in _norm_proj(x.reshape(b * l, d), norm_w, weights,
                                                         [F32, BF16, BF16, F32])]
    o = _moba(q, k, v, z)
    return [o.reshape(b * l, w)], [w_out.astype(BF16)]


def kernel(x, even_norm, even_w_in, even_conv_w, even_conv_b, even_a_log, even_dt_bias, even_d_skip, even_ssd_norm,
           even_gate_w2, even_gate_b, even_gla_norm, even_w_out, odd_norm, odd_w_in, odd_w_out, final_norm):
    b, l, d = x.shape
    depth = even_norm.shape[0] + odd_norm.shape[0]
    for layer in range(depth):
        i = layer // 2
        if layer % 2 == 0:
            acts, weights = _even_layer(x, even_norm[i], even_w_in[i], even_conv_w[i], even_conv_b[i], even_a_log[i],
                                        even_dt_bias[i], even_d_skip[i], even_ssd_norm[i], even_gate_w2[i],
                                        even_gate_b[i], even_gla_norm[i], even_w_out[i])
        else:
            acts, weights = _odd_layer(x, odd_norm[i], odd_w_in[i], odd_w_out[i])
        last = layer == depth - 1
        x = _out_proj(acts, x.reshape(b * l, d), weights, final_norm if last else None).reshape(b, l, d)
    return x
```

```python
import functools

import jax
import jax.numpy as jnp
from jax import lax
from jax.experimental import pallas as pl
from jax.experimental.pallas import tpu as pltpu

F32 = jnp.float32
BF16 = jnp.bfloat16
HIGHEST = lax.Precision.HIGHEST

LANES = 128
SUBLANES = 8
BF16_SUBLANES = 16
VMEM_LIMIT_BYTES = 56 * 1024 * 1024

RMS_EPS = 1e-6
SSD_HEAD_DIM = 64
SSD_GROUPS = 2
SSD_STATE = 64
SSD_CONV = 4
SSD_CHUNK = 128
GLA_HEADS = 8
GLA_GATE_NORMALIZER = 16.0
GLA_CHUNK = 64
MOBA_HEAD_DIM = 64
MOBA_BLOCK = 256
MOBA_TOPK = 3
LOG2_E = 1.4426950408889634
CONV_ROW_STRIDE = 4
CUMSUM_TERMS = 3
GATE_COPIES = 3

IN_PROJ_ROWS = 512
OUT_PROJ_ROWS = 1024
GLA_ROWS = 256
HALF = LANES // 2
VT_ROWS = HALF + BF16_SUBLANES


def _params(*sem):
    return pltpu.CompilerParams(dimension_semantics=sem, vmem_limit_bytes=VMEM_LIMIT_BYTES)


def _rms(x, w):
    return x * lax.rsqrt(jnp.mean(x * x, axis=-1, keepdims=True) + RMS_EPS) * w


def _silu(x):
    h = 0.5 * x
    return h + h * jnp.tanh(h)


def _softplus(x):
    return jnp.maximum(x, 0.0) + jnp.log1p(jnp.exp(-jnp.abs(x)))


def _log_sigmoid(x):
    return jnp.minimum(x, 0.0) - jnp.log(1.0 + jnp.exp(-jnp.abs(x)))


def _dot(a, b):
    return jnp.dot(a.astype(BF16), b.astype(BF16), preferred_element_type=F32)


def _dot_nt(a, b):
    return lax.dot_general(a.astype(BF16), b.astype(BF16), (((1,), (1,)), ((), ())), preferred_element_type=F32)


def _dot_tn(a, b):
    return lax.dot_general(a.astype(BF16), b.astype(BF16), (((0,), (0,)), ((), ())), preferred_element_type=F32)


def _split_bf16(x, terms):
    parts = []
    for _ in range(terms):
        p = x.astype(BF16)
        parts.append(p)
        x = x - p.astype(F32)
    return parts


def _chunk_cumsum(x, chunk):
    rows = x.shape[0]
    ri = lax.broadcasted_iota(jnp.int32, (chunk, CUMSUM_TERMS * chunk), 0)
    ci = lax.broadcasted_iota(jnp.int32, (chunk, CUMSUM_TERMS * chunk), 1)
    tri = jnp.where((ci & (chunk - 1)) <= ri, 1.0, 0.0).astype(BF16)
    out = []
    for r0 in range(0, rows, chunk):
        stacked = jnp.concatenate(_split_bf16(x[r0:r0 + chunk, :], CUMSUM_TERMS), axis=0)
        out.append(jnp.dot(tri, stacked, preferred_element_type=F32))
    return out[0] if len(out) == 1 else jnp.concatenate(out, axis=0)


def _resident(shape):
    return pl.BlockSpec(shape, lambda *_: (0,) * len(shape), pipeline_mode=pl.Buffered(1))


def _norm_proj_body(x_ref, nw_ref, w_ref, *o_refs):
    h = _rms(x_ref[...], nw_ref[...]).astype(BF16)
    off = 0
    for o_ref in o_refs:
        width = o_ref.shape[1]
        o_ref[...] = jnp.dot(h, w_ref[:, off:off + width], preferred_element_type=F32).astype(o_ref.dtype)
        off += width


def _norm_proj(x2d, norm_w, weight, widths, out_dtypes):
    m, d = x2d.shape
    tm = IN_PROJ_ROWS
    assert sum(widths) == weight.shape[1] and all(w % LANES == 0 for w in widths)
    return pl.pallas_call(
        _norm_proj_body,
        name="norm_proj",
        grid=(m // tm,),
        in_specs=[pl.BlockSpec((tm, d), lambda i: (i, 0)), _resident((1, d)), _resident(weight.shape)],
        out_specs=[pl.BlockSpec((tm, w), lambda i: (i, 0)) for w in widths],
        out_shape=[jax.ShapeDtypeStruct((m, w), dt) for w, dt in zip(widths, out_dtypes)],
        compiler_params=_params("parallel"),
    )(x2d, norm_w.reshape(1, d), weight)


def _ssd_body(xbc_ref, sm_ref, z_ref, cw_ref, cb_ref, alog_ref, dtb_ref, dskip_ref, nw_ref,
              y_ref, h_ref, xpad_ref, act_ref, xw_ref, ycat_ref):
    q = SSD_CHUNK
    width = y_ref.shape[-1]
    n_pairs = width // LANES
    n_slabs = xpad_ref.shape[0]
    gstate = SSD_GROUPS * SSD_STATE
    gwidth = width // SSD_GROUPS
    first = pl.program_id(1) == 0

    @pl.when(first)
    def _():
        xpad_ref[:, 0:SUBLANES, :] = jnp.zeros((n_slabs, SUBLANES, LANES), F32)
        h_ref[...] = jnp.zeros_like(h_ref)

    @pl.when(jnp.logical_not(first))
    def _():
        xpad_ref[:, 0:SUBLANES, :] = xpad_ref[:, q:q + SUBLANES, :]

    for c in range(n_slabs):
        lanes = slice(c * LANES, (c + 1) * LANES)
        xpad_ref[c, SUBLANES:SUBLANES + q, :] = xbc_ref[0, :, lanes]
        taps = [jnp.broadcast_to(cw_ref[k:k + 1, lanes], (SUBLANES, LANES)) for k in range(SSD_CONV)]
        bias = jnp.broadcast_to(cb_ref[:, lanes], (SUBLANES, LANES))
        for t0 in range(0, q, SUBLANES * CONV_ROW_STRIDE):
            for g in range(CONV_ROW_STRIDE):
                conv = bias
                for k in range(SSD_CONV):
                    start = SUBLANES + t0 + g - (SSD_CONV - 1 - k)
                    conv = conv + taps[k] * xpad_ref[c, pl.ds(start, SUBLANES, stride=CONV_ROW_STRIDE), :]
                act_ref[c, pl.ds(t0 + g, SUBLANES, stride=CONV_ROW_STRIDE), :] = _silu(conv)
    bm = act_ref[n_pairs]
    cm = act_ref[n_pairs + 1]

    dt = _softplus(sm_ref[0] + dtb_ref[...])
    dta = dt * (-jnp.exp(alog_ref[...]) * LOG2_E)
    ri = lax.broadcasted_iota(jnp.int32, (q, q), 0)
    ci = lax.broadcasted_iota(jnp.int32, (q, q), 1)
    causal = ri >= ci
    a_cs = _chunk_cumsum(dta, q)
    a_cs_t = a_cs.T
    a_last = a_cs[q - 1:q, :]
    bm_t = bm.T

    lane = lax.broadcasted_iota(jnp.int32, (q, LANES), 1)
    lo = lane < HALF
    lo_row = lo[0:1, :]
    cm_g = [jnp.where(lo, cm, 0.0), jnp.where(lo, 0.0, cm)]
    cb = [_dot_nt(c, bm) for c in cm_g]
    h_prev = h_ref[...]
    y_off = [_dot(c, h_prev) for c in cm_g]

    a_last_pairs = []
    for p in range(n_pairs):
        e0, e1 = 2 * p, 2 * p + 1
        g = (p * LANES) // gwidth
        col = slice(p * LANES, (p + 1) * LANES)
        gcol = slice(p * LANES - g * gwidth, (p + 1) * LANES - g * gwidth)
        acs_pair = jnp.where(lo, a_cs[:, e0:e0 + 1], a_cs[:, e1:e1 + 1])
        dt_pair = jnp.where(lo, dt[:, e0:e0 + 1], dt[:, e1:e1 + 1])
        al_pair = jnp.where(lo_row, a_last[:, e0:e0 + 1], a_last[:, e1:e1 + 1])
        a_last_pairs.append(al_pair)
        xs2 = act_ref[p]
        xdt = xs2 * dt_pair
        xdt_b = xdt.astype(BF16)
        yd = []
        for e in (e0, e1):
            seg = a_cs[:, e:e + 1] - a_cs_t[e:e + 1, :]
            decay = jnp.exp2(jnp.where(causal, seg, -jnp.inf))
            yd.append(_dot(cb[g] * decay, xdt_b))
        y2 = jnp.where(lo, yd[0], yd[1])
        y2 = y2 + y_off[g][:, gcol] * jnp.exp2(acs_pair) + dskip_ref[:, col] * xs2
        ycat_ref[:, col] = y2
        xw_ref[:, col] = (xdt * jnp.exp2(al_pair - acs_pair)).astype(BF16)

    for g in range(SSD_GROUPS):
        rows = slice(g * SSD_STATE, (g + 1) * SSD_STATE)
        ppg = n_pairs // SSD_GROUPS
        dec = jnp.exp2(jnp.concatenate(a_last_pairs[g * ppg:(g + 1) * ppg], axis=1))
        s_g = _dot(bm_t[rows, :], xw_ref[:, g * gwidth:(g + 1) * gwidth])
        h_ref[rows, :] = h_prev[rows, :] * dec + s_g

    y_ref[0] = _rms(ycat_ref[...] * _silu(z_ref[0]), nw_ref[...]).astype(y_ref.dtype)


def _ssd(xbc, small, z, conv_w, conv_b, a_log, dt_bias, d_skip, norm_w):
    b, l, cdim = xbc.shape
    width = z.shape[-1]
    q = SSD_CHUNK
    n_heads = a_log.shape[0]
    pad = lambda v: jnp.pad(v, (0, LANES - n_heads)).reshape(1, LANES)
    gwidth = width // SSD_GROUPS
    blk = lambda w: pl.BlockSpec((1, q, w), lambda i, c: (i, c, 0))
    return pl.pallas_call(
        _ssd_body,
        name="ssd",
        grid=(b, l // q),
        in_specs=[blk(cdim), blk(LANES), blk(width),
                  _resident((SSD_CONV, cdim)), _resident((1, cdim)), _resident((1, LANES)),
                  _resident((1, LANES)), _resident((1, width)), _resident((1, width))],
        out_specs=blk(width),
        out_shape=jax.ShapeDtypeStruct((b, l, width), BF16),
        scratch_shapes=[pltpu.VMEM((SSD_GROUPS * SSD_STATE, gwidth), F32),
                        pltpu.VMEM((cdim // LANES, q + SUBLANES, LANES), F32),
                        pltpu.VMEM((cdim // LANES, q, LANES), F32),
                        pltpu.VMEM((q, width), BF16),
                        pltpu.VMEM((q, width), F32)],
        compiler_params=_params("parallel", "arbitrary"),
    )(xbc, small, z, conv_w, conv_b.reshape(1, cdim), pad(a_log), pad(dt_bias),
      jnp.repeat(d_skip, SSD_HEAD_DIM).reshape(1, width), norm_w.reshape(1, width))


def _gla_body(q_ref, k_ref, v_ref, sm_ref, g_ref, w2_ref, gb_ref, nw_ref, o_ref, st_ref, *, mid_lane):
    c = GLA_CHUNK
    kw = q_ref.shape[-1]
    dk = kw // GLA_HEADS
    dv = v_ref.shape[-1] // GLA_HEADS
    n_pairs = kw // LANES

    @pl.when(pl.program_id(1) == 0)
    def _():
        st_ref[...] = jnp.zeros_like(st_ref)

    ri = lax.broadcasted_iota(jnp.int32, (c, c), 0)
    ci = lax.broadcasted_iota(jnp.int32, (c, c), 1)
    causal = ri >= ci
    lo = lax.broadcasted_iota(jnp.int32, (c, LANES), 1) < HALF
    lo_sq = lax.broadcasted_iota(jnp.int32, (dv, LANES), 1) < HALF

    sm_hi, sm_mid = _split_bf16(sm_ref[0], 2)
    sm_lane = lax.broadcasted_iota(jnp.int32, sm_hi.shape, 1)
    pre = jnp.dot(jnp.where(sm_lane < mid_lane, sm_hi, sm_mid), w2_ref[...],
                  preferred_element_type=F32) + gb_ref[...]
    gcs_all = _chunk_cumsum(_log_sigmoid(pre) * (LOG2_E / GLA_GATE_NORMALIZER), c)

    chunks = [slice(ch * c, (ch + 1) * c) for ch in range(q_ref.shape[1] // c)]
    pairs = [slice(p * LANES, (p + 1) * LANES) for p in range(n_pairs)]
    causal2 = jnp.concatenate([causal, causal], axis=0)

    def by_head(x):
        return jnp.concatenate([jnp.where(lo, x, 0.0), jnp.where(lo, 0.0, x)], axis=0).astype(BF16)

    q_in, k_in, q_st, k_st, g_last = [], [], [], [], []
    for rows in chunks:
        gcs = gcs_all[rows, :]
        g_mid = gcs[c // 2:c // 2 + 1, :]
        g_last.append(gcs[c - 1:c, :])
        qs = q_ref[0, rows, :] * (dk ** -0.5)
        ks = k_ref[0, rows, :]
        q_in.append(qs * jnp.exp2(gcs - g_mid))
        k_in.append((ks * jnp.exp2(g_mid - gcs)).astype(BF16))
        q_st.append(qs * jnp.exp2(gcs))
        k_st.append((ks * jnp.exp2(g_last[-1] - gcs)).astype(BF16))

    scores = [[_dot_nt(by_head(q_in[ch][:, col]), k_in[ch][:, col]) for col in pairs]
              for ch in range(len(chunks))]
    scores = [[jnp.where(causal2, s, 0.0).astype(BF16) for s in row] for row in scores]

    o_intra, kv_t = [], []
    for ch, rows in enumerate(chunks):
        o_row, kv_row = [], []
        for p, col in enumerate(pairs):
            v_a = v_ref[0, rows, (2 * p) * dv:(2 * p + 1) * dv].astype(BF16)
            v_b = v_ref[0, rows, (2 * p + 1) * dv:(2 * p + 2) * dv].astype(BF16)
            o_row.append((jnp.dot(scores[ch][p][0:c, :], v_a, preferred_element_type=F32),
                          jnp.dot(scores[ch][p][c:2 * c, :], v_b, preferred_element_type=F32)))
            kv_row.append(jnp.where(lo_sq, _dot_tn(v_a, k_st[ch][:, col]), _dot_tn(v_b, k_st[ch][:, col])))
        o_intra.append(o_row)
        kv_t.append(kv_row)

    o_inter = [[None] * n_pairs for _ in chunks]
    for p, col in enumerate(pairs):
        st = st_ref[:, col]
        for ch in range(len(chunks)):
            o_inter[ch][p] = _dot_nt(by_head(q_st[ch][:, col]), st)
            st = st * jnp.exp2(g_last[ch][:, col]) + kv_t[ch][p]
        st_ref[:, col] = st

    for ch, rows in enumerate(chunks):
        for p in range(n_pairs):
            for half in range(2):
                vcol = slice((2 * p + half) * dv, (2 * p + half + 1) * dv)
                o = o_intra[ch][p][half] + o_inter[ch][p][half * c:(half + 1) * c, :]
                o_ref[0, rows, vcol] = (_rms(o, nw_ref[...]) * _silu(g_ref[0, rows, vcol])).astype(o_ref.dtype)


def _gla(q, k, v, small, g, gate_w2, gate_b, norm_w, lr_lane):
    b, l, kw = q.shape
    vw = v.shape[-1]
    rank = gate_w2.shape[0]
    t = GLA_ROWS
    w2_hi, w2_mid = _split_bf16(gate_w2, 2)
    w2_pad = jnp.zeros((LANES, kw), BF16)
    for copy, part in enumerate((w2_hi, w2_mid, w2_hi)):
        w2_pad = w2_pad.at[lr_lane + copy * rank:lr_lane + (copy + 1) * rank, :].set(part)
    blk = lambda w: pl.BlockSpec((1, t, w), lambda i, c: (i, c, 0))
    return pl.pallas_call(
        functools.partial(_gla_body, mid_lane=lr_lane + 2 * rank),
        name="gla",
        grid=(b, l // t),
        in_specs=[blk(kw), blk(kw), blk(vw), blk(LANES), blk(vw),
                  _resident((LANES, kw)), _resident((1, kw)), _resident((1, vw // GLA_HEADS))],
        out_specs=blk(vw),
        out_shape=jax.ShapeDtypeStruct((b, l, vw), BF16),
        scratch_shapes=[pltpu.VMEM((vw // GLA_HEADS, kw), F32)],
        compiler_params=_params("parallel", "arbitrary"),
    )(q, k, v, small, g, w2_pad, gate_b.reshape(1, kw), norm_w.reshape(1, -1))


def _out_proj_body(*refs, n_in, final_norm):
    a_refs, x_ref, w_ref = refs[:n_in], refs[n_in], refs[n_in + 1]
    rest = refs[n_in + 2:]
    acc = x_ref[...]
    off = 0
    for a_ref in a_refs:
        width = a_ref.shape[1]
        acc = acc + jnp.dot(a_ref[...].astype(BF16), w_ref[off:off + width, :], preferred_element_type=F32)
        off += width
    if final_norm:
        fw_ref, o_ref = rest
        o_ref[...] = _rms(acc, fw_ref[...])
    else:
        (o_ref,) = rest
        o_ref[...] = acc


def _out_proj(acts, x2d, weight, final_w=None):
    m, d = x2d.shape
    tm = OUT_PROJ_ROWS
    n_in = len(acts)
    assert sum(a.shape[1] for a in acts) == weight.shape[0]
    row = lambda w: pl.BlockSpec((tm, w), lambda i: (i, 0))
    in_specs = [row(a.shape[1]) for a in acts] + [row(d), _resident(weight.shape)]
    args = list(acts) + [x2d, weight]
    if final_w is not None:
        in_specs.append(_resident((1, d)))
        args.append(final_w.reshape(1, d))
    return pl.pallas_call(
        functools.partial(_out_proj_body, n_in=n_in, final_norm=final_w is not None),
        name="out_proj",
        grid=(m // tm,),
        in_specs=in_specs,
        out_specs=row(d),
        out_shape=jax.ShapeDtypeStruct((m, d), F32),
        compiler_params=_params("parallel"),
    )(*args)


class _MobaQueryBlock:
    def __init__(self, n_past, q_ref, z_ref, o_ref, kb_ref, vt_ref, kmean_ref, s_ref):
        blk = MOBA_BLOCK
        self.n_past, self.z_ref, self.o_ref, self.kb_ref, self.vt_ref = n_past, z_ref, o_ref, kb_ref, vt_ref
        self.own = slice(n_past * blk, (n_past + 1) * blk)
        self.s_ref = s_ref.at[n_past % s_ref.shape[0]]
        q2 = q_ref[0, self.own, :]
        lo = lax.broadcasted_iota(jnp.int32, (blk, LANES), 1) < HALF
        q_cat = jnp.concatenate([jnp.where(lo, q2, 0.0), jnp.where(lo, 0.0, q2)], axis=0)
        scale = MOBA_HEAD_DIM ** -0.5
        key_i = lax.broadcasted_iota(jnp.int32, (blk, 2 * blk), 0)
        qry_i = lax.broadcasted_iota(jnp.int32, (blk, 2 * blk), 1)
        causal_t = key_i <= jnp.where(qry_i < blk, qry_i, qry_i - blk)

        self.masks = []
        if n_past:
            nb = kmean_ref.shape[0] // 2
            parts = [_dot_nt(kmean_ref[...], q_part) for q_part in _split_bf16(q_cat * scale, 2)]
            gate = (parts[0][0:nb] + parts[0][nb:]) + (parts[1][0:nb] + parts[1][nb:])
            g_rows = [gate[n:n + 1, :] for n in range(n_past)]
            for n in range(n_past):
                rank = jnp.zeros((1, 2 * blk), F32)
                for m in range(n_past):
                    if m != n:
                        ahead = (g_rows[m] >= g_rows[n]) if m < n else (g_rows[m] > g_rows[n])
                        rank = rank + jnp.where(ahead, 1.0, 0.0)
                self.masks.append(rank < MOBA_TOPK)
        self.masks.append(causal_t)
        self.q_s = (q_cat * (scale * LOG2_E)).astype(BF16)
        self.m_run = None
        self.p = []

    def logits_step(self, j):
        rows = slice(j * MOBA_BLOCK, (j + 1) * MOBA_BLOCK)
        sj = jnp.where(self.masks[j], _dot_nt(self.kb_ref[rows, :], self.q_s), -jnp.inf)
        self.s_ref[rows, :] = sj
        mj = jnp.max(sj, axis=0, keepdims=True)
        self.m_run = mj if self.m_run is None else jnp.maximum(self.m_run, mj)

    def value_step(self, j):
        rows = slice(j * MOBA_BLOCK, (j + 1) * MOBA_BLOCK)
        self.p.append(jnp.exp2(self.s_ref[rows, :] - self.m_run).astype(BF16))

    def finish(self):
        blk = MOBA_BLOCK
        nk = len(self.p) * blk
        p_all = jnp.concatenate(self.p, axis=0)
        outs = []
        for half in range(2):
            vt = self.vt_ref[half * VT_ROWS:(half + 1) * VT_ROWS, 0:nk]
            acc = jnp.dot(vt, p_all[:, half * blk:(half + 1) * blk], preferred_element_type=F32)
            outs.append(acc[0:HALF, :] / acc[HALF:HALF + 1, :])
        o_t = jnp.concatenate(outs, axis=0)
        self.o_ref[0, self.own, :] = (o_t.T * _silu(self.z_ref[0, self.own, :])).astype(self.o_ref.dtype)


def _moba_body(q_ref, k_ref, v_ref, z_ref, o_ref, vt_ref, kmean_ref, s_ref):
    blk = MOBA_BLOCK
    nb = k_ref.shape[1] // blk
    kmean = []
    for n in range(nb):
        rows = slice(n * blk, (n + 1) * blk)
        kmean.append(jnp.mean(k_ref[0, rows, :].astype(F32), axis=0, keepdims=True))
        v_t = v_ref[0, rows, :].astype(F32).T.astype(BF16)
        for half in range(2):
            vt_ref[half * VT_ROWS:half * VT_ROWS + HALF, rows] = v_t[half * HALF:(half + 1) * HALF, :]
    for half in range(2):
        vt_ref[half * VT_ROWS + HALF:(half + 1) * VT_ROWS, :] = jnp.ones((BF16_SUBLANES, vt_ref.shape[1]), BF16)
    kmean_ref[...] = jnp.concatenate(_split_bf16(jnp.concatenate(kmean, axis=0), 2), axis=0)
    make = functools.partial(_MobaQueryBlock, q_ref=q_ref, z_ref=z_ref, o_ref=o_ref, kb_ref=k_ref.at[0],
                             vt_ref=vt_ref, kmean_ref=kmean_ref, s_ref=s_ref)
    blocks = {0: make(0), 1: make(1)}
    blocks[0].logits_step(0)
    for i in range(nb):
        if i + 2 < nb:
            blocks[i + 2] = make(i + 2)
        cur, nxt = blocks.pop(i), blocks.get(i + 1)
        for j in range(i + 2):
            if j <= i:
                cur.value_step(j)
            if nxt is not None:
                nxt.logits_step(j)
        cur.finish()


def _moba(q, k, v, z):
    b, l, w = q.shape
    blk = MOBA_BLOCK
    nb = l // blk
    spec = pl.BlockSpec((1, l, LANES), lambda bi, hp: (bi, 0, hp))
    return pl.pallas_call(
        _moba_body,
        name="moba",
        grid=(b, w // LANES),
        in_specs=[spec, spec, spec, spec],
        out_specs=spec,
        out_shape=jax.ShapeDtypeStruct((b, l, w), BF16),
        scratch_shapes=[pltpu.VMEM((2 * VT_ROWS, l), BF16),
                        pltpu.VMEM((2 * nb, LANES), BF16),
                        pltpu.VMEM((2, l, 2 * blk), F32)],
        compiler_params=_params("parallel", "parallel"),
    )(q, k, v, z)


def _even_layer(x, norm_w, w_in, conv_w, conv_b, a_log, dt_bias, d_skip, ssd_norm_w, gate_w2, gate_b,
                gla_norm_w, w_out):
    b, l, d = x.shape
    width = ssd_norm_w.shape[0]
    cdim = conv_w.shape[1]
    n_heads = a_log.shape[0]
    rank, kw = gate_w2.shape
    vw = gla_norm_w.shape[0] * GLA_HEADS
    cuts = [0]
    for s in (width, cdim, n_heads, kw, kw, vw, rank, vw):
        cuts.append(cuts[-1] + s)
    seg = lambda j: w_in[:, cuts[j]:cuts[j + 1]]
    pad = jnp.zeros((d, LANES - n_heads - GATE_COPIES * rank), w_in.dtype)
    w_cat = jnp.concatenate([seg(0), seg(1), seg(3), seg(4), seg(5), seg(7), seg(2)] + [seg(6)] * GATE_COPIES + [pad],
                            axis=1).astype(BF16)
    x2d = x.reshape(b * l, d)
    z_a, xbc, q_b, k_b, v_b, g_b, small = [
        u.reshape(b, l, -1) for u in _norm_proj(x2d, norm_w, w_cat, [width, cdim, kw, kw, vw, vw, LANES],
                                                [F32, F32, F32, F32, BF16, F32, F32])]
    y_a = _ssd(xbc, small, z_a, conv_w, conv_b, a_log, dt_bias, d_skip, ssd_norm_w)
    o_b = _gla(q_b, k_b, v_b, small, g_b, gate_w2, gate_b, gla_norm_w, lr_lane=n_heads)
    return [y_a.reshape(b * l, width), o_b.reshape(b * l, vw)], w_out.astype(BF16)


def _odd_layer(x, norm_w, w_in, w_out):
    b, l, d = x.shape
    w = w_out.shape[0]
    q, k, v, z = [u.reshape(b, l, w) for u in _norm_proj(x.reshape(b * l, d), norm_w, w_in.astype(BF16),
                                                         [w] * 4, [F32, BF16, BF16, F32])]
    o = _moba(q, k, v, z)
    return [o.reshape(b * l, w)], w_out.astype(BF16)


def kernel(x, even_norm, even_w_in, even_conv_w, even_conv_b, even_a_log, even_dt_bias, even_d_skip, even_ssd_norm,
           even_gate_w2, even_gate_b, even_gla_norm, even_w_out, odd_norm, odd_w_in, odd_w_out, final_norm):
    b, l, d = x.shape
    depth = even_norm.shape[0] + odd_norm.shape[0]
    for layer in range(depth):
        i = layer // 2
        if layer % 2 == 0:
            acts, weights = _even_layer(x, even_norm[i], even_w_in[i], even_conv_w[i], even_conv_b[i], even_a_log[i],
                                        even_dt_bias[i], even_d_skip[i], even_ssd_norm[i], even_gate_w2[i],
                                        even_gate_b[i], even_gla_norm[i], even_w_out[i])
        else:
            acts, weights = _odd_layer(x, odd_norm[i], odd_w_in[i], odd_w_out[i])
        last = layer == depth - 1
        x = _out_proj(acts, x.reshape(b * l, d), weights, final_norm if last else None).reshape(b, l, d)
    return x
```

```python
import functools

import jax
import jax.numpy as jnp
from jax import lax
from jax.experimental import pallas as pl
from jax.experimental.pallas import tpu as pltpu

F32 = jnp.float32
BF16 = jnp.bfloat16
HIGHEST = lax.Precision.HIGHEST

LANES = 128
SUBLANES = 8
BF16_SUBLANES = 16
VMEM_LIMIT_BYTES = 56 * 1024 * 1024

RMS_EPS = 1e-6
SSD_HEAD_DIM = 64
SSD_GROUPS = 2
SSD_STATE = 64
SSD_CONV = 4
SSD_CHUNK = 128
GLA_HEADS = 8
GLA_GATE_NORMALIZER = 16.0
GLA_CHUNK = 64
MOBA_HEAD_DIM = 64
MOBA_BLOCK = 256
MOBA_TOPK = 3
LOG2_E = 1.4426950408889634
CONV_ROW_STRIDE = 4
CUMSUM_TERMS = 3
GATE_COPIES = 3

IN_PROJ_ROWS = 512
OUT_PROJ_ROWS = 1024
GLA_ROWS = 256
HALF = LANES // 2
VT_ROWS = HALF + BF16_SUBLANES


def _params(*sem):
    return pltpu.CompilerParams(dimension_semantics=sem, vmem_limit_bytes=VMEM_LIMIT_BYTES)


def _rms(x, w):
    return x * lax.rsqrt(jnp.mean(x * x, axis=-1, keepdims=True) + RMS_EPS) * w


def _silu(x):
    h = 0.5 * x
    return h + h * jnp.tanh(h)


def _softplus(x):
    return jnp.maximum(x, 0.0) + jnp.log1p(jnp.exp(-jnp.abs(x)))


def _log_sigmoid(x):
    return jnp.minimum(x, 0.0) - jnp.log(1.0 + jnp.exp(-jnp.abs(x)))


def _dot(a, b):
    return jnp.dot(a.astype(BF16), b.astype(BF16), preferred_element_type=F32)


def _dot_nt(a, b):
    return lax.dot_general(a.astype(BF16), b.astype(BF16), (((1,), (1,)), ((), ())), preferred_element_type=F32)


def _dot_tn(a, b):
    return lax.dot_general(a.astype(BF16), b.astype(BF16), (((0,), (0,)), ((), ())), preferred_element_type=F32)


def _split_bf16(x, terms):
    parts = []
    for _ in range(terms):
        p = x.astype(BF16)
        parts.append(p)
        x = x - p.astype(F32)
    return parts


def _chunk_cumsum(x, chunk):
    rows = x.shape[0]
    ri = lax.broadcasted_iota(jnp.int32, (chunk, CUMSUM_TERMS * chunk), 0)
    ci = lax.broadcasted_iota(jnp.int32, (chunk, CUMSUM_TERMS * chunk), 1)
    tri = jnp.where((ci & (chunk - 1)) <= ri, 1.0, 0.0).astype(BF16)
    out = []
    for r0 in range(0, rows, chunk):
        stacked = jnp.concatenate(_split_bf16(x[r0:r0 + chunk, :], CUMSUM_TERMS), axis=0)
        out.append(jnp.dot(tri, stacked, preferred_element_type=F32))
    return out[0] if len(out) == 1 else jnp.concatenate(out, axis=0)


def _resident(shape):
    return pl.BlockSpec(shape, lambda *_: (0,) * len(shape), pipeline_mode=pl.Buffered(1))


def _norm_proj_body(x_ref, nw_ref, w_ref, *refs, offsets, n_extra):
    extra_refs, o_refs = refs[:n_extra], refs[n_extra:n_extra + len(offsets) + n_extra]
    shifted = [i for i, off in enumerate(offsets) if off % LANES]
    starts, pos = {}, 0
    for i in shifted:
        starts[i] = pos
        pos += o_refs[i].shape[1]

    if shifted:
        al_ref = refs[-1]

        @pl.when(pl.program_id(0) == 0)
        def _():
            for i in shifted:
                width = o_refs[i].shape[1]
                al_ref[:, starts[i]:starts[i] + width] = w_ref[:, offsets[i]:offsets[i] + width]

    h = _rms(x_ref[...], nw_ref[...]).astype(BF16)
    for i, off in enumerate(offsets):
        width = o_refs[i].shape[1]
        w = al_ref[:, starts[i]:starts[i] + width] if i in starts else w_ref[:, off:off + width]
        o_refs[i][...] = jnp.dot(h, w, preferred_element_type=F32).astype(o_refs[i].dtype)
    for e_ref, o_ref in zip(extra_refs, o_refs[len(offsets):]):
        o_ref[...] = jnp.dot(h, e_ref[...], preferred_element_type=F32).astype(o_ref.dtype)


def _norm_proj(x2d, norm_w, weight, segments, out_dtypes, extras=()):
    m, d = x2d.shape
    tm = IN_PROJ_ROWS
    assert all(w % LANES == 0 for _, w in segments)
    widths = [w for _, w in segments] + [e.shape[1] for e in extras]
    shifted_cols = sum(w for off, w in segments if off % LANES)
    return pl.pallas_call(
        functools.partial(_norm_proj_body, offsets=tuple(off for off, _ in segments), n_extra=len(extras)),
        name="norm_proj",
        grid=(m // tm,),
        in_specs=[pl.BlockSpec((tm, d), lambda i: (i, 0)), _resident((1, d)), _resident(weight.shape)]
        + [_resident(e.shape) for e in extras],
        out_specs=[pl.BlockSpec((tm, w), lambda i: (i, 0)) for w in widths],
        out_shape=[jax.ShapeDtypeStruct((m, w), dt) for w, dt in zip(widths, out_dtypes)],
        scratch_shapes=[pltpu.VMEM((d, shifted_cols), weight.dtype)] if shifted_cols else [],
        compiler_params=_params("arbitrary"),
    )(x2d, norm_w.reshape(1, d), weight, *extras)


def _ssd_body(xbc_ref, sm_ref, z_ref, cw_ref, cb_ref, alog_ref, dtb_ref, dskip_ref, nw_ref,
              y_ref, h_ref, xpad_ref, act_ref, xw_ref, ycat_ref):
    q = SSD_CHUNK
    width = y_ref.shape[-1]
    n_pairs = width // LANES
    n_slabs = xpad_ref.shape[0]
    gstate = SSD_GROUPS * SSD_STATE
    gwidth = width // SSD_GROUPS
    first = pl.program_id(1) == 0

    @pl.when(first)
    def _():
        xpad_ref[:, 0:SUBLANES, :] = jnp.zeros((n_slabs, SUBLANES, LANES), F32)
        h_ref[...] = jnp.zeros_like(h_ref)

    @pl.when(jnp.logical_not(first))
    def _():
        xpad_ref[:, 0:SUBLANES, :] = xpad_ref[:, q:q + SUBLANES, :]

    for c in range(n_slabs):
        lanes = slice(c * LANES, (c + 1) * LANES)
        xpad_ref[c, SUBLANES:SUBLANES + q, :] = xbc_ref[0, :, lanes]
        taps = [jnp.broadcast_to(cw_ref[k:k + 1, lanes], (SUBLANES, LANES)) for k in range(SSD_CONV)]
        bias = jnp.broadcast_to(cb_ref[:, lanes], (SUBLANES, LANES))
        for t0 in range(0, q, SUBLANES * CONV_ROW_STRIDE):
            for g in range(CONV_ROW_STRIDE):
                conv = bias
                for k in range(SSD_CONV):
                    start = SUBLANES + t0 + g - (SSD_CONV - 1 - k)
                    conv = conv + taps[k] * xpad_ref[c, pl.ds(start, SUBLANES, stride=CONV_ROW_STRIDE), :]
                act_ref[c, pl.ds(t0 + g, SUBLANES, stride=CONV_ROW_STRIDE), :] = _silu(conv)
    bm = act_ref[n_pairs]
    cm = act_ref[n_pairs + 1]

    dt = _softplus(sm_ref[0] + dtb_ref[...])
    dta = dt * (-jnp.exp(alog_ref[...]) * LOG2_E)
    ri = lax.broadcasted_iota(jnp.int32, (q, q), 0)
    ci = lax.broadcasted_iota(jnp.int32, (q, q), 1)
    causal = ri >= ci
    a_cs = _chunk_cumsum(dta, q)
    a_cs_t = a_cs.T
    a_last = a_cs[q - 1:q, :]
    bm_t = bm.T

    lane = lax.broadcasted_iota(jnp.int32, (q, LANES), 1)
    lo = lane < HALF
    lo_row = lo[0:1, :]
    cm_g = [jnp.where(lo, cm, 0.0), jnp.where(lo, 0.0, cm)]
    cb = [_dot_nt(c, bm) for c in cm_g]
    h_prev = h_ref[...]
    y_off = [_dot(c, h_prev) for c in cm_g]

    a_last_pairs = []
    for p in range(n_pairs):
        e0, e1 = 2 * p, 2 * p + 1
        g = (p * LANES) // gwidth
        col = slice(p * LANES, (p + 1) * LANES)
        gcol = slice(p * LANES - g * gwidth, (p + 1) * LANES - g * gwidth)
        acs_pair = jnp.where(lo, a_cs[:, e0:e0 + 1], a_cs[:, e1:e1 + 1])
        dt_pair = jnp.where(lo, dt[:, e0:e0 + 1], dt[:, e1:e1 + 1])
        al_pair = jnp.where(lo_row, a_last[:, e0:e0 + 1], a_last[:, e1:e1 + 1])
        a_last_pairs.append(al_pair)
        xs2 = act_ref[p]
        xdt = xs2 * dt_pair
        xdt_b = xdt.astype(BF16)
        yd = []
        for e in (e0, e1):
            seg = a_cs[:, e:e + 1] - a_cs_t[e:e + 1, :]
            decay = jnp.exp2(jnp.where(causal, seg, -jnp.inf))
            yd.append(_dot(cb[g] * decay, xdt_b))
        y2 = jnp.where(lo, yd[0], yd[1])
        y2 = y2 + y_off[g][:, gcol] * jnp.exp2(acs_pair) + dskip_ref[:, col] * xs2
        ycat_ref[:, col] = y2
        xw_ref[:, col] = (xdt * jnp.exp2(al_pair - acs_pair)).astype(BF16)

    for g in range(SSD_GROUPS):
        rows = slice(g * SSD_STATE, (g + 1) * SSD_STATE)
        ppg = n_pairs // SSD_GROUPS
        dec = jnp.exp2(jnp.concatenate(a_last_pairs[g * ppg:(g + 1) * ppg], axis=1))
        s_g = _dot(bm_t[rows, :], xw_ref[:, g * gwidth:(g + 1) * gwidth])
        h_ref[rows, :] = h_prev[rows, :] * dec + s_g

    y_ref[0] = _rms(ycat_ref[...] * _silu(z_ref[0]), nw_ref[...]).astype(y_ref.dtype)


def _ssd(xbc, small, z, conv_w, conv_b, a_log, dt_bias, d_skip, norm_w):
    b, l, cdim = xbc.shape
    width = z.shape[-1]
    q = SSD_CHUNK
    n_heads = a_log.shape[0]
    pad = lambda v: jnp.pad(v, (0, LANES - n_heads)).reshape(1, LANES)
    gwidth = width // SSD_GROUPS
    blk = lambda w: pl.BlockSpec((1, q, w), lambda i, c: (i, c, 0))
    return pl.pallas_call(
        _ssd_body,
        name="ssd",
        grid=(b, l // q),
        in_specs=[blk(cdim), blk(LANES), blk(width),
                  _resident((SSD_CONV, cdim)), _resident((1, cdim)), _resident((1, LANES)),
                  _resident((1, LANES)), _resident((1, width)), _resident((1, width))],
        out_specs=blk(width),
        out_shape=jax.ShapeDtypeStruct((b, l, width), BF16),
        scratch_shapes=[pltpu.VMEM((SSD_GROUPS * SSD_STATE, gwidth), F32),
                        pltpu.VMEM((cdim // LANES, q + SUBLANES, LANES), F32),
                        pltpu.VMEM((cdim // LANES, q, LANES), F32),
                        pltpu.VMEM((q, width), BF16),
                        pltpu.VMEM((q, width), F32)],
        compiler_params=_params("parallel", "arbitrary"),
    )(xbc, small, z, conv_w, conv_b.reshape(1, cdim), pad(a_log), pad(dt_bias),
      jnp.repeat(d_skip, SSD_HEAD_DIM).reshape(1, width), norm_w.reshape(1, width))


def _gla_body(q_ref, k_ref, v_ref, sm_ref, g_ref, w2_ref, gb_ref, nw_ref, o_ref, st_ref, *, mid_lane):
    c = GLA_CHUNK
    kw = q_ref.shape[-1]
    dk = kw // GLA_HEADS
    dv = v_ref.shape[-1] // GLA_HEADS
    n_pairs = kw // LANES

    @pl.when(pl.program_id(1) == 0)
    def _():
        st_ref[...] = jnp.zeros_like(st_ref)

    ri = lax.broadcasted_iota(jnp.int32, (c, c), 0)
    ci = lax.broadcasted_iota(jnp.int32, (c, c), 1)
    causal = ri >= ci
    lo = lax.broadcasted_iota(jnp.int32, (c, LANES), 1) < HALF
    lo_sq = lax.broadcasted_iota(jnp.int32, (dv, LANES), 1) < HALF

    sm_hi, sm_mid = _split_bf16(sm_ref[0], 2)
    sm_lane = lax.broadcasted_iota(jnp.int32, sm_hi.shape, 1)
    pre = jnp.dot(jnp.where(sm_lane < mid_lane, sm_hi, sm_mid), w2_ref[...],
                  preferred_element_type=F32) + gb_ref[...]
    gcs_all = _chunk_cumsum(_log_sigmoid(pre) * (LOG2_E / GLA_GATE_NORMALIZER), c)

    chunks = [slice(ch * c, (ch + 1) * c) for ch in range(q_ref.shape[1] // c)]
    pairs = [slice(p * LANES, (p + 1) * LANES) for p in range(n_pairs)]
    causal2 = jnp.concatenate([causal, causal], axis=0)

    def by_head(x):
        return jnp.concatenate([jnp.where(lo, x, 0.0), jnp.where(lo, 0.0, x)], axis=0).astype(BF16)

    q_in, k_in, q_st, k_st, g_last = [], [], [], [], []
    for rows in chunks:
        gcs = gcs_all[rows, :]
        g_mid = gcs[c // 2:c // 2 + 1, :]
        g_last.append(gcs[c - 1:c, :])
        qs = q_ref[0, rows, :] * (dk ** -0.5)
        ks = k_ref[0, rows, :]
        q_in.append(qs * jnp.exp2(gcs - g_mid))
        k_in.append((ks * jnp.exp2(g_mid - gcs)).astype(BF16))
        q_st.append(qs * jnp.exp2(gcs))
        k_st.append((ks * jnp.exp2(g_last[-1] - gcs)).astype(BF16))

    scores = [[_dot_nt(by_head(q_in[ch][:, col]), k_in[ch][:, col]) for col in pairs]
              for ch in range(len(chunks))]
    scores = [[jnp.where(causal2, s, 0.0).astype(BF16) for s in row] for row in scores]

    o_intra, kv_t = [], []
    for ch, rows in enumerate(chunks):
        o_row, kv_row = [], []
        for p, col in enumerate(pairs):
            v_a = v_ref[0, rows, (2 * p) * dv:(2 * p + 1) * dv].astype(BF16)
            v_b = v_ref[0, rows, (2 * p + 1) * dv:(2 * p + 2) * dv].astype(BF16)
            o_row.append((jnp.dot(scores[ch][p][0:c, :], v_a, preferred_element_type=F32),
                          jnp.dot(scores[ch][p][c:2 * c, :], v_b, preferred_element_type=F32)))
            kv_row.append(jnp.where(lo_sq, _dot_tn(v_a, k_st[ch][:, col]), _dot_tn(v_b, k_st[ch][:, col])))
        o_intra.append(o_row)
        kv_t.append(kv_row)

    o_inter = [[None] * n_pairs for _ in chunks]
    for p, col in enumerate(pairs):
        st = st_ref[:, col]
        for ch in range(len(chunks)):
            o_inter[ch][p] = _dot_nt(by_head(q_st[ch][:, col]), st)
            st = st * jnp.exp2(g_last[ch][:, col]) + kv_t[ch][p]
        st_ref[:, col] = st

    for ch, rows in enumerate(chunks):
        for p in range(n_pairs):
            for half in range(2):
                vcol = slice((2 * p + half) * dv, (2 * p + half + 1) * dv)
                o = o_intra[ch][p][half] + o_inter[ch][p][half * c:(half + 1) * c, :]
                o_ref[0, rows, vcol] = (_rms(o, nw_ref[...]) * _silu(g_ref[0, rows, vcol])).astype(o_ref.dtype)


def _gla(q, k, v, small, g, gate_w2, gate_b, norm_w, lr_lane):
    b, l, kw = q.shape
    vw = v.shape[-1]
    rank = gate_w2.shape[0]
    t = GLA_ROWS
    w2_hi, w2_mid = _split_bf16(gate_w2, 2)
    w2_pad = jnp.zeros((LANES, kw), BF16)
    for copy, part in enumerate((w2_hi, w2_mid, w2_hi)):
        w2_pad = w2_pad.at[lr_lane + copy * rank:lr_lane + (copy + 1) * rank, :].set(part)
    blk = lambda w: pl.BlockSpec((1, t, w), lambda i, c: (i, c, 0))
    return pl.pallas_call(
        functools.partial(_gla_body, mid_lane=lr_lane + 2 * rank),
        name="gla",
        grid=(b, l // t),
        in_specs=[blk(kw), blk(kw), blk(vw), blk(LANES), blk(vw),
                  _resident((LANES, kw)), _resident((1, kw)), _resident((1, vw // GLA_HEADS))],
        out_specs=blk(vw),
        out_shape=jax.ShapeDtypeStruct((b, l, vw), BF16),
        scratch_shapes=[pltpu.VMEM((vw // GLA_HEADS, kw), F32)],
        compiler_params=_params("parallel", "arbitrary"),
    )(q, k, v, small, g, w2_pad, gate_b.reshape(1, kw), norm_w.reshape(1, -1))


def _out_proj_body(*refs, n_in, final_norm):
    a_refs, x_ref, w_ref = refs[:n_in], refs[n_in], refs[n_in + 1]
    rest = refs[n_in + 2:]
    acc = x_ref[...]
    off = 0
    for a_ref in a_refs:
        width = a_ref.shape[1]
        acc = acc + jnp.dot(a_ref[...].astype(BF16), w_ref[off:off + width, :], preferred_element_type=F32)
        off += width
    if final_norm:
        fw_ref, o_ref = rest
        o_ref[...] = _rms(acc, fw_ref[...])
    else:
        (o_ref,) = rest
        o_ref[...] = acc


def _out_proj(acts, x2d, weight, final_w=None):
    m, d = x2d.shape
    tm = OUT_PROJ_ROWS
    n_in = len(acts)
    assert sum(a.shape[1] for a in acts) == weight.shape[0]
    row = lambda w: pl.BlockSpec((tm, w), lambda i: (i, 0))
    in_specs = [row(a.shape[1]) for a in acts] + [row(d), _resident(weight.shape)]
    args = list(acts) + [x2d, weight]
    if final_w is not None:
        in_specs.append(_resident((1, d)))
        args.append(final_w.reshape(1, d))
    return pl.pallas_call(
        functools.partial(_out_proj_body, n_in=n_in, final_norm=final_w is not None),
        name="out_proj",
        grid=(m // tm,),
        in_specs=in_specs,
        out_specs=row(d),
        out_shape=jax.ShapeDtypeStruct((m, d), F32),
        compiler_params=_params("parallel"),
    )(*args)


class _MobaQueryBlock:
    def __init__(self, n_past, q_ref, z_ref, o_ref, kb_ref, vt_ref, kmean_ref, s_ref):
        blk = MOBA_BLOCK
        self.n_past, self.z_ref, self.o_ref, self.kb_ref, self.vt_ref = n_past, z_ref, o_ref, kb_ref, vt_ref
        self.own = slice(n_past * blk, (n_past + 1) * blk)
        self.s_ref = s_ref.at[n_past % s_ref.shape[0]]
        q2 = q_ref[0, self.own, :]
        lo = lax.broadcasted_iota(jnp.int32, (blk, LANES), 1) < HALF
        q_cat = jnp.concatenate([jnp.where(lo, q2, 0.0), jnp.where(lo, 0.0, q2)], axis=0)
        scale = MOBA_HEAD_DIM ** -0.5
        key_i = lax.broadcasted_iota(jnp.int32, (blk, 2 * blk), 0)
        qry_i = lax.broadcasted_iota(jnp.int32, (blk, 2 * blk), 1)
        causal_t = key_i <= jnp.where(qry_i < blk, qry_i, qry_i - blk)

        self.masks = []
        if n_past:
            nb = kmean_ref.shape[0] // 2
            parts = [_dot_nt(kmean_ref[...], q_part) for q_part in _split_bf16(q_cat * scale, 2)]
            gate = (parts[0][0:nb] + parts[0][nb:]) + (parts[1][0:nb] + parts[1][nb:])
            g_rows = [gate[n:n + 1, :] for n in range(n_past)]
            for n in range(n_past):
                rank = jnp.zeros((1, 2 * blk), F32)
                for m in range(n_past):
                    if m != n:
                        ahead = (g_rows[m] >= g_rows[n]) if m < n else (g_rows[m] > g_rows[n])
                        rank = rank + jnp.where(ahead, 1.0, 0.0)
                self.masks.append(rank < MOBA_TOPK)
        self.masks.append(causal_t)
        self.q_s = (q_cat * (scale * LOG2_E)).astype(BF16)
        self.m_run = None
        self.p = []

    def logits_step(self, j):
        rows = slice(j * MOBA_BLOCK, (j + 1) * MOBA_BLOCK)
        sj = jnp.where(self.masks[j], _dot_nt(self.kb_ref[rows, :], self.q_s), -jnp.inf)
        self.s_ref[rows, :] = sj
        mj = jnp.max(sj, axis=0, keepdims=True)
        self.m_run = mj if self.m_run is None else jnp.maximum(self.m_run, mj)

    def value_step(self, j):
        rows = slice(j * MOBA_BLOCK, (j + 1) * MOBA_BLOCK)
        self.p.append(jnp.exp2(self.s_ref[rows, :] - self.m_run).astype(BF16))

    def finish(self):
        blk = MOBA_BLOCK
        nk = len(self.p) * blk
        p_all = jnp.concatenate(self.p, axis=0)
        outs = []
        for half in range(2):
            vt = self.vt_ref[half * VT_ROWS:(half + 1) * VT_ROWS, 0:nk]
            acc = jnp.dot(vt, p_all[:, half * blk:(half + 1) * blk], preferred_element_type=F32)
            outs.append(acc[0:HALF, :] / acc[HALF:HALF + 1, :])
        o_t = jnp.concatenate(outs, axis=0)
        self.o_ref[0, self.own, :] = (o_t.T * _silu(self.z_ref[0, self.own, :])).astype(self.o_ref.dtype)


def _moba_body(q_ref, k_ref, v_ref, z_ref, o_ref, vt_ref, kmean_ref, s_ref):
    blk = MOBA_BLOCK
    nb = k_ref.shape[1] // blk
    kmean = []
    for n in range(nb):
        rows = slice(n * blk, (n + 1) * blk)
        kmean.append(jnp.mean(k_ref[0, rows, :].astype(F32), axis=0, keepdims=True))
        v_t = v_ref[0, rows, :].astype(F32).T.astype(BF16)
        for half in range(2):
            vt_ref[half * VT_ROWS:half * VT_ROWS + HALF, rows] = v_t[half * HALF:(half + 1) * HALF, :]
    for half in range(2):
        vt_ref[half * VT_ROWS + HALF:(half + 1) * VT_ROWS, :] = jnp.ones((BF16_SUBLANES, vt_ref.shape[1]), BF16)
    kmean_ref[...] = jnp.concatenate(_split_bf16(jnp.concatenate(kmean, axis=0), 2), axis=0)
    make = functools.partial(_MobaQueryBlock, q_ref=q_ref, z_ref=z_ref, o_ref=o_ref, kb_ref=k_ref.at[0],
                             vt_ref=vt_ref, kmean_ref=kmean_ref, s_ref=s_ref)
    blocks = {0: make(0), 1: make(1)}
    blocks[0].logits_step(0)
    for i in range(nb):
        if i + 2 < nb:
            blocks[i + 2] = make(i + 2)
        cur, nxt = blocks.pop(i), blocks.get(i + 1)
        for j in range(i + 2):
            if j <= i:
                cur.value_step(j)
            if nxt is not None:
                nxt.logits_step(j)
        cur.finish()


def _moba(q, k, v, z):
    b, l, w = q.shape
    blk = MOBA_BLOCK
    nb = l // blk
    spec = pl.BlockSpec((1, l, LANES), lambda bi, hp: (bi, 0, hp))
    return pl.pallas_call(
        _moba_body,
        name="moba",
        grid=(b, w // LANES),
        in_specs=[spec, spec, spec, spec],
        out_specs=spec,
        out_shape=jax.ShapeDtypeStruct((b, l, w), BF16),
        scratch_shapes=[pltpu.VMEM((2 * VT_ROWS, l), BF16),
                        pltpu.VMEM((2 * nb, LANES), BF16),
                        pltpu.VMEM((2, l, 2 * blk), F32)],
        compiler_params=_params("parallel", "parallel"),
    )(q, k, v, z)


def _even_layer(x, norm_w, w_in, conv_w, conv_b, a_log, dt_bias, d_skip, ssd_norm_w, gate_w2, gate_b,
                gla_norm_w, w_out):
    b, l, d = x.shape
    width = ssd_norm_w.shape[0]
    cdim = conv_w.shape[1]
    n_heads = a_log.shape[0]
    rank, kw = gate_w2.shape
    vw = gla_norm_w.shape[0] * GLA_HEADS
    cuts = [0]
    for s in (width, cdim, n_heads, kw, kw, vw, rank, vw):
        cuts.append(cuts[-1] + s)
    w_bf = w_in.astype(BF16)
    seg = lambda j: w_bf[:, cuts[j]:cuts[j + 1]]
    pad = jnp.zeros((d, LANES - n_heads - GATE_COPIES * rank), BF16)
    w_small = jnp.concatenate([seg(2)] + [seg(6)] * GATE_COPIES + [pad], axis=1)
    segments = [(cuts[j], cuts[j + 1] - cuts[j]) for j in (0, 1, 3, 4, 5, 7)]
    x2d = x.reshape(b * l, d)
    z_a, xbc, q_b, k_b, v_b, g_b, small = [
        u.reshape(b, l, -1) for u in _norm_proj(x2d, norm_w, w_bf, segments,
                                                [F32, F32, F32, F32, BF16, F32, F32], extras=[w_small])]
    y_a = _ssd(xbc, small, z_a, conv_w, conv_b, a_log, dt_bias, d_skip, ssd_norm_w)
    o_b = _gla(q_b, k_b, v_b, small, g_b, gate_w2, gate_b, gla_norm_w, lr_lane=n_heads)
    return [y_a.reshape(b * l, width), o_b.reshape(b * l, vw)], w_out.astype(BF16)


def _odd_layer(x, norm_w, w_in, w_out):
    b, l, d = x.shape
    w = w_out.shape[0]
    q, k, v, z = [u.reshape(b, l, w) for u in _norm_proj(x.reshape(b * l, d), norm_w, w_in.astype(BF16),
                                                         [(j * w, w) for j in range(4)], [F32, BF16, BF16, F32])]
    o = _moba(q, k, v, z)
    return [o.reshape(b * l, w)], w_out.astype(BF16)


def kernel(x, even_norm, even_w_in, even_conv_w, even_conv_b, even_a_log, even_dt_bias, even_d_skip, even_ssd_norm,
           even_gate_w2, even_gate_b, even_gla_norm, even_w_out, odd_norm, odd_w_in, odd_w_out, final_norm):
    b, l, d = x.shape
    depth = even_norm.shape[0] + odd_norm.shape[0]
    for layer in range(depth):
        i = layer // 2
        if layer % 2 == 0:
            acts, weights = _even_layer(x, even_norm[i], even_w_in[i], even_conv_w[i], even_conv_b[i], even_a_log[i],
                                        even_dt_bias[i], even_d_skip[i], even_ssd_norm[i], even_gate_w2[i],
                                        even_gate_b[i], even_gla_norm[i], even_w_out[i])
        else:
            acts, weights = _odd_layer(x, odd_norm[i], odd_w_in[i], odd_w_out[i])
        last = layer == depth - 1
        x = _out_proj(acts, x.reshape(b * l, d), weights, final_norm if last else None).reshape(b, l, d)
    return x
```

```python
import functools

import jax
import jax.numpy as jnp
from jax import lax
from jax.experimental import pallas as pl
from jax.experimental.pallas import tpu as pltpu

F32 = jnp.float32
BF16 = jnp.bfloat16
HIGHEST = lax.Precision.HIGHEST

LANES = 128
SUBLANES = 8
BF16_SUBLANES = 16
VMEM_LIMIT_BYTES = 56 * 1024 * 1024

RMS_EPS = 1e-6
SSD_HEAD_DIM = 64
SSD_GROUPS = 2
SSD_STATE = 64
SSD_CONV = 4
SSD_CHUNK = 128
GLA_HEADS = 8
GLA_GATE_NORMALIZER = 16.0
GLA_CHUNK = 64
MOBA_HEAD_DIM = 64
MOBA_BLOCK = 256
MOBA_TOPK = 3
LOG2_E = 1.4426950408889634
CONV_ROW_STRIDE = 4
CUMSUM_TERMS = 3
GATE_COPIES = 3

IN_PROJ_ROWS = 512
OUT_PROJ_ROWS = 1024
GLA_ROWS = 512
SSD_ROWS = 256
HALF = LANES // 2
VT_ROWS = HALF + BF16_SUBLANES


def _params(*sem):
    return pltpu.CompilerParams(dimension_semantics=sem, vmem_limit_bytes=VMEM_LIMIT_BYTES)


def _rms(x, w):
    return x * lax.rsqrt(jnp.mean(x * x, axis=-1, keepdims=True) + RMS_EPS) * w


def _silu(x):
    h = 0.5 * x
    return h + h * jnp.tanh(h)


def _softplus(x):
    return jnp.maximum(x, 0.0) + jnp.log1p(jnp.exp(-jnp.abs(x)))


def _log_sigmoid(x):
    return jnp.minimum(x, 0.0) - jnp.log(1.0 + jnp.exp(-jnp.abs(x)))


def _dot(a, b):
    return jnp.dot(a.astype(BF16), b.astype(BF16), preferred_element_type=F32)


def _dot_nt(a, b):
    return lax.dot_general(a.astype(BF16), b.astype(BF16), (((1,), (1,)), ((), ())), preferred_element_type=F32)


def _dot_tn(a, b):
    return lax.dot_general(a.astype(BF16), b.astype(BF16), (((0,), (0,)), ((), ())), preferred_element_type=F32)


def _split_bf16(x, terms):
    parts = []
    for _ in range(terms):
        p = x.astype(BF16)
        parts.append(p)
        x = x - p.astype(F32)
    return parts


def _chunk_cumsum(x, chunk):
    rows = x.shape[0]
    ri = lax.broadcasted_iota(jnp.int32, (chunk, CUMSUM_TERMS * chunk), 0)
    ci = lax.broadcasted_iota(jnp.int32, (chunk, CUMSUM_TERMS * chunk), 1)
    tri = jnp.where((ci & (chunk - 1)) <= ri, 1.0, 0.0).astype(BF16)
    out = []
    for r0 in range(0, rows, chunk):
        stacked = jnp.concatenate(_split_bf16(x[r0:r0 + chunk, :], CUMSUM_TERMS), axis=0)
        out.append(jnp.dot(tri, stacked, preferred_element_type=F32))
    return out[0] if len(out) == 1 else jnp.concatenate(out, axis=0)


def _resident(shape):
    return pl.BlockSpec(shape, lambda *_: (0,) * len(shape), pipeline_mode=pl.Buffered(1))


def _norm_proj_body(x_ref, nw_ref, w_ref, *refs, offsets, n_extra):
    extra_refs, o_refs = refs[:n_extra], refs[n_extra:n_extra + len(offsets) + n_extra]
    shifted = [i for i, off in enumerate(offsets) if off % LANES]
    starts, pos = {}, 0
    for i in shifted:
        starts[i] = pos
        pos += o_refs[i].shape[1]

    if shifted:
        al_ref = refs[-1]

        @pl.when(pl.program_id(0) == 0)
        def _():
            for i in shifted:
                width = o_refs[i].shape[1]
                al_ref[:, starts[i]:starts[i] + width] = w_ref[:, offsets[i]:offsets[i] + width]

    h = _rms(x_ref[...], nw_ref[...]).astype(BF16)
    for i, off in enumerate(offsets):
        width = o_refs[i].shape[1]
        w = al_ref[:, starts[i]:starts[i] + width] if i in starts else w_ref[:, off:off + width]
        o_refs[i][...] = jnp.dot(h, w, preferred_element_type=F32).astype(o_refs[i].dtype)
    for e_ref, o_ref in zip(extra_refs, o_refs[len(offsets):]):
        o_ref[...] = jnp.dot(h, e_ref[...], preferred_element_type=F32).astype(o_ref.dtype)


def _norm_proj(x2d, norm_w, weight, segments, out_dtypes, extras=()):
    m, d = x2d.shape
    tm = IN_PROJ_ROWS
    assert all(w % LANES == 0 for _, w in segments)
    widths = [w for _, w in segments] + [e.shape[1] for e in extras]
    shifted_cols = sum(w for off, w in segments if off % LANES)
    return pl.pallas_call(
        functools.partial(_norm_proj_body, offsets=tuple(off for off, _ in segments), n_extra=len(extras)),
        name="norm_proj",
        grid=(m // tm,),
        in_specs=[pl.BlockSpec((tm, d), lambda i: (i, 0)), _resident((1, d)), _resident(weight.shape)]
        + [_resident(e.shape) for e in extras],
        out_specs=[pl.BlockSpec((tm, w), lambda i: (i, 0)) for w in widths],
        out_shape=[jax.ShapeDtypeStruct((m, w), dt) for w, dt in zip(widths, out_dtypes)],
        scratch_shapes=[pltpu.VMEM((d, shifted_cols), weight.dtype)] if shifted_cols else [],
        compiler_params=_params("arbitrary"),
    )(x2d, norm_w.reshape(1, d), weight, *extras)


def _ssd_body(xbc_ref, sm_ref, z_ref, cw_ref, cb_ref, alog_ref, dtb_ref, dskip_ref, nw_ref,
              y_ref, h_ref, xpad_ref, act_ref, xw_ref, ycat_ref):
    q = SSD_CHUNK
    t = xbc_ref.shape[1]
    width = y_ref.shape[-1]
    n_pairs = width // LANES
    n_slabs = xpad_ref.shape[0]
    gstate = SSD_GROUPS * SSD_STATE
    gwidth = width // SSD_GROUPS
    first = pl.program_id(1) == 0

    @pl.when(first)
    def _():
        xpad_ref[:, 0:SUBLANES, :] = jnp.zeros((n_slabs, SUBLANES, LANES), F32)
        h_ref[...] = jnp.zeros_like(h_ref)

    @pl.when(jnp.logical_not(first))
    def _():
        xpad_ref[:, 0:SUBLANES, :] = xpad_ref[:, t:t + SUBLANES, :]

    for c in range(n_slabs):
        lanes = slice(c * LANES, (c + 1) * LANES)
        xpad_ref[c, SUBLANES:SUBLANES + t, :] = xbc_ref[0, :, lanes]
        taps = [jnp.broadcast_to(cw_ref[k:k + 1, lanes], (SUBLANES, LANES)) for k in range(SSD_CONV)]
        bias = jnp.broadcast_to(cb_ref[:, lanes], (SUBLANES, LANES))
        for t0 in range(0, t, SUBLANES * CONV_ROW_STRIDE):
            for g in range(CONV_ROW_STRIDE):
                conv = bias
                for k in range(SSD_CONV):
                    start = SUBLANES + t0 + g - (SSD_CONV - 1 - k)
                    conv = conv + taps[k] * xpad_ref[c, pl.ds(start, SUBLANES, stride=CONV_ROW_STRIDE), :]
                act_ref[c, pl.ds(t0 + g, SUBLANES, stride=CONV_ROW_STRIDE), :] = _silu(conv)

    dt_all = _softplus(sm_ref[0] + dtb_ref[...])
    dta = dt_all * (-jnp.exp(alog_ref[...]) * LOG2_E)
    a_cs_all = _chunk_cumsum(dta, q)
    ri = lax.broadcasted_iota(jnp.int32, (q, q), 0)
    ci = lax.broadcasted_iota(jnp.int32, (q, q), 1)
    causal = ri >= ci
    lane = lax.broadcasted_iota(jnp.int32, (q, LANES), 1)
    lo = lane < HALF
    lo_row = lo[0:1, :]

    h_prev = h_ref[...]
    for rows in [slice(r0, r0 + q) for r0 in range(0, t, q)]:
        bm = act_ref[n_pairs, rows, :]
        cm = act_ref[n_pairs + 1, rows, :]
        dt = dt_all[rows, :]
        a_cs = a_cs_all[rows, :]
        a_cs_t = a_cs.T
        a_last = a_cs[q - 1:q, :]
        bm_t = bm.T
        cm_g = [jnp.where(lo, cm, 0.0), jnp.where(lo, 0.0, cm)]
        cb = [_dot_nt(c, bm) for c in cm_g]
        y_off = [_dot(c, h_prev) for c in cm_g]

        a_last_pairs = []
        for p in range(n_pairs):
            e0, e1 = 2 * p, 2 * p + 1
            g = (p * LANES) // gwidth
            col = slice(p * LANES, (p + 1) * LANES)
            gcol = slice(p * LANES - g * gwidth, (p + 1) * LANES - g * gwidth)
            acs_pair = jnp.where(lo, a_cs[:, e0:e0 + 1], a_cs[:, e1:e1 + 1])
            dt_pair = jnp.where(lo, dt[:, e0:e0 + 1], dt[:, e1:e1 + 1])
            al_pair = jnp.where(lo_row, a_last[:, e0:e0 + 1], a_last[:, e1:e1 + 1])
            a_last_pairs.append(al_pair)
            xs2 = act_ref[p, rows, :]
            xdt = xs2 * dt_pair
            xdt_b = xdt.astype(BF16)
            yd = []
            for e in (e0, e1):
                seg = a_cs[:, e:e + 1] - a_cs_t[e:e + 1, :]
                decay = jnp.exp2(jnp.where(causal, seg, -jnp.inf))
                yd.append(_dot(cb[g] * decay, xdt_b))
            y2 = jnp.where(lo, yd[0], yd[1])
            y2 = y2 + y_off[g][:, gcol] * jnp.exp2(acs_pair) + dskip_ref[:, col] * xs2
            ycat_ref[rows, col] = y2
            xw_ref[rows, col] = (xdt * jnp.exp2(al_pair - acs_pair)).astype(BF16)

        h_next = []
        for g in range(SSD_GROUPS):
            srows = slice(g * SSD_STATE, (g + 1) * SSD_STATE)
            ppg = n_pairs // SSD_GROUPS
            dec = jnp.exp2(jnp.concatenate(a_last_pairs[g * ppg:(g + 1) * ppg], axis=1))
            s_g = _dot(bm_t[srows, :], xw_ref[rows, g * gwidth:(g + 1) * gwidth])
            h_next.append(h_prev[srows, :] * dec + s_g)
        h_prev = jnp.concatenate(h_next, axis=0)
    h_ref[...] = h_prev

    y_ref[0] = _rms(ycat_ref[...] * _silu(z_ref[0]), nw_ref[...]).astype(y_ref.dtype)


def _ssd(xbc, small, z, conv_w, conv_b, a_log, dt_bias, d_skip, norm_w):
    b, l, cdim = xbc.shape
    width = z.shape[-1]
    q = SSD_ROWS
    n_heads = a_log.shape[0]
    pad = lambda v: jnp.pad(v, (0, LANES - n_heads)).reshape(1, LANES)
    gwidth = width // SSD_GROUPS
    blk = lambda w: pl.BlockSpec((1, q, w), lambda i, c: (i, c, 0))
    return pl.pallas_call(
        _ssd_body,
        name="ssd",
        grid=(b, l // q),
        in_specs=[blk(cdim), blk(LANES), blk(width),
                  _resident((SSD_CONV, cdim)), _resident((1, cdim)), _resident((1, LANES)),
                  _resident((1, LANES)), _resident((1, width)), _resident((1, width))],
        out_specs=blk(width),
        out_shape=jax.ShapeDtypeStruct((b, l, width), BF16),
        scratch_shapes=[pltpu.VMEM((SSD_GROUPS * SSD_STATE, gwidth), F32),
                        pltpu.VMEM((cdim // LANES, q + SUBLANES, LANES), F32),
                        pltpu.VMEM((cdim // LANES, q, LANES), F32),
                        pltpu.VMEM((q, width), BF16),
                        pltpu.VMEM((q, width), F32)],
        compiler_params=_params("parallel", "arbitrary"),
    )(xbc, small, z, conv_w, conv_b.reshape(1, cdim), pad(a_log), pad(dt_bias),
      jnp.repeat(d_skip, SSD_HEAD_DIM).reshape(1, width), norm_w.reshape(1, width))


def _gla_body(q_ref, k_ref, v_ref, sm_ref, g_ref, w2_ref, gb_ref, nw_ref, o_ref, st_ref, *, mid_lane):
    c = GLA_CHUNK
    kw = q_ref.shape[-1]
    dk = kw // GLA_HEADS
    dv = v_ref.shape[-1] // GLA_HEADS
    n_pairs = kw // LANES

    @pl.when(pl.program_id(1) == 0)
    def _():
        st_ref[...] = jnp.zeros_like(st_ref)

    ri = lax.broadcasted_iota(jnp.int32, (c, c), 0)
    ci = lax.broadcasted_iota(jnp.int32, (c, c), 1)
    causal = ri >= ci
    lo = lax.broadcasted_iota(jnp.int32, (c, LANES), 1) < HALF
    lo_sq = lax.broadcasted_iota(jnp.int32, (dv, LANES), 1) < HALF

    sm_hi, sm_mid = _split_bf16(sm_ref[0], 2)
    sm_lane = lax.broadcasted_iota(jnp.int32, sm_hi.shape, 1)
    pre = jnp.dot(jnp.where(sm_lane < mid_lane, sm_hi, sm_mid), w2_ref[...],
                  preferred_element_type=F32) + gb_ref[...]
    gcs_all = _chunk_cumsum(_log_sigmoid(pre) * (LOG2_E / GLA_GATE_NORMALIZER), c)

    chunks = [slice(ch * c, (ch + 1) * c) for ch in range(q_ref.shape[1] // c)]
    pairs = [slice(p * LANES, (p + 1) * LANES) for p in range(n_pairs)]
    causal2 = jnp.concatenate([causal, causal], axis=0)

    def by_head(x):
        return jnp.concatenate([jnp.where(lo, x, 0.0), jnp.where(lo, 0.0, x)], axis=0).astype(BF16)

    q_in, k_in, q_st, k_st, g_last = [], [], [], [], []
    for rows in chunks:
        gcs = gcs_all[rows, :]
        g_mid = gcs[c // 2:c // 2 + 1, :]
        g_last.append(gcs[c - 1:c, :])
        qs = q_ref[0, rows, :] * (dk ** -0.5)
        ks = k_ref[0, rows, :]
        q_in.append(qs * jnp.exp2(gcs - g_mid))
        k_in.append((ks * jnp.exp2(g_mid - gcs)).astype(BF16))
        q_st.append(qs * jnp.exp2(gcs))
        k_st.append((ks * jnp.exp2(g_last[-1] - gcs)).astype(BF16))

    scores = [[_dot_nt(by_head(q_in[ch][:, col]), k_in[ch][:, col]) for col in pairs]
              for ch in range(len(chunks))]
    scores = [[jnp.where(causal2, s, 0.0).astype(BF16) for s in row] for row in scores]

    o_intra, kv_t = [], []
    for ch, rows in enumerate(chunks):
        o_row, kv_row = [], []
        for p, col in enumerate(pairs):
            v_a = v_ref[0, rows, (2 * p) * dv:(2 * p + 1) * dv].astype(BF16)
            v_b = v_ref[0, rows, (2 * p + 1) * dv:(2 * p + 2) * dv].astype(BF16)
            o_row.append((jnp.dot(scores[ch][p][0:c, :], v_a, preferred_element_type=F32),
                          jnp.dot(scores[ch][p][c:2 * c, :], v_b, preferred_element_type=F32)))
            kv_row.append(jnp.where(lo_sq, _dot_tn(v_a, k_st[ch][:, col]), _dot_tn(v_b, k_st[ch][:, col])))
        o_intra.append(o_row)
        kv_t.append(kv_row)

    o_inter = [[None] * n_pairs for _ in chunks]
    for p, col in enumerate(pairs):
        st = st_ref[:, col]
        for ch in range(len(chunks)):
            o_inter[ch][p] = _dot_nt(by_head(q_st[ch][:, col]), st)
            st = st * jnp.exp2(g_last[ch][:, col]) + kv_t[ch][p]
        st_ref[:, col] = st

    for ch, rows in enumerate(chunks):
        for p in range(n_pairs):
            for half in range(2):
                vcol = slice((2 * p + half) * dv, (2 * p + half + 1) * dv)
                o = o_intra[ch][p][half] + o_inter[ch][p][half * c:(half + 1) * c, :]
                o_ref[0, rows, vcol] = (_rms(o, nw_ref[...]) * _silu(g_ref[0, rows, vcol])).astype(o_ref.dtype)


def _gla(q, k, v, small, g, gate_w2, gate_b, norm_w, lr_lane):
    b, l, kw = q.shape
    vw = v.shape[-1]
    rank = gate_w2.shape[0]
    t = GLA_ROWS
    w2_hi, w2_mid = _split_bf16(gate_w2, 2)
    w2_pad = jnp.zeros((LANES, kw), BF16)
    for copy, part in enumerate((w2_hi, w2_mid, w2_hi)):
        w2_pad = w2_pad.at[lr_lane + copy * rank:lr_lane + (copy + 1) * rank, :].set(part)
    blk = lambda w: pl.BlockSpec((1, t, w), lambda i, c: (i, c, 0))
    return pl.pallas_call(
        functools.partial(_gla_body, mid_lane=lr_lane + 2 * rank),
        name="gla",
        grid=(b, l // t),
        in_specs=[blk(kw), blk(kw), blk(vw), blk(LANES), blk(vw),
                  _resident((LANES, kw)), _resident((1, kw)), _resident((1, vw // GLA_HEADS))],
        out_specs=blk(vw),
        out_shape=jax.ShapeDtypeStruct((b, l, vw), BF16),
        scratch_shapes=[pltpu.VMEM((vw // GLA_HEADS, kw), F32)],
        compiler_params=_params("parallel", "arbitrary"),
    )(q, k, v, small, g, w2_pad, gate_b.reshape(1, kw), norm_w.reshape(1, -1))


def _out_proj_body(*refs, n_in, final_norm):
    a_refs, x_ref, w_ref = refs[:n_in], refs[n_in], refs[n_in + 1]
    rest = refs[n_in + 2:]
    acc = x_ref[...]
    off = 0
    for a_ref in a_refs:
        width = a_ref.shape[1]
        acc = acc + jnp.dot(a_ref[...].astype(BF16), w_ref[off:off + width, :], preferred_element_type=F32)
        off += width
    if final_norm:
        fw_ref, o_ref = rest
        o_ref[...] = _rms(acc, fw_ref[...])
    else:
        (o_ref,) = rest
        o_ref[...] = acc


def _out_proj(acts, x2d, weight, final_w=None):
    m, d = x2d.shape
    tm = OUT_PROJ_ROWS
    n_in = len(acts)
    assert sum(a.shape[1] for a in acts) == weight.shape[0]
    row = lambda w: pl.BlockSpec((tm, w), lambda i: (i, 0))
    in_specs = [row(a.shape[1]) for a in acts] + [row(d), _resident(weight.shape)]
    args = list(acts) + [x2d, weight]
    if final_w is not None:
        in_specs.append(_resident((1, d)))
        args.append(final_w.reshape(1, d))
    return pl.pallas_call(
        functools.partial(_out_proj_body, n_in=n_in, final_norm=final_w is not None),
        name="out_proj",
        grid=(m // tm,),
        in_specs=in_specs,
        out_specs=row(d),
        out_shape=jax.ShapeDtypeStruct((m, d), F32),
        compiler_params=_params("parallel"),
    )(*args)


class _MobaQueryBlock:
    def __init__(self, n_past, q_ref, z_ref, o_ref, kb_ref, vt_ref, kmean_ref, s_ref):
        blk = MOBA_BLOCK
        self.n_past, self.z_ref, self.o_ref, self.kb_ref, self.vt_ref = n_past, z_ref, o_ref, kb_ref, vt_ref
        self.own = slice(n_past * blk, (n_past + 1) * blk)
        self.s_ref = s_ref.at[n_past % s_ref.shape[0]]
        q2 = q_ref[0, self.own, :]
        lo = lax.broadcasted_iota(jnp.int32, (blk, LANES), 1) < HALF
        q_cat = jnp.concatenate([jnp.where(lo, q2, 0.0), jnp.where(lo, 0.0, q2)], axis=0)
        scale = MOBA_HEAD_DIM ** -0.5
        key_i = lax.broadcasted_iota(jnp.int32, (blk, 2 * blk), 0)
        qry_i = lax.broadcasted_iota(jnp.int32, (blk, 2 * blk), 1)
        causal_t = key_i <= jnp.where(qry_i < blk, qry_i, qry_i - blk)

        self.masks = []
        if n_past:
            nb = kmean_ref.shape[0] // 2
            parts = [_dot_nt(kmean_ref[...], q_part) for q_part in _split_bf16(q_cat * scale, 2)]
            gate = (parts[0][0:nb] + parts[0][nb:]) + (parts[1][0:nb] + parts[1][nb:])
            g_rows = [gate[n:n + 1, :] for n in range(n_past)]
            for n in range(n_past):
                rank = jnp.zeros((1, 2 * blk), F32)
                for m in range(n_past):
                    if m != n:
                        ahead = (g_rows[m] >= g_rows[n]) if m < n else (g_rows[m] > g_rows[n])
                        rank = rank + jnp.where(ahead, 1.0, 0.0)
                self.masks.append(rank < MOBA_TOPK)
        self.masks.append(causal_t)
        self.q_s = (q_cat * (scale * LOG2_E)).astype(BF16)
        self.m_run = None
        self.p = []

    def logits_step(self, j):
        rows = slice(j * MOBA_BLOCK, (j + 1) * MOBA_BLOCK)
        sj = jnp.where(self.masks[j], _dot_nt(self.kb_ref[rows, :], self.q_s), -jnp.inf)
        self.s_ref[rows, :] = sj
        mj = jnp.max(sj, axis=0, keepdims=True)
        self.m_run = mj if self.m_run is None else jnp.maximum(self.m_run, mj)

    def value_step(self, j):
        rows = slice(j * MOBA_BLOCK, (j + 1) * MOBA_BLOCK)
        self.p.append(jnp.exp2(self.s_ref[rows, :] - self.m_run).astype(BF16))

    def finish(self):
        blk = MOBA_BLOCK
        nk = len(self.p) * blk
        p_all = jnp.concatenate(self.p, axis=0)
        outs = []
        for half in range(2):
            vt = self.vt_ref[half * VT_ROWS:(half + 1) * VT_ROWS, 0:nk]
            acc = jnp.dot(vt, p_all[:, half * blk:(half + 1) * blk], preferred_element_type=F32)
            outs.append(acc[0:HALF, :] / acc[HALF:HALF + 1, :])
        o_t = jnp.concatenate(outs, axis=0)
        self.o_ref[0, self.own, :] = (o_t.T * _silu(self.z_ref[0, self.own, :])).astype(self.o_ref.dtype)


def _moba_body(q_ref, k_ref, v_ref, z_ref, o_ref, vt_ref, kmean_ref, s_ref):
    blk = MOBA_BLOCK
    nb = k_ref.shape[1] // blk
    kmean = []
    for n in range(nb):
        rows = slice(n * blk, (n + 1) * blk)
        kmean.append(jnp.mean(k_ref[0, rows, :].astype(F32), axis=0, keepdims=True))
        v_t = v_ref[0, rows, :].astype(F32).T.astype(BF16)
        for half in range(2):
            vt_ref[half * VT_ROWS:half * VT_ROWS + HALF, rows] = v_t[half * HALF:(half + 1) * HALF, :]
    for half in range(2):
        vt_ref[half * VT_ROWS + HALF:(half + 1) * VT_ROWS, :] = jnp.ones((BF16_SUBLANES, vt_ref.shape[1]), BF16)
    kmean_ref[...] = jnp.concatenate(_split_bf16(jnp.concatenate(kmean, axis=0), 2), axis=0)
    make = functools.partial(_MobaQueryBlock, q_ref=q_ref, z_ref=z_ref, o_ref=o_ref, kb_ref=k_ref.at[0],
                             vt_ref=vt_ref, kmean_ref=kmean_ref, s_ref=s_ref)
    blocks = {0: make(0), 1: make(1)}
    blocks[0].logits_step(0)
    for i in range(nb):
        if i + 2 < nb:
            blocks[i + 2] = make(i + 2)
        cur, nxt = blocks.pop(i), blocks.get(i + 1)
        for j in range(i + 2):
            if j <= i:
                cur.value_step(j)
            if nxt is not None:
                nxt.logits_step(j)
        cur.finish()


def _moba(q, k, v, z):
    b, l, w = q.shape
    blk = MOBA_BLOCK
    nb = l // blk
    spec = pl.BlockSpec((1, l, LANES), lambda bi, hp: (bi, 0, hp))
    return pl.pallas_call(
        _moba_body,
        name="moba",
        grid=(b, w // LANES),
        in_specs=[spec, spec, spec, spec],
        out_specs=spec,
        out_shape=jax.ShapeDtypeStruct((b, l, w), BF16),
        scratch_shapes=[pltpu.VMEM((2 * VT_ROWS, l), BF16),
                        pltpu.VMEM((2 * nb, LANES), BF16),
                        pltpu.VMEM((2, l, 2 * blk), F32)],
        compiler_params=_params("parallel", "parallel"),
    )(q, k, v, z)


def _even_layer(x, norm_w, w_in, conv_w, conv_b, a_log, dt_bias, d_skip, ssd_norm_w, gate_w2, gate_b,
                gla_norm_w, w_out):
    b, l, d = x.shape
    width = ssd_norm_w.shape[0]
    cdim = conv_w.shape[1]
    n_heads = a_log.shape[0]
    rank, kw = gate_w2.shape
    vw = gla_norm_w.shape[0] * GLA_HEADS
    cuts = [0]
    for s in (width, cdim, n_heads, kw, kw, vw, rank, vw):
        cuts.append(cuts[-1] + s)
    w_bf = w_in.astype(BF16)
    seg = lambda j: w_bf[:, cuts[j]:cuts[j + 1]]
    pad = jnp.zeros((d, LANES - n_heads - GATE_COPIES * rank), BF16)
    w_small = jnp.concatenate([seg(2)] + [seg(6)] * GATE_COPIES + [pad], axis=1)
    segments = [(cuts[j], cuts[j + 1] - cuts[j]) for j in (0, 1, 3, 4, 5, 7)]
    x2d = x.reshape(b * l, d)
    z_a, xbc, q_b, k_b, v_b, g_b, small = [
        u.reshape(b, l, -1) for u in _norm_proj(x2d, norm_w, w_bf, segments,
                                                [F32, F32, F32, F32, BF16, F32, F32], extras=[w_small])]
    y_a = _ssd(xbc, small, z_a, conv_w, conv_b, a_log, dt_bias, d_skip, ssd_norm_w)
    o_b = _gla(q_b, k_b, v_b, small, g_b, gate_w2, gate_b, gla_norm_w, lr_lane=n_heads)
    return [y_a.reshape(b * l, width), o_b.reshape(b * l, vw)], w_out.astype(BF16)


def _odd_layer(x, norm_w, w_in, w_out):
    b, l, d = x.shape
    w = w_out.shape[0]
    q, k, v, z = [u.reshape(b, l, w) for u in _norm_proj(x.reshape(b * l, d), norm_w, w_in.astype(BF16),
                                                         [(j * w, w) for j in range(4)], [F32, BF16, BF16, F32])]
    o = _moba(q, k, v, z)
    return [o.reshape(b * l, w)], w_out.astype(BF16)


def kernel(x, even_norm, even_w_in, even_conv_w, even_conv_b, even_a_log, even_dt_bias, even_d_skip, even_ssd_norm,
           even_gate_w2, even_gate_b, even_gla_norm, even_w_out, odd_norm, odd_w_in, odd_w_out, final_norm):
    b, l, d = x.shape
    depth = even_norm.shape[0] + odd_norm.shape[0]
    for layer in range(depth):
        i = layer // 2
        if layer % 2 == 0:
            acts, weights = _even_layer(x, even_norm[i], even_w_in[i], even_conv_w[i], even_conv_b[i], even_a_log[i],
                                        even_dt_bias[i], even_d_skip[i], even_ssd_norm[i], even_gate_w2[i],
                                        even_gate_b[i], even_gla_norm[i], even_w_out[i])
        else:
            acts, weights = _odd_layer(x, odd_norm[i], odd_w_in[i], odd_w_out[i])
        last = layer == depth - 1
        x = _out_proj(acts, x.reshape(b * l, d), weights, final_norm if last else None).reshape(b, l, d)
    return x
```

```python
import functools

import jax
import jax.numpy as jnp
from jax import lax
from jax.experimental import pallas as pl
from jax.experimental.pallas import tpu as pltpu

F32 = jnp.float32
BF16 = jnp.bfloat16
HIGHEST = lax.Precision.HIGHEST

LANES = 128
SUBLANES = 8
BF16_SUBLANES = 16
VMEM_LIMIT_BYTES = 56 * 1024 * 1024

RMS_EPS = 1e-6
SSD_HEAD_DIM = 64
SSD_GROUPS = 2
SSD_STATE = 64
SSD_CONV = 4
SSD_CHUNK = 128
GLA_HEADS = 8
GLA_GATE_NORMALIZER = 16.0
GLA_CHUNK = 64
MOBA_HEAD_DIM = 64
MOBA_BLOCK = 256
MOBA_TOPK = 3
LOG2_E = 1.4426950408889634
CONV_ROW_STRIDE = 4
CUMSUM_TERMS = 3
GATE_COPIES = 3

IN_PROJ_ROWS = 512
OUT_PROJ_ROWS = 1024
GLA_ROWS = 512
SSD_ROWS = 512
HALF = LANES // 2
VT_ROWS = HALF + BF16_SUBLANES


def _params(*sem):
    return pltpu.CompilerParams(dimension_semantics=sem, vmem_limit_bytes=VMEM_LIMIT_BYTES)


def _rms(x, w):
    return x * lax.rsqrt(jnp.mean(x * x, axis=-1, keepdims=True) + RMS_EPS) * w


def _silu(x):
    h = 0.5 * x
    return h + h * jnp.tanh(h)


def _softplus(x):
    return jnp.maximum(x, 0.0) + jnp.log1p(jnp.exp(-jnp.abs(x)))


def _log_sigmoid(x):
    return jnp.minimum(x, 0.0) - jnp.log(1.0 + jnp.exp(-jnp.abs(x)))


def _dot(a, b):
    return jnp.dot(a.astype(BF16), b.astype(BF16), preferred_element_type=F32)


def _dot_nt(a, b):
    return lax.dot_general(a.astype(BF16), b.astype(BF16), (((1,), (1,)), ((), ())), preferred_element_type=F32)


def _dot_tn(a, b):
    return lax.dot_general(a.astype(BF16), b.astype(BF16), (((0,), (0,)), ((), ())), preferred_element_type=F32)


def _split_bf16(x, terms):
    parts = []
    for _ in range(terms):
        p = x.astype(BF16)
        parts.append(p)
        x = x - p.astype(F32)
    return parts


def _chunk_cumsum(x, chunk):
    rows = x.shape[0]
    ri = lax.broadcasted_iota(jnp.int32, (chunk, CUMSUM_TERMS * chunk), 0)
    ci = lax.broadcasted_iota(jnp.int32, (chunk, CUMSUM_TERMS * chunk), 1)
    tri = jnp.where((ci & (chunk - 1)) <= ri, 1.0, 0.0).astype(BF16)
    out = []
    for r0 in range(0, rows, chunk):
        stacked = jnp.concatenate(_split_bf16(x[r0:r0 + chunk, :], CUMSUM_TERMS), axis=0)
        out.append(jnp.dot(tri, stacked, preferred_element_type=F32))
    return out[0] if len(out) == 1 else jnp.concatenate(out, axis=0)


def _resident(shape):
    return pl.BlockSpec(shape, lambda *_: (0,) * len(shape), pipeline_mode=pl.Buffered(1))


def _norm_proj_body(x_ref, nw_ref, w_ref, *refs, offsets, n_extra, n_side):
    n_out = len(offsets) + n_extra
    extra_refs, side_in = refs[:n_extra], refs[n_extra:n_extra + n_side]
    o_refs = refs[n_extra + n_side:n_extra + n_side + n_out]
    side_out = refs[n_extra + n_side + n_out:n_extra + 2 * n_side + n_out]
    for s_in, s_out in zip(side_in, side_out):
        s_out[...] = s_in[...].astype(s_out.dtype)
    shifted = [i for i, off in enumerate(offsets) if off % LANES]
    starts, pos = {}, 0
    for i in shifted:
        starts[i] = pos
        pos += o_refs[i].shape[1]

    if shifted:
        al_ref = refs[-1]

        @pl.when(pl.program_id(0) == 0)
        def _():
            for i in shifted:
                width = o_refs[i].shape[1]
                al_ref[:, starts[i]:starts[i] + width] = w_ref[:, offsets[i]:offsets[i] + width]

    h = _rms(x_ref[...], nw_ref[...]).astype(BF16)
    for i, off in enumerate(offsets):
        width = o_refs[i].shape[1]
        w = al_ref[:, starts[i]:starts[i] + width] if i in starts else w_ref[:, off:off + width]
        o_refs[i][...] = jnp.dot(h, w, preferred_element_type=F32).astype(o_refs[i].dtype)
    for e_ref, o_ref in zip(extra_refs, o_refs[len(offsets):]):
        o_ref[...] = jnp.dot(h, e_ref[...], preferred_element_type=F32).astype(o_ref.dtype)


def _norm_proj(x2d, norm_w, weight, segments, out_dtypes, extras=(), side_casts=()):
    m, d = x2d.shape
    tm = IN_PROJ_ROWS
    steps = m // tm
    assert all(w % LANES == 0 for _, w in segments)
    assert all(s.shape[0] % (steps * BF16_SUBLANES) == 0 for s in side_casts)
    widths = [w for _, w in segments] + [e.shape[1] for e in extras]
    shifted_cols = sum(w for off, w in segments if off % LANES)
    side_specs = [pl.BlockSpec((s.shape[0] // steps, s.shape[1]), lambda i: (i, 0)) for s in side_casts]
    return pl.pallas_call(
        functools.partial(_norm_proj_body, offsets=tuple(off for off, _ in segments), n_extra=len(extras),
                          n_side=len(side_casts)),
        name="norm_proj",
        grid=(steps,),
        in_specs=[pl.BlockSpec((tm, d), lambda i: (i, 0)), _resident((1, d)), _resident(weight.shape)]
        + [_resident(e.shape) for e in extras] + side_specs,
        out_specs=[pl.BlockSpec((tm, w), lambda i: (i, 0)) for w in widths] + side_specs,
        out_shape=[jax.ShapeDtypeStruct((m, w), dt) for w, dt in zip(widths, out_dtypes)]
        + [jax.ShapeDtypeStruct(s.shape, BF16) for s in side_casts],
        scratch_shapes=[pltpu.VMEM((d, shifted_cols), weight.dtype)] if shifted_cols else [],
        compiler_params=_params("arbitrary"),
    )(x2d, norm_w.reshape(1, d), weight, *extras, *side_casts)


def _ssd_body(xbc_ref, sm_ref, z_ref, cw_ref, cb_ref, alog_ref, dtb_ref, dskip_ref, nw_ref,
              y_ref, h_ref, xpad_ref, act_ref, xw_ref, ycat_ref):
    q = SSD_CHUNK
    t = xbc_ref.shape[1]
    width = y_ref.shape[-1]
    n_pairs = width // LANES
    n_slabs = xpad_ref.shape[0]
    gstate = SSD_GROUPS * SSD_STATE
    gwidth = width // SSD_GROUPS
    first = pl.program_id(1) == 0

    @pl.when(first)
    def _():
        xpad_ref[:, 0:SUBLANES, :] = jnp.zeros((n_slabs, SUBLANES, LANES), F32)
        h_ref[...] = jnp.zeros_like(h_ref)

    @pl.when(jnp.logical_not(first))
    def _():
        xpad_ref[:, 0:SUBLANES, :] = xpad_ref[:, t:t + SUBLANES, :]

    for c in range(n_slabs):
        lanes = slice(c * LANES, (c + 1) * LANES)
        xpad_ref[c, SUBLANES:SUBLANES + t, :] = xbc_ref[0, :, lanes]
        taps = [jnp.broadcast_to(cw_ref[k:k + 1, lanes], (SUBLANES, LANES)) for k in range(SSD_CONV)]
        bias = jnp.broadcast_to(cb_ref[:, lanes], (SUBLANES, LANES))
        for t0 in range(0, t, SUBLANES * CONV_ROW_STRIDE):
            for g in range(CONV_ROW_STRIDE):
                conv = bias
                for k in range(SSD_CONV):
                    start = SUBLANES + t0 + g - (SSD_CONV - 1 - k)
                    conv = conv + taps[k] * xpad_ref[c, pl.ds(start, SUBLANES, stride=CONV_ROW_STRIDE), :]
                act_ref[c, pl.ds(t0 + g, SUBLANES, stride=CONV_ROW_STRIDE), :] = _silu(conv)

    dt_all = _softplus(sm_ref[0] + dtb_ref[...])
    dta = dt_all * (-jnp.exp(alog_ref[...]) * LOG2_E)
    a_cs_all = _chunk_cumsum(dta, q)
    ri = lax.broadcasted_iota(jnp.int32, (q, q), 0)
    ci = lax.broadcasted_iota(jnp.int32, (q, q), 1)
    causal = ri >= ci
    lane = lax.broadcasted_iota(jnp.int32, (q, LANES), 1)
    lo = lane < HALF
    lo_row = lo[0:1, :]

    h_prev = h_ref[...]
    for rows in [slice(r0, r0 + q) for r0 in range(0, t, q)]:
        bm = act_ref[n_pairs, rows, :]
        cm = act_ref[n_pairs + 1, rows, :]
        dt = dt_all[rows, :]
        a_cs = a_cs_all[rows, :]
        a_cs_t = a_cs.T
        a_last = a_cs[q - 1:q, :]
        bm_t = bm.T
        cm_g = [jnp.where(lo, cm, 0.0), jnp.where(lo, 0.0, cm)]
        cb = [_dot_nt(c, bm) for c in cm_g]
        y_off = [_dot(c, h_prev) for c in cm_g]

        a_last_pairs = []
        for p in range(n_pairs):
            e0, e1 = 2 * p, 2 * p + 1
            g = (p * LANES) // gwidth
            col = slice(p * LANES, (p + 1) * LANES)
            gcol = slice(p * LANES - g * gwidth, (p + 1) * LANES - g * gwidth)
            acs_pair = jnp.where(lo, a_cs[:, e0:e0 + 1], a_cs[:, e1:e1 + 1])
            dt_pair = jnp.where(lo, dt[:, e0:e0 + 1], dt[:, e1:e1 + 1])
            al_pair = jnp.where(lo_row, a_last[:, e0:e0 + 1], a_last[:, e1:e1 + 1])
            a_last_pairs.append(al_pair)
            xs2 = act_ref[p, rows, :]
            xdt = xs2 * dt_pair
            xdt_b = xdt.astype(BF16)
            yd = []
            for e in (e0, e1):
                seg = a_cs[:, e:e + 1] - a_cs_t[e:e + 1, :]
                decay = jnp.exp2(jnp.where(causal, seg, -jnp.inf))
                yd.append(_dot(cb[g] * decay, xdt_b))
            y2 = jnp.where(lo, yd[0], yd[1])
            y2 = y2 + y_off[g][:, gcol] * jnp.exp2(acs_pair) + dskip_ref[:, col] * xs2
            ycat_ref[rows, col] = y2
            xw_ref[rows, col] = (xdt * jnp.exp2(al_pair - acs_pair)).astype(BF16)

        h_next = []
        for g in range(SSD_GROUPS):
            srows = slice(g * SSD_STATE, (g + 1) * SSD_STATE)
            ppg = n_pairs // SSD_GROUPS
            dec = jnp.exp2(jnp.concatenate(a_last_pairs[g * ppg:(g + 1) * ppg], axis=1))
            s_g = _dot(bm_t[srows, :], xw_ref[rows, g * gwidth:(g + 1) * gwidth])
            h_next.append(h_prev[srows, :] * dec + s_g)
        h_prev = jnp.concatenate(h_next, axis=0)
    h_ref[...] = h_prev

    y_ref[0] = _rms(ycat_ref[...] * _silu(z_ref[0]), nw_ref[...]).astype(y_ref.dtype)


def _ssd(xbc, small, z, conv_w, conv_b, a_log, dt_bias, d_skip, norm_w):
    b, l, cdim = xbc.shape
    width = z.shape[-1]
    q = SSD_ROWS
    n_heads = a_log.shape[0]
    pad = lambda v: jnp.pad(v, (0, LANES - n_heads)).reshape(1, LANES)
    gwidth = width // SSD_GROUPS
    blk = lambda w: pl.BlockSpec((1, q, w), lambda i, c: (i, c, 0))
    return pl.pallas_call(
        _ssd_body,
        name="ssd",
        grid=(b, l // q),
        in_specs=[blk(cdim), blk(LANES), blk(width),
                  _resident((SSD_CONV, cdim)), _resident((1, cdim)), _resident((1, LANES)),
                  _resident((1, LANES)), _resident((1, width)), _resident((1, width))],
        out_specs=blk(width),
        out_shape=jax.ShapeDtypeStruct((b, l, width), BF16),
        scratch_shapes=[pltpu.VMEM((SSD_GROUPS * SSD_STATE, gwidth), F32),
                        pltpu.VMEM((cdim // LANES, q + SUBLANES, LANES), F32),
                        pltpu.VMEM((cdim // LANES, q, LANES), F32),
                        pltpu.VMEM((q, width), BF16),
                        pltpu.VMEM((q, width), F32)],
        compiler_params=_params("parallel", "arbitrary"),
    )(xbc, small, z, conv_w, conv_b.reshape(1, cdim), pad(a_log), pad(dt_bias),
      jnp.repeat(d_skip, SSD_HEAD_DIM).reshape(1, width), norm_w.reshape(1, width))


def _gla_body(q_ref, k_ref, v_ref, sm_ref, g_ref, w2_ref, gb_ref, nw_ref, o_ref, st_ref, *, mid_lane):
    c = GLA_CHUNK
    kw = q_ref.shape[-1]
    dk = kw // GLA_HEADS
    dv = v_ref.shape[-1] // GLA_HEADS
    n_pairs = kw // LANES

    @pl.when(pl.program_id(1) == 0)
    def _():
        st_ref[...] = jnp.zeros_like(st_ref)

    ri = lax.broadcasted_iota(jnp.int32, (c, c), 0)
    ci = lax.broadcasted_iota(jnp.int32, (c, c), 1)
    causal = ri >= ci
    lo = lax.broadcasted_iota(jnp.int32, (c, LANES), 1) < HALF
    lo_sq = lax.broadcasted_iota(jnp.int32, (dv, LANES), 1) < HALF

    sm_hi, sm_mid = _split_bf16(sm_ref[0], 2)
    sm_lane = lax.broadcasted_iota(jnp.int32, sm_hi.shape, 1)
    pre = jnp.dot(jnp.where(sm_lane < mid_lane, sm_hi, sm_mid), w2_ref[...],
                  preferred_element_type=F32) + gb_ref[...]
    gcs_all = _chunk_cumsum(_log_sigmoid(pre) * (LOG2_E / GLA_GATE_NORMALIZER), c)

    chunks = [slice(ch * c, (ch + 1) * c) for ch in range(q_ref.shape[1] // c)]
    pairs = [slice(p * LANES, (p + 1) * LANES) for p in range(n_pairs)]
    causal2 = jnp.concatenate([causal, causal], axis=0)

    def by_head(x):
        return jnp.concatenate([jnp.where(lo, x, 0.0), jnp.where(lo, 0.0, x)], axis=0).astype(BF16)

    q_in, k_in, q_st, k_st, g_last = [], [], [], [], []
    for rows in chunks:
        gcs = gcs_all[rows, :]
        g_mid = gcs[c // 2:c // 2 + 1, :]
        g_last.append(gcs[c - 1:c, :])
        qs = q_ref[0, rows, :] * (dk ** -0.5)
        ks = k_ref[0, rows, :]
        q_in.append(qs * jnp.exp2(gcs - g_mid))
        k_in.append((ks * jnp.exp2(g_mid - gcs)).astype(BF16))
        q_st.append(qs * jnp.exp2(gcs))
        k_st.append((ks * jnp.exp2(g_last[-1] - gcs)).astype(BF16))

    scores = [[_dot_nt(by_head(q_in[ch][:, col]), k_in[ch][:, col]) for col in pairs]
              for ch in range(len(chunks))]
    scores = [[jnp.where(causal2, s, 0.0).astype(BF16) for s in row] for row in scores]

    o_intra, kv_t = [], []
    for ch, rows in enumerate(chunks):
        o_row, kv_row = [], []
        for p, col in enumerate(pairs):
            v_a = v_ref[0, rows, (2 * p) * dv:(2 * p + 1) * dv].astype(BF16)
            v_b = v_ref[0, rows, (2 * p + 1) * dv:(2 * p + 2) * dv].astype(BF16)
            o_row.append((jnp.dot(scores[ch][p][0:c, :], v_a, preferred_element_type=F32),
                          jnp.dot(scores[ch][p][c:2 * c, :], v_b, preferred_element_type=F32)))
            kv_row.append(jnp.where(lo_sq, _dot_tn(v_a, k_st[ch][:, col]), _dot_tn(v_b, k_st[ch][:, col])))
        o_intra.append(o_row)
        kv_t.append(kv_row)

    o_inter = [[None] * n_pairs for _ in chunks]
    for p, col in enumerate(pairs):
        st = st_ref[:, col]
        for ch in range(len(chunks)):
            o_inter[ch][p] = _dot_nt(by_head(q_st[ch][:, col]), st)
            st = st * jnp.exp2(g_last[ch][:, col]) + kv_t[ch][p]
        st_ref[:, col] = st

    for ch, rows in enumerate(chunks):
        for p in range(n_pairs):
            for half in range(2):
                vcol = slice((2 * p + half) * dv, (2 * p + half + 1) * dv)
                o = o_intra[ch][p][half] + o_inter[ch][p][half * c:(half + 1) * c, :]
                o_ref[0, rows, vcol] = (_rms(o, nw_ref[...]) * _silu(g_ref[0, rows, vcol])).astype(o_ref.dtype)


def _gla(q, k, v, small, g, gate_w2, gate_b, norm_w, lr_lane):
    b, l, kw = q.shape
    vw = v.shape[-1]
    rank = gate_w2.shape[0]
    t = GLA_ROWS
    w2_hi, w2_mid = _split_bf16(gate_w2, 2)
    w2_pad = jnp.zeros((LANES, kw), BF16)
    for copy, part in enumerate((w2_hi, w2_mid, w2_hi)):
        w2_pad = w2_pad.at[lr_lane + copy * rank:lr_lane + (copy + 1) * rank, :].set(part)
    blk = lambda w: pl.BlockSpec((1, t, w), lambda i, c: (i, c, 0))
    return pl.pallas_call(
        functools.partial(_gla_body, mid_lane=lr_lane + 2 * rank),
        name="gla",
        grid=(b, l // t),
        in_specs=[blk(kw), blk(kw), blk(vw), blk(LANES), blk(vw),
                  _resident((LANES, kw)), _resident((1, kw)), _resident((1, vw // GLA_HEADS))],
        out_specs=blk(vw),
        out_shape=jax.ShapeDtypeStruct((b, l, vw), BF16),
        scratch_shapes=[pltpu.VMEM((vw // GLA_HEADS, kw), F32)],
        compiler_params=_params("parallel", "arbitrary"),
    )(q, k, v, small, g, w2_pad, gate_b.reshape(1, kw), norm_w.reshape(1, -1))


def _out_proj_body(*refs, n_in, final_norm):
    a_refs, x_ref, w_ref = refs[:n_in], refs[n_in], refs[n_in + 1]
    rest = refs[n_in + 2:]
    acc = x_ref[...]
    off = 0
    for a_ref in a_refs:
        width = a_ref.shape[1]
        acc = acc + jnp.dot(a_ref[...].astype(BF16), w_ref[off:off + width, :], preferred_element_type=F32)
        off += width
    if final_norm:
        fw_ref, o_ref = rest
        o_ref[...] = _rms(acc, fw_ref[...])
    else:
        (o_ref,) = rest
        o_ref[...] = acc


def _out_proj(acts, x2d, weight, final_w=None):
    m, d = x2d.shape
    tm = OUT_PROJ_ROWS
    n_in = len(acts)
    assert sum(a.shape[1] for a in acts) == weight.shape[0]
    row = lambda w: pl.BlockSpec((tm, w), lambda i: (i, 0))
    in_specs = [row(a.shape[1]) for a in acts] + [row(d), _resident(weight.shape)]
    args = list(acts) + [x2d, weight]
    if final_w is not None:
        in_specs.append(_resident((1, d)))
        args.append(final_w.reshape(1, d))
    return pl.pallas_call(
        functools.partial(_out_proj_body, n_in=n_in, final_norm=final_w is not None),
        name="out_proj",
        grid=(m // tm,),
        in_specs=in_specs,
        out_specs=row(d),
        out_shape=jax.ShapeDtypeStruct((m, d), F32),
        compiler_params=_params("parallel"),
    )(*args)


class _MobaQueryBlock:
    def __init__(self, n_past, q_ref, z_ref, o_ref, kb_ref, vt_ref, kmean_ref, s_ref):
        blk = MOBA_BLOCK
        self.n_past, self.z_ref, self.o_ref, self.kb_ref, self.vt_ref = n_past, z_ref, o_ref, kb_ref, vt_ref
        self.own = slice(n_past * blk, (n_past + 1) * blk)
        self.s_ref = s_ref.at[n_past % s_ref.shape[0]]
        q2 = q_ref[0, self.own, :]
        lo = lax.broadcasted_iota(jnp.int32, (blk, LANES), 1) < HALF
        q_cat = jnp.concatenate([jnp.where(lo, q2, 0.0), jnp.where(lo, 0.0, q2)], axis=0)
        scale = MOBA_HEAD_DIM ** -0.5
        key_i = lax.broadcasted_iota(jnp.int32, (blk, 2 * blk), 0)
        qry_i = lax.broadcasted_iota(jnp.int32, (blk, 2 * blk), 1)
        causal_t = key_i <= jnp.where(qry_i < blk, qry_i, qry_i - blk)

        self.masks = []
        if n_past:
            nb = kmean_ref.shape[0] // 2
            parts = [_dot_nt(kmean_ref[...], q_part) for q_part in _split_bf16(q_cat * scale, 2)]
            gate = (parts[0][0:nb] + parts[0][nb:]) + (parts[1][0:nb] + parts[1][nb:])
            g_rows = [gate[n:n + 1, :] for n in range(n_past)]
            for n in range(n_past):
                rank = jnp.zeros((1, 2 * blk), F32)
                for m in range(n_past):
                    if m != n:
                        ahead = (g_rows[m] >= g_rows[n]) if m < n else (g_rows[m] > g_rows[n])
                        rank = rank + jnp.where(ahead, 1.0, 0.0)
                self.masks.append(rank < MOBA_TOPK)
        self.masks.append(causal_t)
        self.q_s = (q_cat * (scale * LOG2_E)).astype(BF16)
        self.m_run = None
        self.p = []

    def logits_step(self, j):
        rows = slice(j * MOBA_BLOCK, (j + 1) * MOBA_BLOCK)
        sj = jnp.where(self.masks[j], _dot_nt(self.kb_ref[rows, :], self.q_s), -jnp.inf)
        self.s_ref[rows, :] = sj
        mj = jnp.max(sj, axis=0, keepdims=True)
        self.m_run = mj if self.m_run is None else jnp.maximum(self.m_run, mj)

    def value_step(self, j):
        rows = slice(j * MOBA_BLOCK, (j + 1) * MOBA_BLOCK)
        self.p.append(jnp.exp2(self.s_ref[rows, :] - self.m_run).astype(BF16))

    def finish(self):
        blk = MOBA_BLOCK
        nk = len(self.p) * blk
        p_all = jnp.concatenate(self.p, axis=0)
        outs = []
        for half in range(2):
            vt = self.vt_ref[half * VT_ROWS:(half + 1) * VT_ROWS, 0:nk]
            acc = jnp.dot(vt, p_all[:, half * blk:(half + 1) * blk], preferred_element_type=F32)
            outs.append(acc[0:HALF, :] / acc[HALF:HALF + 1, :])
        o_t = jnp.concatenate(outs, axis=0)
        self.o_ref[0, self.own, :] = (o_t.T * _silu(self.z_ref[0, self.own, :])).astype(self.o_ref.dtype)


def _moba_body(q_ref, k_ref, v_ref, z_ref, o_ref, vt_ref, kmean_ref, s_ref):
    blk = MOBA_BLOCK
    nb = k_ref.shape[1] // blk
    kmean = []
    for n in range(nb):
        rows = slice(n * blk, (n + 1) * blk)
        kmean.append(jnp.mean(k_ref[0, rows, :].astype(F32), axis=0, keepdims=True))
        v_t = v_ref[0, rows, :].astype(F32).T.astype(BF16)
        for half in range(2):
            vt_ref[half * VT_ROWS:half * VT_ROWS + HALF, rows] = v_t[half * HALF:(half + 1) * HALF, :]
    for half in range(2):
        vt_ref[half * VT_ROWS + HALF:(half + 1) * VT_ROWS, :] = jnp.ones((BF16_SUBLANES, vt_ref.shape[1]), BF16)
    kmean_ref[...] = jnp.concatenate(_split_bf16(jnp.concatenate(kmean, axis=0), 2), axis=0)
    make = functools.partial(_MobaQueryBlock, q_ref=q_ref, z_ref=z_ref, o_ref=o_ref, kb_ref=k_ref.at[0],
                             vt_ref=vt_ref, kmean_ref=kmean_ref, s_ref=s_ref)
    blocks = {0: make(0), 1: make(1)}
    blocks[0].logits_step(0)
    for i in range(nb):
        if i + 2 < nb:
            blocks[i + 2] = make(i + 2)
        cur, nxt = blocks.pop(i), blocks.get(i + 1)
        for j in range(i + 2):
            if j <= i:
                cur.value_step(j)
            if nxt is not None:
                nxt.logits_step(j)
        cur.finish()


def _moba(q, k, v, z):
    b, l, w = q.shape
    blk = MOBA_BLOCK
    nb = l // blk
    spec = pl.BlockSpec((1, l, LANES), lambda bi, hp: (bi, 0, hp))
    return pl.pallas_call(
        _moba_body,
        name="moba",
        grid=(b, w // LANES),
        in_specs=[spec, spec, spec, spec],
        out_specs=spec,
        out_shape=jax.ShapeDtypeStruct((b, l, w), BF16),
        scratch_shapes=[pltpu.VMEM((2 * VT_ROWS, l), BF16),
                        pltpu.VMEM((2 * nb, LANES), BF16),
                        pltpu.VMEM((2, l, 2 * blk), F32)],
        compiler_params=_params("parallel", "parallel"),
    )(q, k, v, z)


def _even_layer(x, norm_w, w_in_bf, conv_w, conv_b, a_log, dt_bias, d_skip, ssd_norm_w, gate_w2, gate_b,
                gla_norm_w, side_casts=()):
    b, l, d = x.shape
    width = ssd_norm_w.shape[0]
    cdim = conv_w.shape[1]
    n_heads = a_log.shape[0]
    rank, kw = gate_w2.shape
    vw = gla_norm_w.shape[0] * GLA_HEADS
    cuts = [0]
    for s in (width, cdim, n_heads, kw, kw, vw, rank, vw):
        cuts.append(cuts[-1] + s)
    seg = lambda j: w_in_bf[:, cuts[j]:cuts[j + 1]]
    pad = jnp.zeros((d, LANES - n_heads - GATE_COPIES * rank), BF16)
    w_small = jnp.concatenate([seg(2)] + [seg(6)] * GATE_COPIES + [pad], axis=1)
    segments = [(cuts[j], cuts[j + 1] - cuts[j]) for j in (0, 1, 3, 4, 5, 7)]
    outs = _norm_proj(x.reshape(b * l, d), norm_w, w_in_bf, segments, [F32, F32, F32, F32, BF16, F32, F32],
                      extras=[w_small], side_casts=side_casts)
    z_a, xbc, q_b, k_b, v_b, g_b, small = [u.reshape(b, l, -1) for u in outs[:7]]
    y_a = _ssd(xbc, small, z_a, conv_w, conv_b, a_log, dt_bias, d_skip, ssd_norm_w)
    o_b = _gla(q_b, k_b, v_b, small, g_b, gate_w2, gate_b, gla_norm_w, lr_lane=n_heads)
    return [y_a.reshape(b * l, width), o_b.reshape(b * l, vw)], outs[7:]


def _odd_layer(x, norm_w, w_in_bf, side_casts=()):
    b, l, d = x.shape
    w = w_in_bf.shape[1] // 4
    outs = _norm_proj(x.reshape(b * l, d), norm_w, w_in_bf, [(j * w, w) for j in range(4)],
                      [F32, BF16, BF16, F32], side_casts=side_casts)
    q, k, v, z = [u.reshape(b, l, w) for u in outs[:4]]
    return [_moba(q, k, v, z).reshape(b * l, w)], outs[4:]


def kernel(x, even_norm, even_w_in, even_conv_w, even_conv_b, even_a_log, even_dt_bias, even_d_skip, even_ssd_norm,
           even_gate_w2, even_gate_b, even_gla_norm, even_w_out, odd_norm, odd_w_in, odd_w_out, final_norm):
    b, l, d = x.shape
    depth = even_norm.shape[0] + odd_norm.shape[0]
    w_in = lambda layer: (even_w_in if layer % 2 == 0 else odd_w_in)[layer // 2]
    w_out = lambda layer: (even_w_out if layer % 2 == 0 else odd_w_out)[layer // 2]
    keys = [("out", 0)] + [(kind, layer) for layer in range(1, depth) for kind in ("in", "out")]
    pending = [w_in(layer) if kind == "in" else w_out(layer) for kind, layer in keys]
    bf = {("in", 0): w_in(0).astype(BF16)}
    for layer in range(depth):
        i = layer // 2
        side = pending if layer == 0 else ()
        if layer % 2 == 0:
            acts, cast = _even_layer(x, even_norm[i], bf["in", layer], even_conv_w[i], even_conv_b[i], even_a_log[i],
                                     even_dt_bias[i], even_d_skip[i], even_ssd_norm[i], even_gate_w2[i],
                                     even_gate_b[i], even_gla_norm[i], side_casts=side)
        else:
            acts, cast = _odd_layer(x, odd_norm[i], bf["in", layer], side_casts=side)
        bf.update(zip(keys, cast))
        last = layer == depth - 1
        x = _out_proj(acts, x.reshape(b * l, d), bf["out", layer], final_norm if last else None).reshape(b, l, d)
    return x
```

```python
import functools

import jax
import jax.numpy as jnp
from jax import lax
from jax.experimental import pallas as pl
from jax.experimental.pallas import tpu as pltpu

F32 = jnp.float32
BF16 = jnp.bfloat16
HIGHEST = lax.Precision.HIGHEST

LANES = 128
SUBLANES = 8
BF16_SUBLANES = 16
VMEM_LIMIT_BYTES = 56 * 1024 * 1024

RMS_EPS = 1e-6
SSD_HEAD_DIM = 64
SSD_GROUPS = 2
SSD_STATE = 64
SSD_CONV = 4
SSD_CHUNK = 128
GLA_HEADS = 8
GLA_GATE_NORMALIZER = 16.0
GLA_CHUNK = 64
MOBA_HEAD_DIM = 64
MOBA_BLOCK = 256
MOBA_TOPK = 3
LOG2_E = 1.4426950408889634
CONV_ROW_STRIDE = 4
CUMSUM_TERMS = 3
GATE_COPIES = 3

IN_PROJ_ROWS = 512
OUT_PROJ_ROWS = 1024
GLA_ROWS = 512
SSD_ROWS = 512
HALF = LANES // 2
VT_ROWS = HALF + BF16_SUBLANES


def _params(*sem):
    return pltpu.CompilerParams(dimension_semantics=sem, vmem_limit_bytes=VMEM_LIMIT_BYTES)


def _rms(x, w):
    return x * lax.rsqrt(jnp.mean(x * x, axis=-1, keepdims=True) + RMS_EPS) * w


def _silu(x):
    h = 0.5 * x
    return h + h * jnp.tanh(h)


def _softplus(x):
    return jnp.maximum(x, 0.0) + jnp.log1p(jnp.exp(-jnp.abs(x)))


def _log_sigmoid(x):
    return jnp.minimum(x, 0.0) - jnp.log(1.0 + jnp.exp(-jnp.abs(x)))


def _dot(a, b):
    return jnp.dot(a.astype(BF16), b.astype(BF16), preferred_element_type=F32)


def _dot_nt(a, b):
    return lax.dot_general(a.astype(BF16), b.astype(BF16), (((1,), (1,)), ((), ())), preferred_element_type=F32)


def _dot_tn(a, b):
    return lax.dot_general(a.astype(BF16), b.astype(BF16), (((0,), (0,)), ((), ())), preferred_element_type=F32)


def _split_bf16(x, terms):
    parts = []
    for _ in range(terms):
        p = x.astype(BF16)
        parts.append(p)
        x = x - p.astype(F32)
    return parts


def _chunk_cumsum(x, chunk):
    rows = x.shape[0]
    ri = lax.broadcasted_iota(jnp.int32, (chunk, CUMSUM_TERMS * chunk), 0)
    ci = lax.broadcasted_iota(jnp.int32, (chunk, CUMSUM_TERMS * chunk), 1)
    tri = jnp.where((ci & (chunk - 1)) <= ri, 1.0, 0.0).astype(BF16)
    out = []
    for r0 in range(0, rows, chunk):
        stacked = jnp.concatenate(_split_bf16(x[r0:r0 + chunk, :], CUMSUM_TERMS), axis=0)
        out.append(jnp.dot(tri, stacked, preferred_element_type=F32))
    return out[0] if len(out) == 1 else jnp.concatenate(out, axis=0)


def _resident(shape):
    return pl.BlockSpec(shape, lambda *_: (0,) * len(shape), pipeline_mode=pl.Buffered(1))


def _norm_proj_body(x_ref, nw_ref, w_ref, *refs, offsets, n_extra, n_side):
    n_out = len(offsets) + n_extra
    extra_refs, side_in = refs[:n_extra], refs[n_extra:n_extra + n_side]
    o_refs = refs[n_extra + n_side:n_extra + n_side + n_out]
    side_out = refs[n_extra + n_side + n_out:n_extra + 2 * n_side + n_out]
    for s_in, s_out in zip(side_in, side_out):
        s_out[...] = s_in[...].astype(s_out.dtype)
    shifted = [i for i, off in enumerate(offsets) if off % LANES]
    starts, pos = {}, 0
    for i in shifted:
        starts[i] = pos
        pos += o_refs[i].shape[1]

    if shifted:
        al_ref = refs[-1]

        @pl.when(pl.program_id(0) == 0)
        def _():
            for i in shifted:
                width = o_refs[i].shape[1]
                al_ref[:, starts[i]:starts[i] + width] = w_ref[:, offsets[i]:offsets[i] + width]

    h = _rms(x_ref[...], nw_ref[...]).astype(BF16)
    for i, off in enumerate(offsets):
        width = o_refs[i].shape[1]
        w = al_ref[:, starts[i]:starts[i] + width] if i in starts else w_ref[:, off:off + width]
        o_refs[i][...] = jnp.dot(h, w, preferred_element_type=F32).astype(o_refs[i].dtype)
    for e_ref, o_ref in zip(extra_refs, o_refs[len(offsets):]):
        o_ref[...] = jnp.dot(h, e_ref[...], preferred_element_type=F32).astype(o_ref.dtype)


def _norm_proj(x2d, norm_w, weight, segments, out_dtypes, extras=(), side_casts=()):
    m, d = x2d.shape
    tm = IN_PROJ_ROWS
    steps = m // tm
    assert all(w % LANES == 0 for _, w in segments)
    assert all(s.shape[0] % (steps * BF16_SUBLANES) == 0 for s in side_casts)
    widths = [w for _, w in segments] + [e.shape[1] for e in extras]
    shifted_cols = sum(w for off, w in segments if off % LANES)
    side_specs = [pl.BlockSpec((s.shape[0] // steps, s.shape[1]), lambda i: (i, 0)) for s in side_casts]
    return pl.pallas_call(
        functools.partial(_norm_proj_body, offsets=tuple(off for off, _ in segments), n_extra=len(extras),
                          n_side=len(side_casts)),
        name="norm_proj",
        grid=(steps,),
        in_specs=[pl.BlockSpec((tm, d), lambda i: (i, 0)), _resident((1, d)), _resident(weight.shape)]
        + [_resident(e.shape) for e in extras] + side_specs,
        out_specs=[pl.BlockSpec((tm, w), lambda i: (i, 0)) for w in widths] + side_specs,
        out_shape=[jax.ShapeDtypeStruct((m, w), dt) for w, dt in zip(widths, out_dtypes)]
        + [jax.ShapeDtypeStruct(s.shape, BF16) for s in side_casts],
        scratch_shapes=[pltpu.VMEM((d, shifted_cols), weight.dtype)] if shifted_cols else [],
        compiler_params=_params("arbitrary"),
    )(x2d, norm_w.reshape(1, d), weight, *extras, *side_casts)


def _ssd_body(xbc_ref, sm_ref, z_ref, cw_ref, cb_ref, alog_ref, dtb_ref, dskip_ref, nw_ref,
              y_ref, h_ref, xpad_ref, act_ref, xw_ref, ycat_ref):
    q = SSD_CHUNK
    t = xbc_ref.shape[1]
    width = y_ref.shape[-1]
    n_pairs = width // LANES
    n_slabs = xpad_ref.shape[0]
    gstate = SSD_GROUPS * SSD_STATE
    gwidth = width // SSD_GROUPS
    first = pl.program_id(1) == 0

    @pl.when(first)
    def _():
        xpad_ref[:, 0:SUBLANES, :] = jnp.zeros((n_slabs, SUBLANES, LANES), F32)
        h_ref[...] = jnp.zeros_like(h_ref)

    @pl.when(jnp.logical_not(first))
    def _():
        xpad_ref[:, 0:SUBLANES, :] = xpad_ref[:, t:t + SUBLANES, :]

    for c in range(n_slabs):
        lanes = slice(c * LANES, (c + 1) * LANES)
        xpad_ref[c, SUBLANES:SUBLANES + t, :] = xbc_ref[0, :, lanes]
        taps = [jnp.broadcast_to(cw_ref[k:k + 1, lanes], (SUBLANES, LANES)) for k in range(SSD_CONV)]
        bias = jnp.broadcast_to(cb_ref[:, lanes], (SUBLANES, LANES))
        for t0 in range(0, t, SUBLANES * CONV_ROW_STRIDE):
            for g in range(CONV_ROW_STRIDE):
                conv = bias
                for k in range(SSD_CONV):
                    start = SUBLANES + t0 + g - (SSD_CONV - 1 - k)
                    conv = conv + taps[k] * xpad_ref[c, pl.ds(start, SUBLANES, stride=CONV_ROW_STRIDE), :]
                act_ref[c, pl.ds(t0 + g, SUBLANES, stride=CONV_ROW_STRIDE), :] = _silu(conv)

    dt_all = _softplus(sm_ref[0] + dtb_ref[...])
    dta = dt_all * (-jnp.exp(alog_ref[...]) * LOG2_E)
    a_cs_all = _chunk_cumsum(dta, q)
    ri = lax.broadcasted_iota(jnp.int32, (q, q), 0)
    ci = lax.broadcasted_iota(jnp.int32, (q, q), 1)
    causal = ri >= ci
    lane = lax.broadcasted_iota(jnp.int32, (q, LANES), 1)
    lo = lane < HALF
    lo_row = lo[0:1, :]

    h_prev = h_ref[...]
    for rows in [slice(r0, r0 + q) for r0 in range(0, t, q)]:
        bm = act_ref[n_pairs, rows, :]
        cm = act_ref[n_pairs + 1, rows, :]
        dt = dt_all[rows, :]
        a_cs = a_cs_all[rows, :]
        a_cs_t = a_cs.T
        a_last = a_cs[q - 1:q, :]
        bm_t = bm.T
        cm_g = [jnp.where(lo, cm, 0.0), jnp.where(lo, 0.0, cm)]
        cb = [_dot_nt(c, bm) for c in cm_g]
        y_off = [_dot(c, h_prev) for c in cm_g]

        a_last_pairs = []
        for p in range(n_pairs):
            e0, e1 = 2 * p, 2 * p + 1
            g = (p * LANES) // gwidth
            col = slice(p * LANES, (p + 1) * LANES)
            gcol = slice(p * LANES - g * gwidth, (p + 1) * LANES - g * gwidth)
            acs_pair = jnp.where(lo, a_cs[:, e0:e0 + 1], a_cs[:, e1:e1 + 1])
            dt_pair = jnp.where(lo, dt[:, e0:e0 + 1], dt[:, e1:e1 + 1])
            al_pair = jnp.where(lo_row, a_last[:, e0:e0 + 1], a_last[:, e1:e1 + 1])
            a_last_pairs.append(al_pair)
            xs2 = act_ref[p, rows, :]
            xdt = xs2 * dt_pair
            xdt_b = xdt.astype(BF16)
            yd = []
            for e in (e0, e1):
                seg = a_cs[:, e:e + 1] - a_cs_t[e:e + 1, :]
                decay = jnp.exp2(jnp.where(causal, seg, -jnp.inf))
                yd.append(_dot(cb[g] * decay, xdt_b))
            y2 = jnp.where(lo, yd[0], yd[1])
            y2 = y2 + y_off[g][:, gcol] * jnp.exp2(acs_pair) + dskip_ref[:, col] * xs2
            ycat_ref[rows, col] = y2
            xw_ref[rows, col] = (xdt * jnp.exp2(al_pair - acs_pair)).astype(BF16)

        h_next = []
        for g in range(SSD_GROUPS):
            srows = slice(g * SSD_STATE, (g + 1) * SSD_STATE)
            ppg = n_pairs // SSD_GROUPS
            dec = jnp.exp2(jnp.concatenate(a_last_pairs[g * ppg:(g + 1) * ppg], axis=1))
            s_g = _dot(bm_t[srows, :], xw_ref[rows, g * gwidth:(g + 1) * gwidth])
            h_next.append(h_prev[srows, :] * dec + s_g)
        h_prev = jnp.concatenate(h_next, axis=0)
    h_ref[...] = h_prev

    y_ref[0] = _rms(ycat_ref[...] * _silu(z_ref[0]), nw_ref[...]).astype(y_ref.dtype)


def _ssd(xbc, small, z, conv_w, conv_b, a_log, dt_bias, d_skip, norm_w):
    b, l, cdim = xbc.shape
    width = z.shape[-1]
    q = SSD_ROWS
    n_heads = a_log.shape[0]
    pad = lambda v: jnp.pad(v, (0, LANES - n_heads)).reshape(1, LANES)
    gwidth = width // SSD_GROUPS
    blk = lambda w: pl.BlockSpec((1, q, w), lambda i, c: (i, c, 0))
    return pl.pallas_call(
        _ssd_body,
        name="ssd",
        grid=(b, l // q),
        in_specs=[blk(cdim), blk(LANES), blk(width),
                  _resident((SSD_CONV, cdim)), _resident((1, cdim)), _resident((1, LANES)),
                  _resident((1, LANES)), _resident((1, width)), _resident((1, width))],
        out_specs=blk(width),
        out_shape=jax.ShapeDtypeStruct((b, l, width), BF16),
        scratch_shapes=[pltpu.VMEM((SSD_GROUPS * SSD_STATE, gwidth), F32),
                        pltpu.VMEM((cdim // LANES, q + SUBLANES, LANES), F32),
                        pltpu.VMEM((cdim // LANES, q, LANES), F32),
                        pltpu.VMEM((q, width), BF16),
                        pltpu.VMEM((q, width), F32)],
        compiler_params=_params("parallel", "arbitrary"),
    )(xbc, small, z, conv_w, conv_b.reshape(1, cdim), pad(a_log), pad(dt_bias),
      jnp.repeat(d_skip, SSD_HEAD_DIM).reshape(1, width), norm_w.reshape(1, width))


def _gla_body(q_ref, k_ref, v_ref, sm_ref, g_ref, w2_ref, gb_ref, nw_ref, o_ref, st_ref, *, mid_lane):
    c = GLA_CHUNK
    kw = q_ref.shape[-1]
    dk = kw // GLA_HEADS
    dv = v_ref.shape[-1] // GLA_HEADS
    n_pairs = kw // LANES

    @pl.when(pl.program_id(1) == 0)
    def _():
        st_ref[...] = jnp.zeros_like(st_ref)

    ri = lax.broadcasted_iota(jnp.int32, (c, c), 0)
    ci = lax.broadcasted_iota(jnp.int32, (c, c), 1)
    causal = ri >= ci
    lo = lax.broadcasted_iota(jnp.int32, (c, LANES), 1) < HALF
    lo_sq = lax.broadcasted_iota(jnp.int32, (dv, LANES), 1) < HALF

    sm_hi, sm_mid = _split_bf16(sm_ref[0], 2)
    sm_lane = lax.broadcasted_iota(jnp.int32, sm_hi.shape, 1)
    pre = jnp.dot(jnp.where(sm_lane < mid_lane, sm_hi, sm_mid), w2_ref[...],
                  preferred_element_type=F32) + gb_ref[...]
    gcs_all = _chunk_cumsum(_log_sigmoid(pre) * (LOG2_E / GLA_GATE_NORMALIZER), c)

    chunks = [slice(ch * c, (ch + 1) * c) for ch in range(q_ref.shape[1] // c)]
    pairs = [slice(p * LANES, (p + 1) * LANES) for p in range(n_pairs)]
    causal2 = jnp.concatenate([causal, causal], axis=0)

    def by_head(x):
        return jnp.concatenate([jnp.where(lo, x, 0.0), jnp.where(lo, 0.0, x)], axis=0).astype(BF16)

    q_in, k_in, q_st, k_st, g_last = [], [], [], [], []
    for rows in chunks:
        gcs = gcs_all[rows, :]
        g_mid = gcs[c // 2:c // 2 + 1, :]
        g_last.append(gcs[c - 1:c, :])
        qs = q_ref[0, rows, :] * (dk ** -0.5)
        ks = k_ref[0, rows, :]
        q_mid = qs * jnp.exp2(gcs - g_mid)
        k_mid = ks * jnp.exp2(g_mid - gcs)
        q_in.append(q_mid)
        k_in.append(k_mid.astype(BF16))
        q_st.append(q_mid * jnp.exp2(g_mid))
        k_st.append((k_mid * jnp.exp2(g_last[-1] - g_mid)).astype(BF16))

    scores = [[_dot_nt(by_head(q_in[ch][:, col]), k_in[ch][:, col]) for col in pairs]
              for ch in range(len(chunks))]
    scores = [[jnp.where(causal2, s, 0.0).astype(BF16) for s in row] for row in scores]

    o_intra, kv_t = [], []
    for ch, rows in enumerate(chunks):
        o_row, kv_row = [], []
        for p, col in enumerate(pairs):
            v_a = v_ref[0, rows, (2 * p) * dv:(2 * p + 1) * dv].astype(BF16)
            v_b = v_ref[0, rows, (2 * p + 1) * dv:(2 * p + 2) * dv].astype(BF16)
            o_row.append((jnp.dot(scores[ch][p][0:c, :], v_a, preferred_element_type=F32),
                          jnp.dot(scores[ch][p][c:2 * c, :], v_b, preferred_element_type=F32)))
            kv_row.append(jnp.where(lo_sq, _dot_tn(v_a, k_st[ch][:, col]), _dot_tn(v_b, k_st[ch][:, col])))
        o_intra.append(o_row)
        kv_t.append(kv_row)

    o_inter = [[None] * n_pairs for _ in chunks]
    for p, col in enumerate(pairs):
        st = st_ref[:, col]
        for ch in range(len(chunks)):
            o_inter[ch][p] = _dot_nt(by_head(q_st[ch][:, col]), st)
            st = st * jnp.exp2(g_last[ch][:, col]) + kv_t[ch][p]
        st_ref[:, col] = st

    for ch, rows in enumerate(chunks):
        for p in range(n_pairs):
            for half in range(2):
                vcol = slice((2 * p + half) * dv, (2 * p + half + 1) * dv)
                o = o_intra[ch][p][half] + o_inter[ch][p][half * c:(half + 1) * c, :]
                o_ref[0, rows, vcol] = (_rms(o, nw_ref[...]) * _silu(g_ref[0, rows, vcol])).astype(o_ref.dtype)


def _gla(q, k, v, small, g, gate_w2, gate_b, norm_w, lr_lane):
    b, l, kw = q.shape
    vw = v.shape[-1]
    rank = gate_w2.shape[0]
    t = GLA_ROWS
    w2_hi, w2_mid = _split_bf16(gate_w2, 2)
    w2_pad = jnp.zeros((LANES, kw), BF16)
    for copy, part in enumerate((w2_hi, w2_mid, w2_hi)):
        w2_pad = w2_pad.at[lr_lane + copy * rank:lr_lane + (copy + 1) * rank, :].set(part)
    blk = lambda w: pl.BlockSpec((1, t, w), lambda i, c: (i, c, 0))
    return pl.pallas_call(
        functools.partial(_gla_body, mid_lane=lr_lane + 2 * rank),
        name="gla",
        grid=(b, l // t),
        in_specs=[blk(kw), blk(kw), blk(vw), blk(LANES), blk(vw),
                  _resident((LANES, kw)), _resident((1, kw)), _resident((1, vw // GLA_HEADS))],
        out_specs=blk(vw),
        out_shape=jax.ShapeDtypeStruct((b, l, vw), BF16),
        scratch_shapes=[pltpu.VMEM((vw // GLA_HEADS, kw), F32)],
        compiler_params=_params("parallel", "arbitrary"),
    )(q, k, v, small, g, w2_pad, gate_b.reshape(1, kw), norm_w.reshape(1, -1))


def _out_proj_body(*refs, n_in, final_norm):
    a_refs, x_ref, w_ref = refs[:n_in], refs[n_in], refs[n_in + 1]
    rest = refs[n_in + 2:]
    acc = x_ref[...]
    off = 0
    for a_ref in a_refs:
        width = a_ref.shape[1]
        acc = acc + jnp.dot(a_ref[...].astype(BF16), w_ref[off:off + width, :], preferred_element_type=F32)
        off += width
    if final_norm:
        fw_ref, o_ref = rest
        o_ref[...] = _rms(acc, fw_ref[...])
    else:
        (o_ref,) = rest
        o_ref[...] = acc


def _out_proj(acts, x2d, weight, final_w=None):
    m, d = x2d.shape
    tm = OUT_PROJ_ROWS
    n_in = len(acts)
    assert sum(a.shape[1] for a in acts) == weight.shape[0]
    row = lambda w: pl.BlockSpec((tm, w), lambda i: (i, 0))
    in_specs = [row(a.shape[1]) for a in acts] + [row(d), _resident(weight.shape)]
    args = list(acts) + [x2d, weight]
    if final_w is not None:
        in_specs.append(_resident((1, d)))
        args.append(final_w.reshape(1, d))
    return pl.pallas_call(
        functools.partial(_out_proj_body, n_in=n_in, final_norm=final_w is not None),
        name="out_proj",
        grid=(m // tm,),
        in_specs=in_specs,
        out_specs=row(d),
        out_shape=jax.ShapeDtypeStruct((m, d), F32),
        compiler_params=_params("parallel"),
    )(*args)


class _MobaQueryBlock:
    def __init__(self, n_past, q_ref, z_ref, o_ref, kb_ref, vt_ref, kmean_ref, s_ref):
        blk = MOBA_BLOCK
        self.n_past, self.z_ref, self.o_ref, self.kb_ref, self.vt_ref = n_past, z_ref, o_ref, kb_ref, vt_ref
        self.own = slice(n_past * blk, (n_past + 1) * blk)
        self.s_ref = s_ref.at[n_past % s_ref.shape[0]]
        q2 = q_ref[0, self.own, :]
        lo = lax.broadcasted_iota(jnp.int32, (blk, LANES), 1) < HALF
        q_cat = jnp.concatenate([jnp.where(lo, q2, 0.0), jnp.where(lo, 0.0, q2)], axis=0)
        scale = MOBA_HEAD_DIM ** -0.5
        key_i = lax.broadcasted_iota(jnp.int32, (blk, 2 * blk), 0)
        qry_i = lax.broadcasted_iota(jnp.int32, (blk, 2 * blk), 1)
        causal_t = key_i <= jnp.where(qry_i < blk, qry_i, qry_i - blk)

        self.masks = []
        if n_past:
            nb = kmean_ref.shape[0] // 2
            parts = [_dot_nt(kmean_ref[...], q_part) for q_part in _split_bf16(q_cat * scale, 2)]
            gate = (parts[0][0:nb] + parts[0][nb:]) + (parts[1][0:nb] + parts[1][nb:])
            g_rows = [gate[n:n + 1, :] for n in range(n_past)]
            for n in range(n_past):
                rank = jnp.zeros((1, 2 * blk), F32)
                for m in range(n_past):
                    if m != n:
                        ahead = (g_rows[m] >= g_rows[n]) if m < n else (g_rows[m] > g_rows[n])
                        rank = rank + jnp.where(ahead, 1.0, 0.0)
                self.masks.append(rank < MOBA_TOPK)
        self.masks.append(causal_t)
        self.q_s = (q_cat * (scale * LOG2_E)).astype(BF16)
        self.m_run = None
        self.p = []

    def logits_step(self, j):
        rows = slice(j * MOBA_BLOCK, (j + 1) * MOBA_BLOCK)
        sj = jnp.where(self.masks[j], _dot_nt(self.kb_ref[rows, :], self.q_s), -jnp.inf)
        self.s_ref[rows, :] = sj
        mj = jnp.max(sj, axis=0, keepdims=True)
        self.m_run = mj if self.m_run is None else jnp.maximum(self.m_run, mj)

    def value_step(self, j):
        rows = slice(j * MOBA_BLOCK, (j + 1) * MOBA_BLOCK)
        self.p.append(jnp.exp2(self.s_ref[rows, :] - self.m_run).astype(BF16))

    def finish(self):
        blk = MOBA_BLOCK
        nk = len(self.p) * blk
        p_all = jnp.concatenate(self.p, axis=0)
        outs = []
        for half in range(2):
            vt = self.vt_ref[half * VT_ROWS:(half + 1) * VT_ROWS, 0:nk]
            acc = jnp.dot(vt, p_all[:, half * blk:(half + 1) * blk], preferred_element_type=F32)
            outs.append(acc[0:HALF, :] / acc[HALF:HALF + 1, :])
        o_t = jnp.concatenate(outs, axis=0)
        self.o_ref[0, self.own, :] = (o_t.T * _silu(self.z_ref[0, self.own, :])).astype(self.o_ref.dtype)


def _moba_body(q_ref, k_ref, v_ref, z_ref, o_ref, vt_ref, kmean_ref, s_ref):
    blk = MOBA_BLOCK
    nb = k_ref.shape[1] // blk
    kmean = []
    for n in range(nb):
        rows = slice(n * blk, (n + 1) * blk)
        kmean.append(jnp.mean(k_ref[0, rows, :].astype(F32), axis=0, keepdims=True))
        v_t = v_ref[0, rows, :].astype(F32).T.astype(BF16)
        for half in range(2):
            vt_ref[half * VT_ROWS:half * VT_ROWS + HALF, rows] = v_t[half * HALF:(half + 1) * HALF, :]
    for half in range(2):
        vt_ref[half * VT_ROWS + HALF:(half + 1) * VT_ROWS, :] = jnp.ones((BF16_SUBLANES, vt_ref.shape[1]), BF16)
    kmean_ref[...] = jnp.concatenate(_split_bf16(jnp.concatenate(kmean, axis=0), 2), axis=0)
    make = functools.partial(_MobaQueryBlock, q_ref=q_ref, z_ref=z_ref, o_ref=o_ref, kb_ref=k_ref.at[0],
                             vt_ref=vt_ref, kmean_ref=kmean_ref, s_ref=s_ref)
    blocks = {0: make(0), 1: make(1)}
    blocks[0].logits_step(0)
    for i in range(nb):
        if i + 2 < nb:
            blocks[i + 2] = make(i + 2)
        cur, nxt = blocks.pop(i), blocks.get(i + 1)
        for j in range(i + 2):
            if j <= i:
                cur.value_step(j)
            if nxt is not None:
                nxt.logits_step(j)
        cur.finish()


def _moba(q, k, v, z):
    b, l, w = q.shape
    blk = MOBA_BLOCK
    nb = l // blk
    spec = pl.BlockSpec((1, l, LANES), lambda bi, hp: (bi, 0, hp))
    return pl.pallas_call(
        _moba_body,
        name="moba",
        grid=(b, w // LANES),
        in_specs=[spec, spec, spec, spec],
        out_specs=spec,
        out_shape=jax.ShapeDtypeStruct((b, l, w), BF16),
        scratch_shapes=[pltpu.VMEM((2 * VT_ROWS, l), BF16),
                        pltpu.VMEM((2 * nb, LANES), BF16),
                        pltpu.VMEM((2, l, 2 * blk), F32)],
        compiler_params=_params("parallel", "parallel"),
    )(q, k, v, z)


def _even_layer(x, norm_w, w_in_bf, conv_w, conv_b, a_log, dt_bias, d_skip, ssd_norm_w, gate_w2, gate_b,
                gla_norm_w, side_casts=()):
    b, l, d = x.shape
    width = ssd_norm_w.shape[0]
    cdim = conv_w.shape[1]
    n_heads = a_log.shape[0]
    rank, kw = gate_w2.shape
    vw = gla_norm_w.shape[0] * GLA_HEADS
    cuts = [0]
    for s in (width, cdim, n_heads, kw, kw, vw, rank, vw):
        cuts.append(cuts[-1] + s)
    seg = lambda j: w_in_bf[:, cuts[j]:cuts[j + 1]]
    pad = jnp.zeros((d, LANES - n_heads - GATE_COPIES * rank), BF16)
    w_small = jnp.concatenate([seg(2)] + [seg(6)] * GATE_COPIES + [pad], axis=1)
    segments = [(cuts[j], cuts[j + 1] - cuts[j]) for j in (0, 1, 3, 4, 5, 7)]
    outs = _norm_proj(x.reshape(b * l, d), norm_w, w_in_bf, segments, [F32, F32, F32, F32, BF16, F32, F32],
                      extras=[w_small], side_casts=side_casts)
    z_a, xbc, q_b, k_b, v_b, g_b, small = [u.reshape(b, l, -1) for u in outs[:7]]
    y_a = _ssd(xbc, small, z_a, conv_w, conv_b, a_log, dt_bias, d_skip, ssd_norm_w)
    o_b = _gla(q_b, k_b, v_b, small, g_b, gate_w2, gate_b, gla_norm_w, lr_lane=n_heads)
    return [y_a.reshape(b * l, width), o_b.reshape(b * l, vw)], outs[7:]


def _odd_layer(x, norm_w, w_in_bf, side_casts=()):
    b, l, d = x.shape
    w = w_in_bf.shape[1] // 4
    outs = _norm_proj(x.reshape(b * l, d), norm_w, w_in_bf, [(j * w, w) for j in range(4)],
                      [F32, BF16, BF16, F32], side_casts=side_casts)
    q, k, v, z = [u.reshape(b, l, w) for u in outs[:4]]
    return [_moba(q, k, v, z).reshape(b * l, w)], outs[4:]


def kernel(x, even_norm, even_w_in, even_conv_w, even_conv_b, even_a_log, even_dt_bias, even_d_skip, even_ssd_norm,
           even_gate_w2, even_gate_b, even_gla_norm, even_w_out, odd_norm, odd_w_in, odd_w_out, final_norm):
    b, l, d = x.shape
    depth = even_norm.shape[0] + odd_norm.shape[0]
    w_in = lambda layer: (even_w_in if layer % 2 == 0 else odd_w_in)[layer // 2]
    w_out = lambda layer: (even_w_out if layer % 2 == 0 else odd_w_out)[layer // 2]
    keys = [("out", 0)] + [(kind, layer) for layer in range(1, depth) for kind in ("in", "out")]
    pending = [w_in(layer) if kind == "in" else w_out(layer) for kind, layer in keys]
    bf = {("in", 0): w_in(0).astype(BF16)}
    for layer in range(depth):
        i = layer // 2
        side = pending if layer == 0 else ()
        if layer % 2 == 0:
            acts, cast = _even_layer(x, even_norm[i], bf["in", layer], even_conv_w[i], even_conv_b[i], even_a_log[i],
                                     even_dt_bias[i], even_d_skip[i], even_ssd_norm[i], even_gate_w2[i],
                                     even_gate_b[i], even_gla_norm[i], side_casts=side)
        else:
            acts, cast = _odd_layer(x, odd_norm[i], bf["in", layer], side_casts=side)
        bf.update(zip(keys, cast))
        last = layer == depth - 1
        x = _out_proj(acts, x.reshape(b * l, d), bf["out", layer], final_norm if last else None).reshape(b, l, d)
    return x
```

```python
import functools

import jax
import jax.numpy as jnp
from jax import lax
from jax.experimental import pallas as pl
from jax.experimental.pallas import tpu as pltpu

F32 = jnp.float32
BF16 = jnp.bfloat16

LANES = 128
SUBLANES = 8
BF16_SUBLANES = 16
VMEM_LIMIT_BYTES = 56 * 1024 * 1024

RMS_EPS = 1e-6
SSD_HEAD_DIM = 64
SSD_GROUPS = 2
SSD_STATE = 64
SSD_CONV = 4
SSD_CHUNK = 128
GLA_HEADS = 8
GLA_GATE_NORMALIZER = 16.0
GLA_CHUNK = 64
MOBA_HEAD_DIM = 64
MOBA_BLOCK = 256
MOBA_TOPK = 3
LOG2_E = 1.4426950408889634
CONV_ROW_STRIDE = 4
CUMSUM_TERMS = 3
GATE_COPIES = 3

IN_PROJ_ROWS = 512
OUT_PROJ_ROWS = 1024
GLA_ROWS = 512
SSD_ROWS = 512
HALF = LANES // 2
VT_ROWS = HALF + BF16_SUBLANES


def _params(*sem):
    return pltpu.CompilerParams(dimension_semantics=sem, vmem_limit_bytes=VMEM_LIMIT_BYTES)


def _rms(x, w):
    return x * lax.rsqrt(jnp.mean(x * x, axis=-1, keepdims=True) + RMS_EPS) * w


def _silu(x):
    h = 0.5 * x
    return h + h * jnp.tanh(h)


def _softplus(x):
    return jnp.maximum(x, 0.0) + jnp.log1p(jnp.exp(-jnp.abs(x)))


def _log_sigmoid(x):
    return jnp.minimum(x, 0.0) - jnp.log(1.0 + jnp.exp(-jnp.abs(x)))


def _dot(a, b):
    return jnp.dot(a.astype(BF16), b.astype(BF16), preferred_element_type=F32)


def _dot_nt(a, b):
    return lax.dot_general(a.astype(BF16), b.astype(BF16), (((1,), (1,)), ((), ())), preferred_element_type=F32)


def _dot_tn(a, b):
    return lax.dot_general(a.astype(BF16), b.astype(BF16), (((0,), (0,)), ((), ())), preferred_element_type=F32)


def _split_bf16(x, terms):
    parts = []
    for _ in range(terms):
        p = x.astype(BF16)
        parts.append(p)
        x = x - p.astype(F32)
    return parts


def _chunk_cumsum(x, chunk):
    rows = x.shape[0]
    ri = lax.broadcasted_iota(jnp.int32, (chunk, CUMSUM_TERMS * chunk), 0)
    ci = lax.broadcasted_iota(jnp.int32, (chunk, CUMSUM_TERMS * chunk), 1)
    tri = jnp.where((ci & (chunk - 1)) <= ri, 1.0, 0.0).astype(BF16)
    out = []
    for r0 in range(0, rows, chunk):
        stacked = jnp.concatenate(_split_bf16(x[r0:r0 + chunk, :], CUMSUM_TERMS), axis=0)
        out.append(jnp.dot(tri, stacked, preferred_element_type=F32))
    return out[0] if len(out) == 1 else jnp.concatenate(out, axis=0)


def _resident(shape):
    return pl.BlockSpec(shape, lambda *_: (0,) * len(shape), pipeline_mode=pl.Buffered(1))


def _norm_proj_body(x_ref, nw_ref, w_ref, *refs, offsets, n_extra, n_side):
    n_out = len(offsets) + n_extra
    extra_refs, side_in = refs[:n_extra], refs[n_extra:n_extra + n_side]
    o_refs = refs[n_extra + n_side:n_extra + n_side + n_out]
    side_out = refs[n_extra + n_side + n_out:n_extra + 2 * n_side + n_out]
    for s_in, s_out in zip(side_in, side_out):
        s_out[...] = s_in[...].astype(s_out.dtype)
    shifted = [i for i, off in enumerate(offsets) if off % LANES]
    starts, pos = {}, 0
    for i in shifted:
        starts[i] = pos
        pos += o_refs[i].shape[1]

    if shifted:
        al_ref = refs[-1]

        @pl.when(pl.program_id(0) == 0)
        def _():
            for i in shifted:
                width = o_refs[i].shape[1]
                al_ref[:, starts[i]:starts[i] + width] = w_ref[:, offsets[i]:offsets[i] + width]

    h = _rms(x_ref[...], nw_ref[...]).astype(BF16)
    for i, off in enumerate(offsets):
        width = o_refs[i].shape[1]
        w = al_ref[:, starts[i]:starts[i] + width] if i in starts else w_ref[:, off:off + width]
        o_refs[i][...] = jnp.dot(h, w, preferred_element_type=F32).astype(o_refs[i].dtype)
    for e_ref, o_ref in zip(extra_refs, o_refs[len(offsets):]):
        o_ref[...] = jnp.dot(h, e_ref[...], preferred_element_type=F32).astype(o_ref.dtype)


def _norm_proj(x2d, norm_w, weight, segments, out_dtypes, extras=(), side_casts=()):
    m, d = x2d.shape
    tm = IN_PROJ_ROWS
    steps = m // tm
    assert all(w % LANES == 0 for _, w in segments)
    assert all(s.shape[0] % (steps * BF16_SUBLANES) == 0 for s in side_casts)
    widths = [w for _, w in segments] + [e.shape[1] for e in extras]
    shifted_cols = sum(w for off, w in segments if off % LANES)
    side_specs = [pl.BlockSpec((s.shape[0] // steps, s.shape[1]), lambda i: (i, 0)) for s in side_casts]
    return pl.pallas_call(
        functools.partial(_norm_proj_body, offsets=tuple(off for off, _ in segments), n_extra=len(extras),
                          n_side=len(side_casts)),
        name="norm_proj",
        grid=(steps,),
        in_specs=[pl.BlockSpec((tm, d), lambda i: (i, 0)), _resident((1, d)), _resident(weight.shape)]
        + [_resident(e.shape) for e in extras] + side_specs,
        out_specs=[pl.BlockSpec((tm, w), lambda i: (i, 0)) for w in widths] + side_specs,
        out_shape=[jax.ShapeDtypeStruct((m, w), dt) for w, dt in zip(widths, out_dtypes)]
        + [jax.ShapeDtypeStruct(s.shape, BF16) for s in side_casts],
        scratch_shapes=[pltpu.VMEM((d, shifted_cols), weight.dtype)] if shifted_cols else [],
        compiler_params=_params("arbitrary"),
    )(x2d, norm_w.reshape(1, d), weight, *extras, *side_casts)


def _ssd_body(xbc_ref, sm_ref, z_ref, cw_ref, cb_ref, alog_ref, dtb_ref, dskip_ref, nw_ref,
              y_ref, h_ref, xpad_ref, act_ref, xw_ref, ycat_ref):
    q = SSD_CHUNK
    t = xbc_ref.shape[1]
    width = y_ref.shape[-1]
    n_pairs = width // LANES
    n_slabs = xpad_ref.shape[0]
    gstate = SSD_GROUPS * SSD_STATE
    gwidth = width // SSD_GROUPS
    first = pl.program_id(1) == 0

    @pl.when(first)
    def _():
        xpad_ref[:, 0:SUBLANES, :] = jnp.zeros((n_slabs, SUBLANES, LANES), F32)
        h_ref[...] = jnp.zeros_like(h_ref)

    @pl.when(jnp.logical_not(first))
    def _():
        xpad_ref[:, 0:SUBLANES, :] = xpad_ref[:, t:t + SUBLANES, :]

    for c in range(n_slabs):
        lanes = slice(c * LANES, (c + 1) * LANES)
        xpad_ref[c, SUBLANES:SUBLANES + t, :] = xbc_ref[0, :, lanes]
        taps = [jnp.broadcast_to(cw_ref[k:k + 1, lanes], (SUBLANES, LANES)) for k in range(SSD_CONV)]
        bias = jnp.broadcast_to(cb_ref[:, lanes], (SUBLANES, LANES))
        for t0 in range(0, t, SUBLANES * CONV_ROW_STRIDE):
            for g in range(CONV_ROW_STRIDE):
                conv = bias
                for k in range(SSD_CONV):
                    start = SUBLANES + t0 + g - (SSD_CONV - 1 - k)
                    conv = conv + taps[k] * xpad_ref[c, pl.ds(start, SUBLANES, stride=CONV_ROW_STRIDE), :]
                act_ref[c, pl.ds(t0 + g, SUBLANES, stride=CONV_ROW_STRIDE), :] = _silu(conv)

    dt_all = _softplus(sm_ref[0] + dtb_ref[...])
    dta = dt_all * (-jnp.exp(alog_ref[...]) * LOG2_E)
    a_cs_all = _chunk_cumsum(dta, q)
    ri = lax.broadcasted_iota(jnp.int32, (q, q), 0)
    ci = lax.broadcasted_iota(jnp.int32, (q, q), 1)
    causal = ri >= ci
    lane = lax.broadcasted_iota(jnp.int32, (q, LANES), 1)
    lo = lane < HALF
    lo_row = lo[0:1, :]

    h_prev = h_ref[...]
    for rows in [slice(r0, r0 + q) for r0 in range(0, t, q)]:
        bm = act_ref[n_pairs, rows, :]
        cm = act_ref[n_pairs + 1, rows, :]
        dt = dt_all[rows, :]
        a_cs = a_cs_all[rows, :]
        a_cs_t = a_cs.T
        a_last = a_cs[q - 1:q, :]
        bm_t = bm.T
        cm_g = [jnp.where(lo, cm, 0.0), jnp.where(lo, 0.0, cm)]
        cb = [_dot_nt(c, bm) for c in cm_g]
        y_off = [_dot(c, h_prev) for c in cm_g]

        a_last_pairs = []
        for p in range(n_pairs):
            e0, e1 = 2 * p, 2 * p + 1
            g = (p * LANES) // gwidth
            col = slice(p * LANES, (p + 1) * LANES)
            gcol = slice(p * LANES - g * gwidth, (p + 1) * LANES - g * gwidth)
            acs_pair = jnp.where(lo, a_cs[:, e0:e0 + 1], a_cs[:, e1:e1 + 1])
            dt_pair = jnp.where(lo, dt[:, e0:e0 + 1], dt[:, e1:e1 + 1])
            al_pair = jnp.where(lo_row, a_last[:, e0:e0 + 1], a_last[:, e1:e1 + 1])
            a_last_pairs.append(al_pair)
            xs2 = act_ref[p, rows, :]
            xdt = xs2 * dt_pair
            xdt_b = xdt.astype(BF16)
            yd = []
            for e in (e0, e1):
                seg = a_cs[:, e:e + 1] - a_cs_t[e:e + 1, :]
                decay = jnp.exp2(jnp.where(causal, seg, -jnp.inf))
                yd.append(_dot(cb[g] * decay, xdt_b))
            y2 = jnp.where(lo, yd[0], yd[1])
            y2 = y2 + y_off[g][:, gcol] * jnp.exp2(acs_pair) + dskip_ref[:, col] * xs2
            ycat_ref[rows, col] = y2
            xw_ref[rows, col] = (xdt * jnp.exp2(al_pair - acs_pair)).astype(BF16)

        h_next = []
        for g in range(SSD_GROUPS):
            srows = slice(g * SSD_STATE, (g + 1) * SSD_STATE)
            ppg = n_pairs // SSD_GROUPS
            dec = jnp.exp2(jnp.concatenate(a_last_pairs[g * ppg:(g + 1) * ppg], axis=1))
            s_g = _dot(bm_t[srows, :], xw_ref[rows, g * gwidth:(g + 1) * gwidth])
            h_next.append(h_prev[srows, :] * dec + s_g)
        h_prev = jnp.concatenate(h_next, axis=0)
    h_ref[...] = h_prev

    y_ref[0] = _rms(ycat_ref[...] * _silu(z_ref[0]), nw_ref[...]).astype(y_ref.dtype)


def _ssd(xbc, small, z, conv_w, conv_b, a_log, dt_bias, d_skip, norm_w):
    b, l, cdim = xbc.shape
    width = z.shape[-1]
    q = SSD_ROWS
    n_heads = a_log.shape[0]
    pad = lambda v: jnp.pad(v, (0, LANES - n_heads)).reshape(1, LANES)
    gwidth = width // SSD_GROUPS
    blk = lambda w: pl.BlockSpec((1, q, w), lambda i, c: (i, c, 0))
    return pl.pallas_call(
        _ssd_body,
        name="ssd",
        grid=(b, l // q),
        in_specs=[blk(cdim), blk(LANES), blk(width),
                  _resident((SSD_CONV, cdim)), _resident((1, cdim)), _resident((1, LANES)),
                  _resident((1, LANES)), _resident((1, width)), _resident((1, width))],
        out_specs=blk(width),
        out_shape=jax.ShapeDtypeStruct((b, l, width), BF16),
        scratch_shapes=[pltpu.VMEM((SSD_GROUPS * SSD_STATE, gwidth), F32),
                        pltpu.VMEM((cdim // LANES, q + SUBLANES, LANES), F32),
                        pltpu.VMEM((cdim // LANES, q, LANES), F32),
                        pltpu.VMEM((q, width), BF16),
                        pltpu.VMEM((q, width), F32)],
        compiler_params=_params("parallel", "arbitrary"),
    )(xbc, small, z, conv_w, conv_b.reshape(1, cdim), pad(a_log), pad(dt_bias),
      jnp.repeat(d_skip, SSD_HEAD_DIM).reshape(1, width), norm_w.reshape(1, width))


def _gla_body(q_ref, k_ref, v_ref, sm_ref, g_ref, w2_ref, gb_ref, nw_ref, o_ref, st_ref, *, mid_lane):
    c = GLA_CHUNK
    kw = q_ref.shape[-1]
    dk = kw // GLA_HEADS
    dv = v_ref.shape[-1] // GLA_HEADS
    n_pairs = kw // LANES

    @pl.when(pl.program_id(1) == 0)
    def _():
        st_ref[...] = jnp.zeros_like(st_ref)

    ri = lax.broadcasted_iota(jnp.int32, (c, c), 0)
    ci = lax.broadcasted_iota(jnp.int32, (c, c), 1)
    causal = ri >= ci
    lo = lax.broadcasted_iota(jnp.int32, (c, LANES), 1) < HALF
    lo_sq = lax.broadcasted_iota(jnp.int32, (dv, LANES), 1) < HALF

    sm_hi, sm_mid = _split_bf16(sm_ref[0], 2)
    sm_lane = lax.broadcasted_iota(jnp.int32, sm_hi.shape, 1)
    pre = jnp.dot(jnp.where(sm_lane < mid_lane, sm_hi, sm_mid), w2_ref[...],
                  preferred_element_type=F32) + gb_ref[...]
    gcs_all = _chunk_cumsum(_log_sigmoid(pre) * (LOG2_E / GLA_GATE_NORMALIZER), c)

    chunks = [slice(ch * c, (ch + 1) * c) for ch in range(q_ref.shape[1] // c)]
    pairs = [slice(p * LANES, (p + 1) * LANES) for p in range(n_pairs)]
    causal2 = jnp.concatenate([causal, causal], axis=0)

    def by_head(x):
        return jnp.concatenate([jnp.where(lo, x, 0.0), jnp.where(lo, 0.0, x)], axis=0).astype(BF16)

    q_in, k_in, q_st, k_st, g_last = [], [], [], [], []
    for rows in chunks:
        gcs = gcs_all[rows, :]
        g_mid = gcs[c // 2:c // 2 + 1, :]
        g_last.append(gcs[c - 1:c, :])
        qs = q_ref[0, rows, :] * (dk ** -0.5)
        ks = k_ref[0, rows, :]
        q_mid = qs * jnp.exp2(gcs - g_mid)
        k_mid = ks * jnp.exp2(g_mid - gcs)
        q_in.append(q_mid)
        k_in.append(k_mid.astype(BF16))
        q_st.append(q_mid * jnp.exp2(g_mid))
        k_st.append((k_mid * jnp.exp2(g_last[-1] - g_mid)).astype(BF16))

    scores = [[_dot_nt(by_head(q_in[ch][:, col]), k_in[ch][:, col]) for col in pairs]
              for ch in range(len(chunks))]
    scores = [[jnp.where(causal2, s, 0.0).astype(BF16) for s in row] for row in scores]

    o_intra, kv_t = [], []
    for ch, rows in enumerate(chunks):
        o_row, kv_row = [], []
        for p, col in enumerate(pairs):
            v_a = v_ref[0, rows, (2 * p) * dv:(2 * p + 1) * dv].astype(BF16)
            v_b = v_ref[0, rows, (2 * p + 1) * dv:(2 * p + 2) * dv].astype(BF16)
            o_row.append((jnp.dot(scores[ch][p][0:c, :], v_a, preferred_element_type=F32),
                          jnp.dot(scores[ch][p][c:2 * c, :], v_b, preferred_element_type=F32)))
            kv_row.append(jnp.where(lo_sq, _dot_tn(v_a, k_st[ch][:, col]), _dot_tn(v_b, k_st[ch][:, col])))
        o_intra.append(o_row)
        kv_t.append(kv_row)

    o_inter = [[None] * n_pairs for _ in chunks]
    for p, col in enumerate(pairs):
        st = st_ref[:, col]
        for ch in range(len(chunks)):
            o_inter[ch][p] = _dot_nt(by_head(q_st[ch][:, col]), st)
            st = st * jnp.exp2(g_last[ch][:, col]) + kv_t[ch][p]
        st_ref[:, col] = st

    for ch, rows in enumerate(chunks):
        for p in range(n_pairs):
            for half in range(2):
                vcol = slice((2 * p + half) * dv, (2 * p + half + 1) * dv)
                o = o_intra[ch][p][half] + o_inter[ch][p][half * c:(half + 1) * c, :]
                o_ref[0, rows, vcol] = (_rms(o, nw_ref[...]) * _silu(g_ref[0, rows, vcol])).astype(o_ref.dtype)


def _gla(q, k, v, small, g, gate_w2, gate_b, norm_w, lr_lane):
    b, l, kw = q.shape
    vw = v.shape[-1]
    rank = gate_w2.shape[0]
    t = GLA_ROWS
    w2_hi, w2_mid = _split_bf16(gate_w2, 2)
    w2_pad = jnp.zeros((LANES, kw), BF16)
    for copy, part in enumerate((w2_hi, w2_mid, w2_hi)):
        w2_pad = w2_pad.at[lr_lane + copy * rank:lr_lane + (copy + 1) * rank, :].set(part)
    blk = lambda w: pl.BlockSpec((1, t, w), lambda i, c: (i, c, 0))
    return pl.pallas_call(
        functools.partial(_gla_body, mid_lane=lr_lane + 2 * rank),
        name="gla",
        grid=(b, l // t),
        in_specs=[blk(kw), blk(kw), blk(vw), blk(LANES), blk(vw),
                  _resident((LANES, kw)), _resident((1, kw)), _resident((1, vw // GLA_HEADS))],
        out_specs=blk(vw),
        out_shape=jax.ShapeDtypeStruct((b, l, vw), BF16),
        scratch_shapes=[pltpu.VMEM((vw // GLA_HEADS, kw), F32)],
        compiler_params=_params("parallel", "arbitrary"),
    )(q, k, v, small, g, w2_pad, gate_b.reshape(1, kw), norm_w.reshape(1, -1))


def _out_proj_body(*refs, n_in, final_norm):
    a_refs, x_ref, w_ref = refs[:n_in], refs[n_in], refs[n_in + 1]
    rest = refs[n_in + 2:]
    acc = x_ref[...]
    off = 0
    for a_ref in a_refs:
        width = a_ref.shape[1]
        acc = acc + jnp.dot(a_ref[...].astype(BF16), w_ref[off:off + width, :], preferred_element_type=F32)
        off += width
    if final_norm:
        fw_ref, o_ref = rest
        o_ref[...] = _rms(acc, fw_ref[...])
    else:
        (o_ref,) = rest
        o_ref[...] = acc


def _out_proj(acts, x2d, weight, final_w=None):
    m, d = x2d.shape
    tm = OUT_PROJ_ROWS
    n_in = len(acts)
    assert sum(a.shape[1] for a in acts) == weight.shape[0]
    row = lambda w: pl.BlockSpec((tm, w), lambda i: (i, 0))
    in_specs = [row(a.shape[1]) for a in acts] + [row(d), _resident(weight.shape)]
    args = list(acts) + [x2d, weight]
    if final_w is not None:
        in_specs.append(_resident((1, d)))
        args.append(final_w.reshape(1, d))
    return pl.pallas_call(
        functools.partial(_out_proj_body, n_in=n_in, final_norm=final_w is not None),
        name="out_proj",
        grid=(m // tm,),
        in_specs=in_specs,
        out_specs=row(d),
        out_shape=jax.ShapeDtypeStruct((m, d), F32),
        compiler_params=_params("parallel"),
    )(*args)


class _MobaQueryBlock:
    def __init__(self, n_past, q_ref, z_ref, o_ref, kb_ref, vt_ref, kmean_ref, s_ref):
        blk = MOBA_BLOCK
        self.n_past, self.z_ref, self.o_ref, self.kb_ref, self.vt_ref = n_past, z_ref, o_ref, kb_ref, vt_ref
        self.own = slice(n_past * blk, (n_past + 1) * blk)
        self.s_ref = s_ref.at[n_past % s_ref.shape[0]]
        q2 = q_ref[0, self.own, :]
        lo = lax.broadcasted_iota(jnp.int32, (blk, LANES), 1) < HALF
        q_cat = jnp.concatenate([jnp.where(lo, q2, 0.0), jnp.where(lo, 0.0, q2)], axis=0)
        scale = MOBA_HEAD_DIM ** -0.5
        key_i = lax.broadcasted_iota(jnp.int32, (blk, 2 * blk), 0)
        qry_i = lax.broadcasted_iota(jnp.int32, (blk, 2 * blk), 1)
        causal_t = key_i <= jnp.where(qry_i < blk, qry_i, qry_i - blk)

        self.masks = []
        if n_past:
            nb = kmean_ref.shape[0] // 2
            parts = [_dot_nt(kmean_ref[...], q_part) for q_part in _split_bf16(q_cat * scale, 2)]
            gate = (parts[0][0:nb] + parts[0][nb:]) + (parts[1][0:nb] + parts[1][nb:])
            g_rows = [gate[n:n + 1, :] for n in range(n_past)]
            for n in range(n_past):
                rank = jnp.zeros((1, 2 * blk), F32)
                for m in range(n_past):
                    if m != n:
                        ahead = (g_rows[m] >= g_rows[n]) if m < n else (g_rows[m] > g_rows[n])
                        rank = rank + jnp.where(ahead, 1.0, 0.0)
                self.masks.append(rank < MOBA_TOPK)
        self.masks.append(causal_t)
        self.q_s = (q_cat * (scale * LOG2_E)).astype(BF16)
        self.m_run = None
        self.p = []

    def logits_step(self, j):
        rows = slice(j * MOBA_BLOCK, (j + 1) * MOBA_BLOCK)
        sj = jnp.where(self.masks[j], _dot_nt(self.kb_ref[rows, :], self.q_s), -jnp.inf)
        self.s_ref[rows, :] = sj
        mj = jnp.max(sj, axis=0, keepdims=True)
        self.m_run = mj if self.m_run is None else jnp.maximum(self.m_run, mj)

    def value_step(self, j):
        rows = slice(j * MOBA_BLOCK, (j + 1) * MOBA_BLOCK)
        self.p.append(jnp.exp2(self.s_ref[rows, :] - self.m_run).astype(BF16))

    def finish(self):
        blk = MOBA_BLOCK
        nk = len(self.p) * blk
        p_all = jnp.concatenate(self.p, axis=0)
        outs = []
        for half in range(2):
            vt = self.vt_ref[half * VT_ROWS:(half + 1) * VT_ROWS, 0:nk]
            acc = jnp.dot(vt, p_all[:, half * blk:(half + 1) * blk], preferred_element_type=F32)
            outs.append(acc[0:HALF, :] / acc[HALF:HALF + 1, :])
        o_t = jnp.concatenate(outs, axis=0)
        self.o_ref[0, self.own, :] = (o_t.T * _silu(self.z_ref[0, self.own, :])).astype(self.o_ref.dtype)


def _moba_body(q_ref, k_ref, v_ref, z_ref, o_ref, vt_ref, kmean_ref, s_ref):
    blk = MOBA_BLOCK
    nb = k_ref.shape[1] // blk
    kmean = []
    for n in range(nb):
        rows = slice(n * blk, (n + 1) * blk)
        kmean.append(jnp.mean(k_ref[0, rows, :].astype(F32), axis=0, keepdims=True))
        v_t = v_ref[0, rows, :].astype(F32).T.astype(BF16)
        for half in range(2):
            vt_ref[half * VT_ROWS:half * VT_ROWS + HALF, rows] = v_t[half * HALF:(half + 1) * HALF, :]
    for half in range(2):
        vt_ref[half * VT_ROWS + HALF:(half + 1) * VT_ROWS, :] = jnp.ones((BF16_SUBLANES, vt_ref.shape[1]), BF16)
    kmean_ref[...] = jnp.concatenate(_split_bf16(jnp.concatenate(kmean, axis=0), 2), axis=0)
    make = functools.partial(_MobaQueryBlock, q_ref=q_ref, z_ref=z_ref, o_ref=o_ref, kb_ref=k_ref.at[0],
                             vt_ref=vt_ref, kmean_ref=kmean_ref, s_ref=s_ref)
    blocks = {0: make(0), 1: make(1)}
    blocks[0].logits_step(0)
    for i in range(nb):
        if i + 2 < nb:
            blocks[i + 2] = make(i + 2)
        cur, nxt = blocks.pop(i), blocks.get(i + 1)
        for j in range(i + 2):
            if j <= i:
                cur.value_step(j)
            if nxt is not None:
                nxt.logits_step(j)
        cur.finish()


def _moba(q, k, v, z):
    b, l, w = q.shape
    blk = MOBA_BLOCK
    nb = l // blk
    spec = pl.BlockSpec((1, l, LANES), lambda bi, hp: (bi, 0, hp))
    return pl.pallas_call(
        _moba_body,
        name="moba",
        grid=(b, w // LANES),
        in_specs=[spec, spec, spec, spec],
        out_specs=spec,
        out_shape=jax.ShapeDtypeStruct((b, l, w), BF16),
        scratch_shapes=[pltpu.VMEM((2 * VT_ROWS, l), BF16),
                        pltpu.VMEM((2 * nb, LANES), BF16),
                        pltpu.VMEM((2, l, 2 * blk), F32)],
        compiler_params=_params("parallel", "parallel"),
    )(q, k, v, z)


def _even_layer(x, norm_w, w_in_bf, conv_w, conv_b, a_log, dt_bias, d_skip, ssd_norm_w, gate_w2, gate_b,
                gla_norm_w, side_casts=()):
    b, l, d = x.shape
    width = ssd_norm_w.shape[0]
    cdim = conv_w.shape[1]
    n_heads = a_log.shape[0]
    rank, kw = gate_w2.shape
    vw = gla_norm_w.shape[0] * GLA_HEADS
    cuts = [0]
    for s in (width, cdim, n_heads, kw, kw, vw, rank, vw):
        cuts.append(cuts[-1] + s)
    seg = lambda j: w_in_bf[:, cuts[j]:cuts[j + 1]]
    pad = jnp.zeros((d, LANES - n_heads - GATE_COPIES * rank), BF16)
    w_small = jnp.concatenate([seg(2)] + [seg(6)] * GATE_COPIES + [pad], axis=1)
    segments = [(cuts[j], cuts[j + 1] - cuts[j]) for j in (0, 1, 3, 4, 5, 7)]
    outs = _norm_proj(x.reshape(b * l, d), norm_w, w_in_bf, segments, [F32, F32, F32, F32, BF16, F32, F32],
                      extras=[w_small], side_casts=side_casts)
    z_a, xbc, q_b, k_b, v_b, g_b, small = [u.reshape(b, l, -1) for u in outs[:7]]
    y_a = _ssd(xbc, small, z_a, conv_w, conv_b, a_log, dt_bias, d_skip, ssd_norm_w)
    o_b = _gla(q_b, k_b, v_b, small, g_b, gate_w2, gate_b, gla_norm_w, lr_lane=n_heads)
    return [y_a.reshape(b * l, width), o_b.reshape(b * l, vw)], outs[7:]


def _odd_layer(x, norm_w, w_in_bf, side_casts=()):
    b, l, d = x.shape
    w = w_in_bf.shape[1] // 4
    outs = _norm_proj(x.reshape(b * l, d), norm_w, w_in_bf, [(j * w, w) for j in range(4)],
                      [F32, BF16, BF16, F32], side_casts=side_casts)
    q, k, v, z = [u.reshape(b, l, w) for u in outs[:4]]
    return [_moba(q, k, v, z).reshape(b * l, w)], outs[4:]


def kernel(x, even_norm, even_w_in, even_conv_w, even_conv_b, even_a_log, even_dt_bias, even_d_skip, even_ssd_norm,
           even_gate_w2, even_gate_b, even_gla_norm, even_w_out, odd_norm, odd_w_in, odd_w_out, final_norm):
    b, l, d = x.shape
    depth = even_norm.shape[0] + odd_norm.shape[0]
    w_in = lambda layer: (even_w_in if layer % 2 == 0 else odd_w_in)[layer // 2]
    w_out = lambda layer: (even_w_out if layer % 2 == 0 else odd_w_out)[layer // 2]
    keys = [("out", 0)] + [(kind, layer) for layer in range(1, depth) for kind in ("in", "out")]
    pending = [w_in(layer) if kind == "in" else w_out(layer) for kind, layer in keys]
    bf = {("in", 0): w_in(0).astype(BF16)}
    for layer in range(depth):
        i = layer // 2
        side = pending if layer == 0 else ()
        if layer % 2 == 0:
            acts, cast = _even_layer(x, even_norm[i], bf["in", layer], even_conv_w[i], even_conv_b[i], even_a_log[i],
                                     even_dt_bias[i], even_d_skip[i], even_ssd_norm[i], even_gate_w2[i],
                                     even_gate_b[i], even_gla_norm[i], side_casts=side)
        else:
            acts, cast = _odd_layer(x, odd_norm[i], bf["in", layer], side_casts=side)
        bf.update(zip(keys, cast))
        last = layer == depth - 1
        x = _out_proj(acts, x.reshape(b * l, d), bf["out", layer], final_norm if last else None).reshape(b, l, d)
    return x
```

```python
import functools

import jax
import jax.numpy as jnp
from jax import lax
from jax.experimental import pallas as pl
from jax.experimental.pallas import tpu as pltpu

F32 = jnp.float32
BF16 = jnp.bfloat16

LANES = 128
SUBLANES = 8
BF16_SUBLANES = 16
VMEM_LIMIT_BYTES = 56 * 1024 * 1024

RMS_EPS = 1e-6
SSD_HEAD_DIM = 64
SSD_GROUPS = 2
SSD_STATE = 64
SSD_CONV = 4
SSD_CHUNK = 128
GLA_HEADS = 8
GLA_GATE_NORMALIZER = 16.0
GLA_CHUNK = 64
MOBA_HEAD_DIM = 64
MOBA_BLOCK = 256
MOBA_TOPK = 3
LOG2_E = 1.4426950408889634
CONV_ROW_STRIDE = 4
CUMSUM_TERMS = 3
GATE_COPIES = 3

IN_PROJ_ROWS = 512
OUT_PROJ_ROWS = 1024
GLA_ROWS = 1024
SSD_ROWS = 512
HALF = LANES // 2
VT_ROWS = HALF + BF16_SUBLANES


def _params(*sem):
    return pltpu.CompilerParams(dimension_semantics=sem, vmem_limit_bytes=VMEM_LIMIT_BYTES)


def _rms(x, w):
    return x * lax.rsqrt(jnp.mean(x * x, axis=-1, keepdims=True) + RMS_EPS) * w


def _silu(x):
    h = 0.5 * x
    return h + h * jnp.tanh(h)


def _softplus(x):
    return jnp.maximum(x, 0.0) + jnp.log1p(jnp.exp(-jnp.abs(x)))


def _log_sigmoid(x):
    return jnp.minimum(x, 0.0) - jnp.log(1.0 + jnp.exp(-jnp.abs(x)))


def _dot(a, b):
    return jnp.dot(a.astype(BF16), b.astype(BF16), preferred_element_type=F32)


def _dot_nt(a, b):
    return lax.dot_general(a.astype(BF16), b.astype(BF16), (((1,), (1,)), ((), ())), preferred_element_type=F32)


def _dot_tn(a, b):
    return lax.dot_general(a.astype(BF16), b.astype(BF16), (((0,), (0,)), ((), ())), preferred_element_type=F32)


def _split_bf16(x, terms):
    parts = []
    for _ in range(terms):
        p = x.astype(BF16)
        parts.append(p)
        x = x - p.astype(F32)
    return parts


def _chunk_cumsum(x, chunk):
    rows = x.shape[0]
    assert chunk & (chunk - 1) == 0
    ri = lax.broadcasted_iota(jnp.int32, (chunk, CUMSUM_TERMS * chunk), 0)
    ci = lax.broadcasted_iota(jnp.int32, (chunk, CUMSUM_TERMS * chunk), 1)
    tri = jnp.where((ci & (chunk - 1)) <= ri, 1.0, 0.0).astype(BF16)
    out = []
    for r0 in range(0, rows, chunk):
        stacked = jnp.concatenate(_split_bf16(x[r0:r0 + chunk, :], CUMSUM_TERMS), axis=0)
        out.append(jnp.dot(tri, stacked, preferred_element_type=F32))
    return out[0] if len(out) == 1 else jnp.concatenate(out, axis=0)


def _resident(shape):
    return pl.BlockSpec(shape, lambda *_: (0,) * len(shape), pipeline_mode=pl.Buffered(1))


def _norm_proj_body(x_ref, nw_ref, w_ref, *refs, offsets, n_extra, n_side):
    n_out = len(offsets) + n_extra
    extra_refs, side_in = refs[:n_extra], refs[n_extra:n_extra + n_side]
    o_refs = refs[n_extra + n_side:n_extra + n_side + n_out]
    side_out = refs[n_extra + n_side + n_out:n_extra + 2 * n_side + n_out]
    for s_in, s_out in zip(side_in, side_out):
        s_out[...] = s_in[...].astype(s_out.dtype)
    shifted = [i for i, off in enumerate(offsets) if off % LANES]
    starts, pos = {}, 0
    for i in shifted:
        starts[i] = pos
        pos += o_refs[i].shape[1]

    if shifted:
        al_ref = refs[-1]

        @pl.when(pl.program_id(0) == 0)
        def _():
            for i in shifted:
                width = o_refs[i].shape[1]
                al_ref[:, starts[i]:starts[i] + width] = w_ref[:, offsets[i]:offsets[i] + width]

    h = _rms(x_ref[...], nw_ref[...]).astype(BF16)
    for i, off in enumerate(offsets):
        width = o_refs[i].shape[1]
        w = al_ref[:, starts[i]:starts[i] + width] if i in starts else w_ref[:, off:off + width]
        o_refs[i][...] = jnp.dot(h, w, preferred_element_type=F32).astype(o_refs[i].dtype)
    for e_ref, o_ref in zip(extra_refs, o_refs[len(offsets):]):
        o_ref[...] = jnp.dot(h, e_ref[...], preferred_element_type=F32).astype(o_ref.dtype)


def _norm_proj(x2d, norm_w, weight, segments, out_dtypes, extras=(), side_casts=()):
    m, d = x2d.shape
    tm = IN_PROJ_ROWS
    steps = m // tm
    assert all(w % LANES == 0 for _, w in segments)
    assert all(s.shape[0] % (steps * BF16_SUBLANES) == 0 for s in side_casts)
    widths = [w for _, w in segments] + [e.shape[1] for e in extras]
    shifted_cols = sum(w for off, w in segments if off % LANES)
    side_specs = [pl.BlockSpec((s.shape[0] // steps, s.shape[1]), lambda i: (i, 0)) for s in side_casts]
    return pl.pallas_call(
        functools.partial(_norm_proj_body, offsets=tuple(off for off, _ in segments), n_extra=len(extras),
                          n_side=len(side_casts)),
        name="norm_proj",
        grid=(steps,),
        in_specs=[pl.BlockSpec((tm, d), lambda i: (i, 0)), _resident((1, d)), _resident(weight.shape)]
        + [_resident(e.shape) for e in extras] + side_specs,
        out_specs=[pl.BlockSpec((tm, w), lambda i: (i, 0)) for w in widths] + side_specs,
        out_shape=[jax.ShapeDtypeStruct((m, w), dt) for w, dt in zip(widths, out_dtypes)]
        + [jax.ShapeDtypeStruct(s.shape, BF16) for s in side_casts],
        scratch_shapes=[pltpu.VMEM((d, shifted_cols), weight.dtype)] if shifted_cols else [],
        compiler_params=_params("arbitrary"),
    )(x2d, norm_w.reshape(1, d), weight, *extras, *side_casts)


def _ssd_body(xbc_ref, sm_ref, z_ref, cw_ref, cb_ref, alog_ref, dtb_ref, dskip_ref, nw_ref,
              y_ref, h_ref, xpad_ref, act_ref, xw_ref, ycat_ref):
    q = SSD_CHUNK
    t = xbc_ref.shape[1]
    width = y_ref.shape[-1]
    n_pairs = width // LANES
    n_slabs = xpad_ref.shape[0]
    gstate = SSD_GROUPS * SSD_STATE
    gwidth = width // SSD_GROUPS
    first = pl.program_id(1) == 0

    @pl.when(first)
    def _():
        xpad_ref[:, 0:SUBLANES, :] = jnp.zeros((n_slabs, SUBLANES, LANES), F32)
        h_ref[...] = jnp.zeros_like(h_ref)

    @pl.when(jnp.logical_not(first))
    def _():
        xpad_ref[:, 0:SUBLANES, :] = xpad_ref[:, t:t + SUBLANES, :]

    for c in range(n_slabs):
        lanes = slice(c * LANES, (c + 1) * LANES)
        xpad_ref[c, SUBLANES:SUBLANES + t, :] = xbc_ref[0, :, lanes]
        taps = [jnp.broadcast_to(cw_ref[k:k + 1, lanes], (SUBLANES, LANES)) for k in range(SSD_CONV)]
        bias = jnp.broadcast_to(cb_ref[:, lanes], (SUBLANES, LANES))
        for t0 in range(0, t, SUBLANES * CONV_ROW_STRIDE):
            for g in range(CONV_ROW_STRIDE):
                conv = bias
                for k in range(SSD_CONV):
                    start = SUBLANES + t0 + g - (SSD_CONV - 1 - k)
                    conv = conv + taps[k] * xpad_ref[c, pl.ds(start, SUBLANES, stride=CONV_ROW_STRIDE), :]
                act_ref[c, pl.ds(t0 + g, SUBLANES, stride=CONV_ROW_STRIDE), :] = _silu(conv)

    dt_all = _softplus(sm_ref[0] + dtb_ref[...])
    dta = dt_all * (-jnp.exp(alog_ref[...]) * LOG2_E)
    a_cs_all = _chunk_cumsum(dta, q)
    ri = lax.broadcasted_iota(jnp.int32, (q, q), 0)
    ci = lax.broadcasted_iota(jnp.int32, (q, q), 1)
    causal = ri >= ci
    lane = lax.broadcasted_iota(jnp.int32, (q, LANES), 1)
    lo = lane < HALF
    lo_row = lo[0:1, :]

    h_prev = h_ref[...]
    for rows in [slice(r0, r0 + q) for r0 in range(0, t, q)]:
        bm = act_ref[n_pairs, rows, :]
        cm = act_ref[n_pairs + 1, rows, :]
        dt = dt_all[rows, :]
        a_cs = a_cs_all[rows, :]
        a_cs_t = a_cs.T
        a_last = a_cs[q - 1:q, :]
        bm_t = bm.T
        cm_g = [jnp.where(lo, cm, 0.0), jnp.where(lo, 0.0, cm)]
        cb = [_dot_nt(c, bm) for c in cm_g]
        y_off = [_dot(c, h_prev) for c in cm_g]

        a_last_pairs = []
        for p in range(n_pairs):
            e0, e1 = 2 * p, 2 * p + 1
            g = (p * LANES) // gwidth
            col = slice(p * LANES, (p + 1) * LANES)
            gcol = slice(p * LANES - g * gwidth, (p + 1) * LANES - g * gwidth)
            acs_pair = jnp.where(lo, a_cs[:, e0:e0 + 1], a_cs[:, e1:e1 + 1])
            dt_pair = jnp.where(lo, dt[:, e0:e0 + 1], dt[:, e1:e1 + 1])
            al_pair = jnp.where(lo_row, a_last[:, e0:e0 + 1], a_last[:, e1:e1 + 1])
            a_last_pairs.append(al_pair)
            xs2 = act_ref[p, rows, :]
            xdt = xs2 * dt_pair
            xdt_b = xdt.astype(BF16)
            yd = []
            for e in (e0, e1):
                seg = a_cs[:, e:e + 1] - a_cs_t[e:e + 1, :]
                decay = jnp.exp2(jnp.where(causal, seg, -jnp.inf))
                yd.append(_dot(cb[g] * decay, xdt_b))
            y2 = jnp.where(lo, yd[0], yd[1])
            y2 = y2 + y_off[g][:, gcol] * jnp.exp2(acs_pair) + dskip_ref[:, col] * xs2
            ycat_ref[rows, col] = y2
            xw_ref[rows, col] = (xdt * jnp.exp2(al_pair - acs_pair)).astype(BF16)

        h_next = []
        for g in range(SSD_GROUPS):
            srows = slice(g * SSD_STATE, (g + 1) * SSD_STATE)
            ppg = n_pairs // SSD_GROUPS
            dec = jnp.exp2(jnp.concatenate(a_last_pairs[g * ppg:(g + 1) * ppg], axis=1))
            s_g = _dot(bm_t[srows, :], xw_ref[rows, g * gwidth:(g + 1) * gwidth])
            h_next.append(h_prev[srows, :] * dec + s_g)
        h_prev = jnp.concatenate(h_next, axis=0)
    h_ref[...] = h_prev

    y_ref[0] = _rms(ycat_ref[...] * _silu(z_ref[0]), nw_ref[...]).astype(y_ref.dtype)


def _ssd(xbc, small, z, conv_w, conv_b, a_log, dt_bias, d_skip, norm_w):
    b, l, cdim = xbc.shape
    width = z.shape[-1]
    q = SSD_ROWS
    n_heads = a_log.shape[0]
    pad = lambda v: jnp.pad(v, (0, LANES - n_heads)).reshape(1, LANES)
    gwidth = width // SSD_GROUPS
    blk = lambda w: pl.BlockSpec((1, q, w), lambda i, c: (i, c, 0))
    return pl.pallas_call(
        _ssd_body,
        name="ssd",
        grid=(b, l // q),
        in_specs=[blk(cdim), blk(LANES), blk(width),
                  _resident((SSD_CONV, cdim)), _resident((1, cdim)), _resident((1, LANES)),
                  _resident((1, LANES)), _resident((1, width)), _resident((1, width))],
        out_specs=blk(width),
        out_shape=jax.ShapeDtypeStruct((b, l, width), BF16),
        scratch_shapes=[pltpu.VMEM((SSD_GROUPS * SSD_STATE, gwidth), F32),
                        pltpu.VMEM((cdim // LANES, q + SUBLANES, LANES), F32),
                        pltpu.VMEM((cdim // LANES, q, LANES), F32),
                        pltpu.VMEM((q, width), BF16),
                        pltpu.VMEM((q, width), F32)],
        compiler_params=_params("parallel", "arbitrary"),
    )(xbc, small, z, conv_w, conv_b.reshape(1, cdim), pad(a_log), pad(dt_bias),
      jnp.repeat(d_skip, SSD_HEAD_DIM).reshape(1, width), norm_w.reshape(1, width))


def _gla_body(q_ref, k_ref, v_ref, sm_ref, g_ref, w2_ref, gb_ref, nw_ref, o_ref, st_ref, *, mid_lane):
    c = GLA_CHUNK
    kw = q_ref.shape[-1]
    dk = kw // GLA_HEADS
    dv = v_ref.shape[-1] // GLA_HEADS
    n_pairs = kw // LANES

    @pl.when(pl.program_id(1) == 0)
    def _():
        st_ref[...] = jnp.zeros_like(st_ref)

    ri = lax.broadcasted_iota(jnp.int32, (c, c), 0)
    ci = lax.broadcasted_iota(jnp.int32, (c, c), 1)
    causal = ri >= ci
    lo = lax.broadcasted_iota(jnp.int32, (c, LANES), 1) < HALF
    lo_sq = lax.broadcasted_iota(jnp.int32, (dv, LANES), 1) < HALF

    sm_hi, sm_mid = _split_bf16(sm_ref[0], 2)
    sm_lane = lax.broadcasted_iota(jnp.int32, sm_hi.shape, 1)
    pre = jnp.dot(jnp.where(sm_lane < mid_lane, sm_hi, sm_mid), w2_ref[...],
                  preferred_element_type=F32) + gb_ref[...]
    gcs_all = _chunk_cumsum(_log_sigmoid(pre) * (LOG2_E / GLA_GATE_NORMALIZER), c)

    chunks = [slice(ch * c, (ch + 1) * c) for ch in range(q_ref.shape[1] // c)]
    pairs = [slice(p * LANES, (p + 1) * LANES) for p in range(n_pairs)]
    causal2 = jnp.concatenate([causal, causal], axis=0)

    def by_head(x):
        return jnp.concatenate([jnp.where(lo, x, 0.0), jnp.where(lo, 0.0, x)], axis=0).astype(BF16)

    q_in, k_in, q_st, k_st, g_last = [], [], [], [], []
    for rows in chunks:
        gcs = gcs_all[rows, :]
        g_mid = gcs[c // 2:c // 2 + 1, :]
        g_last.append(gcs[c - 1:c, :])
        qs = q_ref[0, rows, :] * (dk ** -0.5)
        ks = k_ref[0, rows, :]
        q_mid = qs * jnp.exp2(gcs - g_mid)
        k_mid = ks * jnp.exp2(g_mid - gcs)
        q_in.append(q_mid)
        k_in.append(k_mid.astype(BF16))
        q_st.append(q_mid * jnp.exp2(g_mid))
        k_st.append((k_mid * jnp.exp2(g_last[-1] - g_mid)).astype(BF16))

    scores = [[_dot_nt(by_head(q_in[ch][:, col]), k_in[ch][:, col]) for col in pairs]
              for ch in range(len(chunks))]
    scores = [[jnp.where(causal2, s, 0.0).astype(BF16) for s in row] for row in scores]

    o_intra, kv_t = [], []
    for ch, rows in enumerate(chunks):
        o_row, kv_row = [], []
        for p, col in enumerate(pairs):
            v_a = v_ref[0, rows, (2 * p) * dv:(2 * p + 1) * dv].astype(BF16)
            v_b = v_ref[0, rows, (2 * p + 1) * dv:(2 * p + 2) * dv].astype(BF16)
            o_row.append((jnp.dot(scores[ch][p][0:c, :], v_a, preferred_element_type=F32),
                          jnp.dot(scores[ch][p][c:2 * c, :], v_b, preferred_element_type=F32)))
            kv_row.append(jnp.where(lo_sq, _dot_tn(v_a, k_st[ch][:, col]), _dot_tn(v_b, k_st[ch][:, col])))
        o_intra.append(o_row)
        kv_t.append(kv_row)

    o_inter = [[None] * n_pairs for _ in chunks]
    for p, col in enumerate(pairs):
        st = st_ref[:, col]
        for ch in range(len(chunks)):
            o_inter[ch][p] = _dot_nt(by_head(q_st[ch][:, col]), st)
            st = st * jnp.exp2(g_last[ch][:, col]) + kv_t[ch][p]
        st_ref[:, col] = st

    for ch, rows in enumerate(chunks):
        for p in range(n_pairs):
            for half in range(2):
                vcol = slice((2 * p + half) * dv, (2 * p + half + 1) * dv)
                o = o_intra[ch][p][half] + o_inter[ch][p][half * c:(half + 1) * c, :]
                o_ref[0, rows, vcol] = (_rms(o, nw_ref[...]) * _silu(g_ref[0, rows, vcol])).astype(o_ref.dtype)


def _gla(q, k, v, small, g, gate_w2, gate_b, norm_w, lr_lane):
    b, l, kw = q.shape
    vw = v.shape[-1]
    rank = gate_w2.shape[0]
    t = GLA_ROWS
    w2_hi, w2_mid = _split_bf16(gate_w2, 2)
    w2_pad = jnp.zeros((LANES, kw), BF16)
    for copy, part in enumerate((w2_hi, w2_mid, w2_hi)):
        w2_pad = w2_pad.at[lr_lane + copy * rank:lr_lane + (copy + 1) * rank, :].set(part)
    blk = lambda w: pl.BlockSpec((1, t, w), lambda i, c: (i, c, 0))
    return pl.pallas_call(
        functools.partial(_gla_body, mid_lane=lr_lane + 2 * rank),
        name="gla",
        grid=(b, l // t),
        in_specs=[blk(kw), blk(kw), blk(vw), blk(LANES), blk(vw),
                  _resident((LANES, kw)), _resident((1, kw)), _resident((1, vw // GLA_HEADS))],
        out_specs=blk(vw),
        out_shape=jax.ShapeDtypeStruct((b, l, vw), BF16),
        scratch_shapes=[pltpu.VMEM((vw // GLA_HEADS, kw), F32)],
        compiler_params=_params("parallel", "arbitrary"),
    )(q, k, v, small, g, w2_pad, gate_b.reshape(1, kw), norm_w.reshape(1, -1))


def _out_proj_body(*refs, n_in, final_norm):
    a_refs, x_ref, w_ref = refs[:n_in], refs[n_in], refs[n_in + 1]
    rest = refs[n_in + 2:]
    acc = x_ref[...]
    off = 0
    for a_ref in a_refs:
        width = a_ref.shape[1]
        acc = acc + jnp.dot(a_ref[...].astype(BF16), w_ref[off:off + width, :], preferred_element_type=F32)
        off += width
    if final_norm:
        fw_ref, o_ref = rest
        o_ref[...] = _rms(acc, fw_ref[...])
    else:
        (o_ref,) = rest
        o_ref[...] = acc


def _out_proj(acts, x2d, weight, final_w=None):
    m, d = x2d.shape
    tm = OUT_PROJ_ROWS
    n_in = len(acts)
    assert sum(a.shape[1] for a in acts) == weight.shape[0]
    row = lambda w: pl.BlockSpec((tm, w), lambda i: (i, 0))
    in_specs = [row(a.shape[1]) for a in acts] + [row(d), _resident(weight.shape)]
    args = list(acts) + [x2d, weight]
    if final_w is not None:
        in_specs.append(_resident((1, d)))
        args.append(final_w.reshape(1, d))
    return pl.pallas_call(
        functools.partial(_out_proj_body, n_in=n_in, final_norm=final_w is not None),
        name="out_proj",
        grid=(m // tm,),
        in_specs=in_specs,
        out_specs=row(d),
        out_shape=jax.ShapeDtypeStruct((m, d), F32),
        compiler_params=_params("parallel"),
    )(*args)


class _MobaQueryBlock:
    def __init__(self, n_past, q_ref, z_ref, o_ref, kb_ref, vt_ref, kmean_ref, s_ref):
        blk = MOBA_BLOCK
        self.n_past, self.z_ref, self.o_ref, self.kb_ref, self.vt_ref = n_past, z_ref, o_ref, kb_ref, vt_ref
        self.own = slice(n_past * blk, (n_past + 1) * blk)
        self.s_ref = s_ref.at[n_past % s_ref.shape[0]]
        q2 = q_ref[0, self.own, :]
        lo = lax.broadcasted_iota(jnp.int32, (blk, LANES), 1) < HALF
        q_cat = jnp.concatenate([jnp.where(lo, q2, 0.0), jnp.where(lo, 0.0, q2)], axis=0)
        scale = MOBA_HEAD_DIM ** -0.5
        key_i = lax.broadcasted_iota(jnp.int32, (blk, 2 * blk), 0)
        qry_i = lax.broadcasted_iota(jnp.int32, (blk, 2 * blk), 1)
        causal_t = key_i <= jnp.where(qry_i < blk, qry_i, qry_i - blk)

        self.masks = []
        if n_past:
            nb = kmean_ref.shape[0] // 2
            parts = [_dot_nt(kmean_ref[...], q_part) for q_part in _split_bf16(q_cat * scale, 2)]
            gate = (parts[0][0:nb] + parts[0][nb:]) + (parts[1][0:nb] + parts[1][nb:])
            g_rows = [gate[n:n + 1, :] for n in range(n_past)]
            for n in range(n_past):
                rank = jnp.zeros((1, 2 * blk), F32)
                for m in range(n_past):
                    if m != n:
                        ahead = (g_rows[m] >= g_rows[n]) if m < n else (g_rows[m] > g_rows[n])
                        rank = rank + jnp.where(ahead, 1.0, 0.0)
                self.masks.append(rank < MOBA_TOPK)
        self.masks.append(causal_t)
        self.q_s = (q_cat * (scale * LOG2_E)).astype(BF16)
        self.m_run = None
        self.p = []

    def logits_step(self, j):
        rows = slice(j * MOBA_BLOCK, (j + 1) * MOBA_BLOCK)
        sj = jnp.where(self.masks[j], _dot_nt(self.kb_ref[rows, :], self.q_s), -jnp.inf)
        self.s_ref[rows, :] = sj
        mj = jnp.max(sj, axis=0, keepdims=True)
        self.m_run = mj if self.m_run is None else jnp.maximum(self.m_run, mj)

    def value_step(self, j):
        rows = slice(j * MOBA_BLOCK, (j + 1) * MOBA_BLOCK)
        self.p.append(jnp.exp2(self.s_ref[rows, :] - self.m_run).astype(BF16))

    def finish(self):
        blk = MOBA_BLOCK
        nk = len(self.p) * blk
        p_all = jnp.concatenate(self.p, axis=0)
        outs = []
        for half in range(2):
            vt = self.vt_ref[half * VT_ROWS:(half + 1) * VT_ROWS, 0:nk]
            acc = jnp.dot(vt, p_all[:, half * blk:(half + 1) * blk], preferred_element_type=F32)
            outs.append(acc[0:HALF, :] / acc[HALF:HALF + 1, :])
        o_t = jnp.concatenate(outs, axis=0)
        self.o_ref[0, self.own, :] = (o_t.T * _silu(self.z_ref[0, self.own, :])).astype(self.o_ref.dtype)


def _moba_body(q_ref, k_ref, v_ref, z_ref, o_ref, vt_ref, kmean_ref, s_ref):
    blk = MOBA_BLOCK
    nb = k_ref.shape[1] // blk
    kmean = []
    for n in range(nb):
        rows = slice(n * blk, (n + 1) * blk)
        kmean.append(jnp.mean(k_ref[0, rows, :].astype(F32), axis=0, keepdims=True))
        v_t = v_ref[0, rows, :].astype(F32).T.astype(BF16)
        for half in range(2):
            vt_ref[half * VT_ROWS:half * VT_ROWS + HALF, rows] = v_t[half * HALF:(half + 1) * HALF, :]
    for half in range(2):
        vt_ref[half * VT_ROWS + HALF:(half + 1) * VT_ROWS, :] = jnp.ones((BF16_SUBLANES, vt_ref.shape[1]), BF16)
    kmean_ref[...] = jnp.concatenate(_split_bf16(jnp.concatenate(kmean, axis=0), 2), axis=0)
    make = functools.partial(_MobaQueryBlock, q_ref=q_ref, z_ref=z_ref, o_ref=o_ref, kb_ref=k_ref.at[0],
                             vt_ref=vt_ref, kmean_ref=kmean_ref, s_ref=s_ref)
    blocks = {0: make(0), 1: make(1)}
    blocks[0].logits_step(0)
    for i in range(nb):
        if i + 2 < nb:
            blocks[i + 2] = make(i + 2)
        cur, nxt = blocks.pop(i), blocks.get(i + 1)
        for j in range(i + 2):
            if j <= i:
                cur.value_step(j)
            if nxt is not None:
                nxt.logits_step(j)
        cur.finish()


def _moba(q, k, v, z):
    b, l, w = q.shape
    blk = MOBA_BLOCK
    nb = l // blk
    spec = pl.BlockSpec((1, l, LANES), lambda bi, hp: (bi, 0, hp))
    return pl.pallas_call(
        _moba_body,
        name="moba",
        grid=(b, w // LANES),
        in_specs=[spec, spec, spec, spec],
        out_specs=spec,
        out_shape=jax.ShapeDtypeStruct((b, l, w), BF16),
        scratch_shapes=[pltpu.VMEM((2 * VT_ROWS, l), BF16),
                        pltpu.VMEM((2 * nb, LANES), BF16),
                        pltpu.VMEM((2, l, 2 * blk), F32)],
        compiler_params=_params("parallel", "parallel"),
    )(q, k, v, z)


def _even_layer(x, norm_w, w_in_bf, conv_w, conv_b, a_log, dt_bias, d_skip, ssd_norm_w, gate_w2, gate_b,
                gla_norm_w, side_casts=()):
    b, l, d = x.shape
    width = ssd_norm_w.shape[0]
    cdim = conv_w.shape[1]
    n_heads = a_log.shape[0]
    rank, kw = gate_w2.shape
    vw = gla_norm_w.shape[0] * GLA_HEADS
    cuts = [0]
    for s in (width, cdim, n_heads, kw, kw, vw, rank, vw):
        cuts.append(cuts[-1] + s)
    seg = lambda j: w_in_bf[:, cuts[j]:cuts[j + 1]]
    pad = jnp.zeros((d, LANES - n_heads - GATE_COPIES * rank), BF16)
    w_small = jnp.concatenate([seg(2)] + [seg(6)] * GATE_COPIES + [pad], axis=1)
    segments = [(cuts[j], cuts[j + 1] - cuts[j]) for j in (0, 1, 3, 4, 5, 7)]
    outs = _norm_proj(x.reshape(b * l, d), norm_w, w_in_bf, segments, [F32, F32, F32, F32, BF16, F32, F32],
                      extras=[w_small], side_casts=side_casts)
    z_a, xbc, q_b, k_b, v_b, g_b, small = [u.reshape(b, l, -1) for u in outs[:7]]
    y_a = _ssd(xbc, small, z_a, conv_w, conv_b, a_log, dt_bias, d_skip, ssd_norm_w)
    o_b = _gla(q_b, k_b, v_b, small, g_b, gate_w2, gate_b, gla_norm_w, lr_lane=n_heads)
    return [y_a.reshape(b * l, width), o_b.reshape(b * l, vw)], outs[7:]


def _odd_layer(x, norm_w, w_in_bf, side_casts=()):
    b, l, d = x.shape
    w = w_in_bf.shape[1] // 4
    outs = _norm_proj(x.reshape(b * l, d), norm_w, w_in_bf, [(j * w, w) for j in range(4)],
                      [F32, BF16, BF16, F32], side_casts=side_casts)
    q, k, v, z = [u.reshape(b, l, w) for u in outs[:4]]
    return [_moba(q, k, v, z).reshape(b * l, w)], outs[4:]


def kernel(x, even_norm, even_w_in, even_conv_w, even_conv_b, even_a_log, even_dt_bias, even_d_skip, even_ssd_norm,
           even_gate_w2, even_gate_b, even_gla_norm, even_w_out, odd_norm, odd_w_in, odd_w_out, final_norm):
    b, l, d = x.shape
    depth = even_norm.shape[0] + odd_norm.shape[0]
    w_in = lambda layer: (even_w_in if layer % 2 == 0 else odd_w_in)[layer // 2]
    w_out = lambda layer: (even_w_out if layer % 2 == 0 else odd_w_out)[layer // 2]
    keys = [("out", 0)] + [(kind, layer) for layer in range(1, depth) for kind in ("in", "out")]
    pending = [w_in(layer) if kind == "in" else w_out(layer) for kind, layer in keys]
    bf = {("in", 0): w_in(0).astype(BF16)}
    for layer in range(depth):
        i = layer // 2
        side = pending if layer == 0 else ()
        if layer % 2 == 0:
            acts, cast = _even_layer(x, even_norm[i], bf["in", layer], even_conv_w[i], even_conv_b[i], even_a_log[i],
                                     even_dt_bias[i], even_d_skip[i], even_ssd_norm[i], even_gate_w2[i],
                                     even_gate_b[i], even_gla_norm[i], side_casts=side)
        else:
            acts, cast = _odd_layer(x, odd_norm[i], bf["in", layer], side_casts=side)
        bf.update(zip(keys, cast))
        last = layer == depth - 1
        x = _out_proj(acts, x.reshape(b * l, d), bf["out", layer], final_norm if last else None).reshape(b, l, d)
    return x
```

```python
import functools

import jax
import jax.numpy as jnp
from jax import lax
from jax.experimental import pallas as pl
from jax.experimental.pallas import tpu as pltpu

F32 = jnp.float32
BF16 = jnp.bfloat16

LANES = 128
SUBLANES = 8
BF16_SUBLANES = 16
VMEM_LIMIT_BYTES = 56 * 1024 * 1024

RMS_EPS = 1e-6
SSD_HEAD_DIM = 64
SSD_GROUPS = 2
SSD_STATE = 64
SSD_CONV = 4
SSD_CHUNK = 128
GLA_HEADS = 8
GLA_GATE_NORMALIZER = 16.0
GLA_CHUNK = 64
MOBA_HEAD_DIM = 64
MOBA_BLOCK = 256
MOBA_TOPK = 3
LOG2_E = 1.4426950408889634
CONV_ROW_STRIDE = 4
CUMSUM_TERMS = 3
GATE_COPIES = 3

IN_PROJ_ROWS = 512
OUT_PROJ_ROWS = 1024
GLA_ROWS = 1024
SSD_ROWS = 512
HALF = LANES // 2
VT_ROWS = HALF + BF16_SUBLANES


def _params(*sem):
    return pltpu.CompilerParams(dimension_semantics=sem, vmem_limit_bytes=VMEM_LIMIT_BYTES)


def _rms(x, w):
    return x * lax.rsqrt(jnp.mean(x * x, axis=-1, keepdims=True) + RMS_EPS) * w


def _silu(x):
    h = 0.5 * x
    return h + h * jnp.tanh(h)


def _softplus(x):
    return jnp.maximum(x, 0.0) + jnp.log1p(jnp.exp(-jnp.abs(x)))


def _log_sigmoid(x):
    return jnp.minimum(x, 0.0) - jnp.log(1.0 + jnp.exp(-jnp.abs(x)))


def _dot(a, b):
    return jnp.dot(a.astype(BF16), b.astype(BF16), preferred_element_type=F32)


def _dot_nt(a, b):
    return lax.dot_general(a.astype(BF16), b.astype(BF16), (((1,), (1,)), ((), ())), preferred_element_type=F32)


def _dot_tn(a, b):
    return lax.dot_general(a.astype(BF16), b.astype(BF16), (((0,), (0,)), ((), ())), preferred_element_type=F32)


def _split_bf16(x, terms):
    parts = []
    for _ in range(terms):
        p = x.astype(BF16)
        parts.append(p)
        x = x - p.astype(F32)
    return parts


def _chunk_cumsum(x, chunk):
    rows = x.shape[0]
    assert chunk & (chunk - 1) == 0
    ri = lax.broadcasted_iota(jnp.int32, (chunk, CUMSUM_TERMS * chunk), 0)
    ci = lax.broadcasted_iota(jnp.int32, (chunk, CUMSUM_TERMS * chunk), 1)
    tri = jnp.where((ci & (chunk - 1)) <= ri, 1.0, 0.0).astype(BF16)
    out = []
    for r0 in range(0, rows, chunk):
        stacked = jnp.concatenate(_split_bf16(x[r0:r0 + chunk, :], CUMSUM_TERMS), axis=0)
        out.append(jnp.dot(tri, stacked, preferred_element_type=F32))
    return out[0] if len(out) == 1 else jnp.concatenate(out, axis=0)


def _resident(shape):
    return pl.BlockSpec(shape, lambda *_: (0,) * len(shape), pipeline_mode=pl.Buffered(1))


def _norm_proj_body(x_ref, nw_ref, w_ref, *refs, offsets, n_extra, n_side):
    n_out = len(offsets) + n_extra
    extra_refs, side_in = refs[:n_extra], refs[n_extra:n_extra + n_side]
    o_refs = refs[n_extra + n_side:n_extra + n_side + n_out]
    side_out = refs[n_extra + n_side + n_out:n_extra + 2 * n_side + n_out]
    for s_in, s_out in zip(side_in, side_out):
        s_out[...] = s_in[...].astype(s_out.dtype)
    shifted = [i for i, off in enumerate(offsets) if off % LANES]
    starts, pos = {}, 0
    for i in shifted:
        starts[i] = pos
        pos += o_refs[i].shape[1]

    if shifted:
        al_ref = refs[-1]

        @pl.when(pl.program_id(0) == 0)
        def _():
            for i in shifted:
                width = o_refs[i].shape[1]
                al_ref[:, starts[i]:starts[i] + width] = w_ref[:, offsets[i]:offsets[i] + width]

    h = _rms(x_ref[...], nw_ref[...]).astype(BF16)
    for i, off in enumerate(offsets):
        width = o_refs[i].shape[1]
        w = al_ref[:, starts[i]:starts[i] + width] if i in starts else w_ref[:, off:off + width]
        o_refs[i][...] = jnp.dot(h, w, preferred_element_type=F32).astype(o_refs[i].dtype)
    for e_ref, o_ref in zip(extra_refs, o_refs[len(offsets):]):
        o_ref[...] = jnp.dot(h, e_ref[...], preferred_element_type=F32).astype(o_ref.dtype)


def _norm_proj(x2d, norm_w, weight, segments, out_dtypes, extras=(), side_casts=()):
    m, d = x2d.shape
    tm = IN_PROJ_ROWS
    steps = m // tm
    assert all(w % LANES == 0 for _, w in segments)
    assert all(s.shape[0] % (steps * BF16_SUBLANES) == 0 for s in side_casts)
    widths = [w for _, w in segments] + [e.shape[1] for e in extras]
    shifted_cols = sum(w for off, w in segments if off % LANES)
    side_specs = [pl.BlockSpec((s.shape[0] // steps, s.shape[1]), lambda i: (i, 0)) for s in side_casts]
    return pl.pallas_call(
        functools.partial(_norm_proj_body, offsets=tuple(off for off, _ in segments), n_extra=len(extras),
                          n_side=len(side_casts)),
        name="norm_proj",
        grid=(steps,),
        in_specs=[pl.BlockSpec((tm, d), lambda i: (i, 0)), _resident((1, d)), _resident(weight.shape)]
        + [_resident(e.shape) for e in extras] + side_specs,
        out_specs=[pl.BlockSpec((tm, w), lambda i: (i, 0)) for w in widths] + side_specs,
        out_shape=[jax.ShapeDtypeStruct((m, w), dt) for w, dt in zip(widths, out_dtypes)]
        + [jax.ShapeDtypeStruct(s.shape, BF16) for s in side_casts],
        scratch_shapes=[pltpu.VMEM((d, shifted_cols), weight.dtype)] if shifted_cols else [],
        compiler_params=_params("arbitrary"),
    )(x2d, norm_w.reshape(1, d), weight, *extras, *side_casts)


def _ssd_body(xbc_ref, sm_ref, z_ref, cw_ref, cb_ref, alog_ref, dtb_ref, dskip_ref, nw_ref,
              y_ref, h_ref, xpad_ref, act_ref, xw_ref, ycat_ref):
    q = SSD_CHUNK
    t = xbc_ref.shape[1]
    width = y_ref.shape[-1]
    n_pairs = width // LANES
    n_slabs = xpad_ref.shape[0]
    gstate = SSD_GROUPS * SSD_STATE
    gwidth = width // SSD_GROUPS
    first = pl.program_id(1) == 0

    @pl.when(first)
    def _():
        xpad_ref[:, 0:SUBLANES, :] = jnp.zeros((n_slabs, SUBLANES, LANES), F32)
        h_ref[...] = jnp.zeros_like(h_ref)

    @pl.when(jnp.logical_not(first))
    def _():
        xpad_ref[:, 0:SUBLANES, :] = xpad_ref[:, t:t + SUBLANES, :]

    for c in range(n_slabs):
        lanes = slice(c * LANES, (c + 1) * LANES)
        xpad_ref[c, SUBLANES:SUBLANES + t, :] = xbc_ref[0, :, lanes]
        taps = [jnp.broadcast_to(cw_ref[k:k + 1, lanes], (SUBLANES, LANES)) for k in range(SSD_CONV)]
        bias = jnp.broadcast_to(cb_ref[:, lanes], (SUBLANES, LANES))
        for t0 in range(0, t, SUBLANES * CONV_ROW_STRIDE):
            for g in range(CONV_ROW_STRIDE):
                conv = bias
                for k in range(SSD_CONV):
                    start = SUBLANES + t0 + g - (SSD_CONV - 1 - k)
                    conv = conv + taps[k] * xpad_ref[c, pl.ds(start, SUBLANES, stride=CONV_ROW_STRIDE), :]
                act_ref[c, pl.ds(t0 + g, SUBLANES, stride=CONV_ROW_STRIDE), :] = _silu(conv)

    dt_all = _softplus(sm_ref[0] + dtb_ref[...])
    dta = dt_all * (-jnp.exp(alog_ref[...]) * LOG2_E)
    a_cs_all = _chunk_cumsum(dta, q)
    ri = lax.broadcasted_iota(jnp.int32, (q, q), 0)
    ci = lax.broadcasted_iota(jnp.int32, (q, q), 1)
    causal = ri >= ci
    lane = lax.broadcasted_iota(jnp.int32, (q, LANES), 1)
    lo = lane < HALF
    lo_row = lo[0:1, :]

    h_prev = h_ref[...]
    for rows in [slice(r0, r0 + q) for r0 in range(0, t, q)]:
        bm = act_ref[n_pairs, rows, :]
        cm = act_ref[n_pairs + 1, rows, :]
        dt = dt_all[rows, :]
        a_cs = a_cs_all[rows, :]
        a_cs_t = a_cs.T
        a_last = a_cs[q - 1:q, :]
        bm_t = bm.T
        cm_g = [jnp.where(lo, cm, 0.0), jnp.where(lo, 0.0, cm)]
        cb = [_dot_nt(c, bm) for c in cm_g]
        y_off = [_dot(c, h_prev) for c in cm_g]

        a_last_pairs = []
        for p in range(n_pairs):
            e0, e1 = 2 * p, 2 * p + 1
            g = (p * LANES) // gwidth
            col = slice(p * LANES, (p + 1) * LANES)
            gcol = slice(p * LANES - g * gwidth, (p + 1) * LANES - g * gwidth)
            acs_pair = jnp.where(lo, a_cs[:, e0:e0 + 1], a_cs[:, e1:e1 + 1])
            dt_pair = jnp.where(lo, dt[:, e0:e0 + 1], dt[:, e1:e1 + 1])
            al_pair = jnp.where(lo_row, a_last[:, e0:e0 + 1], a_last[:, e1:e1 + 1])
            a_last_pairs.append(al_pair)
            xs2 = act_ref[p, rows, :]
            xdt = xs2 * dt_pair
            xdt_b = xdt.astype(BF16)
            yd = []
            for e in (e0, e1):
                seg = a_cs[:, e:e + 1] - a_cs_t[e:e + 1, :]
                decay = jnp.exp2(jnp.where(causal, seg, -jnp.inf))
                yd.append(_dot(cb[g] * decay, xdt_b))
            y2 = jnp.where(lo, yd[0], yd[1])
            y2 = y2 + y_off[g][:, gcol] * jnp.exp2(acs_pair) + dskip_ref[:, col] * xs2
            ycat_ref[rows, col] = y2
            xw_ref[rows, col] = (xdt * jnp.exp2(al_pair - acs_pair)).astype(BF16)

        h_next = []
        for g in range(SSD_GROUPS):
            srows = slice(g * SSD_STATE, (g + 1) * SSD_STATE)
            ppg = n_pairs // SSD_GROUPS
            dec = jnp.exp2(jnp.concatenate(a_last_pairs[g * ppg:(g + 1) * ppg], axis=1))
            s_g = _dot(bm_t[srows, :], xw_ref[rows, g * gwidth:(g + 1) * gwidth])
            h_next.append(h_prev[srows, :] * dec + s_g)
        h_prev = jnp.concatenate(h_next, axis=0)
        y_ref[0, rows, :] = _rms(ycat_ref[rows, :] * _silu(z_ref[0, rows, :]), nw_ref[...]).astype(y_ref.dtype)
    h_ref[...] = h_prev


def _ssd(xbc, small, z, conv_w, conv_b, a_log, dt_bias, d_skip, norm_w):
    b, l, cdim = xbc.shape
    width = z.shape[-1]
    q = SSD_ROWS
    n_heads = a_log.shape[0]
    pad = lambda v: jnp.pad(v, (0, LANES - n_heads)).reshape(1, LANES)
    gwidth = width // SSD_GROUPS
    blk = lambda w: pl.BlockSpec((1, q, w), lambda i, c: (i, c, 0))
    return pl.pallas_call(
        _ssd_body,
        name="ssd",
        grid=(b, l // q),
        in_specs=[blk(cdim), blk(LANES), blk(width),
                  _resident((SSD_CONV, cdim)), _resident((1, cdim)), _resident((1, LANES)),
                  _resident((1, LANES)), _resident((1, width)), _resident((1, width))],
        out_specs=blk(width),
        out_shape=jax.ShapeDtypeStruct((b, l, width), BF16),
        scratch_shapes=[pltpu.VMEM((SSD_GROUPS * SSD_STATE, gwidth), F32),
                        pltpu.VMEM((cdim // LANES, q + SUBLANES, LANES), F32),
                        pltpu.VMEM((cdim // LANES, q, LANES), F32),
                        pltpu.VMEM((q, width), BF16),
                        pltpu.VMEM((q, width), F32)],
        compiler_params=_params("parallel", "arbitrary"),
    )(xbc, small, z, conv_w, conv_b.reshape(1, cdim), pad(a_log), pad(dt_bias),
      jnp.repeat(d_skip, SSD_HEAD_DIM).reshape(1, width), norm_w.reshape(1, width))


def _gla_body(q_ref, k_ref, v_ref, sm_ref, g_ref, w2_ref, gb_ref, nw_ref, o_ref, st_ref, *, mid_lane):
    c = GLA_CHUNK
    kw = q_ref.shape[-1]
    dk = kw // GLA_HEADS
    dv = v_ref.shape[-1] // GLA_HEADS
    n_pairs = kw // LANES

    @pl.when(pl.program_id(1) == 0)
    def _():
        st_ref[...] = jnp.zeros_like(st_ref)

    ri = lax.broadcasted_iota(jnp.int32, (c, c), 0)
    ci = lax.broadcasted_iota(jnp.int32, (c, c), 1)
    causal = ri >= ci
    lo = lax.broadcasted_iota(jnp.int32, (c, LANES), 1) < HALF
    lo_sq = lax.broadcasted_iota(jnp.int32, (dv, LANES), 1) < HALF

    sm_hi, sm_mid = _split_bf16(sm_ref[0], 2)
    sm_lane = lax.broadcasted_iota(jnp.int32, sm_hi.shape, 1)
    pre = jnp.dot(jnp.where(sm_lane < mid_lane, sm_hi, sm_mid), w2_ref[...],
                  preferred_element_type=F32) + gb_ref[...]
    gcs_all = _chunk_cumsum(_log_sigmoid(pre) * (LOG2_E / GLA_GATE_NORMALIZER), c)

    chunks = [slice(ch * c, (ch + 1) * c) for ch in range(q_ref.shape[1] // c)]
    pairs = [slice(p * LANES, (p + 1) * LANES) for p in range(n_pairs)]
    causal2 = jnp.concatenate([causal, causal], axis=0)

    def by_head(x):
        return jnp.concatenate([jnp.where(lo, x, 0.0), jnp.where(lo, 0.0, x)], axis=0).astype(BF16)

    q_st, k_st, g_last, scores = {}, {}, {}, {}

    def front(ch):
        rows = chunks[ch]
        gcs = gcs_all[rows, :]
        g_mid = gcs[c // 2:c // 2 + 1, :]
        g_last[ch] = gcs[c - 1:c, :]
        qs = q_ref[0, rows, :] * (dk ** -0.5)
        ks = k_ref[0, rows, :]
        q_mid = qs * jnp.exp2(gcs - g_mid)
        k_mid = ks * jnp.exp2(g_mid - gcs)
        k_in = k_mid.astype(BF16)
        q_st[ch] = q_mid * jnp.exp2(g_mid)
        k_st[ch] = (k_mid * jnp.exp2(g_last[ch] - g_mid)).astype(BF16)
        scores[ch] = [jnp.where(causal2, _dot_nt(by_head(q_mid[:, col]), k_in[:, col]), 0.0).astype(BF16)
                      for col in pairs]

    def back(ch, states):
        rows = chunks[ch]
        for p, col in enumerate(pairs):
            v_pair = [v_ref[0, rows, (2 * p + half) * dv:(2 * p + half + 1) * dv].astype(BF16) for half in range(2)]
            o_inter = _dot_nt(by_head(q_st[ch][:, col]), states[p])
            kv_t = jnp.where(lo_sq, _dot_tn(v_pair[0], k_st[ch][:, col]), _dot_tn(v_pair[1], k_st[ch][:, col]))
            states[p] = states[p] * jnp.exp2(g_last[ch][:, col]) + kv_t
            for half in range(2):
                vcol = slice((2 * p + half) * dv, (2 * p + half + 1) * dv)
                o = (jnp.dot(scores[ch][p][half * c:(half + 1) * c, :], v_pair[half], preferred_element_type=F32)
                     + o_inter[half * c:(half + 1) * c, :])
                o_ref[0, rows, vcol] = (_rms(o, nw_ref[...]) * _silu(g_ref[0, rows, vcol])).astype(o_ref.dtype)
        del q_st[ch], k_st[ch], g_last[ch], scores[ch]

    states = [st_ref[:, col] for col in pairs]
    front(0)
    for ch in range(len(chunks)):
        if ch + 1 < len(chunks):
            front(ch + 1)
        back(ch, states)
    for p, col in enumerate(pairs):
        st_ref[:, col] = states[p]


def _gla(q, k, v, small, g, gate_w2, gate_b, norm_w, lr_lane):
    b, l, kw = q.shape
    vw = v.shape[-1]
    rank = gate_w2.shape[0]
    t = GLA_ROWS
    w2_hi, w2_mid = _split_bf16(gate_w2, 2)
    w2_pad = jnp.zeros((LANES, kw), BF16)
    for copy, part in enumerate((w2_hi, w2_mid, w2_hi)):
        w2_pad = w2_pad.at[lr_lane + copy * rank:lr_lane + (copy + 1) * rank, :].set(part)
    blk = lambda w: pl.BlockSpec((1, t, w), lambda i, c: (i, c, 0))
    return pl.pallas_call(
        functools.partial(_gla_body, mid_lane=lr_lane + 2 * rank),
        name="gla",
        grid=(b, l // t),
        in_specs=[blk(kw), blk(kw), blk(vw), blk(LANES), blk(vw),
                  _resident((LANES, kw)), _resident((1, kw)), _resident((1, vw // GLA_HEADS))],
        out_specs=blk(vw),
        out_shape=jax.ShapeDtypeStruct((b, l, vw), BF16),
        scratch_shapes=[pltpu.VMEM((vw // GLA_HEADS, kw), F32)],
        compiler_params=_params("parallel", "arbitrary"),
    )(q, k, v, small, g, w2_pad, gate_b.reshape(1, kw), norm_w.reshape(1, -1))


def _out_proj_body(*refs, n_in, final_norm):
    a_refs, x_ref, w_ref = refs[:n_in], refs[n_in], refs[n_in + 1]
    rest = refs[n_in + 2:]
    acc = x_ref[...]
    off = 0
    for a_ref in a_refs:
        width = a_ref.shape[1]
        acc = acc + jnp.dot(a_ref[...].astype(BF16), w_ref[off:off + width, :], preferred_element_type=F32)
        off += width
    if final_norm:
        fw_ref, o_ref = rest
        o_ref[...] = _rms(acc, fw_ref[...])
    else:
        (o_ref,) = rest
        o_ref[...] = acc


def _out_proj(acts, x2d, weight, final_w=None):
    m, d = x2d.shape
    tm = OUT_PROJ_ROWS
    n_in = len(acts)
    assert sum(a.shape[1] for a in acts) == weight.shape[0]
    row = lambda w: pl.BlockSpec((tm, w), lambda i: (i, 0))
    in_specs = [row(a.shape[1]) for a in acts] + [row(d), _resident(weight.shape)]
    args = list(acts) + [x2d, weight]
    if final_w is not None:
        in_specs.append(_resident((1, d)))
        args.append(final_w.reshape(1, d))
    return pl.pallas_call(
        functools.partial(_out_proj_body, n_in=n_in, final_norm=final_w is not None),
        name="out_proj",
        grid=(m // tm,),
        in_specs=in_specs,
        out_specs=row(d),
        out_shape=jax.ShapeDtypeStruct((m, d), F32),
        compiler_params=_params("parallel"),
    )(*args)


class _MobaQueryBlock:
    def __init__(self, n_past, q_ref, z_ref, o_ref, kb_ref, vt_ref, kmean_ref, s_ref):
        blk = MOBA_BLOCK
        self.n_past, self.z_ref, self.o_ref, self.kb_ref, self.vt_ref = n_past, z_ref, o_ref, kb_ref, vt_ref
        self.own = slice(n_past * blk, (n_past + 1) * blk)
        self.s_ref = s_ref.at[n_past % s_ref.shape[0]]
        q2 = q_ref[0, self.own, :]
        lo = lax.broadcasted_iota(jnp.int32, (blk, LANES), 1) < HALF
        q_cat = jnp.concatenate([jnp.where(lo, q2, 0.0), jnp.where(lo, 0.0, q2)], axis=0)
        scale = MOBA_HEAD_DIM ** -0.5
        key_i = lax.broadcasted_iota(jnp.int32, (blk, 2 * blk), 0)
        qry_i = lax.broadcasted_iota(jnp.int32, (blk, 2 * blk), 1)
        causal_t = key_i <= jnp.where(qry_i < blk, qry_i, qry_i - blk)

        self.masks = []
        if n_past:
            nb = kmean_ref.shape[0] // 2
            parts = [_dot_nt(kmean_ref[...], q_part) for q_part in _split_bf16(q_cat * scale, 2)]
            gate = (parts[0][0:nb] + parts[0][nb:]) + (parts[1][0:nb] + parts[1][nb:])
            g_rows = [gate[n:n + 1, :] for n in range(n_past)]
            for n in range(n_past):
                rank = jnp.zeros((1, 2 * blk), F32)
                for m in range(n_past):
                    if m != n:
                        ahead = (g_rows[m] >= g_rows[n]) if m < n else (g_rows[m] > g_rows[n])
                        rank = rank + jnp.where(ahead, 1.0, 0.0)
                self.masks.append(rank < MOBA_TOPK)
        self.masks.append(causal_t)
        self.q_s = (q_cat * (scale * LOG2_E)).astype(BF16)
        self.m_run = None
        self.p = []

    def logits_step(self, j):
        rows = slice(j * MOBA_BLOCK, (j + 1) * MOBA_BLOCK)
        sj = jnp.where(self.masks[j], _dot_nt(self.kb_ref[rows, :], self.q_s), -jnp.inf)
        self.s_ref[rows, :] = sj
        mj = jnp.max(sj, axis=0, keepdims=True)
        self.m_run = mj if self.m_run is None else jnp.maximum(self.m_run, mj)

    def value_step(self, j):
        rows = slice(j * MOBA_BLOCK, (j + 1) * MOBA_BLOCK)
        self.p.append(jnp.exp2(self.s_ref[rows, :] - self.m_run).astype(BF16))

    def finish(self):
        blk = MOBA_BLOCK
        nk = len(self.p) * blk
        p_all = jnp.concatenate(self.p, axis=0)
        outs = []
        for half in range(2):
            vt = self.vt_ref[half * VT_ROWS:(half + 1) * VT_ROWS, 0:nk]
            acc = jnp.dot(vt, p_all[:, half * blk:(half + 1) * blk], preferred_element_type=F32)
            outs.append(acc[0:HALF, :] / acc[HALF:HALF + 1, :])
        o_t = jnp.concatenate(outs, axis=0)
        self.o_ref[0, self.own, :] = (o_t.T * _silu(self.z_ref[0, self.own, :])).astype(self.o_ref.dtype)


def _moba_body(q_ref, k_ref, v_ref, z_ref, o_ref, vt_ref, kmean_ref, s_ref):
    blk = MOBA_BLOCK
    nb = k_ref.shape[1] // blk
    kmean = []
    for n in range(nb):
        rows = slice(n * blk, (n + 1) * blk)
        kmean.append(jnp.mean(k_ref[0, rows, :].astype(F32), axis=0, keepdims=True))
        v_t = v_ref[0, rows, :].astype(F32).T.astype(BF16)
        for half in range(2):
            vt_ref[half * VT_ROWS:half * VT_ROWS + HALF, rows] = v_t[half * HALF:(half + 1) * HALF, :]
    for half in range(2):
        vt_ref[half * VT_ROWS + HALF:(half + 1) * VT_ROWS, :] = jnp.ones((BF16_SUBLANES, vt_ref.shape[1]), BF16)
    kmean_ref[...] = jnp.concatenate(_split_bf16(jnp.concatenate(kmean, axis=0), 2), axis=0)
    make = functools.partial(_MobaQueryBlock, q_ref=q_ref, z_ref=z_ref, o_ref=o_ref, kb_ref=k_ref.at[0],
                             vt_ref=vt_ref, kmean_ref=kmean_ref, s_ref=s_ref)
    blocks = {0: make(0), 1: make(1)}
    blocks[0].logits_step(0)
    for i in range(nb):
        if i + 2 < nb:
            blocks[i + 2] = make(i + 2)
        cur, nxt = blocks.pop(i), blocks.get(i + 1)
        for j in range(i + 2):
            if j <= i:
                cur.value_step(j)
            if nxt is not None:
                nxt.logits_step(j)
        cur.finish()


def _moba(q, k, v, z):
    b, l, w = q.shape
    blk = MOBA_BLOCK
    nb = l // blk
    spec = pl.BlockSpec((1, l, LANES), lambda bi, hp: (bi, 0, hp))
    return pl.pallas_call(
        _moba_body,
        name="moba",
        grid=(b, w // LANES),
        in_specs=[spec, spec, spec, spec],
        out_specs=spec,
        out_shape=jax.ShapeDtypeStruct((b, l, w), BF16),
        scratch_shapes=[pltpu.VMEM((2 * VT_ROWS, l), BF16),
                        pltpu.VMEM((2 * nb, LANES), BF16),
                        pltpu.VMEM((2, l, 2 * blk), F32)],
        compiler_params=_params("parallel", "parallel"),
    )(q, k, v, z)


def _even_layer(x, norm_w, w_in_bf, conv_w, conv_b, a_log, dt_bias, d_skip, ssd_norm_w, gate_w2, gate_b,
                gla_norm_w, side_casts=()):
    b, l, d = x.shape
    width = ssd_norm_w.shape[0]
    cdim = conv_w.shape[1]
    n_heads = a_log.shape[0]
    rank, kw = gate_w2.shape
    vw = gla_norm_w.shape[0] * GLA_HEADS
    cuts = [0]
    for s in (width, cdim, n_heads, kw, kw, vw, rank, vw):
        cuts.append(cuts[-1] + s)
    seg = lambda j: w_in_bf[:, cuts[j]:cuts[j + 1]]
    pad = jnp.zeros((d, LANES - n_heads - GATE_COPIES * rank), BF16)
    w_small = jnp.concatenate([seg(2)] + [seg(6)] * GATE_COPIES + [pad], axis=1)
    segments = [(cuts[j], cuts[j + 1] - cuts[j]) for j in (0, 1, 3, 4, 5, 7)]
    outs = _norm_proj(x.reshape(b * l, d), norm_w, w_in_bf, segments, [F32, F32, F32, F32, BF16, F32, F32],
                      extras=[w_small], side_casts=side_casts)
    z_a, xbc, q_b, k_b, v_b, g_b, small = [u.reshape(b, l, -1) for u in outs[:7]]
    y_a = _ssd(xbc, small, z_a, conv_w, conv_b, a_log, dt_bias, d_skip, ssd_norm_w)
    o_b = _gla(q_b, k_b, v_b, small, g_b, gate_w2, gate_b, gla_norm_w, lr_lane=n_heads)
    return [y_a.reshape(b * l, width), o_b.reshape(b * l, vw)], outs[7:]


def _odd_layer(x, norm_w, w_in_bf, side_casts=()):
    b, l, d = x.shape
    w = w_in_bf.shape[1] // 4
    outs = _norm_proj(x.reshape(b * l, d), norm_w, w_in_bf, [(j * w, w) for j in range(4)],
                      [F32, BF16, BF16, F32], side_casts=side_casts)
    q, k, v, z = [u.reshape(b, l, w) for u in outs[:4]]
    return [_moba(q, k, v, z).reshape(b * l, w)], outs[4:]


def kernel(x, even_norm, even_w_in, even_conv_w, even_conv_b, even_a_log, even_dt_bias, even_d_skip, even_ssd_norm,
           even_gate_w2, even_gate_b, even_gla_norm, even_w_out, odd_norm, odd_w_in, odd_w_out, final_norm):
    b, l, d = x.shape
    depth = even_norm.shape[0] + odd_norm.shape[0]
    w_in = lambda layer: (even_w_in if layer % 2 == 0 else odd_w_in)[layer // 2]
    w_out = lambda layer: (even_w_out if layer % 2 == 0 else odd_w_out)[layer // 2]
    keys = [("out", 0)] + [(kind, layer) for layer in range(1, depth) for kind in ("in", "out")]
    pending = [w_in(layer) if kind == "in" else w_out(layer) for kind, layer in keys]
    bf = {("in", 0): w_in(0).astype(BF16)}
    for layer in range(depth):
        i = layer // 2
        side = pending if layer == 0 else ()
        if layer % 2 == 0:
            acts, cast = _even_layer(x, even_norm[i], bf["in", layer], even_conv_w[i], even_conv_b[i], even_a_log[i],
                                     even_dt_bias[i], even_d_skip[i], even_ssd_norm[i], even_gate_w2[i],
                                     even_gate_b[i], even_gla_norm[i], side_casts=side)
        else:
            acts, cast = _odd_layer(x, odd_norm[i], bf["in", layer], side_casts=side)
        bf.update(zip(keys, cast))
        last = layer == depth - 1
        x = _out_proj(acts, x.reshape(b * l, d), bf["out", layer], final_norm if last else None).reshape(b, l, d)
    return x
```

```python
import functools

import jax
import jax.numpy as jnp
from jax import lax
from jax.experimental import pallas as pl
from jax.experimental.pallas import tpu as pltpu

F32 = jnp.float32
BF16 = jnp.bfloat16

LANES = 128
SUBLANES = 8
BF16_SUBLANES = 16
VMEM_LIMIT_BYTES = 56 * 1024 * 1024

RMS_EPS = 1e-6
SSD_HEAD_DIM = 64
SSD_GROUPS = 2
SSD_STATE = 64
SSD_CONV = 4
SSD_CHUNK = 128
GLA_HEADS = 8
GLA_GATE_NORMALIZER = 16.0
GLA_CHUNK = 64
MOBA_HEAD_DIM = 64
MOBA_BLOCK = 256
MOBA_TOPK = 3
LOG2_E = 1.4426950408889634
CONV_ROW_STRIDE = 4
CUMSUM_TERMS = 3
GATE_COPIES = 3

CAST_ROWS = 128
IN_PROJ_ROWS = 512
OUT_PROJ_ROWS = 1024
GLA_ROWS = 1024
SSD_ROWS = 512
HALF = LANES // 2
VT_ROWS = HALF + BF16_SUBLANES


def _params(*sem):
    return pltpu.CompilerParams(dimension_semantics=sem, vmem_limit_bytes=VMEM_LIMIT_BYTES)


def _rms(x, w):
    return x * lax.rsqrt(jnp.mean(x * x, axis=-1, keepdims=True) + RMS_EPS) * w


def _silu(x):
    h = 0.5 * x
    return h + h * jnp.tanh(h)


def _softplus(x):
    return jnp.maximum(x, 0.0) + jnp.log1p(jnp.exp(-jnp.abs(x)))


def _log_sigmoid(x):
    return jnp.minimum(x, 0.0) - jnp.log(1.0 + jnp.exp(-jnp.abs(x)))


def _dot(a, b):
    return jnp.dot(a.astype(BF16), b.astype(BF16), preferred_element_type=F32)


def _dot_nt(a, b):
    return lax.dot_general(a.astype(BF16), b.astype(BF16), (((1,), (1,)), ((), ())), preferred_element_type=F32)


def _dot_tn(a, b):
    return lax.dot_general(a.astype(BF16), b.astype(BF16), (((0,), (0,)), ((), ())), preferred_element_type=F32)


def _split_bf16(x, terms):
    parts = []
    for _ in range(terms):
        p = x.astype(BF16)
        parts.append(p)
        x = x - p.astype(F32)
    return parts


def _chunk_cumsum(x, chunk):
    rows = x.shape[0]
    assert chunk & (chunk - 1) == 0
    ri = lax.broadcasted_iota(jnp.int32, (chunk, CUMSUM_TERMS * chunk), 0)
    ci = lax.broadcasted_iota(jnp.int32, (chunk, CUMSUM_TERMS * chunk), 1)
    tri = jnp.where((ci & (chunk - 1)) <= ri, 1.0, 0.0).astype(BF16)
    out = []
    for r0 in range(0, rows, chunk):
        stacked = jnp.concatenate(_split_bf16(x[r0:r0 + chunk, :], CUMSUM_TERMS), axis=0)
        out.append(jnp.dot(tri, stacked, preferred_element_type=F32))
    return out[0] if len(out) == 1 else jnp.concatenate(out, axis=0)


def _resident(shape):
    return pl.BlockSpec(shape, lambda *_: (0,) * len(shape), pipeline_mode=pl.Buffered(1))


def _cast_body(w_ref, o_ref):
    o_ref[...] = w_ref[...].astype(o_ref.dtype)


def _to_bf16(w):
    rows, cols = w.shape
    spec = pl.BlockSpec((CAST_ROWS, cols), lambda i: (i, 0))
    return pl.pallas_call(
        _cast_body,
        name="to_bf16",
        grid=(rows // CAST_ROWS,),
        in_specs=[spec],
        out_specs=spec,
        out_shape=jax.ShapeDtypeStruct(w.shape, BF16),
        compiler_params=_params("parallel"),
    )(w)


def _norm_proj_body(x_ref, nw_ref, w_ref, *refs, offsets, n_extra, n_side):
    n_out = len(offsets) + n_extra
    extra_refs, side_in = refs[:n_extra], refs[n_extra:n_extra + n_side]
    o_refs = refs[n_extra + n_side:n_extra + n_side + n_out]
    side_out = refs[n_extra + n_side + n_out:n_extra + 2 * n_side + n_out]
    for s_in, s_out in zip(side_in, side_out):
        s_out[...] = s_in[...].astype(s_out.dtype)
    shifted = [i for i, off in enumerate(offsets) if off % LANES]
    starts, pos = {}, 0
    for i in shifted:
        starts[i] = pos
        pos += o_refs[i].shape[1]

    if shifted:
        al_ref = refs[-1]

        @pl.when(pl.program_id(0) == 0)
        def _():
            for i in shifted:
                width = o_refs[i].shape[1]
                al_ref[:, starts[i]:starts[i] + width] = w_ref[:, offsets[i]:offsets[i] + width]

    h = _rms(x_ref[...], nw_ref[...]).astype(BF16)
    for i, off in enumerate(offsets):
        width = o_refs[i].shape[1]
        w = al_ref[:, starts[i]:starts[i] + width] if i in starts else w_ref[:, off:off + width]
        o_refs[i][...] = jnp.dot(h, w, preferred_element_type=F32).astype(o_refs[i].dtype)
    for e_ref, o_ref in zip(extra_refs, o_refs[len(offsets):]):
        o_ref[...] = jnp.dot(h, e_ref[...], preferred_element_type=F32).astype(o_ref.dtype)


def _norm_proj(x2d, norm_w, weight, segments, out_dtypes, extras=(), side_casts=()):
    m, d = x2d.shape
    tm = IN_PROJ_ROWS
    steps = m // tm
    assert all(w % LANES == 0 for _, w in segments)
    assert all(s.shape[0] % (steps * BF16_SUBLANES) == 0 for s in side_casts)
    widths = [w for _, w in segments] + [e.shape[1] for e in extras]
    shifted_cols = sum(w for off, w in segments if off % LANES)
    side_specs = [pl.BlockSpec((s.shape[0] // steps, s.shape[1]), lambda i: (i, 0)) for s in side_casts]
    return pl.pallas_call(
        functools.partial(_norm_proj_body, offsets=tuple(off for off, _ in segments), n_extra=len(extras),
                          n_side=len(side_casts)),
        name="norm_proj",
        grid=(steps,),
        in_specs=[pl.BlockSpec((tm, d), lambda i: (i, 0)), _resident((1, d)), _resident(weight.shape)]
        + [_resident(e.shape) for e in extras] + side_specs,
        out_specs=[pl.BlockSpec((tm, w), lambda i: (i, 0)) for w in widths] + side_specs,
        out_shape=[jax.ShapeDtypeStruct((m, w), dt) for w, dt in zip(widths, out_dtypes)]
        + [jax.ShapeDtypeStruct(s.shape, BF16) for s in side_casts],
        scratch_shapes=[pltpu.VMEM((d, shifted_cols), weight.dtype)] if shifted_cols else [],
        compiler_params=_params("arbitrary"),
    )(x2d, norm_w.reshape(1, d), weight, *extras, *side_casts)


def _ssd_body(xbc_ref, sm_ref, z_ref, cw_ref, cb_ref, alog_ref, dtb_ref, dskip_ref, nw_ref,
              y_ref, h_ref, xpad_ref, act_ref, xw_ref, ycat_ref):
    q = SSD_CHUNK
    t = xbc_ref.shape[1]
    width = y_ref.shape[-1]
    n_pairs = width // LANES
    n_slabs = xpad_ref.shape[0]
    gstate = SSD_GROUPS * SSD_STATE
    gwidth = width // SSD_GROUPS
    first = pl.program_id(1) == 0

    @pl.when(first)
    def _():
        xpad_ref[:, 0:SUBLANES, :] = jnp.zeros((n_slabs, SUBLANES, LANES), F32)
        h_ref[...] = jnp.zeros_like(h_ref)

    @pl.when(jnp.logical_not(first))
    def _():
        xpad_ref[:, 0:SUBLANES, :] = xpad_ref[:, t:t + SUBLANES, :]

    for c in range(n_slabs):
        lanes = slice(c * LANES, (c + 1) * LANES)
        xpad_ref[c, SUBLANES:SUBLANES + t, :] = xbc_ref[0, :, lanes]
        taps = [jnp.broadcast_to(cw_ref[k:k + 1, lanes], (SUBLANES, LANES)) for k in range(SSD_CONV)]
        bias = jnp.broadcast_to(cb_ref[:, lanes], (SUBLANES, LANES))
        for t0 in range(0, t, SUBLANES * CONV_ROW_STRIDE):
            for g in range(CONV_ROW_STRIDE):
                conv = bias
                for k in range(SSD_CONV):
                    start = SUBLANES + t0 + g - (SSD_CONV - 1 - k)
                    conv = conv + taps[k] * xpad_ref[c, pl.ds(start, SUBLANES, stride=CONV_ROW_STRIDE), :]
                act_ref[c, pl.ds(t0 + g, SUBLANES, stride=CONV_ROW_STRIDE), :] = _silu(conv)

    dt_all = _softplus(sm_ref[0] + dtb_ref[...])
    dta = dt_all * (-jnp.exp(alog_ref[...]) * LOG2_E)
    a_cs_all = _chunk_cumsum(dta, q)
    ri = lax.broadcasted_iota(jnp.int32, (q, q), 0)
    ci = lax.broadcasted_iota(jnp.int32, (q, q), 1)
    causal = ri >= ci
    lane = lax.broadcasted_iota(jnp.int32, (q, LANES), 1)
    lo = lane < HALF
    lo_row = lo[0:1, :]

    h_prev = h_ref[...]
    for rows in [slice(r0, r0 + q) for r0 in range(0, t, q)]:
        bm = act_ref[n_pairs, rows, :]
        cm = act_ref[n_pairs + 1, rows, :]
        dt = dt_all[rows, :]
        a_cs = a_cs_all[rows, :]
        a_cs_t = a_cs.T
        a_last = a_cs[q - 1:q, :]
        bm_t = bm.T
        cm_g = [jnp.where(lo, cm, 0.0), jnp.where(lo, 0.0, cm)]
        cb = [_dot_nt(c, bm) for c in cm_g]
        y_off = [_dot(c, h_prev) for c in cm_g]

        a_last_pairs = []
        for p in range(n_pairs):
            e0, e1 = 2 * p, 2 * p + 1
            g = (p * LANES) // gwidth
            col = slice(p * LANES, (p + 1) * LANES)
            gcol = slice(p * LANES - g * gwidth, (p + 1) * LANES - g * gwidth)
            acs_pair = jnp.where(lo, a_cs[:, e0:e0 + 1], a_cs[:, e1:e1 + 1])
            dt_pair = jnp.where(lo, dt[:, e0:e0 + 1], dt[:, e1:e1 + 1])
            al_pair = jnp.where(lo_row, a_last[:, e0:e0 + 1], a_last[:, e1:e1 + 1])
            a_last_pairs.append(al_pair)
            xs2 = act_ref[p, rows, :]
            xdt = xs2 * dt_pair
            xdt_b = xdt.astype(BF16)
            yd = []
            for e in (e0, e1):
                seg = a_cs[:, e:e + 1] - a_cs_t[e:e + 1, :]
                decay = jnp.exp2(jnp.where(causal, seg, -jnp.inf))
                yd.append(_dot(cb[g] * decay, xdt_b))
            y2 = jnp.where(lo, yd[0], yd[1])
            y2 = y2 + y_off[g][:, gcol] * jnp.exp2(acs_pair) + dskip_ref[:, col] * xs2
            ycat_ref[rows, col] = y2
            xw_ref[rows, col] = (xdt * jnp.exp2(al_pair - acs_pair)).astype(BF16)

        h_next = []
        for g in range(SSD_GROUPS):
            srows = slice(g * SSD_STATE, (g + 1) * SSD_STATE)
            ppg = n_pairs // SSD_GROUPS
            dec = jnp.exp2(jnp.concatenate(a_last_pairs[g * ppg:(g + 1) * ppg], axis=1))
            s_g = _dot(bm_t[srows, :], xw_ref[rows, g * gwidth:(g + 1) * gwidth])
            h_next.append(h_prev[srows, :] * dec + s_g)
        h_prev = jnp.concatenate(h_next, axis=0)
        y_ref[0, rows, :] = _rms(ycat_ref[rows, :] * _silu(z_ref[0, rows, :]), nw_ref[...]).astype(y_ref.dtype)
    h_ref[...] = h_prev


def _ssd(xbc, small, z, conv_w, conv_b, a_log, dt_bias, d_skip, norm_w):
    b, l, cdim = xbc.shape
    width = z.shape[-1]
    q = SSD_ROWS
    n_heads = a_log.shape[0]
    pad = lambda v: jnp.pad(v, (0, LANES - n_heads)).reshape(1, LANES)
    gwidth = width // SSD_GROUPS
    blk = lambda w: pl.BlockSpec((1, q, w), lambda i, c: (i, c, 0))
    return pl.pallas_call(
        _ssd_body,
        name="ssd",
        grid=(b, l // q),
        in_specs=[blk(cdim), blk(LANES), blk(width),
                  _resident((SSD_CONV, cdim)), _resident((1, cdim)), _resident((1, LANES)),
                  _resident((1, LANES)), _resident((1, width)), _resident((1, width))],
        out_specs=blk(width),
        out_shape=jax.ShapeDtypeStruct((b, l, width), BF16),
        scratch_shapes=[pltpu.VMEM((SSD_GROUPS * SSD_STATE, gwidth), F32),
                        pltpu.VMEM((cdim // LANES, q + SUBLANES, LANES), F32),
                        pltpu.VMEM((cdim // LANES, q, LANES), F32),
                        pltpu.VMEM((q, width), BF16),
                        pltpu.VMEM((q, width), F32)],
        compiler_params=_params("parallel", "arbitrary"),
    )(xbc, small, z, conv_w, conv_b.reshape(1, cdim), pad(a_log), pad(dt_bias),
      jnp.repeat(d_skip, SSD_HEAD_DIM).reshape(1, width), norm_w.reshape(1, width))


def _gla_body(q_ref, k_ref, v_ref, sm_ref, g_ref, w2_ref, gb_ref, nw_ref, o_ref, st_ref, *, mid_lane):
    c = GLA_CHUNK
    kw = q_ref.shape[-1]
    dk = kw // GLA_HEADS
    dv = v_ref.shape[-1] // GLA_HEADS
    n_pairs = kw // LANES

    @pl.when(pl.program_id(1) == 0)
    def _():
        st_ref[...] = jnp.zeros_like(st_ref)

    ri = lax.broadcasted_iota(jnp.int32, (c, c), 0)
    ci = lax.broadcasted_iota(jnp.int32, (c, c), 1)
    causal = ri >= ci
    lo = lax.broadcasted_iota(jnp.int32, (c, LANES), 1) < HALF
    lo_sq = lax.broadcasted_iota(jnp.int32, (dv, LANES), 1) < HALF

    sm_hi, sm_mid = _split_bf16(sm_ref[0], 2)
    sm_lane = lax.broadcasted_iota(jnp.int32, sm_hi.shape, 1)
    pre = jnp.dot(jnp.where(sm_lane < mid_lane, sm_hi, sm_mid), w2_ref[...],
                  preferred_element_type=F32) + gb_ref[...]
    gcs_all = _chunk_cumsum(_log_sigmoid(pre) * (LOG2_E / GLA_GATE_NORMALIZER), c)

    chunks = [slice(ch * c, (ch + 1) * c) for ch in range(q_ref.shape[1] // c)]
    pairs = [slice(p * LANES, (p + 1) * LANES) for p in range(n_pairs)]
    causal2 = jnp.concatenate([causal, causal], axis=0)

    def by_head(x):
        return jnp.concatenate([jnp.where(lo, x, 0.0), jnp.where(lo, 0.0, x)], axis=0).astype(BF16)

    q_st, k_st, g_last, scores = {}, {}, {}, {}

    def front(ch):
        rows = chunks[ch]
        gcs = gcs_all[rows, :]
        g_mid = gcs[c // 2:c // 2 + 1, :]
        g_last[ch] = gcs[c - 1:c, :]
        qs = q_ref[0, rows, :] * (dk ** -0.5)
        ks = k_ref[0, rows, :]
        q_mid = qs * jnp.exp2(gcs - g_mid)
        k_mid = ks * jnp.exp2(g_mid - gcs)
        k_in = k_mid.astype(BF16)
        q_st[ch] = q_mid * jnp.exp2(g_mid)
        k_st[ch] = (k_mid * jnp.exp2(g_last[ch] - g_mid)).astype(BF16)
        scores[ch] = [jnp.where(causal2, _dot_nt(by_head(q_mid[:, col]), k_in[:, col]), 0.0).astype(BF16)
                      for col in pairs]

    def back(ch, states):
        rows = chunks[ch]
        for p, col in enumerate(pairs):
            v_pair = [v_ref[0, rows, (2 * p + half) * dv:(2 * p + half + 1) * dv].astype(BF16) for half in range(2)]
            o_inter = _dot_nt(by_head(q_st[ch][:, col]), states[p])
            kv_t = jnp.where(lo_sq, _dot_tn(v_pair[0], k_st[ch][:, col]), _dot_tn(v_pair[1], k_st[ch][:, col]))
            states[p] = states[p] * jnp.exp2(g_last[ch][:, col]) + kv_t
            for half in range(2):
                vcol = slice((2 * p + half) * dv, (2 * p + half + 1) * dv)
                o = (jnp.dot(scores[ch][p][half * c:(half + 1) * c, :], v_pair[half], preferred_element_type=F32)
                     + o_inter[half * c:(half + 1) * c, :])
                o_ref[0, rows, vcol] = (_rms(o, nw_ref[...]) * _silu(g_ref[0, rows, vcol])).astype(o_ref.dtype)
        del q_st[ch], k_st[ch], g_last[ch], scores[ch]

    states = [st_ref[:, col] for col in pairs]
    front(0)
    for ch in range(len(chunks)):
        if ch + 1 < len(chunks):
            front(ch + 1)
        back(ch, states)
    for p, col in enumerate(pairs):
        st_ref[:, col] = states[p]


def _gla(q, k, v, small, g, gate_w2, gate_b, norm_w, lr_lane):
    b, l, kw = q.shape
    vw = v.shape[-1]
    rank = gate_w2.shape[0]
    t = GLA_ROWS
    w2_hi, w2_mid = _split_bf16(gate_w2, 2)
    w2_pad = jnp.zeros((LANES, kw), BF16)
    for copy, part in enumerate((w2_hi, w2_mid, w2_hi)):
        w2_pad = w2_pad.at[lr_lane + copy * rank:lr_lane + (copy + 1) * rank, :].set(part)
    blk = lambda w: pl.BlockSpec((1, t, w), lambda i, c: (i, c, 0))
    return pl.pallas_call(
        functools.partial(_gla_body, mid_lane=lr_lane + 2 * rank),
        name="gla",
        grid=(b, l // t),
        in_specs=[blk(kw), blk(kw), blk(vw), blk(LANES), blk(vw),
                  _resident((LANES, kw)), _resident((1, kw)), _resident((1, vw // GLA_HEADS))],
        out_specs=blk(vw),
        out_shape=jax.ShapeDtypeStruct((b, l, vw), BF16),
        scratch_shapes=[pltpu.VMEM((vw // GLA_HEADS, kw), F32)],
        compiler_params=_params("parallel", "arbitrary"),
    )(q, k, v, small, g, w2_pad, gate_b.reshape(1, kw), norm_w.reshape(1, -1))


def _out_proj_body(*refs, n_in, final_norm):
    a_refs, x_ref, w_ref = refs[:n_in], refs[n_in], refs[n_in + 1]
    rest = refs[n_in + 2:]
    acc = x_ref[...]
    off = 0
    for a_ref in a_refs:
        width = a_ref.shape[1]
        acc = acc + jnp.dot(a_ref[...].astype(BF16), w_ref[off:off + width, :], preferred_element_type=F32)
        off += width
    if final_norm:
        fw_ref, o_ref = rest
        o_ref[...] = _rms(acc, fw_ref[...])
    else:
        (o_ref,) = rest
        o_ref[...] = acc


def _out_proj(acts, x2d, weight, final_w=None):
    m, d = x2d.shape
    tm = OUT_PROJ_ROWS
    n_in = len(acts)
    assert sum(a.shape[1] for a in acts) == weight.shape[0]
    row = lambda w: pl.BlockSpec((tm, w), lambda i: (i, 0))
    in_specs = [row(a.shape[1]) for a in acts] + [row(d), _resident(weight.shape)]
    args = list(acts) + [x2d, weight]
    if final_w is not None:
        in_specs.append(_resident((1, d)))
        args.append(final_w.reshape(1, d))
    return pl.pallas_call(
        functools.partial(_out_proj_body, n_in=n_in, final_norm=final_w is not None),
        name="out_proj",
        grid=(m // tm,),
        in_specs=in_specs,
        out_specs=row(d),
        out_shape=jax.ShapeDtypeStruct((m, d), F32),
        compiler_params=_params("parallel"),
    )(*args)


class _MobaQueryBlock:
    def __init__(self, n_past, q_ref, z_ref, o_ref, kb_ref, vt_ref, kmean_ref, s_ref):
        blk = MOBA_BLOCK
        self.n_past, self.z_ref, self.o_ref, self.kb_ref, self.vt_ref = n_past, z_ref, o_ref, kb_ref, vt_ref
        self.own = slice(n_past * blk, (n_past + 1) * blk)
        self.s_ref = s_ref.at[n_past % s_ref.shape[0]]
        q2 = q_ref[0, self.own, :]
        lo = lax.broadcasted_iota(jnp.int32, (blk, LANES), 1) < HALF
        q_cat = jnp.concatenate([jnp.where(lo, q2, 0.0), jnp.where(lo, 0.0, q2)], axis=0)
        scale = MOBA_HEAD_DIM ** -0.5
        key_i = lax.broadcasted_iota(jnp.int32, (blk, 2 * blk), 0)
        qry_i = lax.broadcasted_iota(jnp.int32, (blk, 2 * blk), 1)
        causal_t = key_i <= jnp.where(qry_i < blk, qry_i, qry_i - blk)

        self.masks = []
        if n_past:
            nb = kmean_ref.shape[0] // 2
            parts = [_dot_nt(kmean_ref[...], q_part) for q_part in _split_bf16(q_cat * scale, 2)]
            gate = (parts[0][0:nb] + parts[0][nb:]) + (parts[1][0:nb] + parts[1][nb:])
            g_rows = [gate[n:n + 1, :] for n in range(n_past)]
            for n in range(n_past):
                rank = jnp.zeros((1, 2 * blk), F32)
                for m in range(n_past):
                    if m != n:
                        ahead = (g_rows[m] >= g_rows[n]) if m < n else (g_rows[m] > g_rows[n])
                        rank = rank + jnp.where(ahead, 1.0, 0.0)
                self.masks.append(rank < MOBA_TOPK)
        self.masks.append(causal_t)
        self.q_s = (q_cat * (scale * LOG2_E)).astype(BF16)
        self.m_run = None
        self.p = []

    def logits_step(self, j):
        rows = slice(j * MOBA_BLOCK, (j + 1) * MOBA_BLOCK)
        sj = jnp.where(self.masks[j], _dot_nt(self.kb_ref[rows, :], self.q_s), -jnp.inf)
        self.s_ref[rows, :] = sj
        mj = jnp.max(sj, axis=0, keepdims=True)
        self.m_run = mj if self.m_run is None else jnp.maximum(self.m_run, mj)

    def value_step(self, j):
        rows = slice(j * MOBA_BLOCK, (j + 1) * MOBA_BLOCK)
        self.p.append(jnp.exp2(self.s_ref[rows, :] - self.m_run).astype(BF16))

    def finish(self):
        blk = MOBA_BLOCK
        nk = len(self.p) * blk
        p_all = jnp.concatenate(self.p, axis=0)
        outs = []
        for half in range(2):
            vt = self.vt_ref[half * VT_ROWS:(half + 1) * VT_ROWS, 0:nk]
            acc = jnp.dot(vt, p_all[:, half * blk:(half + 1) * blk], preferred_element_type=F32)
            outs.append(acc[0:HALF, :] / acc[HALF:HALF + 1, :])
        o_t = jnp.concatenate(outs, axis=0)
        self.o_ref[0, self.own, :] = (o_t.T * _silu(self.z_ref[0, self.own, :])).astype(self.o_ref.dtype)


def _moba_body(q_ref, k_ref, v_ref, z_ref, o_ref, vt_ref, kmean_ref, s_ref):
    blk = MOBA_BLOCK
    nb = k_ref.shape[1] // blk
    kmean = []
    for n in range(nb):
        rows = slice(n * blk, (n + 1) * blk)
        kmean.append(jnp.mean(k_ref[0, rows, :].astype(F32), axis=0, keepdims=True))
        v_t = v_ref[0, rows, :].astype(F32).T.astype(BF16)
        for half in range(2):
            vt_ref[half * VT_ROWS:half * VT_ROWS + HALF, rows] = v_t[half * HALF:(half + 1) * HALF, :]
    for half in range(2):
        vt_ref[half * VT_ROWS + HALF:(half + 1) * VT_ROWS, :] = jnp.ones((BF16_SUBLANES, vt_ref.shape[1]), BF16)
    kmean_ref[...] = jnp.concatenate(_split_bf16(jnp.concatenate(kmean, axis=0), 2), axis=0)
    make = functools.partial(_MobaQueryBlock, q_ref=q_ref, z_ref=z_ref, o_ref=o_ref, kb_ref=k_ref.at[0],
                             vt_ref=vt_ref, kmean_ref=kmean_ref, s_ref=s_ref)
    blocks = {0: make(0), 1: make(1)}
    blocks[0].logits_step(0)
    for i in range(nb):
        if i + 2 < nb:
            blocks[i + 2] = make(i + 2)
        cur, nxt = blocks.pop(i), blocks.get(i + 1)
        for j in range(i + 2):
            if j <= i:
                cur.value_step(j)
            if nxt is not None:
                nxt.logits_step(j)
        cur.finish()


def _moba(q, k, v, z):
    b, l, w = q.shape
    blk = MOBA_BLOCK
    nb = l // blk
    spec = pl.BlockSpec((1, l, LANES), lambda bi, hp: (bi, 0, hp))
    return pl.pallas_call(
        _moba_body,
        name="moba",
        grid=(b, w // LANES),
        in_specs=[spec, spec, spec, spec],
        out_specs=spec,
        out_shape=jax.ShapeDtypeStruct((b, l, w), BF16),
        scratch_shapes=[pltpu.VMEM((2 * VT_ROWS, l), BF16),
                        pltpu.VMEM((2 * nb, LANES), BF16),
                        pltpu.VMEM((2, l, 2 * blk), F32)],
        compiler_params=_params("parallel", "parallel"),
    )(q, k, v, z)


def _even_layer(x, norm_w, w_in_bf, conv_w, conv_b, a_log, dt_bias, d_skip, ssd_norm_w, gate_w2, gate_b,
                gla_norm_w, side_casts=()):
    b, l, d = x.shape
    width = ssd_norm_w.shape[0]
    cdim = conv_w.shape[1]
    n_heads = a_log.shape[0]
    rank, kw = gate_w2.shape
    vw = gla_norm_w.shape[0] * GLA_HEADS
    cuts = [0]
    for s in (width, cdim, n_heads, kw, kw, vw, rank, vw):
        cuts.append(cuts[-1] + s)
    seg = lambda j: w_in_bf[:, cuts[j]:cuts[j + 1]]
    pad = jnp.zeros((d, LANES - n_heads - GATE_COPIES * rank), BF16)
    w_small = jnp.concatenate([seg(2)] + [seg(6)] * GATE_COPIES + [pad], axis=1)
    segments = [(cuts[j], cuts[j + 1] - cuts[j]) for j in (0, 1, 3, 4, 5, 7)]
    outs = _norm_proj(x.reshape(b * l, d), norm_w, w_in_bf, segments, [F32, F32, F32, F32, BF16, F32, F32],
                      extras=[w_small], side_casts=side_casts)
    z_a, xbc, q_b, k_b, v_b, g_b, small = [u.reshape(b, l, -1) for u in outs[:7]]
    y_a = _ssd(xbc, small, z_a, conv_w, conv_b, a_log, dt_bias, d_skip, ssd_norm_w)
    o_b = _gla(q_b, k_b, v_b, small, g_b, gate_w2, gate_b, gla_norm_w, lr_lane=n_heads)
    return [y_a.reshape(b * l, width), o_b.reshape(b * l, vw)], outs[7:]


def _odd_layer(x, norm_w, w_in_bf, side_casts=()):
    b, l, d = x.shape
    w = w_in_bf.shape[1] // 4
    outs = _norm_proj(x.reshape(b * l, d), norm_w, w_in_bf, [(j * w, w) for j in range(4)],
                      [F32, BF16, BF16, F32], side_casts=side_casts)
    q, k, v, z = [u.reshape(b, l, w) for u in outs[:4]]
    return [_moba(q, k, v, z).reshape(b * l, w)], outs[4:]


def kernel(x, even_norm, even_w_in, even_conv_w, even_conv_b, even_a_log, even_dt_bias, even_d_skip, even_ssd_norm,
           even_gate_w2, even_gate_b, even_gla_norm, even_w_out, odd_norm, odd_w_in, odd_w_out, final_norm):
    b, l, d = x.shape
    depth = even_norm.shape[0] + odd_norm.shape[0]
    w_in = lambda layer: (even_w_in if layer % 2 == 0 else odd_w_in)[layer // 2]
    w_out = lambda layer: (even_w_out if layer % 2 == 0 else odd_w_out)[layer // 2]
    keys = [("out", 0)] + [(kind, layer) for layer in range(1, depth) for kind in ("in", "out")]
    pending = [w_in(layer) if kind == "in" else w_out(layer) for kind, layer in keys]
    bf = {("in", 0): _to_bf16(w_in(0))}
    for layer in range(depth):
        i = layer // 2
        side = pending if layer == 0 else ()
        if layer % 2 == 0:
            acts, cast = _even_layer(x, even_norm[i], bf["in", layer], even_conv_w[i], even_conv_b[i], even_a_log[i],
                                     even_dt_bias[i], even_d_skip[i], even_ssd_norm[i], even_gate_w2[i],
                                     even_gate_b[i], even_gla_norm[i], side_casts=side)
        else:
            acts, cast = _odd_layer(x, odd_norm[i], bf["in", layer], side_casts=side)
        bf.update(zip(keys, cast))
        last = layer == depth - 1
        x = _out_proj(acts, x.reshape(b * l, d), bf["out", layer], final_norm if last else None).reshape(b, l, d)
    return x
```

```python
import functools

import jax
import jax.numpy as jnp
from jax import lax
from jax.experimental import pallas as pl
from jax.experimental.pallas import tpu as pltpu

F32 = jnp.float32
BF16 = jnp.bfloat16

LANES = 128
SUBLANES = 8
BF16_SUBLANES = 16
VMEM_LIMIT_BYTES = 56 * 1024 * 1024

RMS_EPS = 1e-6
SSD_HEAD_DIM = 64
SSD_GROUPS = 2
SSD_STATE = 64
SSD_CONV = 4
SSD_CHUNK = 128
GLA_HEADS = 8
GLA_GATE_NORMALIZER = 16.0
GLA_CHUNK = 64
MOBA_HEAD_DIM = 64
MOBA_BLOCK = 256
MOBA_TOPK = 3
LOG2_E = 1.4426950408889634
CONV_ROW_STRIDE = 4
CUMSUM_TERMS = 3
GATE_COPIES = 3

IN_PROJ_ROWS = 512
OUT_PROJ_ROWS = 1024
GLA_ROWS = 1024
SSD_ROWS = 512
HALF = LANES // 2
VT_ROWS = HALF + BF16_SUBLANES


def _params(*sem):
    return pltpu.CompilerParams(dimension_semantics=sem, vmem_limit_bytes=VMEM_LIMIT_BYTES)


def _rms(x, w):
    return x * lax.rsqrt(jnp.mean(x * x, axis=-1, keepdims=True) + RMS_EPS) * w


def _silu(x):
    h = 0.5 * x
    return h + h * jnp.tanh(h)


def _softplus(x):
    return jnp.maximum(x, 0.0) + jnp.log1p(jnp.exp(-jnp.abs(x)))


def _log_sigmoid(x):
    return jnp.minimum(x, 0.0) - jnp.log(1.0 + jnp.exp(-jnp.abs(x)))


def _dot(a, b):
    return jnp.dot(a.astype(BF16), b.astype(BF16), preferred_element_type=F32)


def _dot_nt(a, b):
    return lax.dot_general(a.astype(BF16), b.astype(BF16), (((1,), (1,)), ((), ())), preferred_element_type=F32)


def _dot_tn(a, b):
    return lax.dot_general(a.astype(BF16), b.astype(BF16), (((0,), (0,)), ((), ())), preferred_element_type=F32)


def _split_bf16(x, terms):
    parts = []
    for _ in range(terms):
        p = x.astype(BF16)
        parts.append(p)
        x = x - p.astype(F32)
    return parts


def _chunk_cumsum(x, chunk):
    rows = x.shape[0]
    assert chunk & (chunk - 1) == 0
    ri = lax.broadcasted_iota(jnp.int32, (chunk, CUMSUM_TERMS * chunk), 0)
    ci = lax.broadcasted_iota(jnp.int32, (chunk, CUMSUM_TERMS * chunk), 1)
    tri = jnp.where((ci & (chunk - 1)) <= ri, 1.0, 0.0).astype(BF16)
    out = []
    for r0 in range(0, rows, chunk):
        stacked = jnp.concatenate(_split_bf16(x[r0:r0 + chunk, :], CUMSUM_TERMS), axis=0)
        out.append(jnp.dot(tri, stacked, preferred_element_type=F32))
    return out[0] if len(out) == 1 else jnp.concatenate(out, axis=0)


def _resident(shape):
    return pl.BlockSpec(shape, lambda *_: (0,) * len(shape), pipeline_mode=pl.Buffered(1))


def _norm_proj_body(x_ref, nw_ref, w_ref, *refs, offsets, n_extra, n_side):
    n_out = len(offsets) + n_extra
    extra_refs, side_in = refs[:n_extra], refs[n_extra:n_extra + n_side]
    o_refs = refs[n_extra + n_side:n_extra + n_side + n_out]
    side_out = refs[n_extra + n_side + n_out:n_extra + 2 * n_side + n_out]
    for s_in, s_out in zip(side_in, side_out):
        s_out[...] = s_in[...].astype(s_out.dtype)
    shifted = [i for i, off in enumerate(offsets) if off % LANES]
    starts, pos = {}, 0
    for i in shifted:
        starts[i] = pos
        pos += o_refs[i].shape[1]

    if shifted:
        al_ref = refs[-1]

        @pl.when(pl.program_id(0) == 0)
        def _():
            for i in shifted:
                width = o_refs[i].shape[1]
                al_ref[:, starts[i]:starts[i] + width] = w_ref[:, offsets[i]:offsets[i] + width]

    h = _rms(x_ref[...], nw_ref[...]).astype(BF16)
    for i, off in enumerate(offsets):
        width = o_refs[i].shape[1]
        w = al_ref[:, starts[i]:starts[i] + width] if i in starts else w_ref[:, off:off + width]
        o_refs[i][...] = jnp.dot(h, w, preferred_element_type=F32).astype(o_refs[i].dtype)
    for e_ref, o_ref in zip(extra_refs, o_refs[len(offsets):]):
        o_ref[...] = jnp.dot(h, e_ref[...], preferred_element_type=F32).astype(o_ref.dtype)


def _norm_proj(x2d, norm_w, weight, segments, out_dtypes, extras=(), side_casts=()):
    m, d = x2d.shape
    tm = IN_PROJ_ROWS
    steps = m // tm
    assert all(w % LANES == 0 for _, w in segments)
    assert all(s.shape[0] % (steps * BF16_SUBLANES) == 0 for s in side_casts)
    widths = [w for _, w in segments] + [e.shape[1] for e in extras]
    shifted_cols = sum(w for off, w in segments if off % LANES)
    side_specs = [pl.BlockSpec((s.shape[0] // steps, s.shape[1]), lambda i: (i, 0)) for s in side_casts]
    return pl.pallas_call(
        functools.partial(_norm_proj_body, offsets=tuple(off for off, _ in segments), n_extra=len(extras),
                          n_side=len(side_casts)),
        name="norm_proj",
        grid=(steps,),
        in_specs=[pl.BlockSpec((tm, d), lambda i: (i, 0)), _resident((1, d)), _resident(weight.shape)]
        + [_resident(e.shape) for e in extras] + side_specs,
        out_specs=[pl.BlockSpec((tm, w), lambda i: (i, 0)) for w in widths] + side_specs,
        out_shape=[jax.ShapeDtypeStruct((m, w), dt) for w, dt in zip(widths, out_dtypes)]
        + [jax.ShapeDtypeStruct(s.shape, BF16) for s in side_casts],
        scratch_shapes=[pltpu.VMEM((d, shifted_cols), weight.dtype)] if shifted_cols else [],
        compiler_params=_params("arbitrary"),
    )(x2d, norm_w.reshape(1, d), weight, *extras, *side_casts)


def _ssd_body(xbc_ref, sm_ref, z_ref, cw_ref, cb_ref, alog_ref, dtb_ref, dskip_ref, nw_ref,
              y_ref, h_ref, xpad_ref, act_ref, xw_ref, ycat_ref):
    q = SSD_CHUNK
    t = xbc_ref.shape[1]
    width = y_ref.shape[-1]
    n_pairs = width // LANES
    n_slabs = xpad_ref.shape[0]
    gstate = SSD_GROUPS * SSD_STATE
    gwidth = width // SSD_GROUPS
    first = pl.program_id(1) == 0

    @pl.when(first)
    def _():
        xpad_ref[:, 0:SUBLANES, :] = jnp.zeros((n_slabs, SUBLANES, LANES), F32)
        h_ref[...] = jnp.zeros_like(h_ref)

    @pl.when(jnp.logical_not(first))
    def _():
        xpad_ref[:, 0:SUBLANES, :] = xpad_ref[:, t:t + SUBLANES, :]

    for c in range(n_slabs):
        lanes = slice(c * LANES, (c + 1) * LANES)
        xpad_ref[c, SUBLANES:SUBLANES + t, :] = xbc_ref[0, :, lanes]
        taps = [jnp.broadcast_to(cw_ref[k:k + 1, lanes], (SUBLANES, LANES)) for k in range(SSD_CONV)]
        bias = jnp.broadcast_to(cb_ref[:, lanes], (SUBLANES, LANES))
        for t0 in range(0, t, SUBLANES * CONV_ROW_STRIDE):
            for g in range(CONV_ROW_STRIDE):
                conv = bias
                for k in range(SSD_CONV):
                    start = SUBLANES + t0 + g - (SSD_CONV - 1 - k)
                    conv = conv + taps[k] * xpad_ref[c, pl.ds(start, SUBLANES, stride=CONV_ROW_STRIDE), :]
                act_ref[c, pl.ds(t0 + g, SUBLANES, stride=CONV_ROW_STRIDE), :] = _silu(conv)

    dt_all = _softplus(sm_ref[0] + dtb_ref[...])
    dta = dt_all * (-jnp.exp(alog_ref[...]) * LOG2_E)
    a_cs_all = _chunk_cumsum(dta, q)
    ri = lax.broadcasted_iota(jnp.int32, (q, q), 0)
    ci = lax.broadcasted_iota(jnp.int32, (q, q), 1)
    causal = ri >= ci
    lane = lax.broadcasted_iota(jnp.int32, (q, LANES), 1)
    lo = lane < HALF
    lo_row = lo[0:1, :]

    h_prev = h_ref[...]
    for rows in [slice(r0, r0 + q) for r0 in range(0, t, q)]:
        bm = act_ref[n_pairs, rows, :]
        cm = act_ref[n_pairs + 1, rows, :]
        dt = dt_all[rows, :]
        a_cs = a_cs_all[rows, :]
        a_cs_t = a_cs.T
        a_last = a_cs[q - 1:q, :]
        bm_t = bm.T
        cm_g = [jnp.where(lo, cm, 0.0), jnp.where(lo, 0.0, cm)]
        cb = [_dot_nt(c, bm) for c in cm_g]
        y_off = [_dot(c, h_prev) for c in cm_g]

        a_last_pairs = []
        for p in range(n_pairs):
            e0, e1 = 2 * p, 2 * p + 1
            g = (p * LANES) // gwidth
            col = slice(p * LANES, (p + 1) * LANES)
            gcol = slice(p * LANES - g * gwidth, (p + 1) * LANES - g * gwidth)
            acs_pair = jnp.where(lo, a_cs[:, e0:e0 + 1], a_cs[:, e1:e1 + 1])
            dt_pair = jnp.where(lo, dt[:, e0:e0 + 1], dt[:, e1:e1 + 1])
            al_pair = jnp.where(lo_row, a_last[:, e0:e0 + 1], a_last[:, e1:e1 + 1])
            a_last_pairs.append(al_pair)
            xs2 = act_ref[p, rows, :]
            xdt = xs2 * dt_pair
            xdt_b = xdt.astype(BF16)
            yd = []
            for e in (e0, e1):
                seg = a_cs[:, e:e + 1] - a_cs_t[e:e + 1, :]
                decay = jnp.exp2(jnp.where(causal, seg, -jnp.inf))
                yd.append(_dot(cb[g] * decay, xdt_b))
            y2 = jnp.where(lo, yd[0], yd[1])
            y2 = y2 + y_off[g][:, gcol] * jnp.exp2(acs_pair) + dskip_ref[:, col] * xs2
            ycat_ref[rows, col] = y2
            xw_ref[rows, col] = (xdt * jnp.exp2(al_pair - acs_pair)).astype(BF16)

        h_next = []
        for g in range(SSD_GROUPS):
            srows = slice(g * SSD_STATE, (g + 1) * SSD_STATE)
            ppg = n_pairs // SSD_GROUPS
            dec = jnp.exp2(jnp.concatenate(a_last_pairs[g * ppg:(g + 1) * ppg], axis=1))
            s_g = _dot(bm_t[srows, :], xw_ref[rows, g * gwidth:(g + 1) * gwidth])
            h_next.append(h_prev[srows, :] * dec + s_g)
        h_prev = jnp.concatenate(h_next, axis=0)
        y_ref[0, rows, :] = _rms(ycat_ref[rows, :] * _silu(z_ref[0, rows, :]), nw_ref[...]).astype(y_ref.dtype)
    h_ref[...] = h_prev


def _ssd(xbc, small, z, conv_w, conv_b, a_log, dt_bias, d_skip, norm_w):
    b, l, cdim = xbc.shape
    width = z.shape[-1]
    q = SSD_ROWS
    n_heads = a_log.shape[0]
    pad = lambda v: jnp.pad(v, (0, LANES - n_heads)).reshape(1, LANES)
    gwidth = width // SSD_GROUPS
    blk = lambda w: pl.BlockSpec((1, q, w), lambda i, c: (i, c, 0))
    return pl.pallas_call(
        _ssd_body,
        name="ssd",
        grid=(b, l // q),
        in_specs=[blk(cdim), blk(LANES), blk(width),
                  _resident((SSD_CONV, cdim)), _resident((1, cdim)), _resident((1, LANES)),
                  _resident((1, LANES)), _resident((1, width)), _resident((1, width))],
        out_specs=blk(width),
        out_shape=jax.ShapeDtypeStruct((b, l, width), BF16),
        scratch_shapes=[pltpu.VMEM((SSD_GROUPS * SSD_STATE, gwidth), F32),
                        pltpu.VMEM((cdim // LANES, q + SUBLANES, LANES), F32),
                        pltpu.VMEM((cdim // LANES, q, LANES), F32),
                        pltpu.VMEM((q, width), BF16),
                        pltpu.VMEM((q, width), F32)],
        compiler_params=_params("parallel", "arbitrary"),
    )(xbc, small, z, conv_w, conv_b.reshape(1, cdim), pad(a_log), pad(dt_bias),
      jnp.repeat(d_skip, SSD_HEAD_DIM).reshape(1, width), norm_w.reshape(1, width))


def _gla_body(q_ref, k_ref, v_ref, sm_ref, g_ref, w2_ref, gb_ref, nw_ref, o_ref, st_ref, *, mid_lane):
    c = GLA_CHUNK
    kw = q_ref.shape[-1]
    dk = kw // GLA_HEADS
    dv = v_ref.shape[-1] // GLA_HEADS
    n_pairs = kw // LANES

    @pl.when(pl.program_id(1) == 0)
    def _():
        st_ref[...] = jnp.zeros_like(st_ref)

    ri = lax.broadcasted_iota(jnp.int32, (c, c), 0)
    ci = lax.broadcasted_iota(jnp.int32, (c, c), 1)
    causal = ri >= ci
    lo = lax.broadcasted_iota(jnp.int32, (c, LANES), 1) < HALF
    lo_sq = lax.broadcasted_iota(jnp.int32, (dv, LANES), 1) < HALF

    sm_hi, sm_mid = _split_bf16(sm_ref[0], 2)
    sm_lane = lax.broadcasted_iota(jnp.int32, sm_hi.shape, 1)
    pre = jnp.dot(jnp.where(sm_lane < mid_lane, sm_hi, sm_mid), w2_ref[...],
                  preferred_element_type=F32) + gb_ref[...]
    gcs_all = _chunk_cumsum(_log_sigmoid(pre) * (LOG2_E / GLA_GATE_NORMALIZER), c)

    chunks = [slice(ch * c, (ch + 1) * c) for ch in range(q_ref.shape[1] // c)]
    pairs = [slice(p * LANES, (p + 1) * LANES) for p in range(n_pairs)]
    causal2 = jnp.concatenate([causal, causal], axis=0)

    def by_head(x):
        return jnp.concatenate([jnp.where(lo, x, 0.0), jnp.where(lo, 0.0, x)], axis=0).astype(BF16)

    q_st, k_st, g_last, scores = {}, {}, {}, {}

    def front(ch):
        rows = chunks[ch]
        gcs = gcs_all[rows, :]
        g_mid = gcs[c // 2:c // 2 + 1, :]
        g_last[ch] = gcs[c - 1:c, :]
        qs = q_ref[0, rows, :] * (dk ** -0.5)
        ks = k_ref[0, rows, :]
        q_mid = qs * jnp.exp2(gcs - g_mid)
        k_mid = ks * jnp.exp2(g_mid - gcs)
        k_in = k_mid.astype(BF16)
        q_st[ch] = q_mid * jnp.exp2(g_mid)
        k_st[ch] = (k_mid * jnp.exp2(g_last[ch] - g_mid)).astype(BF16)
        scores[ch] = [jnp.where(causal2, _dot_nt(by_head(q_mid[:, col]), k_in[:, col]), 0.0).astype(BF16)
                      for col in pairs]

    def back(ch, states):
        rows = chunks[ch]
        for p, col in enumerate(pairs):
            v_pair = [v_ref[0, rows, (2 * p + half) * dv:(2 * p + half + 1) * dv].astype(BF16) for half in range(2)]
            o_inter = _dot_nt(by_head(q_st[ch][:, col]), states[p])
            kv_t = jnp.where(lo_sq, _dot_tn(v_pair[0], k_st[ch][:, col]), _dot_tn(v_pair[1], k_st[ch][:, col]))
            states[p] = states[p] * jnp.exp2(g_last[ch][:, col]) + kv_t
            for half in range(2):
                vcol = slice((2 * p + half) * dv, (2 * p + half + 1) * dv)
                o = (jnp.dot(scores[ch][p][half * c:(half + 1) * c, :], v_pair[half], preferred_element_type=F32)
                     + o_inter[half * c:(half + 1) * c, :])
                o_ref[0, rows, vcol] = (_rms(o, nw_ref[...]) * _silu(g_ref[0, rows, vcol])).astype(o_ref.dtype)
        del q_st[ch], k_st[ch], g_last[ch], scores[ch]

    states = [st_ref[:, col] for col in pairs]
    front(0)
    for ch in range(len(chunks)):
        if ch + 1 < len(chunks):
            front(ch + 1)
        back(ch, states)
    for p, col in enumerate(pairs):
        st_ref[:, col] = states[p]


def _gla(q, k, v, small, g, gate_w2, gate_b, norm_w, lr_lane):
    b, l, kw = q.shape
    vw = v.shape[-1]
    rank = gate_w2.shape[0]
    t = GLA_ROWS
    w2_hi, w2_mid = _split_bf16(gate_w2, 2)
    w2_pad = jnp.zeros((LANES, kw), BF16)
    for copy, part in enumerate((w2_hi, w2_mid, w2_hi)):
        w2_pad = w2_pad.at[lr_lane + copy * rank:lr_lane + (copy + 1) * rank, :].set(part)
    blk = lambda w: pl.BlockSpec((1, t, w), lambda i, c: (i, c, 0))
    return pl.pallas_call(
        functools.partial(_gla_body, mid_lane=lr_lane + 2 * rank),
        name="gla",
        grid=(b, l // t),
        in_specs=[blk(kw), blk(kw), blk(vw), blk(LANES), blk(vw),
                  _resident((LANES, kw)), _resident((1, kw)), _resident((1, vw // GLA_HEADS))],
        out_specs=blk(vw),
        out_shape=jax.ShapeDtypeStruct((b, l, vw), BF16),
        scratch_shapes=[pltpu.VMEM((vw // GLA_HEADS, kw), F32)],
        compiler_params=_params("parallel", "arbitrary"),
    )(q, k, v, small, g, w2_pad, gate_b.reshape(1, kw), norm_w.reshape(1, -1))


def _out_proj_body(*refs, n_in, final_norm, next_widths):
    a_refs, x_ref, w_ref = refs[:n_in], refs[n_in], refs[n_in + 1]
    rest = refs[n_in + 2:]
    acc = x_ref[...]
    off = 0
    for a_ref in a_refs:
        width = a_ref.shape[1]
        acc = acc + jnp.dot(a_ref[...].astype(BF16), w_ref[off:off + width, :], preferred_element_type=F32)
        off += width
    if final_norm:
        fw_ref, o_ref = rest
        o_ref[...] = _rms(acc, fw_ref[...])
        return
    if not next_widths:
        (o_ref,) = rest
        o_ref[...] = acc
        return
    nw_ref, w2_ref, o_ref = rest[:3]
    o_ref[...] = acc
    h = _rms(acc, nw_ref[...]).astype(BF16)
    off = 0
    for p_ref, width in zip(rest[3:], next_widths):
        p_ref[...] = jnp.dot(h, w2_ref[:, off:off + width], preferred_element_type=F32).astype(p_ref.dtype)
        off += width


def _out_proj(acts, x2d, weight, final_w=None, next_proj=None):
    m, d = x2d.shape
    tm = OUT_PROJ_ROWS if next_proj is None else IN_PROJ_ROWS
    n_in = len(acts)
    assert sum(a.shape[1] for a in acts) == weight.shape[0]
    row = lambda w: pl.BlockSpec((tm, w), lambda i: (i, 0))
    in_specs = [row(a.shape[1]) for a in acts] + [row(d), _resident(weight.shape)]
    args = list(acts) + [x2d, weight]
    out_specs, out_shape, widths = row(d), jax.ShapeDtypeStruct((m, d), F32), ()
    if final_w is not None:
        in_specs.append(_resident((1, d)))
        args.append(final_w.reshape(1, d))
    elif next_proj is not None:
        norm_w, w2, widths, dtypes = next_proj
        assert sum(widths) == w2.shape[1] and all(w % LANES == 0 for w in widths)
        in_specs += [_resident((1, d)), _resident(w2.shape)]
        args += [norm_w.reshape(1, d), w2]
        out_specs = [out_specs] + [row(w) for w in widths]
        out_shape = [out_shape] + [jax.ShapeDtypeStruct((m, w), dt) for w, dt in zip(widths, dtypes)]
    outs = pl.pallas_call(
        functools.partial(_out_proj_body, n_in=n_in, final_norm=final_w is not None, next_widths=tuple(widths)),
        name="out_proj",
        grid=(m // tm,),
        in_specs=in_specs,
        out_specs=out_specs,
        out_shape=out_shape,
        compiler_params=_params("parallel"),
    )(*args)
    return outs if next_proj is None else (outs[0], outs[1:])


class _MobaQueryBlock:
    def __init__(self, n_past, q_ref, z_ref, o_ref, kb_ref, vt_ref, kmean_ref, s_ref):
        blk = MOBA_BLOCK
        self.n_past, self.z_ref, self.o_ref, self.kb_ref, self.vt_ref = n_past, z_ref, o_ref, kb_ref, vt_ref
        self.own = slice(n_past * blk, (n_past + 1) * blk)
        self.s_ref = s_ref.at[n_past % s_ref.shape[0]]
        q2 = q_ref[0, self.own, :]
        lo = lax.broadcasted_iota(jnp.int32, (blk, LANES), 1) < HALF
        q_cat = jnp.concatenate([jnp.where(lo, q2, 0.0), jnp.where(lo, 0.0, q2)], axis=0)
        scale = MOBA_HEAD_DIM ** -0.5
        key_i = lax.broadcasted_iota(jnp.int32, (blk, 2 * blk), 0)
        qry_i = lax.broadcasted_iota(jnp.int32, (blk, 2 * blk), 1)
        causal_t = key_i <= jnp.where(qry_i < blk, qry_i, qry_i - blk)

        self.masks = []
        if n_past:
            nb = kmean_ref.shape[0] // 2
            parts = [_dot_nt(kmean_ref[...], q_part) for q_part in _split_bf16(q_cat * scale, 2)]
            gate = (parts[0][0:nb] + parts[0][nb:]) + (parts[1][0:nb] + parts[1][nb:])
            g_rows = [gate[n:n + 1, :] for n in range(n_past)]
            for n in range(n_past):
                rank = jnp.zeros((1, 2 * blk), F32)
                for m in range(n_past):
                    if m != n:
                        ahead = (g_rows[m] >= g_rows[n]) if m < n else (g_rows[m] > g_rows[n])
                        rank = rank + jnp.where(ahead, 1.0, 0.0)
                self.masks.append(rank < MOBA_TOPK)
        self.masks.append(causal_t)
        self.q_s = (q_cat * (scale * LOG2_E)).astype(BF16)
        self.m_run = None
        self.p = []

    def logits_step(self, j):
        rows = slice(j * MOBA_BLOCK, (j + 1) * MOBA_BLOCK)
        sj = jnp.where(self.masks[j], _dot_nt(self.kb_ref[rows, :], self.q_s), -jnp.inf)
        self.s_ref[rows, :] = sj
        mj = jnp.max(sj, axis=0, keepdims=True)
        self.m_run = mj if self.m_run is None else jnp.maximum(self.m_run, mj)

    def value_step(self, j):
        rows = slice(j * MOBA_BLOCK, (j + 1) * MOBA_BLOCK)
        self.p.append(jnp.exp2(self.s_ref[rows, :] - self.m_run).astype(BF16))

    def finish(self):
        blk = MOBA_BLOCK
        nk = len(self.p) * blk
        p_all = jnp.concatenate(self.p, axis=0)
        outs = []
        for half in range(2):
            vt = self.vt_ref[half * VT_ROWS:(half + 1) * VT_ROWS, 0:nk]
            acc = jnp.dot(vt, p_all[:, half * blk:(half + 1) * blk], preferred_element_type=F32)
            outs.append(acc[0:HALF, :] / acc[HALF:HALF + 1, :])
        o_t = jnp.concatenate(outs, axis=0)
        self.o_ref[0, self.own, :] = (o_t.T * _silu(self.z_ref[0, self.own, :])).astype(self.o_ref.dtype)


def _moba_body(q_ref, k_ref, v_ref, z_ref, o_ref, vt_ref, kmean_ref, s_ref):
    blk = MOBA_BLOCK
    nb = k_ref.shape[1] // blk
    kmean = []
    for n in range(nb):
        rows = slice(n * blk, (n + 1) * blk)
        kmean.append(jnp.mean(k_ref[0, rows, :].astype(F32), axis=0, keepdims=True))
        v_t = v_ref[0, rows, :].astype(F32).T.astype(BF16)
        for half in range(2):
            vt_ref[half * VT_ROWS:half * VT_ROWS + HALF, rows] = v_t[half * HALF:(half + 1) * HALF, :]
    for half in range(2):
        vt_ref[half * VT_ROWS + HALF:(half + 1) * VT_ROWS, :] = jnp.ones((BF16_SUBLANES, vt_ref.shape[1]), BF16)
    kmean_ref[...] = jnp.concatenate(_split_bf16(jnp.concatenate(kmean, axis=0), 2), axis=0)
    make = functools.partial(_MobaQueryBlock, q_ref=q_ref, z_ref=z_ref, o_ref=o_ref, kb_ref=k_ref.at[0],
                             vt_ref=vt_ref, kmean_ref=kmean_ref, s_ref=s_ref)
    blocks = {0: make(0), 1: make(1)}
    blocks[0].logits_step(0)
    for i in range(nb):
        if i + 2 < nb:
            blocks[i + 2] = make(i + 2)
        cur, nxt = blocks.pop(i), blocks.get(i + 1)
        for j in range(i + 2):
            if j <= i:
                cur.value_step(j)
            if nxt is not None:
                nxt.logits_step(j)
        cur.finish()


def _moba(q, k, v, z):
    b, l, w = q.shape
    blk = MOBA_BLOCK
    nb = l // blk
    spec = pl.BlockSpec((1, l, LANES), lambda bi, hp: (bi, 0, hp))
    return pl.pallas_call(
        _moba_body,
        name="moba",
        grid=(b, w // LANES),
        in_specs=[spec, spec, spec, spec],
        out_specs=spec,
        out_shape=jax.ShapeDtypeStruct((b, l, w), BF16),
        scratch_shapes=[pltpu.VMEM((2 * VT_ROWS, l), BF16),
                        pltpu.VMEM((2 * nb, LANES), BF16),
                        pltpu.VMEM((2, l, 2 * blk), F32)],
        compiler_params=_params("parallel", "parallel"),
    )(q, k, v, z)


def _even_layer(x, norm_w, w_in_bf, conv_w, conv_b, a_log, dt_bias, d_skip, ssd_norm_w, gate_w2, gate_b,
                gla_norm_w, side_casts=()):
    b, l, d = x.shape
    width = ssd_norm_w.shape[0]
    cdim = conv_w.shape[1]
    n_heads = a_log.shape[0]
    rank, kw = gate_w2.shape
    vw = gla_norm_w.shape[0] * GLA_HEADS
    cuts = [0]
    for s in (width, cdim, n_heads, kw, kw, vw, rank, vw):
        cuts.append(cuts[-1] + s)
    seg = lambda j: w_in_bf[:, cuts[j]:cuts[j + 1]]
    pad = jnp.zeros((d, LANES - n_heads - GATE_COPIES * rank), BF16)
    w_small = jnp.concatenate([seg(2)] + [seg(6)] * GATE_COPIES + [pad], axis=1)
    segments = [(cuts[j], cuts[j + 1] - cuts[j]) for j in (0, 1, 3, 4, 5, 7)]
    outs = _norm_proj(x.reshape(b * l, d), norm_w, w_in_bf, segments, [F32, F32, F32, F32, BF16, F32, F32],
                      extras=[w_small], side_casts=side_casts)
    z_a, xbc, q_b, k_b, v_b, g_b, small = [u.reshape(b, l, -1) for u in outs[:7]]
    y_a = _ssd(xbc, small, z_a, conv_w, conv_b, a_log, dt_bias, d_skip, ssd_norm_w)
    o_b = _gla(q_b, k_b, v_b, small, g_b, gate_w2, gate_b, gla_norm_w, lr_lane=n_heads)
    return [y_a.reshape(b * l, width), o_b.reshape(b * l, vw)], outs[7:]


ODD_PROJ_DTYPES = (F32, BF16, BF16, F32)


def _odd_layer(x, norm_w, w_in_bf, side_casts=(), projected=None):
    b, l, d = x.shape
    w = w_in_bf.shape[1] // 4
    outs = projected
    if outs is None:
        outs = _norm_proj(x.reshape(b * l, d), norm_w, w_in_bf, [(j * w, w) for j in range(4)],
                          list(ODD_PROJ_DTYPES), side_casts=side_casts)
    q, k, v, z = [u.reshape(b, l, w) for u in outs[:4]]
    return [_moba(q, k, v, z).reshape(b * l, w)], outs[4:]


def kernel(x, even_norm, even_w_in, even_conv_w, even_conv_b, even_a_log, even_dt_bias, even_d_skip, even_ssd_norm,
           even_gate_w2, even_gate_b, even_gla_norm, even_w_out, odd_norm, odd_w_in, odd_w_out, final_norm):
    b, l, d = x.shape
    depth = even_norm.shape[0] + odd_norm.shape[0]
    w_in = lambda layer: (even_w_in if layer % 2 == 0 else odd_w_in)[layer // 2]
    w_out = lambda layer: (even_w_out if layer % 2 == 0 else odd_w_out)[layer // 2]
    keys = [("out", 0)] + [(kind, layer) for layer in range(1, depth) for kind in ("in", "out")]
    pending = [w_in(layer) if kind == "in" else w_out(layer) for kind, layer in keys]
    bf = {("in", 0): w_in(0).astype(BF16)}
    projected = None
    for layer in range(depth):
        i = layer // 2
        side = pending if layer == 0 else ()
        if layer % 2 == 0:
            acts, cast = _even_layer(x, even_norm[i], bf["in", layer], even_conv_w[i], even_conv_b[i], even_a_log[i],
                                     even_dt_bias[i], even_d_skip[i], even_ssd_norm[i], even_gate_w2[i],
                                     even_gate_b[i], even_gla_norm[i], side_casts=side)
        else:
            acts, cast = _odd_layer(x, odd_norm[i], bf["in", layer], side_casts=side, projected=projected)
        bf.update(zip(keys, cast))
        projected = None
        x2d = x.reshape(b * l, d)
        if layer + 1 < depth and (layer + 1) % 2 == 1:
            nxt = layer + 1
            w2 = bf["in", nxt]
            x2d, projected = _out_proj(acts, x2d, bf["out", layer],
                                       next_proj=(odd_norm[nxt // 2], w2, [w2.shape[1] // 4] * 4, ODD_PROJ_DTYPES))
        else:
            x2d = _out_proj(acts, x2d, bf["out", layer], final_norm if layer == depth - 1 else None)
        x = x2d.reshape(b, l, d)
    return x
```

```python
import functools

import jax
import jax.numpy as jnp
from jax import lax
from jax.experimental import pallas as pl
from jax.experimental.pallas import tpu as pltpu

F32 = jnp.float32
BF16 = jnp.bfloat16

LANES = 128
SUBLANES = 8
BF16_SUBLANES = 16
VMEM_LIMIT_BYTES = 56 * 1024 * 1024

RMS_EPS = 1e-6
SSD_HEAD_DIM = 64
SSD_GROUPS = 2
SSD_STATE = 64
SSD_CONV = 4
SSD_CHUNK = 128
GLA_HEADS = 8
GLA_GATE_NORMALIZER = 16.0
GLA_CHUNK = 64
MOBA_HEAD_DIM = 64
MOBA_BLOCK = 256
MOBA_TOPK = 3
LOG2_E = 1.4426950408889634
CONV_ROW_STRIDE = 4
CUMSUM_TERMS = 3
GATE_COPIES = 3

IN_PROJ_ROWS = 512
OUT_PROJ_ROWS = 1024
EVEN_PIPE_ROWS = 256
JOB_COLS = 512
HALF = LANES // 2
VT_ROWS = HALF + BF16_SUBLANES


def _params(*sem):
    return pltpu.CompilerParams(dimension_semantics=sem, vmem_limit_bytes=VMEM_LIMIT_BYTES)


def _rms(x, w):
    return x * lax.rsqrt(jnp.mean(x * x, axis=-1, keepdims=True) + RMS_EPS) * w


def _silu(x):
    h = 0.5 * x
    return h + h * jnp.tanh(h)


def _softplus(x):
    return jnp.maximum(x, 0.0) + jnp.log1p(jnp.exp(-jnp.abs(x)))


def _log_sigmoid(x):
    return jnp.minimum(x, 0.0) - jnp.log(1.0 + jnp.exp(-jnp.abs(x)))


def _dot(a, b):
    return jnp.dot(a.astype(BF16), b.astype(BF16), preferred_element_type=F32)


def _dot_nt(a, b):
    return lax.dot_general(a.astype(BF16), b.astype(BF16), (((1,), (1,)), ((), ())), preferred_element_type=F32)


def _dot_tn(a, b):
    return lax.dot_general(a.astype(BF16), b.astype(BF16), (((0,), (0,)), ((), ())), preferred_element_type=F32)


def _split_bf16(x, terms):
    parts = []
    for _ in range(terms):
        p = x.astype(BF16)
        parts.append(p)
        x = x - p.astype(F32)
    return parts


def _chunk_cumsum(x, chunk):
    rows = x.shape[0]
    assert chunk & (chunk - 1) == 0
    ri = lax.broadcasted_iota(jnp.int32, (chunk, CUMSUM_TERMS * chunk), 0)
    ci = lax.broadcasted_iota(jnp.int32, (chunk, CUMSUM_TERMS * chunk), 1)
    tri = jnp.where((ci & (chunk - 1)) <= ri, 1.0, 0.0).astype(BF16)
    out = []
    for r0 in range(0, rows, chunk):
        stacked = jnp.concatenate(_split_bf16(x[r0:r0 + chunk, :], CUMSUM_TERMS), axis=0)
        out.append(jnp.dot(tri, stacked, preferred_element_type=F32))
    return out[0] if len(out) == 1 else jnp.concatenate(out, axis=0)


def _resident(shape):
    return pl.BlockSpec(shape, lambda *_: (0,) * len(shape), pipeline_mode=pl.Buffered(1))


def _norm_proj_body(x_ref, nw_ref, w_ref, *refs, offsets, n_extra, n_side, defer=False, realign=True):
    n_out = len(offsets) + n_extra
    extra_refs, side_in = refs[:n_extra], refs[n_extra:n_extra + n_side]
    o_refs = refs[n_extra + n_side:n_extra + n_side + n_out]
    side_out = refs[n_extra + n_side + n_out:n_extra + 2 * n_side + n_out]
    for s_in, s_out in zip(side_in, side_out):
        s_out[...] = s_in[...].astype(s_out.dtype)
    shifted = [i for i, off in enumerate(offsets) if off % LANES]
    starts, pos = {}, 0
    for i in shifted:
        starts[i] = pos
        pos += o_refs[i].shape[1]

    if shifted:
        al_ref = refs[-1]

        @pl.when(jnp.logical_and(pl.program_id(0) == 0, realign))
        def _():
            for i in shifted:
                width = o_refs[i].shape[1]
                al_ref[:, starts[i]:starts[i] + width] = w_ref[:, offsets[i]:offsets[i] + width]

    h = _rms(x_ref[...], nw_ref[...]).astype(BF16)
    jobs = []

    def seg_job(i, off, c0, c1):
        def run():
            w = al_ref[:, starts[i] + c0:starts[i] + c1] if i in starts else w_ref[:, off + c0:off + c1]
            o_refs[i][:, c0:c1] = jnp.dot(h, w, preferred_element_type=F32).astype(o_refs[i].dtype)
        return run

    def extra_job(e_ref, o_ref):
        def run():
            o_ref[...] = jnp.dot(h, e_ref[...], preferred_element_type=F32).astype(o_ref.dtype)
        return run

    for i, off in enumerate(offsets):
        width = o_refs[i].shape[1]
        jobs += [seg_job(i, off, c0, min(c0 + JOB_COLS, width)) for c0 in range(0, width, JOB_COLS)]
    jobs += [extra_job(e, o) for e, o in zip(extra_refs, o_refs[len(offsets):])]
    if defer:
        return jobs
    for job in jobs:
        job()


def _norm_proj(x2d, norm_w, weight, segments, out_dtypes, extras=(), side_casts=()):
    m, d = x2d.shape
    tm = IN_PROJ_ROWS
    steps = m // tm
    assert all(w % LANES == 0 for _, w in segments)
    assert all(s.shape[0] % (steps * BF16_SUBLANES) == 0 for s in side_casts)
    widths = [w for _, w in segments] + [e.shape[1] for e in extras]
    shifted_cols = sum(w for off, w in segments if off % LANES)
    side_specs = [pl.BlockSpec((s.shape[0] // steps, s.shape[1]), lambda i: (i, 0)) for s in side_casts]
    return pl.pallas_call(
        functools.partial(_norm_proj_body, offsets=tuple(off for off, _ in segments), n_extra=len(extras),
                          n_side=len(side_casts)),
        name="norm_proj",
        grid=(steps,),
        in_specs=[pl.BlockSpec((tm, d), lambda i: (i, 0)), _resident((1, d)), _resident(weight.shape)]
        + [_resident(e.shape) for e in extras] + side_specs,
        out_specs=[pl.BlockSpec((tm, w), lambda i: (i, 0)) for w in widths] + side_specs,
        out_shape=[jax.ShapeDtypeStruct((m, w), dt) for w, dt in zip(widths, out_dtypes)]
        + [jax.ShapeDtypeStruct(s.shape, BF16) for s in side_casts],
        scratch_shapes=[pltpu.VMEM((d, shifted_cols), weight.dtype)] if shifted_cols else [],
        compiler_params=_params("arbitrary"),
    )(x2d, norm_w.reshape(1, d), weight, *extras, *side_casts)


def _ssd_body(xbc_ref, sm_ref, z_ref, cw_ref, cb_ref, alog_ref, dtb_ref, dskip_ref, nw_ref,
              y_ref, h_ref, xpad_ref, act_ref, xw_ref, ycat_ref, side=None, first=None):
    q = SSD_CHUNK
    t = xbc_ref.shape[1]
    width = y_ref.shape[-1]
    n_pairs = width // LANES
    n_slabs = xpad_ref.shape[0]
    gstate = SSD_GROUPS * SSD_STATE
    gwidth = width // SSD_GROUPS
    if first is None:
        first = pl.program_id(1) == 0

    @pl.when(first)
    def _():
        xpad_ref[:, 0:SUBLANES, :] = jnp.zeros((n_slabs, SUBLANES, LANES), F32)
        h_ref[...] = jnp.zeros_like(h_ref)

    @pl.when(jnp.logical_not(first))
    def _():
        xpad_ref[:, 0:SUBLANES, :] = xpad_ref[:, t:t + SUBLANES, :]

    for c in range(n_slabs):
        lanes = slice(c * LANES, (c + 1) * LANES)
        xpad_ref[c, SUBLANES:SUBLANES + t, :] = xbc_ref[0, :, lanes]
        taps = [jnp.broadcast_to(cw_ref[k:k + 1, lanes], (SUBLANES, LANES)) for k in range(SSD_CONV)]
        bias = jnp.broadcast_to(cb_ref[:, lanes], (SUBLANES, LANES))
        for t0 in range(0, t, SUBLANES * CONV_ROW_STRIDE):
            for g in range(CONV_ROW_STRIDE):
                conv = bias
                for k in range(SSD_CONV):
                    start = SUBLANES + t0 + g - (SSD_CONV - 1 - k)
                    conv = conv + taps[k] * xpad_ref[c, pl.ds(start, SUBLANES, stride=CONV_ROW_STRIDE), :]
                act_ref[c, pl.ds(t0 + g, SUBLANES, stride=CONV_ROW_STRIDE), :] = _silu(conv)

    dt_all = _softplus(sm_ref[0] + dtb_ref[...])
    dta = dt_all * (-jnp.exp(alog_ref[...]) * LOG2_E)
    a_cs_all = _chunk_cumsum(dta, q)
    ri = lax.broadcasted_iota(jnp.int32, (q, q), 0)
    ci = lax.broadcasted_iota(jnp.int32, (q, q), 1)
    causal = ri >= ci
    lane = lax.broadcasted_iota(jnp.int32, (q, LANES), 1)
    lo = lane < HALF
    lo_row = lo[0:1, :]

    h_prev = h_ref[...]
    for rows in [slice(r0, r0 + q) for r0 in range(0, t, q)]:
        if side:
            side.pop(0)()
        bm = act_ref[n_pairs, rows, :]
        cm = act_ref[n_pairs + 1, rows, :]
        dt = dt_all[rows, :]
        a_cs = a_cs_all[rows, :]
        a_cs_t = a_cs.T
        a_last = a_cs[q - 1:q, :]
        bm_t = bm.T
        cm_g = [jnp.where(lo, cm, 0.0), jnp.where(lo, 0.0, cm)]
        cb = [_dot_nt(c, bm) for c in cm_g]
        y_off = [_dot(c, h_prev) for c in cm_g]

        a_last_pairs = []
        for p in range(n_pairs):
            if side and p % 2 == 1:
                side.pop(0)()
            e0, e1 = 2 * p, 2 * p + 1
            g = (p * LANES) // gwidth
            col = slice(p * LANES, (p + 1) * LANES)
            gcol = slice(p * LANES - g * gwidth, (p + 1) * LANES - g * gwidth)
            acs_pair = jnp.where(lo, a_cs[:, e0:e0 + 1], a_cs[:, e1:e1 + 1])
            dt_pair = jnp.where(lo, dt[:, e0:e0 + 1], dt[:, e1:e1 + 1])
            al_pair = jnp.where(lo_row, a_last[:, e0:e0 + 1], a_last[:, e1:e1 + 1])
            a_last_pairs.append(al_pair)
            xs2 = act_ref[p, rows, :]
            xdt = xs2 * dt_pair
            xdt_b = xdt.astype(BF16)
            yd = []
            for e in (e0, e1):
                seg = a_cs[:, e:e + 1] - a_cs_t[e:e + 1, :]
                decay = jnp.exp2(jnp.where(causal, seg, -jnp.inf))
                yd.append(_dot(cb[g] * decay, xdt_b))
            y2 = jnp.where(lo, yd[0], yd[1])
            y2 = y2 + y_off[g][:, gcol] * jnp.exp2(acs_pair) + dskip_ref[:, col] * xs2
            ycat_ref[rows, col] = y2
            xw_ref[rows, col] = (xdt * jnp.exp2(al_pair - acs_pair)).astype(BF16)

        h_next = []
        for g in range(SSD_GROUPS):
            srows = slice(g * SSD_STATE, (g + 1) * SSD_STATE)
            ppg = n_pairs // SSD_GROUPS
            dec = jnp.exp2(jnp.concatenate(a_last_pairs[g * ppg:(g + 1) * ppg], axis=1))
            s_g = _dot(bm_t[srows, :], xw_ref[rows, g * gwidth:(g + 1) * gwidth])
            h_next.append(h_prev[srows, :] * dec + s_g)
        h_prev = jnp.concatenate(h_next, axis=0)
        y_ref[0, rows, :] = _rms(ycat_ref[rows, :] * _silu(z_ref[0, rows, :]), nw_ref[...]).astype(y_ref.dtype)
    h_ref[...] = h_prev


def _gla_body(q_ref, k_ref, v_ref, sm_ref, g_ref, w2_ref, gb_ref, nw_ref, o_ref, st_ref, *, mid_lane, side=None, first=None):
    c = GLA_CHUNK
    kw = q_ref.shape[-1]
    dk = kw // GLA_HEADS
    dv = v_ref.shape[-1] // GLA_HEADS
    n_pairs = kw // LANES

    if first is None:
        first = pl.program_id(1) == 0

    @pl.when(first)
    def _():
        st_ref[...] = jnp.zeros_like(st_ref)

    ri = lax.broadcasted_iota(jnp.int32, (c, c), 0)
    ci = lax.broadcasted_iota(jnp.int32, (c, c), 1)
    causal = ri >= ci
    lo = lax.broadcasted_iota(jnp.int32, (c, LANES), 1) < HALF
    lo_sq = lax.broadcasted_iota(jnp.int32, (dv, LANES), 1) < HALF

    sm_hi, sm_mid = _split_bf16(sm_ref[0], 2)
    sm_lane = lax.broadcasted_iota(jnp.int32, sm_hi.shape, 1)
    pre = jnp.dot(jnp.where(sm_lane < mid_lane, sm_hi, sm_mid), w2_ref[...],
                  preferred_element_type=F32) + gb_ref[...]
    gcs_all = _chunk_cumsum(_log_sigmoid(pre) * (LOG2_E / GLA_GATE_NORMALIZER), c)

    chunks = [slice(ch * c, (ch + 1) * c) for ch in range(q_ref.shape[1] // c)]
    pairs = [slice(p * LANES, (p + 1) * LANES) for p in range(n_pairs)]
    causal2 = jnp.concatenate([causal, causal], axis=0)

    def by_head(x):
        return jnp.concatenate([jnp.where(lo, x, 0.0), jnp.where(lo, 0.0, x)], axis=0).astype(BF16)

    q_st, k_st, g_last, scores = {}, {}, {}, {}

    def front(ch):
        rows = chunks[ch]
        gcs = gcs_all[rows, :]
        g_mid = gcs[c // 2:c // 2 + 1, :]
        g_last[ch] = gcs[c - 1:c, :]
        qs = q_ref[0, rows, :] * (dk ** -0.5)
        ks = k_ref[0, rows, :]
        q_mid = qs * jnp.exp2(gcs - g_mid)
        k_mid = ks * jnp.exp2(g_mid - gcs)
        k_in = k_mid.astype(BF16)
        q_st[ch] = q_mid * jnp.exp2(g_mid)
        k_st[ch] = (k_mid * jnp.exp2(g_last[ch] - g_mid)).astype(BF16)
        scores[ch] = [jnp.where(causal2, _dot_nt(by_head(q_mid[:, col]), k_in[:, col]), 0.0).astype(BF16)
                      for col in pairs]

    def back(ch, states):
        rows = chunks[ch]
        for p, col in enumerate(pairs):
            if side and p % 2 == 1:
                side.pop(0)()
            v_pair = [v_ref[0, rows, (2 * p + half) * dv:(2 * p + half + 1) * dv].astype(BF16) for half in range(2)]
            o_inter = _dot_nt(by_head(q_st[ch][:, col]), states[p])
            kv_t = jnp.where(lo_sq, _dot_tn(v_pair[0], k_st[ch][:, col]), _dot_tn(v_pair[1], k_st[ch][:, col]))
            states[p] = states[p] * jnp.exp2(g_last[ch][:, col]) + kv_t
            for half in range(2):
                vcol = slice((2 * p + half) * dv, (2 * p + half + 1) * dv)
                o = (jnp.dot(scores[ch][p][half * c:(half + 1) * c, :], v_pair[half], preferred_element_type=F32)
                     + o_inter[half * c:(half + 1) * c, :])
                o_ref[0, rows, vcol] = (_rms(o, nw_ref[...]) * _silu(g_ref[0, rows, vcol])).astype(o_ref.dtype)
        del q_st[ch], k_st[ch], g_last[ch], scores[ch]

    states = [st_ref[:, col] for col in pairs]
    front(0)
    for ch in range(len(chunks)):
        if ch + 1 < len(chunks):
            front(ch + 1)
        if side:
            side.pop(0)()
        back(ch, states)
    for p, col in enumerate(pairs):
        st_ref[:, col] = states[p]


def _even_pipe_body(*refs, n_side, offsets, mid_lane, blocks_per_seq):
    it = iter(refs)
    take = lambda n: [next(it) for _ in range(n)]
    x_ref, nw_ref, w_ref, wsm_ref = take(4)
    side_in = take(n_side)
    ssd_par = take(6)
    gla_par = take(3)
    y_ref, o_ref = take(2)
    side_out = take(n_side)
    (al_ref,) = take(1)
    u_refs = take(7)
    ssd_scr = take(5)
    gla_scr = take(1)
    z_s, xbc_s, q_s, k_s, v_s, g_s, sm_s = u_refs
    s = pl.program_id(0)

    for s_in, s_out in zip(side_in, side_out):
        s_out[...] = s_in[...].astype(s_out.dtype)

    @pl.when(s == 0)
    def _():
        for u in u_refs:
            u[1] = jnp.zeros(u.shape[1:], u.dtype)

    scanned = jnp.maximum(s - 1, 0)
    first = scanned % blocks_per_seq == 0

    for parity in range(2):
        def step(wslot=parity, rslot=1 - parity):
            outs = [u.at[wslot] for u in (z_s, xbc_s, q_s, k_s, v_s, g_s, sm_s)]
            jobs = _norm_proj_body(x_ref, nw_ref, w_ref, wsm_ref, *outs, al_ref, offsets=offsets, n_extra=1,
                                   n_side=0, defer=True, realign=parity == 0)
            rd = lambda u: u.at[rslot:rslot + 1]
            _ssd_body(rd(xbc_s), rd(sm_s), rd(z_s), *ssd_par, y_ref, *ssd_scr, side=jobs, first=first)
            _gla_body(rd(q_s), rd(k_s), rd(v_s), rd(sm_s), rd(g_s), *gla_par, o_ref, *gla_scr, mid_lane=mid_lane,
                      side=jobs, first=first)
            for job in jobs:
                job()
        pl.when(s % 2 == parity)(step)


def _even_pipe(x, norm_w, w_in_bf, w_small, segments, conv_w, conv_b, a_log, dt_bias, d_skip, ssd_norm_w,
               gate_w2, gate_b, gla_norm_w, lr_lane, side_casts):
    b, l, d = x.shape
    t = EVEN_PIPE_ROWS
    width = ssd_norm_w.shape[0]
    cdim = conv_w.shape[1]
    n_heads = a_log.shape[0]
    rank, kw = gate_w2.shape
    vw = gla_norm_w.shape[0] * GLA_HEADS
    bps = l // t
    n = b * bps
    pad = lambda v: jnp.pad(v, (0, LANES - n_heads)).reshape(1, LANES)
    w2_hi, w2_mid = _split_bf16(gate_w2, 2)
    w2_pad = jnp.zeros((LANES, kw), BF16)
    for copy, part in enumerate((w2_hi, w2_mid, w2_hi)):
        w2_pad = w2_pad.at[lr_lane + copy * rank:lr_lane + (copy + 1) * rank, :].set(part)
    assert all(sc.shape[0] % (n * BF16_SUBLANES) == 0 for sc in side_casts)
    shifted_cols = sum(w for off, w in segments if off % LANES)
    clamp = lambda s: jnp.minimum(s, n - 1)
    side_specs = [pl.BlockSpec((sc.shape[0] // n, sc.shape[1]), lambda s: (clamp(s), 0)) for sc in side_casts]
    scan_blk = lambda w: pl.BlockSpec((1, t, w), lambda s: (jnp.maximum(s - 1, 0) // bps, jnp.maximum(s - 1, 0) % bps, 0))
    widths = [w for _, w in segments] + [LANES]
    dtypes = [F32, F32, F32, F32, BF16, F32, F32]
    gwidth = width // SSD_GROUPS
    outs = pl.pallas_call(
        functools.partial(_even_pipe_body, n_side=len(side_casts), offsets=tuple(off for off, _ in segments),
                          mid_lane=lr_lane + 2 * rank, blocks_per_seq=bps),
        name="even_pipe",
        grid=(n + 1,),
        in_specs=[pl.BlockSpec((t, d), lambda s: (clamp(s), 0)), _resident((1, d)), _resident(w_in_bf.shape),
                  _resident(w_small.shape)] + side_specs
        + [_resident((SSD_CONV, cdim)), _resident((1, cdim)), _resident((1, LANES)), _resident((1, LANES)),
           _resident((1, width)), _resident((1, width)),
           _resident((LANES, kw)), _resident((1, kw)), _resident((1, vw // GLA_HEADS))],
        out_specs=[scan_blk(width), scan_blk(vw)] + side_specs,
        out_shape=[jax.ShapeDtypeStruct((b, l, width), BF16), jax.ShapeDtypeStruct((b, l, vw), BF16)]
        + [jax.ShapeDtypeStruct(sc.shape, BF16) for sc in side_casts],
        scratch_shapes=[pltpu.VMEM((d, shifted_cols), BF16)]
        + [pltpu.VMEM((2, t, w), dt) for w, dt in zip(widths, dtypes)]
        + [pltpu.VMEM((SSD_GROUPS * SSD_STATE, gwidth), F32),
           pltpu.VMEM((cdim // LANES, t + SUBLANES, LANES), F32),
           pltpu.VMEM((cdim // LANES, t, LANES), F32),
           pltpu.VMEM((t, width), BF16),
           pltpu.VMEM((t, width), F32),
           pltpu.VMEM((vw // GLA_HEADS, kw), F32)],
        compiler_params=_params("arbitrary"),
    )(x.reshape(b * l, d), norm_w.reshape(1, d), w_in_bf, w_small, *side_casts,
      conv_w, conv_b.reshape(1, cdim), pad(a_log), pad(dt_bias), jnp.repeat(d_skip, SSD_HEAD_DIM).reshape(1, width),
      ssd_norm_w.reshape(1, width), w2_pad, gate_b.reshape(1, kw), gla_norm_w.reshape(1, -1))
    return outs[0], outs[1], outs[2:]


def _out_proj_body(*refs, n_in, final_norm, next_widths):
    a_refs, x_ref, w_ref = refs[:n_in], refs[n_in], refs[n_in + 1]
    rest = refs[n_in + 2:]
    acc = x_ref[...]
    off = 0
    for a_ref in a_refs:
        width = a_ref.shape[1]
        acc = acc + jnp.dot(a_ref[...].astype(BF16), w_ref[off:off + width, :], preferred_element_type=F32)
        off += width
    if final_norm:
        fw_ref, o_ref = rest
        o_ref[...] = _rms(acc, fw_ref[...])
        return
    if not next_widths:
        (o_ref,) = rest
        o_ref[...] = acc
        return
    nw_ref, w2_ref, o_ref = rest[:3]
    o_ref[...] = acc
    h = _rms(acc, nw_ref[...]).astype(BF16)
    off = 0
    for p_ref, width in zip(rest[3:], next_widths):
        p_ref[...] = jnp.dot(h, w2_ref[:, off:off + width], preferred_element_type=F32).astype(p_ref.dtype)
        off += width


def _out_proj(acts, x2d, weight, final_w=None, next_proj=None):
    m, d = x2d.shape
    tm = OUT_PROJ_ROWS if next_proj is None else IN_PROJ_ROWS
    n_in = len(acts)
    assert sum(a.shape[1] for a in acts) == weight.shape[0]
    row = lambda w: pl.BlockSpec((tm, w), lambda i: (i, 0))
    in_specs = [row(a.shape[1]) for a in acts] + [row(d), _resident(weight.shape)]
    args = list(acts) + [x2d, weight]
    out_specs, out_shape, widths = row(d), jax.ShapeDtypeStruct((m, d), F32), ()
    if final_w is not None:
        in_specs.append(_resident((1, d)))
        args.append(final_w.reshape(1, d))
    elif next_proj is not None:
        norm_w, w2, widths, dtypes = next_proj
        assert sum(widths) == w2.shape[1] and all(w % LANES == 0 for w in widths)
        in_specs += [_resident((1, d)), _resident(w2.shape)]
        args += [norm_w.reshape(1, d), w2]
        out_specs = [out_specs] + [row(w) for w in widths]
        out_shape = [out_shape] + [jax.ShapeDtypeStruct((m, w), dt) for w, dt in zip(widths, dtypes)]
    outs = pl.pallas_call(
        functools.partial(_out_proj_body, n_in=n_in, final_norm=final_w is not None, next_widths=tuple(widths)),
        name="out_proj",
        grid=(m // tm,),
        in_specs=in_specs,
        out_specs=out_specs,
        out_shape=out_shape,
        compiler_params=_params("parallel"),
    )(*args)
    return outs if next_proj is None else (outs[0], outs[1:])


class _MobaQueryBlock:
    def __init__(self, n_past, q_ref, z_ref, o_ref, kb_ref, vt_ref, kmean_ref, s_ref):
        blk = MOBA_BLOCK
        self.n_past, self.z_ref, self.o_ref, self.kb_ref, self.vt_ref = n_past, z_ref, o_ref, kb_ref, vt_ref
        self.own = slice(n_past * blk, (n_past + 1) * blk)
        self.s_ref = s_ref.at[n_past % s_ref.shape[0]]
        q2 = q_ref[0, self.own, :]
        lo = lax.broadcasted_iota(jnp.int32, (blk, LANES), 1) < HALF
        q_cat = jnp.concatenate([jnp.where(lo, q2, 0.0), jnp.where(lo, 0.0, q2)], axis=0)
        scale = MOBA_HEAD_DIM ** -0.5
        key_i = lax.broadcasted_iota(jnp.int32, (blk, 2 * blk), 0)
        qry_i = lax.broadcasted_iota(jnp.int32, (blk, 2 * blk), 1)
        causal_t = key_i <= jnp.where(qry_i < blk, qry_i, qry_i - blk)

        self.masks = []
        if n_past:
            nb = kmean_ref.shape[0] // 2
            parts = [_dot_nt(kmean_ref[...], q_part) for q_part in _split_bf16(q_cat * scale, 2)]
            gate = (parts[0][0:nb] + parts[0][nb:]) + (parts[1][0:nb] + parts[1][nb:])
            g_rows = [gate[n:n + 1, :] for n in range(n_past)]
            for n in range(n_past):
                rank = jnp.zeros((1, 2 * blk), F32)
                for m in range(n_past):
                    if m != n:
                        ahead = (g_rows[m] >= g_rows[n]) if m < n else (g_rows[m] > g_rows[n])
                        rank = rank + jnp.where(ahead, 1.0, 0.0)
                self.masks.append(rank < MOBA_TOPK)
        self.masks.append(causal_t)
        self.q_s = (q_cat * (scale * LOG2_E)).astype(BF16)
        self.m_run = None
        self.p = []

    def logits_step(self, j):
        rows = slice(j * MOBA_BLOCK, (j + 1) * MOBA_BLOCK)
        sj = jnp.where(self.masks[j], _dot_nt(self.kb_ref[rows, :], self.q_s), -jnp.inf)
        self.s_ref[rows, :] = sj
        mj = jnp.max(sj, axis=0, keepdims=True)
        self.m_run = mj if self.m_run is None else jnp.maximum(self.m_run, mj)

    def value_step(self, j):
        rows = slice(j * MOBA_BLOCK, (j + 1) * MOBA_BLOCK)
        self.p.append(jnp.exp2(self.s_ref[rows, :] - self.m_run).astype(BF16))

    def finish(self):
        blk = MOBA_BLOCK
        nk = len(self.p) * blk
        p_all = jnp.concatenate(self.p, axis=0)
        outs = []
        for half in range(2):
            vt = self.vt_ref[half * VT_ROWS:(half + 1) * VT_ROWS, 0:nk]
            acc = jnp.dot(vt, p_all[:, half * blk:(half + 1) * blk], preferred_element_type=F32)
            outs.append(acc[0:HALF, :] / acc[HALF:HALF + 1, :])
        o_t = jnp.concatenate(outs, axis=0)
        self.o_ref[0, self.own, :] = (o_t.T * _silu(self.z_ref[0, self.own, :])).astype(self.o_ref.dtype)


def _moba_body(q_ref, k_ref, v_ref, z_ref, o_ref, vt_ref, kmean_ref, s_ref):
    blk = MOBA_BLOCK
    nb = k_ref.shape[1] // blk
    kmean = []
    for n in range(nb):
        rows = slice(n * blk, (n + 1) * blk)
        kmean.append(jnp.mean(k_ref[0, rows, :].astype(F32), axis=0, keepdims=True))
        v_t = v_ref[0, rows, :].astype(F32).T.astype(BF16)
        for half in range(2):
            vt_ref[half * VT_ROWS:half * VT_ROWS + HALF, rows] = v_t[half * HALF:(half + 1) * HALF, :]
    for half in range(2):
        vt_ref[half * VT_ROWS + HALF:(half + 1) * VT_ROWS, :] = jnp.ones((BF16_SUBLANES, vt_ref.shape[1]), BF16)
    kmean_ref[...] = jnp.concatenate(_split_bf16(jnp.concatenate(kmean, axis=0), 2), axis=0)
    make = functools.partial(_MobaQueryBlock, q_ref=q_ref, z_ref=z_ref, o_ref=o_ref, kb_ref=k_ref.at[0],
                             vt_ref=vt_ref, kmean_ref=kmean_ref, s_ref=s_ref)
    blocks = {0: make(0), 1: make(1)}
    blocks[0].logits_step(0)
    for i in range(nb):
        if i + 2 < nb:
            blocks[i + 2] = make(i + 2)
        cur, nxt = blocks.pop(i), blocks.get(i + 1)
        for j in range(i + 2):
            if j <= i:
                cur.value_step(j)
            if nxt is not None:
                nxt.logits_step(j)
        cur.finish()


def _moba(q, k, v, z):
    b, l, w = q.shape
    blk = MOBA_BLOCK
    nb = l // blk
    spec = pl.BlockSpec((1, l, LANES), lambda bi, hp: (bi, 0, hp))
    return pl.pallas_call(
        _moba_body,
        name="moba",
        grid=(b, w // LANES),
        in_specs=[spec, spec, spec, spec],
        out_specs=spec,
        out_shape=jax.ShapeDtypeStruct((b, l, w), BF16),
        scratch_shapes=[pltpu.VMEM((2 * VT_ROWS, l), BF16),
                        pltpu.VMEM((2 * nb, LANES), BF16),
                        pltpu.VMEM((2, l, 2 * blk), F32)],
        compiler_params=_params("parallel", "parallel"),
    )(q, k, v, z)


def _even_layer(x, norm_w, w_in_bf, conv_w, conv_b, a_log, dt_bias, d_skip, ssd_norm_w, gate_w2, gate_b,
                gla_norm_w, side_casts=()):
    b, l, d = x.shape
    width = ssd_norm_w.shape[0]
    cdim = conv_w.shape[1]
    n_heads = a_log.shape[0]
    rank, kw = gate_w2.shape
    vw = gla_norm_w.shape[0] * GLA_HEADS
    cuts = [0]
    for s in (width, cdim, n_heads, kw, kw, vw, rank, vw):
        cuts.append(cuts[-1] + s)
    seg = lambda j: w_in_bf[:, cuts[j]:cuts[j + 1]]
    pad = jnp.zeros((d, LANES - n_heads - GATE_COPIES * rank), BF16)
    w_small = jnp.concatenate([seg(2)] + [seg(6)] * GATE_COPIES + [pad], axis=1)
    segments = [(cuts[j], cuts[j + 1] - cuts[j]) for j in (0, 1, 3, 4, 5, 7)]
    y_a, o_b, cast = _even_pipe(x, norm_w, w_in_bf, w_small, segments, conv_w, conv_b, a_log, dt_bias, d_skip,
                                ssd_norm_w, gate_w2, gate_b, gla_norm_w, n_heads, side_casts)
    return [y_a.reshape(b * l, width), o_b.reshape(b * l, vw)], cast


ODD_PROJ_DTYPES = (F32, BF16, BF16, F32)


def _odd_layer(x, norm_w, w_in_bf, side_casts=(), projected=None):
    b, l, d = x.shape
    w = w_in_bf.shape[1] // 4
    outs = projected
    if outs is None:
        outs = _norm_proj(x.reshape(b * l, d), norm_w, w_in_bf, [(j * w, w) for j in range(4)],
                          list(ODD_PROJ_DTYPES), side_casts=side_casts)
    q, k, v, z = [u.reshape(b, l, w) for u in outs[:4]]
    return [_moba(q, k, v, z).reshape(b * l, w)], outs[4:]


def kernel(x, even_norm, even_w_in, even_conv_w, even_conv_b, even_a_log, even_dt_bias, even_d_skip, even_ssd_norm,
           even_gate_w2, even_gate_b, even_gla_norm, even_w_out, odd_norm, odd_w_in, odd_w_out, final_norm):
    b, l, d = x.shape
    depth = even_norm.shape[0] + odd_norm.shape[0]
    w_in = lambda layer: (even_w_in if layer % 2 == 0 else odd_w_in)[layer // 2]
    w_out = lambda layer: (even_w_out if layer % 2 == 0 else odd_w_out)[layer // 2]
    keys = [("out", 0)] + [(kind, layer) for layer in range(1, depth) for kind in ("in", "out")]
    pending = [w_in(layer) if kind == "in" else w_out(layer) for kind, layer in keys]
    bf = {("in", 0): w_in(0).astype(BF16)}
    projected = None
    for layer in range(depth):
        i = layer // 2
        side = pending if layer == 0 else ()
        if layer % 2 == 0:
            acts, cast = _even_layer(x, even_norm[i], bf["in", layer], even_conv_w[i], even_conv_b[i], even_a_log[i],
                                     even_dt_bias[i], even_d_skip[i], even_ssd_norm[i], even_gate_w2[i],
                                     even_gate_b[i], even_gla_norm[i], side_casts=side)
        else:
            acts, cast = _odd_layer(x, odd_norm[i], bf["in", layer], side_casts=side, projected=projected)
        bf.update(zip(keys, cast))
        projected = None
        x2d = x.reshape(b * l, d)
        if layer + 1 < depth and (layer + 1) % 2 == 1:
            nxt = layer + 1
            w2 = bf["in", nxt]
            x2d, projected = _out_proj(acts, x2d, bf["out", layer],
                                       next_proj=(odd_norm[nxt // 2], w2, [w2.shape[1] // 4] * 4, ODD_PROJ_DTYPES))
        else:
            x2d = _out_proj(acts, x2d, bf["out", layer], final_norm if layer == depth - 1 else None)
        x = x2d.reshape(b, l, d)
    return x
```

```python
import functools

import jax
import jax.numpy as jnp
from jax import lax
from jax.experimental import pallas as pl
from jax.experimental.pallas import tpu as pltpu

F32 = jnp.float32
BF16 = jnp.bfloat16

LANES = 128
SUBLANES = 8
BF16_SUBLANES = 16
VMEM_LIMIT_BYTES = 56 * 1024 * 1024

RMS_EPS = 1e-6
SSD_HEAD_DIM = 64
SSD_GROUPS = 2
SSD_STATE = 64
SSD_CONV = 4
SSD_CHUNK = 128
GLA_HEADS = 8
GLA_GATE_NORMALIZER = 16.0
GLA_CHUNK = 64
MOBA_HEAD_DIM = 64
MOBA_BLOCK = 256
MOBA_TOPK = 3
LOG2_E = 1.4426950408889634
CONV_ROW_STRIDE = 4
CUMSUM_TERMS = 3
GATE_COPIES = 3

IN_PROJ_ROWS = 512
OUT_PROJ_ROWS = 1024
EVEN_PIPE_ROWS = 256
JOB_COLS = 512
HALF = LANES // 2
VT_ROWS = HALF + BF16_SUBLANES


def _params(*sem):
    return pltpu.CompilerParams(dimension_semantics=sem, vmem_limit_bytes=VMEM_LIMIT_BYTES)


def _rms(x, w):
    return x * lax.rsqrt(jnp.mean(x * x, axis=-1, keepdims=True) + RMS_EPS) * w


def _silu(x):
    h = 0.5 * x
    return h + h * jnp.tanh(h)


def _softplus(x):
    return jnp.maximum(x, 0.0) + jnp.log1p(jnp.exp(-jnp.abs(x)))


def _log_sigmoid(x):
    return jnp.minimum(x, 0.0) - jnp.log(1.0 + jnp.exp(-jnp.abs(x)))


def _dot(a, b):
    return jnp.dot(a.astype(BF16), b.astype(BF16), preferred_element_type=F32)


def _dot_nt(a, b):
    return lax.dot_general(a.astype(BF16), b.astype(BF16), (((1,), (1,)), ((), ())), preferred_element_type=F32)


def _dot_tn(a, b):
    return lax.dot_general(a.astype(BF16), b.astype(BF16), (((0,), (0,)), ((), ())), preferred_element_type=F32)


def _split_bf16(x, terms):
    parts = []
    for _ in range(terms):
        p = x.astype(BF16)
        parts.append(p)
        x = x - p.astype(F32)
    return parts


def _chunk_cumsum(x, chunk):
    rows = x.shape[0]
    assert chunk & (chunk - 1) == 0
    ri = lax.broadcasted_iota(jnp.int32, (chunk, CUMSUM_TERMS * chunk), 0)
    ci = lax.broadcasted_iota(jnp.int32, (chunk, CUMSUM_TERMS * chunk), 1)
    tri = jnp.where((ci & (chunk - 1)) <= ri, 1.0, 0.0).astype(BF16)
    out = []
    for r0 in range(0, rows, chunk):
        stacked = jnp.concatenate(_split_bf16(x[r0:r0 + chunk, :], CUMSUM_TERMS), axis=0)
        out.append(jnp.dot(tri, stacked, preferred_element_type=F32))
    return out[0] if len(out) == 1 else jnp.concatenate(out, axis=0)


def _resident(shape):
    return pl.BlockSpec(shape, lambda *_: (0,) * len(shape), pipeline_mode=pl.Buffered(1))


def _norm_proj_body(x_ref, nw_ref, w_ref, *refs, offsets, n_extra, n_side, defer=False, realign=True):
    n_out = len(offsets) + n_extra
    extra_refs, side_in = refs[:n_extra], refs[n_extra:n_extra + n_side]
    o_refs = refs[n_extra + n_side:n_extra + n_side + n_out]
    side_out = refs[n_extra + n_side + n_out:n_extra + 2 * n_side + n_out]
    for s_in, s_out in zip(side_in, side_out):
        s_out[...] = s_in[...].astype(s_out.dtype)
    shifted = [i for i, off in enumerate(offsets) if off % LANES]
    starts, pos = {}, 0
    for i in shifted:
        starts[i] = pos
        pos += o_refs[i].shape[1]

    if shifted:
        al_ref = refs[-1]

        @pl.when(jnp.logical_and(pl.program_id(0) == 0, realign))
        def _():
            for i in shifted:
                width = o_refs[i].shape[1]
                al_ref[:, starts[i]:starts[i] + width] = w_ref[:, offsets[i]:offsets[i] + width]

    h = _rms(x_ref[...], nw_ref[...]).astype(BF16)
    jobs = []

    def seg_job(i, off, c0, c1):
        def run():
            w = al_ref[:, starts[i] + c0:starts[i] + c1] if i in starts else w_ref[:, off + c0:off + c1]
            o_refs[i][:, c0:c1] = jnp.dot(h, w, preferred_element_type=F32).astype(o_refs[i].dtype)
        return run

    def extra_job(e_ref, o_ref):
        def run():
            o_ref[...] = jnp.dot(h, e_ref[...], preferred_element_type=F32).astype(o_ref.dtype)
        return run

    for i, off in enumerate(offsets):
        width = o_refs[i].shape[1]
        jobs += [seg_job(i, off, c0, min(c0 + JOB_COLS, width)) for c0 in range(0, width, JOB_COLS)]
    jobs += [extra_job(e, o) for e, o in zip(extra_refs, o_refs[len(offsets):])]
    if defer:
        return jobs
    for job in jobs:
        job()


def _norm_proj(x2d, norm_w, weight, segments, out_dtypes, extras=(), side_casts=()):
    m, d = x2d.shape
    tm = IN_PROJ_ROWS
    steps = m // tm
    assert all(w % LANES == 0 for _, w in segments)
    assert all(s.shape[0] % (steps * BF16_SUBLANES) == 0 for s in side_casts)
    widths = [w for _, w in segments] + [e.shape[1] for e in extras]
    shifted_cols = sum(w for off, w in segments if off % LANES)
    side_specs = [pl.BlockSpec((s.shape[0] // steps, s.shape[1]), lambda i: (i, 0)) for s in side_casts]
    return pl.pallas_call(
        functools.partial(_norm_proj_body, offsets=tuple(off for off, _ in segments), n_extra=len(extras),
                          n_side=len(side_casts)),
        name="norm_proj",
        grid=(steps,),
        in_specs=[pl.BlockSpec((tm, d), lambda i: (i, 0)), _resident((1, d)), _resident(weight.shape)]
        + [_resident(e.shape) for e in extras] + side_specs,
        out_specs=[pl.BlockSpec((tm, w), lambda i: (i, 0)) for w in widths] + side_specs,
        out_shape=[jax.ShapeDtypeStruct((m, w), dt) for w, dt in zip(widths, out_dtypes)]
        + [jax.ShapeDtypeStruct(s.shape, BF16) for s in side_casts],
        scratch_shapes=[pltpu.VMEM((d, shifted_cols), weight.dtype)] if shifted_cols else [],
        compiler_params=_params("arbitrary"),
    )(x2d, norm_w.reshape(1, d), weight, *extras, *side_casts)


def _ssd_body(xbc_ref, sm_ref, z_ref, cw_ref, cb_ref, alog_ref, dtb_ref, dskip_ref, nw_ref,
              y_ref, h_ref, xpad_ref, act_ref, xw_ref, ycat_ref, side=None, first=None):
    q = SSD_CHUNK
    t = xbc_ref.shape[1]
    width = y_ref.shape[-1]
    n_pairs = width // LANES
    n_slabs = xpad_ref.shape[0]
    gstate = SSD_GROUPS * SSD_STATE
    gwidth = width // SSD_GROUPS
    if first is None:
        first = pl.program_id(1) == 0

    @pl.when(first)
    def _():
        xpad_ref[:, 0:SUBLANES, :] = jnp.zeros((n_slabs, SUBLANES, LANES), F32)
        h_ref[...] = jnp.zeros_like(h_ref)

    @pl.when(jnp.logical_not(first))
    def _():
        xpad_ref[:, 0:SUBLANES, :] = xpad_ref[:, t:t + SUBLANES, :]

    for c in range(n_slabs):
        lanes = slice(c * LANES, (c + 1) * LANES)
        xpad_ref[c, SUBLANES:SUBLANES + t, :] = xbc_ref[0, :, lanes]
        taps = [jnp.broadcast_to(cw_ref[k:k + 1, lanes], (SUBLANES, LANES)) for k in range(SSD_CONV)]
        bias = jnp.broadcast_to(cb_ref[:, lanes], (SUBLANES, LANES))
        for t0 in range(0, t, SUBLANES * CONV_ROW_STRIDE):
            for g in range(CONV_ROW_STRIDE):
                conv = bias
                for k in range(SSD_CONV):
                    start = SUBLANES + t0 + g - (SSD_CONV - 1 - k)
                    conv = conv + taps[k] * xpad_ref[c, pl.ds(start, SUBLANES, stride=CONV_ROW_STRIDE), :]
                act_ref[c, pl.ds(t0 + g, SUBLANES, stride=CONV_ROW_STRIDE), :] = _silu(conv)

    dt_all = _softplus(sm_ref[0] + dtb_ref[...])
    dta = dt_all * (-jnp.exp(alog_ref[...]) * LOG2_E)
    a_cs_all = _chunk_cumsum(dta, q)
    ri = lax.broadcasted_iota(jnp.int32, (q, q), 0)
    ci = lax.broadcasted_iota(jnp.int32, (q, q), 1)
    causal = ri >= ci
    lane = lax.broadcasted_iota(jnp.int32, (q, LANES), 1)
    lo = lane < HALF
    lo_row = lo[0:1, :]

    h_prev = h_ref[...]
    for rows in [slice(r0, r0 + q) for r0 in range(0, t, q)]:
        if side:
            side.pop(0)()
        bm = act_ref[n_pairs, rows, :]
        cm = act_ref[n_pairs + 1, rows, :]
        dt = dt_all[rows, :]
        a_cs = a_cs_all[rows, :]
        a_cs_t = a_cs.T
        a_last = a_cs[q - 1:q, :]
        bm_t = bm.T
        cm_g = [jnp.where(lo, cm, 0.0), jnp.where(lo, 0.0, cm)]
        cb = [_dot_nt(c, bm) for c in cm_g]
        y_off = [_dot(c, h_prev) for c in cm_g]

        a_last_pairs = []
        for p in range(n_pairs):
            if side and p % 2 == 1:
                side.pop(0)()
            e0, e1 = 2 * p, 2 * p + 1
            g = (p * LANES) // gwidth
            col = slice(p * LANES, (p + 1) * LANES)
            gcol = slice(p * LANES - g * gwidth, (p + 1) * LANES - g * gwidth)
            acs_pair = jnp.where(lo, a_cs[:, e0:e0 + 1], a_cs[:, e1:e1 + 1])
            dt_pair = jnp.where(lo, dt[:, e0:e0 + 1], dt[:, e1:e1 + 1])
            al_pair = jnp.where(lo_row, a_last[:, e0:e0 + 1], a_last[:, e1:e1 + 1])
            a_last_pairs.append(al_pair)
            xs2 = act_ref[p, rows, :]
            xdt = xs2 * dt_pair
            xdt_b = xdt.astype(BF16)
            yd = []
            for e in (e0, e1):
                seg = a_cs[:, e:e + 1] - a_cs_t[e:e + 1, :]
                decay = jnp.exp2(jnp.where(causal, seg, -jnp.inf))
                yd.append(_dot(cb[g] * decay, xdt_b))
            y2 = jnp.where(lo, yd[0], yd[1])
            y2 = y2 + y_off[g][:, gcol] * jnp.exp2(acs_pair) + dskip_ref[:, col] * xs2
            ycat_ref[rows, col] = y2
            xw_ref[rows, col] = (xdt * jnp.exp2(al_pair - acs_pair)).astype(BF16)

        h_next = []
        for g in range(SSD_GROUPS):
            srows = slice(g * SSD_STATE, (g + 1) * SSD_STATE)
            ppg = n_pairs // SSD_GROUPS
            dec = jnp.exp2(jnp.concatenate(a_last_pairs[g * ppg:(g + 1) * ppg], axis=1))
            s_g = _dot(bm_t[srows, :], xw_ref[rows, g * gwidth:(g + 1) * gwidth])
            h_next.append(h_prev[srows, :] * dec + s_g)
        h_prev = jnp.concatenate(h_next, axis=0)
        y_ref[0, rows, :] = _rms(ycat_ref[rows, :] * _silu(z_ref[0, rows, :]), nw_ref[...]).astype(y_ref.dtype)
    h_ref[...] = h_prev


def _gla_body(q_ref, k_ref, v_ref, sm_ref, g_ref, w2_ref, gb_ref, nw_ref, o_ref, st_ref, *, mid_lane, side=None, first=None):
    c = GLA_CHUNK
    kw = q_ref.shape[-1]
    dk = kw // GLA_HEADS
    dv = v_ref.shape[-1] // GLA_HEADS
    n_pairs = kw // LANES

    if first is None:
        first = pl.program_id(1) == 0

    @pl.when(first)
    def _():
        st_ref[...] = jnp.zeros_like(st_ref)

    ri = lax.broadcasted_iota(jnp.int32, (c, c), 0)
    ci = lax.broadcasted_iota(jnp.int32, (c, c), 1)
    causal = ri >= ci
    lo = lax.broadcasted_iota(jnp.int32, (c, LANES), 1) < HALF
    lo_sq = lax.broadcasted_iota(jnp.int32, (dv, LANES), 1) < HALF

    sm_hi, sm_mid = _split_bf16(sm_ref[0], 2)
    sm_lane = lax.broadcasted_iota(jnp.int32, sm_hi.shape, 1)
    pre = jnp.dot(jnp.where(sm_lane < mid_lane, sm_hi, sm_mid), w2_ref[...],
                  preferred_element_type=F32) + gb_ref[...]
    gcs_all = _chunk_cumsum(_log_sigmoid(pre) * (LOG2_E / GLA_GATE_NORMALIZER), c)

    chunks = [slice(ch * c, (ch + 1) * c) for ch in range(q_ref.shape[1] // c)]
    pairs = [slice(p * LANES, (p + 1) * LANES) for p in range(n_pairs)]
    causal2 = jnp.concatenate([causal, causal], axis=0)

    def by_head(x):
        return jnp.concatenate([jnp.where(lo, x, 0.0), jnp.where(lo, 0.0, x)], axis=0).astype(BF16)

    q_st, k_st, g_last, scores = {}, {}, {}, {}

    def front(ch):
        rows = chunks[ch]
        gcs = gcs_all[rows, :]
        g_mid = gcs[c // 2:c // 2 + 1, :]
        g_last[ch] = gcs[c - 1:c, :]
        qs = q_ref[0, rows, :] * (dk ** -0.5)
        ks = k_ref[0, rows, :]
        q_mid = qs * jnp.exp2(gcs - g_mid)
        k_mid = ks * jnp.exp2(g_mid - gcs)
        k_in = k_mid.astype(BF16)
        q_st[ch] = q_mid * jnp.exp2(g_mid)
        k_st[ch] = (k_mid * jnp.exp2(g_last[ch] - g_mid)).astype(BF16)
        scores[ch] = [jnp.where(causal2, _dot_nt(by_head(q_mid[:, col]), k_in[:, col]), 0.0).astype(BF16)
                      for col in pairs]

    def back(ch, states):
        rows = chunks[ch]
        for p, col in enumerate(pairs):
            if side and p % 2 == 1:
                side.pop(0)()
            v_pair = [v_ref[0, rows, (2 * p + half) * dv:(2 * p + half + 1) * dv].astype(BF16) for half in range(2)]
            o_inter = _dot_nt(by_head(q_st[ch][:, col]), states[p])
            kv_t = jnp.where(lo_sq, _dot_tn(v_pair[0], k_st[ch][:, col]), _dot_tn(v_pair[1], k_st[ch][:, col]))
            states[p] = states[p] * jnp.exp2(g_last[ch][:, col]) + kv_t
            for half in range(2):
                vcol = slice((2 * p + half) * dv, (2 * p + half + 1) * dv)
                o = (jnp.dot(scores[ch][p][half * c:(half + 1) * c, :], v_pair[half], preferred_element_type=F32)
                     + o_inter[half * c:(half + 1) * c, :])
                o_ref[0, rows, vcol] = (_rms(o, nw_ref[...]) * _silu(g_ref[0, rows, vcol])).astype(o_ref.dtype)
        del q_st[ch], k_st[ch], g_last[ch], scores[ch]

    states = [st_ref[:, col] for col in pairs]
    front(0)
    for ch in range(len(chunks)):
        if ch + 1 < len(chunks):
            front(ch + 1)
        if side:
            side.pop(0)()
        back(ch, states)
    for p, col in enumerate(pairs):
        st_ref[:, col] = states[p]


def _even_pipe_body(*refs, n_side, offsets, mid_lane, blocks_per_seq):
    it = iter(refs)
    take = lambda n: [next(it) for _ in range(n)]
    x0_ref, xa_ref, xb_ref, nw_ref, w_ref, wsm_ref = take(6)
    side_in = take(n_side)
    ssd_par = take(6)
    gla_par = take(3)
    y_ref, o_ref = take(2)
    side_out = take(n_side)
    (al_ref,) = take(1)
    u_refs = take(7)
    ssd_scr = take(5)
    gla_scr = take(1)
    z_s, xbc_s, q_s, k_s, v_s, g_s, sm_s = u_refs
    t = pl.program_id(0)
    rows_blk = xa_ref.shape[0]

    for s_in, s_out in zip(side_in, side_out):
        s_out[...] = s_in[...].astype(s_out.dtype)

    def project(x_ref, slot, defer, realign=False):
        outs = [u.at[slot] for u in u_refs]
        return _norm_proj_body(x_ref, nw_ref, w_ref, wsm_ref, *outs, al_ref, offsets=offsets, n_extra=1,
                               n_side=0, defer=defer, realign=realign)

    @pl.when(t == 0)
    def _():
        project(x0_ref, 0, defer=False, realign=True)

    for half, (x_ref, wslot) in enumerate(((xa_ref, 1), (xb_ref, 0))):
        rslot = 1 - wslot
        first = (2 * t + half) % blocks_per_seq == 0
        jobs = project(x_ref, wslot, defer=True)
        rd = lambda u: u.at[rslot:rslot + 1]
        out_rows = pl.ds(half * rows_blk, rows_blk)
        _ssd_body(rd(xbc_s), rd(sm_s), rd(z_s), *ssd_par, y_ref.at[:, out_rows], *ssd_scr, side=jobs, first=first)
        _gla_body(rd(q_s), rd(k_s), rd(v_s), rd(sm_s), rd(g_s), *gla_par, o_ref.at[:, out_rows], *gla_scr,
                  mid_lane=mid_lane, side=jobs, first=first)
        for job in jobs:
            job()


def _even_pipe(x, norm_w, w_in_bf, w_small, segments, conv_w, conv_b, a_log, dt_bias, d_skip, ssd_norm_w,
               gate_w2, gate_b, gla_norm_w, lr_lane, side_casts):
    b, l, d = x.shape
    t = EVEN_PIPE_ROWS
    width = ssd_norm_w.shape[0]
    cdim = conv_w.shape[1]
    n_heads = a_log.shape[0]
    rank, kw = gate_w2.shape
    vw = gla_norm_w.shape[0] * GLA_HEADS
    bps = l // t
    n = b * bps
    steps = n // 2
    assert bps % 2 == 0
    pad = lambda v: jnp.pad(v, (0, LANES - n_heads)).reshape(1, LANES)
    w2_hi, w2_mid = _split_bf16(gate_w2, 2)
    w2_pad = jnp.zeros((LANES, kw), BF16)
    for copy, part in enumerate((w2_hi, w2_mid, w2_hi)):
        w2_pad = w2_pad.at[lr_lane + copy * rank:lr_lane + (copy + 1) * rank, :].set(part)
    assert all(sc.shape[0] % (steps * BF16_SUBLANES) == 0 for sc in side_casts)
    shifted_cols = sum(w for off, w in segments if off % LANES)
    x_blk = lambda off: pl.BlockSpec((t, d), lambda s: (jnp.minimum(2 * s + off, n - 1), 0))
    side_specs = [pl.BlockSpec((sc.shape[0] // steps, sc.shape[1]), lambda s: (s, 0)) for sc in side_casts]
    scan_blk = lambda w: pl.BlockSpec((1, 2 * t, w), lambda s: (s // (bps // 2), s % (bps // 2), 0))
    widths = [w for _, w in segments] + [LANES]
    dtypes = [F32, F32, F32, F32, BF16, F32, F32]
    gwidth = width // SSD_GROUPS
    x2d = x.reshape(b * l, d)
    outs = pl.pallas_call(
        functools.partial(_even_pipe_body, n_side=len(side_casts), offsets=tuple(off for off, _ in segments),
                          mid_lane=lr_lane + 2 * rank, blocks_per_seq=bps),
        name="even_pipe",
        grid=(steps,),
        in_specs=[pl.BlockSpec((t, d), lambda s: (0, 0)), x_blk(1), x_blk(2), _resident((1, d)),
                  _resident(w_in_bf.shape), _resident(w_small.shape)] + side_specs
        + [_resident((SSD_CONV, cdim)), _resident((1, cdim)), _resident((1, LANES)), _resident((1, LANES)),
           _resident((1, width)), _resident((1, width)),
           _resident((LANES, kw)), _resident((1, kw)), _resident((1, vw // GLA_HEADS))],
        out_specs=[scan_blk(width), scan_blk(vw)] + side_specs,
        out_shape=[jax.ShapeDtypeStruct((b, l, width), BF16), jax.ShapeDtypeStruct((b, l, vw), BF16)]
        + [jax.ShapeDtypeStruct(sc.shape, BF16) for sc in side_casts],
        scratch_shapes=[pltpu.VMEM((d, shifted_cols), BF16)]
        + [pltpu.VMEM((2, t, w), dt) for w, dt in zip(widths, dtypes)]
        + [pltpu.VMEM((SSD_GROUPS * SSD_STATE, gwidth), F32),
           pltpu.VMEM((cdim // LANES, t + SUBLANES, LANES), F32),
           pltpu.VMEM((cdim // LANES, t, LANES), F32),
           pltpu.VMEM((t, width), BF16),
           pltpu.VMEM((t, width), F32),
           pltpu.VMEM((vw // GLA_HEADS, kw), F32)],
        compiler_params=_params("arbitrary"),
    )(x2d, x2d, x2d, norm_w.reshape(1, d), w_in_bf, w_small, *side_casts,
      conv_w, conv_b.reshape(1, cdim), pad(a_log), pad(dt_bias), jnp.repeat(d_skip, SSD_HEAD_DIM).reshape(1, width),
      ssd_norm_w.reshape(1, width), w2_pad, gate_b.reshape(1, kw), gla_norm_w.reshape(1, -1))
    return outs[0], outs[1], outs[2:]


def _out_proj_body(*refs, n_in, final_norm, next_widths):
    a_refs, x_ref, w_ref = refs[:n_in], refs[n_in], refs[n_in + 1]
    rest = refs[n_in + 2:]
    acc = x_ref[...]
    off = 0
    for a_ref in a_refs:
        width = a_ref.shape[1]
        acc = acc + jnp.dot(a_ref[...].astype(BF16), w_ref[off:off + width, :], preferred_element_type=F32)
        off += width
    if final_norm:
        fw_ref, o_ref = rest
        o_ref[...] = _rms(acc, fw_ref[...])
        return
    if not next_widths:
        (o_ref,) = rest
        o_ref[...] = acc
        return
    nw_ref, w2_ref, o_ref = rest[:3]
    o_ref[...] = acc
    h = _rms(acc, nw_ref[...]).astype(BF16)
    off = 0
    for p_ref, width in zip(rest[3:], next_widths):
        p_ref[...] = jnp.dot(h, w2_ref[:, off:off + width], preferred_element_type=F32).astype(p_ref.dtype)
        off += width


def _out_proj(acts, x2d, weight, final_w=None, next_proj=None):
    m, d = x2d.shape
    tm = OUT_PROJ_ROWS if next_proj is None else IN_PROJ_ROWS
    n_in = len(acts)
    assert sum(a.shape[1] for a in acts) == weight.shape[0]
    row = lambda w: pl.BlockSpec((tm, w), lambda i: (i, 0))
    in_specs = [row(a.shape[1]) for a in acts] + [row(d), _resident(weight.shape)]
    args = list(acts) + [x2d, weight]
    out_specs, out_shape, widths = row(d), jax.ShapeDtypeStruct((m, d), F32), ()
    if final_w is not None:
        in_specs.append(_resident((1, d)))
        args.append(final_w.reshape(1, d))
    elif next_proj is not None:
        norm_w, w2, widths, dtypes = next_proj
        assert sum(widths) == w2.shape[1] and all(w % LANES == 0 for w in widths)
        in_specs += [_resident((1, d)), _resident(w2.shape)]
        args += [norm_w.reshape(1, d), w2]
        out_specs = [out_specs] + [row(w) for w in widths]
        out_shape = [out_shape] + [jax.ShapeDtypeStruct((m, w), dt) for w, dt in zip(widths, dtypes)]
    outs = pl.pallas_call(
        functools.partial(_out_proj_body, n_in=n_in, final_norm=final_w is not None, next_widths=tuple(widths)),
        name="out_proj",
        grid=(m // tm,),
        in_specs=in_specs,
        out_specs=out_specs,
        out_shape=out_shape,
        compiler_params=_params("parallel"),
    )(*args)
    return outs if next_proj is None else (outs[0], outs[1:])


class _MobaQueryBlock:
    def __init__(self, n_past, q_ref, z_ref, o_ref, kb_ref, vt_ref, kmean_ref, s_ref):
        blk = MOBA_BLOCK
        self.n_past, self.z_ref, self.o_ref, self.kb_ref, self.vt_ref = n_past, z_ref, o_ref, kb_ref, vt_ref
        self.own = slice(n_past * blk, (n_past + 1) * blk)
        self.s_ref = s_ref.at[n_past % s_ref.shape[0]]
        q2 = q_ref[0, self.own, :]
        lo = lax.broadcasted_iota(jnp.int32, (blk, LANES), 1) < HALF
        q_cat = jnp.concatenate([jnp.where(lo, q2, 0.0), jnp.where(lo, 0.0, q2)], axis=0)
        scale = MOBA_HEAD_DIM ** -0.5
        key_i = lax.broadcasted_iota(jnp.int32, (blk, 2 * blk), 0)
        qry_i = lax.broadcasted_iota(jnp.int32, (blk, 2 * blk), 1)
        causal_t = key_i <= jnp.where(qry_i < blk, qry_i, qry_i - blk)

        self.masks = []
        if n_past:
            nb = kmean_ref.shape[0] // 2
            parts = [_dot_nt(kmean_ref[...], q_part) for q_part in _split_bf16(q_cat * scale, 2)]
            gate = (parts[0][0:nb] + parts[0][nb:]) + (parts[1][0:nb] + parts[1][nb:])
            g_rows = [gate[n:n + 1, :] for n in range(n_past)]
            for n in range(n_past):
                rank = jnp.zeros((1, 2 * blk), F32)
                for m in range(n_past):
                    if m != n:
                        ahead = (g_rows[m] >= g_rows[n]) if m < n else (g_rows[m] > g_rows[n])
                        rank = rank + jnp.where(ahead, 1.0, 0.0)
                self.masks.append(rank < MOBA_TOPK)
        self.masks.append(causal_t)
        self.q_s = (q_cat * (scale * LOG2_E)).astype(BF16)
        self.m_run = None
        self.p = []

    def logits_step(self, j):
        rows = slice(j * MOBA_BLOCK, (j + 1) * MOBA_BLOCK)
        sj = jnp.where(self.masks[j], _dot_nt(self.kb_ref[rows, :], self.q_s), -jnp.inf)
        self.s_ref[rows, :] = sj
        mj = jnp.max(sj, axis=0, keepdims=True)
        self.m_run = mj if self.m_run is None else jnp.maximum(self.m_run, mj)

    def value_step(self, j):
        rows = slice(j * MOBA_BLOCK, (j + 1) * MOBA_BLOCK)
        self.p.append(jnp.exp2(self.s_ref[rows, :] - self.m_run).astype(BF16))

    def finish(self):
        blk = MOBA_BLOCK
        nk = len(self.p) * blk
        p_all = jnp.concatenate(self.p, axis=0)
        outs = []
        for half in range(2):
            vt = self.vt_ref[half * VT_ROWS:(half + 1) * VT_ROWS, 0:nk]
            acc = jnp.dot(vt, p_all[:, half * blk:(half + 1) * blk], preferred_element_type=F32)
            outs.append(acc[0:HALF, :] / acc[HALF:HALF + 1, :])
        o_t = jnp.concatenate(outs, axis=0)
        self.o_ref[0, self.own, :] = (o_t.T * _silu(self.z_ref[0, self.own, :])).astype(self.o_ref.dtype)


def _moba_body(q_ref, k_ref, v_ref, z_ref, o_ref, vt_ref, kmean_ref, s_ref):
    blk = MOBA_BLOCK
    nb = k_ref.shape[1] // blk
    kmean = []
    for n in range(nb):
        rows = slice(n * blk, (n + 1) * blk)
        kmean.append(jnp.mean(k_ref[0, rows, :].astype(F32), axis=0, keepdims=True))
        v_t = v_ref[0, rows, :].astype(F32).T.astype(BF16)
        for half in range(2):
            vt_ref[half * VT_ROWS:half * VT_ROWS + HALF, rows] = v_t[half * HALF:(half + 1) * HALF, :]
    for half in range(2):
        vt_ref[half * VT_ROWS + HALF:(half + 1) * VT_ROWS, :] = jnp.ones((BF16_SUBLANES, vt_ref.shape[1]), BF16)
    kmean_ref[...] = jnp.concatenate(_split_bf16(jnp.concatenate(kmean, axis=0), 2), axis=0)
    make = functools.partial(_MobaQueryBlock, q_ref=q_ref, z_ref=z_ref, o_ref=o_ref, kb_ref=k_ref.at[0],
                             vt_ref=vt_ref, kmean_ref=kmean_ref, s_ref=s_ref)
    blocks = {0: make(0), 1: make(1)}
    blocks[0].logits_step(0)
    for i in range(nb):
        if i + 2 < nb:
            blocks[i + 2] = make(i + 2)
        cur, nxt = blocks.pop(i), blocks.get(i + 1)
        for j in range(i + 2):
            if j <= i:
                cur.value_step(j)
            if nxt is not None:
                nxt.logits_step(j)
        cur.finish()


def _moba(q, k, v, z):
    b, l, w = q.shape
    blk = MOBA_BLOCK
    nb = l // blk
    spec = pl.BlockSpec((1, l, LANES), lambda bi, hp: (bi, 0, hp))
    return pl.pallas_call(
        _moba_body,
        name="moba",
        grid=(b, w // LANES),
        in_specs=[spec, spec, spec, spec],
        out_specs=spec,
        out_shape=jax.ShapeDtypeStruct((b, l, w), BF16),
        scratch_shapes=[pltpu.VMEM((2 * VT_ROWS, l), BF16),
                        pltpu.VMEM((2 * nb, LANES), BF16),
                        pltpu.VMEM((2, l, 2 * blk), F32)],
        compiler_params=_params("parallel", "parallel"),
    )(q, k, v, z)


def _even_layer(x, norm_w, w_in_bf, conv_w, conv_b, a_log, dt_bias, d_skip, ssd_norm_w, gate_w2, gate_b,
                gla_norm_w, side_casts=()):
    b, l, d = x.shape
    width = ssd_norm_w.shape[0]
    cdim = conv_w.shape[1]
    n_heads = a_log.shape[0]
    rank, kw = gate_w2.shape
    vw = gla_norm_w.shape[0] * GLA_HEADS
    cuts = [0]
    for s in (width, cdim, n_heads, kw, kw, vw, rank, vw):
        cuts.append(cuts[-1] + s)
    seg = lambda j: w_in_bf[:, cuts[j]:cuts[j + 1]]
    pad = jnp.zeros((d, LANES - n_heads - GATE_COPIES * rank), BF16)
    w_small = jnp.concatenate([seg(2)] + [seg(6)] * GATE_COPIES + [pad], axis=1)
    segments = [(cuts[j], cuts[j + 1] - cuts[j]) for j in (0, 1, 3, 4, 5, 7)]
    y_a, o_b, cast = _even_pipe(x, norm_w, w_in_bf, w_small, segments, conv_w, conv_b, a_log, dt_bias, d_skip,
                                ssd_norm_w, gate_w2, gate_b, gla_norm_w, n_heads, side_casts)
    return [y_a.reshape(b * l, width), o_b.reshape(b * l, vw)], cast


ODD_PROJ_DTYPES = (F32, BF16, BF16, F32)


def _odd_layer(x, norm_w, w_in_bf, side_casts=(), projected=None):
    b, l, d = x.shape
    w = w_in_bf.shape[1] // 4
    outs = projected
    if outs is None:
        outs = _norm_proj(x.reshape(b * l, d), norm_w, w_in_bf, [(j * w, w) for j in range(4)],
                          list(ODD_PROJ_DTYPES), side_casts=side_casts)
    q, k, v, z = [u.reshape(b, l, w) for u in outs[:4]]
    return [_moba(q, k, v, z).reshape(b * l, w)], outs[4:]


def kernel(x, even_norm, even_w_in, even_conv_w, even_conv_b, even_a_log, even_dt_bias, even_d_skip, even_ssd_norm,
           even_gate_w2, even_gate_b, even_gla_norm, even_w_out, odd_norm, odd_w_in, odd_w_out, final_norm):
    b, l, d = x.shape
    depth = even_norm.shape[0] + odd_norm.shape[0]
    w_in = lambda layer: (even_w_in if layer % 2 == 0 else odd_w_in)[layer // 2]
    w_out = lambda layer: (even_w_out if layer % 2 == 0 else odd_w_out)[layer // 2]
    keys = [("out", 0)] + [(kind, layer) for layer in range(1, depth) for kind in ("in", "out")]
    pending = [w_in(layer) if kind == "in" else w_out(layer) for kind, layer in keys]
    bf = {("in", 0): w_in(0).astype(BF16)}
    projected = None
    for layer in range(depth):
        i = layer // 2
        side = pending if layer == 0 else ()
        if layer % 2 == 0:
            acts, cast = _even_layer(x, even_norm[i], bf["in", layer], even_conv_w[i], even_conv_b[i], even_a_log[i],
                                     even_dt_bias[i], even_d_skip[i], even_ssd_norm[i], even_gate_w2[i],
                                     even_gate_b[i], even_gla_norm[i], side_casts=side)
        else:
            acts, cast = _odd_layer(x, odd_norm[i], bf["in", layer], side_casts=side, projected=projected)
        bf.update(zip(keys, cast))
        projected = None
        x2d = x.reshape(b * l, d)
        if layer + 1 < depth and (layer + 1) % 2 == 1:
            nxt = layer + 1
            w2 = bf["in", nxt]
            x2d, projected = _out_proj(acts, x2d, bf["out", layer],
                                       next_proj=(odd_norm[nxt // 2], w2, [w2.shape[1] // 4] * 4, ODD_PROJ_DTYPES))
        else:
            x2d = _out_proj(acts, x2d, bf["out", layer], final_norm if layer == depth - 1 else None)
        x = x2d.reshape(b, l, d)
    return x
```

```python
import functools

import jax
import jax.numpy as jnp
from jax import lax
from jax.experimental import pallas as pl
from jax.experimental.pallas import tpu as pltpu

F32 = jnp.float32
BF16 = jnp.bfloat16

LANES = 128
SUBLANES = 8
BF16_SUBLANES = 16
VMEM_LIMIT_BYTES = 56 * 1024 * 1024

RMS_EPS = 1e-6
SSD_HEAD_DIM = 64
SSD_GROUPS = 2
SSD_STATE = 64
SSD_CONV = 4
SSD_CHUNK = 128
GLA_HEADS = 8
GLA_GATE_NORMALIZER = 16.0
GLA_CHUNK = 64
MOBA_HEAD_DIM = 64
MOBA_BLOCK = 256
MOBA_TOPK = 3
LOG2_E = 1.4426950408889634
CONV_ROW_STRIDE = 4
CUMSUM_TERMS = 3
GATE_COPIES = 3

IN_PROJ_ROWS = 512
OUT_PROJ_ROWS = 1024
EVEN_PIPE_ROWS = 256
JOB_COLS = 512
HALF = LANES // 2
VT_ROWS = HALF + BF16_SUBLANES


def _params(*sem):
    return pltpu.CompilerParams(dimension_semantics=sem, vmem_limit_bytes=VMEM_LIMIT_BYTES)


def _rms(x, w):
    return x * lax.rsqrt(jnp.mean(x * x, axis=-1, keepdims=True) + RMS_EPS) * w


def _silu(x):
    h = 0.5 * x
    return h + h * jnp.tanh(h)


def _softplus(x):
    return jnp.maximum(x, 0.0) + jnp.log1p(jnp.exp(-jnp.abs(x)))


def _log_sigmoid(x):
    return jnp.minimum(x, 0.0) - jnp.log(1.0 + jnp.exp(-jnp.abs(x)))


def _dot(a, b):
    return jnp.dot(a.astype(BF16), b.astype(BF16), preferred_element_type=F32)


def _dot_nt(a, b):
    return lax.dot_general(a.astype(BF16), b.astype(BF16), (((1,), (1,)), ((), ())), preferred_element_type=F32)


def _dot_tn(a, b):
    return lax.dot_general(a.astype(BF16), b.astype(BF16), (((0,), (0,)), ((), ())), preferred_element_type=F32)


def _split_bf16(x, terms):
    parts = []
    for _ in range(terms):
        p = x.astype(BF16)
        parts.append(p)
        x = x - p.astype(F32)
    return parts


def _chunk_cumsum(x, chunk):
    rows = x.shape[0]
    assert chunk & (chunk - 1) == 0
    ri = lax.broadcasted_iota(jnp.int32, (chunk, CUMSUM_TERMS * chunk), 0)
    ci = lax.broadcasted_iota(jnp.int32, (chunk, CUMSUM_TERMS * chunk), 1)
    tri = jnp.where((ci & (chunk - 1)) <= ri, 1.0, 0.0).astype(BF16)
    out = []
    for r0 in range(0, rows, chunk):
        stacked = jnp.concatenate(_split_bf16(x[r0:r0 + chunk, :], CUMSUM_TERMS), axis=0)
        out.append(jnp.dot(tri, stacked, preferred_element_type=F32))
    return out[0] if len(out) == 1 else jnp.concatenate(out, axis=0)


def _resident(shape):
    return pl.BlockSpec(shape, lambda *_: (0,) * len(shape), pipeline_mode=pl.Buffered(1))


def _norm_proj_body(x_ref, nw_ref, w_ref, *refs, offsets, n_extra, n_side, defer=False, realign=True):
    n_out = len(offsets) + n_extra
    extra_refs, side_in = refs[:n_extra], refs[n_extra:n_extra + n_side]
    o_refs = refs[n_extra + n_side:n_extra + n_side + n_out]
    side_out = refs[n_extra + n_side + n_out:n_extra + 2 * n_side + n_out]
    for s_in, s_out in zip(side_in, side_out):
        s_out[...] = s_in[...].astype(s_out.dtype)
    shifted = [i for i, off in enumerate(offsets) if off % LANES]
    starts, pos = {}, 0
    for i in shifted:
        starts[i] = pos
        pos += o_refs[i].shape[1]

    if shifted:
        al_ref = refs[-1]

        @pl.when(jnp.logical_and(pl.program_id(0) == 0, realign))
        def _():
            for i in shifted:
                width = o_refs[i].shape[1]
                al_ref[:, starts[i]:starts[i] + width] = w_ref[:, offsets[i]:offsets[i] + width]

    h = _rms(x_ref[...], nw_ref[...]).astype(BF16)
    jobs = []

    def seg_job(i, off, c0, c1):
        def run():
            w = al_ref[:, starts[i] + c0:starts[i] + c1] if i in starts else w_ref[:, off + c0:off + c1]
            o_refs[i][:, c0:c1] = jnp.dot(h, w, preferred_element_type=F32).astype(o_refs[i].dtype)
        return run

    def extra_job(e_ref, o_ref):
        def run():
            o_ref[...] = jnp.dot(h, e_ref[...], preferred_element_type=F32).astype(o_ref.dtype)
        return run

    for i, off in enumerate(offsets):
        width = o_refs[i].shape[1]
        jobs += [seg_job(i, off, c0, min(c0 + JOB_COLS, width)) for c0 in range(0, width, JOB_COLS)]
    jobs += [extra_job(e, o) for e, o in zip(extra_refs, o_refs[len(offsets):])]
    if defer:
        return jobs
    for job in jobs:
        job()


def _norm_proj(x2d, norm_w, weight, segments, out_dtypes, extras=(), side_casts=()):
    m, d = x2d.shape
    tm = IN_PROJ_ROWS
    steps = m // tm
    assert all(w % LANES == 0 for _, w in segments)
    assert all(s.shape[0] % (steps * BF16_SUBLANES) == 0 for s in side_casts)
    widths = [w for _, w in segments] + [e.shape[1] for e in extras]
    shifted_cols = sum(w for off, w in segments if off % LANES)
    side_specs = [pl.BlockSpec((s.shape[0] // steps, s.shape[1]), lambda i: (i, 0)) for s in side_casts]
    return pl.pallas_call(
        functools.partial(_norm_proj_body, offsets=tuple(off for off, _ in segments), n_extra=len(extras),
                          n_side=len(side_casts)),
        name="norm_proj",
        grid=(steps,),
        in_specs=[pl.BlockSpec((tm, d), lambda i: (i, 0)), _resident((1, d)), _resident(weight.shape)]
        + [_resident(e.shape) for e in extras] + side_specs,
        out_specs=[pl.BlockSpec((tm, w), lambda i: (i, 0)) for w in widths] + side_specs,
        out_shape=[jax.ShapeDtypeStruct((m, w), dt) for w, dt in zip(widths, out_dtypes)]
        + [jax.ShapeDtypeStruct(s.shape, BF16) for s in side_casts],
        scratch_shapes=[pltpu.VMEM((d, shifted_cols), weight.dtype)] if shifted_cols else [],
        compiler_params=_params("arbitrary"),
    )(x2d, norm_w.reshape(1, d), weight, *extras, *side_casts)


def _ssd_body(xbc_ref, sm_ref, z_ref, cw_ref, cb_ref, alog_ref, dtb_ref, dskip_ref, nw_ref,
              y_ref, h_ref, xpad_ref, act_ref, xw_ref, ycat_ref, side=None, first=None):
    q = SSD_CHUNK
    t = xbc_ref.shape[1]
    width = y_ref.shape[-1]
    n_pairs = width // LANES
    n_slabs = xpad_ref.shape[0]
    gstate = SSD_GROUPS * SSD_STATE
    gwidth = width // SSD_GROUPS
    if first is None:
        first = pl.program_id(1) == 0

    @pl.when(first)
    def _():
        xpad_ref[:, 0:SUBLANES, :] = jnp.zeros((n_slabs, SUBLANES, LANES), F32)
        h_ref[...] = jnp.zeros_like(h_ref)

    @pl.when(jnp.logical_not(first))
    def _():
        xpad_ref[:, 0:SUBLANES, :] = xpad_ref[:, t:t + SUBLANES, :]

    for c in range(n_slabs):
        lanes = slice(c * LANES, (c + 1) * LANES)
        xpad_ref[c, SUBLANES:SUBLANES + t, :] = xbc_ref[0, :, lanes]
        taps = [jnp.broadcast_to(cw_ref[k:k + 1, lanes], (SUBLANES, LANES)) for k in range(SSD_CONV)]
        bias = jnp.broadcast_to(cb_ref[:, lanes], (SUBLANES, LANES))
        for t0 in range(0, t, SUBLANES * CONV_ROW_STRIDE):
            for g in range(CONV_ROW_STRIDE):
                conv = bias
                for k in range(SSD_CONV):
                    start = SUBLANES + t0 + g - (SSD_CONV - 1 - k)
                    conv = conv + taps[k] * xpad_ref[c, pl.ds(start, SUBLANES, stride=CONV_ROW_STRIDE), :]
                act_ref[c, pl.ds(t0 + g, SUBLANES, stride=CONV_ROW_STRIDE), :] = _silu(conv)

    dt_all = _softplus(sm_ref[0] + dtb_ref[...])
    dta = dt_all * (-jnp.exp(alog_ref[...]) * LOG2_E)
    a_cs_all = _chunk_cumsum(dta, q)
    ri = lax.broadcasted_iota(jnp.int32, (q, q), 0)
    ci = lax.broadcasted_iota(jnp.int32, (q, q), 1)
    causal = ri >= ci
    lane = lax.broadcasted_iota(jnp.int32, (q, LANES), 1)
    lo = lane < HALF
    lo_row = lo[0:1, :]

    h_prev = h_ref[...]
    for rows in [slice(r0, r0 + q) for r0 in range(0, t, q)]:
        if side:
            side.pop(0)()
        bm = act_ref[n_pairs, rows, :]
        cm = act_ref[n_pairs + 1, rows, :]
        dt = dt_all[rows, :]
        a_cs = a_cs_all[rows, :]
        a_cs_t = a_cs.T
        a_last = a_cs[q - 1:q, :]
        bm_t = bm.T
        cm_g = [jnp.where(lo, cm, 0.0), jnp.where(lo, 0.0, cm)]
        cb = [_dot_nt(c, bm) for c in cm_g]
        y_off = [_dot(c, h_prev) for c in cm_g]

        a_last_pairs = []
        for p in range(n_pairs):
            if side and p % 2 == 1:
                side.pop(0)()
            e0, e1 = 2 * p, 2 * p + 1
            g = (p * LANES) // gwidth
            col = slice(p * LANES, (p + 1) * LANES)
            gcol = slice(p * LANES - g * gwidth, (p + 1) * LANES - g * gwidth)
            acs_pair = jnp.where(lo, a_cs[:, e0:e0 + 1], a_cs[:, e1:e1 + 1])
            dt_pair = jnp.where(lo, dt[:, e0:e0 + 1], dt[:, e1:e1 + 1])
            al_pair = jnp.where(lo_row, a_last[:, e0:e0 + 1], a_last[:, e1:e1 + 1])
            a_last_pairs.append(al_pair)
            xs2 = act_ref[p, rows, :]
            xdt = xs2 * dt_pair
            xdt_b = xdt.astype(BF16)
            m_pair = []
            for e in (e0, e1):
                seg = a_cs[:, e:e + 1] - a_cs_t[e:e + 1, :]
                decay = jnp.exp2(jnp.where(causal, seg, -jnp.inf))
                m_pair.append((cb[g] * decay).astype(BF16))
            zero_b = jnp.zeros_like(xdt_b)
            x_diag = jnp.concatenate([jnp.where(lo, xdt_b, zero_b), jnp.where(lo, zero_b, xdt_b)], axis=0)
            y2 = jnp.dot(jnp.concatenate(m_pair, axis=1), x_diag, preferred_element_type=F32)
            y2 = y2 + y_off[g][:, gcol] * jnp.exp2(acs_pair) + dskip_ref[:, col] * xs2
            ycat_ref[rows, col] = y2
            xw_ref[rows, col] = (xdt * jnp.exp2(al_pair - acs_pair)).astype(BF16)

        h_next = []
        for g in range(SSD_GROUPS):
            srows = slice(g * SSD_STATE, (g + 1) * SSD_STATE)
            ppg = n_pairs // SSD_GROUPS
            dec = jnp.exp2(jnp.concatenate(a_last_pairs[g * ppg:(g + 1) * ppg], axis=1))
            s_g = _dot(bm_t[srows, :], xw_ref[rows, g * gwidth:(g + 1) * gwidth])
            h_next.append(h_prev[srows, :] * dec + s_g)
        h_prev = jnp.concatenate(h_next, axis=0)
        y_ref[0, rows, :] = _rms(ycat_ref[rows, :] * _silu(z_ref[0, rows, :]), nw_ref[...]).astype(y_ref.dtype)
    h_ref[...] = h_prev


def _gla_body(q_ref, k_ref, v_ref, sm_ref, g_ref, w2_ref, gb_ref, nw_ref, o_ref, st_ref, *, mid_lane, side=None, first=None):
    c = GLA_CHUNK
    kw = q_ref.shape[-1]
    dk = kw // GLA_HEADS
    dv = v_ref.shape[-1] // GLA_HEADS
    n_pairs = kw // LANES

    if first is None:
        first = pl.program_id(1) == 0

    @pl.when(first)
    def _():
        st_ref[...] = jnp.zeros_like(st_ref)

    ri = lax.broadcasted_iota(jnp.int32, (c, c), 0)
    ci = lax.broadcasted_iota(jnp.int32, (c, c), 1)
    causal = ri >= ci
    lo = lax.broadcasted_iota(jnp.int32, (c, LANES), 1) < HALF
    lo_sq = lax.broadcasted_iota(jnp.int32, (dv, LANES), 1) < HALF

    sm_hi, sm_mid = _split_bf16(sm_ref[0], 2)
    sm_lane = lax.broadcasted_iota(jnp.int32, sm_hi.shape, 1)
    pre = jnp.dot(jnp.where(sm_lane < mid_lane, sm_hi, sm_mid), w2_ref[...],
                  preferred_element_type=F32) + gb_ref[...]
    gcs_all = _chunk_cumsum(_log_sigmoid(pre) * (LOG2_E / GLA_GATE_NORMALIZER), c)

    chunks = [slice(ch * c, (ch + 1) * c) for ch in range(q_ref.shape[1] // c)]
    pairs = [slice(p * LANES, (p + 1) * LANES) for p in range(n_pairs)]
    causal2 = jnp.concatenate([causal, causal], axis=0)

    def by_head(x):
        return jnp.concatenate([jnp.where(lo, x, 0.0), jnp.where(lo, 0.0, x)], axis=0).astype(BF16)

    q_st, k_st, g_last, scores = {}, {}, {}, {}

    def front(ch):
        rows = chunks[ch]
        gcs = gcs_all[rows, :]
        g_mid = gcs[c // 2:c // 2 + 1, :]
        g_last[ch] = gcs[c - 1:c, :]
        qs = q_ref[0, rows, :] * (dk ** -0.5)
        ks = k_ref[0, rows, :]
        q_mid = qs * jnp.exp2(gcs - g_mid)
        k_mid = ks * jnp.exp2(g_mid - gcs)
        k_in = k_mid.astype(BF16)
        q_st[ch] = q_mid * jnp.exp2(g_mid)
        k_st[ch] = (k_mid * jnp.exp2(g_last[ch] - g_mid)).astype(BF16)
        scores[ch] = [jnp.where(causal2, _dot_nt(by_head(q_mid[:, col]), k_in[:, col]), 0.0).astype(BF16)
                      for col in pairs]

    def back(ch, states):
        rows = chunks[ch]
        for p, col in enumerate(pairs):
            if side and p % 2 == 1:
                side.pop(0)()
            v_pair = [v_ref[0, rows, (2 * p + half) * dv:(2 * p + half + 1) * dv].astype(BF16) for half in range(2)]
            o_inter = _dot_nt(by_head(q_st[ch][:, col]), states[p])
            kv_t = jnp.where(lo_sq, _dot_tn(v_pair[0], k_st[ch][:, col]), _dot_tn(v_pair[1], k_st[ch][:, col]))
            states[p] = states[p] * jnp.exp2(g_last[ch][:, col]) + kv_t
            for half in range(2):
                vcol = slice((2 * p + half) * dv, (2 * p + half + 1) * dv)
                o = (jnp.dot(scores[ch][p][half * c:(half + 1) * c, :], v_pair[half], preferred_element_type=F32)
                     + o_inter[half * c:(half + 1) * c, :])
                o_ref[0, rows, vcol] = (_rms(o, nw_ref[...]) * _silu(g_ref[0, rows, vcol])).astype(o_ref.dtype)
        del q_st[ch], k_st[ch], g_last[ch], scores[ch]

    states = [st_ref[:, col] for col in pairs]
    front(0)
    for ch in range(len(chunks)):
        if ch + 1 < len(chunks):
            front(ch + 1)
        if side:
            side.pop(0)()
        back(ch, states)
    for p, col in enumerate(pairs):
        st_ref[:, col] = states[p]


def _even_pipe_body(*refs, n_side, offsets, mid_lane, blocks_per_seq):
    it = iter(refs)
    take = lambda n: [next(it) for _ in range(n)]
    x0_ref, xa_ref, xb_ref, nw_ref, w_ref, wsm_ref = take(6)
    side_in = take(n_side)
    ssd_par = take(6)
    gla_par = take(3)
    y_ref, o_ref = take(2)
    side_out = take(n_side)
    (al_ref,) = take(1)
    u_refs = take(7)
    ssd_scr = take(5)
    gla_scr = take(1)
    z_s, xbc_s, q_s, k_s, v_s, g_s, sm_s = u_refs
    t = pl.program_id(0)
    rows_blk = xa_ref.shape[0]

    for s_in, s_out in zip(side_in, side_out):
        s_out[...] = s_in[...].astype(s_out.dtype)

    def project(x_ref, slot, defer, realign=False):
        outs = [u.at[slot] for u in u_refs]
        return _norm_proj_body(x_ref, nw_ref, w_ref, wsm_ref, *outs, al_ref, offsets=offsets, n_extra=1,
                               n_side=0, defer=defer, realign=realign)

    @pl.when(t == 0)
    def _():
        project(x0_ref, 0, defer=False, realign=True)

    for half, (x_ref, wslot) in enumerate(((xa_ref, 1), (xb_ref, 0))):
        rslot = 1 - wslot
        first = (2 * t + half) % blocks_per_seq == 0
        jobs = project(x_ref, wslot, defer=True)
        rd = lambda u: u.at[rslot:rslot + 1]
        out_rows = pl.ds(half * rows_blk, rows_blk)
        _ssd_body(rd(xbc_s), rd(sm_s), rd(z_s), *ssd_par, y_ref.at[:, out_rows], *ssd_scr, side=jobs, first=first)
        _gla_body(rd(q_s), rd(k_s), rd(v_s), rd(sm_s), rd(g_s), *gla_par, o_ref.at[:, out_rows], *gla_scr,
                  mid_lane=mid_lane, side=jobs, first=first)
        for job in jobs:
            job()


def _even_pipe(x, norm_w, w_in_bf, w_small, segments, conv_w, conv_b, a_log, dt_bias, d_skip, ssd_norm_w,
               gate_w2, gate_b, gla_norm_w, lr_lane, side_casts):
    b, l, d = x.shape
    t = EVEN_PIPE_ROWS
    width = ssd_norm_w.shape[0]
    cdim = conv_w.shape[1]
    n_heads = a_log.shape[0]
    rank, kw = gate_w2.shape
    vw = gla_norm_w.shape[0] * GLA_HEADS
    bps = l // t
    n = b * bps
    steps = n // 2
    assert bps % 2 == 0
    pad = lambda v: jnp.pad(v, (0, LANES - n_heads)).reshape(1, LANES)
    w2_hi, w2_mid = _split_bf16(gate_w2, 2)
    w2_pad = jnp.zeros((LANES, kw), BF16)
    for copy, part in enumerate((w2_hi, w2_mid, w2_hi)):
        w2_pad = w2_pad.at[lr_lane + copy * rank:lr_lane + (copy + 1) * rank, :].set(part)
    assert all(sc.shape[0] % (steps * BF16_SUBLANES) == 0 for sc in side_casts)
    shifted_cols = sum(w for off, w in segments if off % LANES)
    x_blk = lambda off: pl.BlockSpec((t, d), lambda s: (jnp.minimum(2 * s + off, n - 1), 0))
    side_specs = [pl.BlockSpec((sc.shape[0] // steps, sc.shape[1]), lambda s: (s, 0)) for sc in side_casts]
    scan_blk = lambda w: pl.BlockSpec((1, 2 * t, w), lambda s: (s // (bps // 2), s % (bps // 2), 0))
    widths = [w for _, w in segments] + [LANES]
    dtypes = [F32, F32, F32, F32, BF16, F32, F32]
    gwidth = width // SSD_GROUPS
    x2d = x.reshape(b * l, d)
    outs = pl.pallas_call(
        functools.partial(_even_pipe_body, n_side=len(side_casts), offsets=tuple(off for off, _ in segments),
                          mid_lane=lr_lane + 2 * rank, blocks_per_seq=bps),
        name="even_pipe",
        grid=(steps,),
        in_specs=[pl.BlockSpec((t, d), lambda s: (0, 0)), x_blk(1), x_blk(2), _resident((1, d)),
                  _resident(w_in_bf.shape), _resident(w_small.shape)] + side_specs
        + [_resident((SSD_CONV, cdim)), _resident((1, cdim)), _resident((1, LANES)), _resident((1, LANES)),
           _resident((1, width)), _resident((1, width)),
           _resident((LANES, kw)), _resident((1, kw)), _resident((1, vw // GLA_HEADS))],
        out_specs=[scan_blk(width), scan_blk(vw)] + side_specs,
        out_shape=[jax.ShapeDtypeStruct((b, l, width), BF16), jax.ShapeDtypeStruct((b, l, vw), BF16)]
        + [jax.ShapeDtypeStruct(sc.shape, BF16) for sc in side_casts],
        scratch_shapes=[pltpu.VMEM((d, shifted_cols), BF16)]
        + [pltpu.VMEM((2, t, w), dt) for w, dt in zip(widths, dtypes)]
        + [pltpu.VMEM((SSD_GROUPS * SSD_STATE, gwidth), F32),
           pltpu.VMEM((cdim // LANES, t + SUBLANES, LANES), F32),
           pltpu.VMEM((cdim // LANES, t, LANES), F32),
           pltpu.VMEM((t, width), BF16),
           pltpu.VMEM((t, width), F32),
           pltpu.VMEM((vw // GLA_HEADS, kw), F32)],
        compiler_params=_params("arbitrary"),
    )(x2d, x2d, x2d, norm_w.reshape(1, d), w_in_bf, w_small, *side_casts,
      conv_w, conv_b.reshape(1, cdim), pad(a_log), pad(dt_bias), jnp.repeat(d_skip, SSD_HEAD_DIM).reshape(1, width),
      ssd_norm_w.reshape(1, width), w2_pad, gate_b.reshape(1, kw), gla_norm_w.reshape(1, -1))
    return outs[0], outs[1], outs[2:]


def _out_proj_body(*refs, n_in, final_norm, next_widths):
    a_refs, x_ref, w_ref = refs[:n_in], refs[n_in], refs[n_in + 1]
    rest = refs[n_in + 2:]
    acc = x_ref[...]
    off = 0
    for a_ref in a_refs:
        width = a_ref.shape[1]
        acc = acc + jnp.dot(a_ref[...].astype(BF16), w_ref[off:off + width, :], preferred_element_type=F32)
        off += width
    if final_norm:
        fw_ref, o_ref = rest
        o_ref[...] = _rms(acc, fw_ref[...])
        return
    if not next_widths:
        (o_ref,) = rest
        o_ref[...] = acc
        return
    nw_ref, w2_ref, o_ref = rest[:3]
    o_ref[...] = acc
    h = _rms(acc, nw_ref[...]).astype(BF16)
    off = 0
    for p_ref, width in zip(rest[3:], next_widths):
        p_ref[...] = jnp.dot(h, w2_ref[:, off:off + width], preferred_element_type=F32).astype(p_ref.dtype)
        off += width


def _out_proj(acts, x2d, weight, final_w=None, next_proj=None):
    m, d = x2d.shape
    tm = OUT_PROJ_ROWS if next_proj is None else IN_PROJ_ROWS
    n_in = len(acts)
    assert sum(a.shape[1] for a in acts) == weight.shape[0]
    row = lambda w: pl.BlockSpec((tm, w), lambda i: (i, 0))
    in_specs = [row(a.shape[1]) for a in acts] + [row(d), _resident(weight.shape)]
    args = list(acts) + [x2d, weight]
    out_specs, out_shape, widths = row(d), jax.ShapeDtypeStruct((m, d), F32), ()
    if final_w is not None:
        in_specs.append(_resident((1, d)))
        args.append(final_w.reshape(1, d))
    elif next_proj is not None:
        norm_w, w2, widths, dtypes = next_proj
        assert sum(widths) == w2.shape[1] and all(w % LANES == 0 for w in widths)
        in_specs += [_resident((1, d)), _resident(w2.shape)]
        args += [norm_w.reshape(1, d), w2]
        out_specs = [out_specs] + [row(w) for w in widths]
        out_shape = [out_shape] + [jax.ShapeDtypeStruct((m, w), dt) for w, dt in zip(widths, dtypes)]
    outs = pl.pallas_call(
        functools.partial(_out_proj_body, n_in=n_in, final_norm=final_w is not None, next_widths=tuple(widths)),
        name="out_proj",
        grid=(m // tm,),
        in_specs=in_specs,
        out_specs=out_specs,
        out_shape=out_shape,
        compiler_params=_params("parallel"),
    )(*args)
    return outs if next_proj is None else (outs[0], outs[1:])


class _MobaQueryBlock:
    def __init__(self, n_past, q_ref, z_ref, o_ref, kb_ref, vt_ref, kmean_ref, s_ref):
        blk = MOBA_BLOCK
        self.n_past, self.z_ref, self.o_ref, self.kb_ref, self.vt_ref = n_past, z_ref, o_ref, kb_ref, vt_ref
        self.own = slice(n_past * blk, (n_past + 1) * blk)
        self.s_ref = s_ref.at[n_past % s_ref.shape[0]]
        q2 = q_ref[0, self.own, :]
        lo = lax.broadcasted_iota(jnp.int32, (blk, LANES), 1) < HALF
        q_cat = jnp.concatenate([jnp.where(lo, q2, 0.0), jnp.where(lo, 0.0, q2)], axis=0)
        scale = MOBA_HEAD_DIM ** -0.5
        key_i = lax.broadcasted_iota(jnp.int32, (blk, 2 * blk), 0)
        qry_i = lax.broadcasted_iota(jnp.int32, (blk, 2 * blk), 1)
        causal_t = key_i <= jnp.where(qry_i < blk, qry_i, qry_i - blk)

        self.masks = []
        if n_past:
            nb = kmean_ref.shape[0] // 2
            parts = [_dot_nt(kmean_ref[...], q_part) for q_part in _split_bf16(q_cat * scale, 2)]
            gate = (parts[0][0:nb] + parts[0][nb:]) + (parts[1][0:nb] + parts[1][nb:])
            g_rows = [gate[n:n + 1, :] for n in range(n_past)]
            for n in range(n_past):
                rank = jnp.zeros((1, 2 * blk), F32)
                for m in range(n_past):
                    if m != n:
                        ahead = (g_rows[m] >= g_rows[n]) if m < n else (g_rows[m] > g_rows[n])
                        rank = rank + jnp.where(ahead, 1.0, 0.0)
                self.masks.append(rank < MOBA_TOPK)
        self.masks.append(causal_t)
        self.q_s = (q_cat * (scale * LOG2_E)).astype(BF16)
        self.m_run = None
        self.p = []

    def logits_step(self, j):
        rows = slice(j * MOBA_BLOCK, (j + 1) * MOBA_BLOCK)
        sj = jnp.where(self.masks[j], _dot_nt(self.kb_ref[rows, :], self.q_s), -jnp.inf)
        self.s_ref[rows, :] = sj
        mj = jnp.max(sj, axis=0, keepdims=True)
        self.m_run = mj if self.m_run is None else jnp.maximum(self.m_run, mj)

    def value_step(self, j):
        rows = slice(j * MOBA_BLOCK, (j + 1) * MOBA_BLOCK)
        self.p.append(jnp.exp2(self.s_ref[rows, :] - self.m_run).astype(BF16))

    def finish(self):
        blk = MOBA_BLOCK
        nk = len(self.p) * blk
        p_all = jnp.concatenate(self.p, axis=0)
        outs = []
        for half in range(2):
            vt = self.vt_ref[half * VT_ROWS:(half + 1) * VT_ROWS, 0:nk]
            acc = jnp.dot(vt, p_all[:, half * blk:(half + 1) * blk], preferred_element_type=F32)
            outs.append(acc[0:HALF, :] / acc[HALF:HALF + 1, :])
        o_t = jnp.concatenate(outs, axis=0)
        self.o_ref[0, self.own, :] = (o_t.T * _silu(self.z_ref[0, self.own, :])).astype(self.o_ref.dtype)


def _moba_body(q_ref, k_ref, v_ref, z_ref, o_ref, vt_ref, kmean_ref, s_ref):
    blk = MOBA_BLOCK
    nb = k_ref.shape[1] // blk
    kmean = []
    for n in range(nb):
        rows = slice(n * blk, (n + 1) * blk)
        kmean.append(jnp.mean(k_ref[0, rows, :].astype(F32), axis=0, keepdims=True))
        v_t = v_ref[0, rows, :].astype(F32).T.astype(BF16)
        for half in range(2):
            vt_ref[half * VT_ROWS:half * VT_ROWS + HALF, rows] = v_t[half * HALF:(half + 1) * HALF, :]
    for half in range(2):
        vt_ref[half * VT_ROWS + HALF:(half + 1) * VT_ROWS, :] = jnp.ones((BF16_SUBLANES, vt_ref.shape[1]), BF16)
    kmean_ref[...] = jnp.concatenate(_split_bf16(jnp.concatenate(kmean, axis=0), 2), axis=0)
    make = functools.partial(_MobaQueryBlock, q_ref=q_ref, z_ref=z_ref, o_ref=o_ref, kb_ref=k_ref.at[0],
                             vt_ref=vt_ref, kmean_ref=kmean_ref, s_ref=s_ref)
    blocks = {0: make(0), 1: make(1)}
    blocks[0].logits_step(0)
    for i in range(nb):
        if i + 2 < nb:
            blocks[i + 2] = make(i + 2)
        cur, nxt = blocks.pop(i), blocks.get(i + 1)
        for j in range(i + 2):
            if j <= i:
                cur.value_step(j)
            if nxt is not None:
                nxt.logits_step(j)
        cur.finish()


def _moba(q, k, v, z):
    b, l, w = q.shape
    blk = MOBA_BLOCK
    nb = l // blk
    spec = pl.BlockSpec((1, l, LANES), lambda bi, hp: (bi, 0, hp))
    return pl.pallas_call(
        _moba_body,
        name="moba",
        grid=(b, w // LANES),
        in_specs=[spec, spec, spec, spec],
        out_specs=spec,
        out_shape=jax.ShapeDtypeStruct((b, l, w), BF16),
        scratch_shapes=[pltpu.VMEM((2 * VT_ROWS, l), BF16),
                        pltpu.VMEM((2 * nb, LANES), BF16),
                        pltpu.VMEM((2, l, 2 * blk), F32)],
        compiler_params=_params("parallel", "parallel"),
    )(q, k, v, z)


def _even_layer(x, norm_w, w_in_bf, conv_w, conv_b, a_log, dt_bias, d_skip, ssd_norm_w, gate_w2, gate_b,
                gla_norm_w, side_casts=()):
    b, l, d = x.shape
    width = ssd_norm_w.shape[0]
    cdim = conv_w.shape[1]
    n_heads = a_log.shape[0]
    rank, kw = gate_w2.shape
    vw = gla_norm_w.shape[0] * GLA_HEADS
    cuts = [0]
    for s in (width, cdim, n_heads, kw, kw, vw, rank, vw):
        cuts.append(cuts[-1] + s)
    seg = lambda j: w_in_bf[:, cuts[j]:cuts[j + 1]]
    pad = jnp.zeros((d, LANES - n_heads - GATE_COPIES * rank), BF16)
    w_small = jnp.concatenate([seg(2)] + [seg(6)] * GATE_COPIES + [pad], axis=1)
    segments = [(cuts[j], cuts[j + 1] - cuts[j]) for j in (0, 1, 3, 4, 5, 7)]
    y_a, o_b, cast = _even_pipe(x, norm_w, w_in_bf, w_small, segments, conv_w, conv_b, a_log, dt_bias, d_skip,
                                ssd_norm_w, gate_w2, gate_b, gla_norm_w, n_heads, side_casts)
    return [y_a.reshape(b * l, width), o_b.reshape(b * l, vw)], cast


ODD_PROJ_DTYPES = (F32, BF16, BF16, F32)


def _odd_layer(x, norm_w, w_in_bf, side_casts=(), projected=None):
    b, l, d = x.shape
    w = w_in_bf.shape[1] // 4
    outs = projected
    if outs is None:
        outs = _norm_proj(x.reshape(b * l, d), norm_w, w_in_bf, [(j * w, w) for j in range(4)],
                          list(ODD_PROJ_DTYPES), side_casts=side_casts)
    q, k, v, z = [u.reshape(b, l, w) for u in outs[:4]]
    return [_moba(q, k, v, z).reshape(b * l, w)], outs[4:]


def kernel(x, even_norm, even_w_in, even_conv_w, even_conv_b, even_a_log, even_dt_bias, even_d_skip, even_ssd_norm,
           even_gate_w2, even_gate_b, even_gla_norm, even_w_out, odd_norm, odd_w_in, odd_w_out, final_norm):
    b, l, d = x.shape
    depth = even_norm.shape[0] + odd_norm.shape[0]
    w_in = lambda layer: (even_w_in if layer % 2 == 0 else odd_w_in)[layer // 2]
    w_out = lambda layer: (even_w_out if layer % 2 == 0 else odd_w_out)[layer // 2]
    keys = [("out", 0)] + [(kind, layer) for layer in range(1, depth) for kind in ("in", "out")]
    pending = [w_in(layer) if kind == "in" else w_out(layer) for kind, layer in keys]
    bf = {("in", 0): w_in(0).astype(BF16)}
    projected = None
    for layer in range(depth):
        i = layer // 2
        side = pending if layer == 0 else ()
        if layer % 2 == 0:
            acts, cast = _even_layer(x, even_norm[i], bf["in", layer], even_conv_w[i], even_conv_b[i], even_a_log[i],
                                     even_dt_bias[i], even_d_skip[i], even_ssd_norm[i], even_gate_w2[i],
                                     even_gate_b[i], even_gla_norm[i], side_casts=side)
        else:
            acts, cast = _odd_layer(x, odd_norm[i], bf["in", layer], side_casts=side, projected=projected)
        bf.update(zip(keys, cast))
        projected = None
        x2d = x.reshape(b * l, d)
        if layer + 1 < depth and (layer + 1) % 2 == 1:
            nxt = layer + 1
            w2 = bf["in", nxt]
            x2d, projected = _out_proj(acts, x2d, bf["out", layer],
                                       next_proj=(odd_norm[nxt // 2], w2, [w2.shape[1] // 4] * 4, ODD_PROJ_DTYPES))
        else:
            x2d = _out_proj(acts, x2d, bf["out", layer], final_norm if layer == depth - 1 else None)
        x = x2d.reshape(b, l, d)
    return x
```

```python
import functools

import jax
import jax.numpy as jnp
from jax import lax
from jax.experimental import pallas as pl
from jax.experimental.pallas import tpu as pltpu

F32 = jnp.float32
BF16 = jnp.bfloat16

LANES = 128
SUBLANES = 8
BF16_SUBLANES = 16
VMEM_LIMIT_BYTES = 56 * 1024 * 1024

RMS_EPS = 1e-6
SSD_HEAD_DIM = 64
SSD_GROUPS = 2
SSD_STATE = 64
SSD_CONV = 4
SSD_CHUNK = 128
GLA_HEADS = 8
GLA_GATE_NORMALIZER = 16.0
GLA_CHUNK = 64
MOBA_HEAD_DIM = 64
MOBA_BLOCK = 256
MOBA_TOPK = 3
LOG2_E = 1.4426950408889634
CONV_ROW_STRIDE = 4
CUMSUM_TERMS = 3
GATE_COPIES = 3

IN_PROJ_ROWS = 512
OUT_PROJ_ROWS = 1024
EVEN_PIPE_ROWS = 256
JOB_COLS = 512
HALF = LANES // 2
VT_ROWS = HALF + BF16_SUBLANES


def _params(*sem):
    return pltpu.CompilerParams(dimension_semantics=sem, vmem_limit_bytes=VMEM_LIMIT_BYTES)


def _rms(x, w):
    return x * lax.rsqrt(jnp.mean(x * x, axis=-1, keepdims=True) + RMS_EPS) * w


def _silu(x):
    h = 0.5 * x
    return h + h * jnp.tanh(h)


def _softplus(x):
    return jnp.maximum(x, 0.0) + jnp.log1p(jnp.exp(-jnp.abs(x)))


def _log_sigmoid(x):
    return jnp.minimum(x, 0.0) - jnp.log(1.0 + jnp.exp(-jnp.abs(x)))


def _dot(a, b):
    return jnp.dot(a.astype(BF16), b.astype(BF16), preferred_element_type=F32)


def _dot_nt(a, b):
    return lax.dot_general(a.astype(BF16), b.astype(BF16), (((1,), (1,)), ((), ())), preferred_element_type=F32)


def _dot_tn(a, b):
    return lax.dot_general(a.astype(BF16), b.astype(BF16), (((0,), (0,)), ((), ())), preferred_element_type=F32)


def _split_bf16(x, terms):
    parts = []
    for _ in range(terms):
        p = x.astype(BF16)
        parts.append(p)
        x = x - p.astype(F32)
    return parts


def _chunk_cumsum(x, chunk):
    rows = x.shape[0]
    assert chunk & (chunk - 1) == 0
    ri = lax.broadcasted_iota(jnp.int32, (chunk, CUMSUM_TERMS * chunk), 0)
    ci = lax.broadcasted_iota(jnp.int32, (chunk, CUMSUM_TERMS * chunk), 1)
    tri = jnp.where((ci & (chunk - 1)) <= ri, 1.0, 0.0).astype(BF16)
    out = []
    for r0 in range(0, rows, chunk):
        stacked = jnp.concatenate(_split_bf16(x[r0:r0 + chunk, :], CUMSUM_TERMS), axis=0)
        out.append(jnp.dot(tri, stacked, preferred_element_type=F32))
    return out[0] if len(out) == 1 else jnp.concatenate(out, axis=0)


def _resident(shape):
    return pl.BlockSpec(shape, lambda *_: (0,) * len(shape), pipeline_mode=pl.Buffered(1))


def _norm_proj_body(x_ref, nw_ref, w_ref, *refs, offsets, n_extra, n_side, defer=False, realign=True):
    n_out = len(offsets) + n_extra
    extra_refs, side_in = refs[:n_extra], refs[n_extra:n_extra + n_side]
    o_refs = refs[n_extra + n_side:n_extra + n_side + n_out]
    side_out = refs[n_extra + n_side + n_out:n_extra + 2 * n_side + n_out]
    for s_in, s_out in zip(side_in, side_out):
        s_out[...] = s_in[...].astype(s_out.dtype)
    shifted = [i for i, off in enumerate(offsets) if off % LANES]
    starts, pos = {}, 0
    for i in shifted:
        starts[i] = pos
        pos += o_refs[i].shape[1]

    if shifted:
        al_ref = refs[-1]

        @pl.when(jnp.logical_and(pl.program_id(0) == 0, realign))
        def _():
            for i in shifted:
                width = o_refs[i].shape[1]
                al_ref[:, starts[i]:starts[i] + width] = w_ref[:, offsets[i]:offsets[i] + width]

    h = _rms(x_ref[...], nw_ref[...]).astype(BF16)
    jobs = []

    def seg_job(i, off, c0, c1):
        def run():
            w = al_ref[:, starts[i] + c0:starts[i] + c1] if i in starts else w_ref[:, off + c0:off + c1]
            o_refs[i][:, c0:c1] = jnp.dot(h, w, preferred_element_type=F32).astype(o_refs[i].dtype)
        return run

    def extra_job(e_ref, o_ref):
        def run():
            o_ref[...] = jnp.dot(h, e_ref[...], preferred_element_type=F32).astype(o_ref.dtype)
        return run

    for i, off in enumerate(offsets):
        width = o_refs[i].shape[1]
        jobs += [seg_job(i, off, c0, min(c0 + JOB_COLS, width)) for c0 in range(0, width, JOB_COLS)]
    jobs += [extra_job(e, o) for e, o in zip(extra_refs, o_refs[len(offsets):])]
    if defer:
        return jobs
    for job in jobs:
        job()


def _norm_proj(x2d, norm_w, weight, segments, out_dtypes, extras=(), side_casts=()):
    m, d = x2d.shape
    tm = IN_PROJ_ROWS
    steps = m // tm
    assert all(w % LANES == 0 for _, w in segments)
    assert all(s.shape[0] % (steps * BF16_SUBLANES) == 0 for s in side_casts)
    widths = [w for _, w in segments] + [e.shape[1] for e in extras]
    shifted_cols = sum(w for off, w in segments if off % LANES)
    side_specs = [pl.BlockSpec((s.shape[0] // steps, s.shape[1]), lambda i: (i, 0)) for s in side_casts]
    return pl.pallas_call(
        functools.partial(_norm_proj_body, offsets=tuple(off for off, _ in segments), n_extra=len(extras),
                          n_side=len(side_casts)),
        name="norm_proj",
        grid=(steps,),
        in_specs=[pl.BlockSpec((tm, d), lambda i: (i, 0)), _resident((1, d)), _resident(weight.shape)]
        + [_resident(e.shape) for e in extras] + side_specs,
        out_specs=[pl.BlockSpec((tm, w), lambda i: (i, 0)) for w in widths] + side_specs,
        out_shape=[jax.ShapeDtypeStruct((m, w), dt) for w, dt in zip(widths, out_dtypes)]
        + [jax.ShapeDtypeStruct(s.shape, BF16) for s in side_casts],
        scratch_shapes=[pltpu.VMEM((d, shifted_cols), weight.dtype)] if shifted_cols else [],
        compiler_params=_params("arbitrary"),
    )(x2d, norm_w.reshape(1, d), weight, *extras, *side_casts)


def _ssd_body(xbc_ref, sm_ref, z_ref, cw_ref, cb_ref, alog_ref, dtb_ref, dskip_ref, nw_ref,
              y_ref, h_ref, xpad_ref, act_ref, xw_ref, ycat_ref, side=None, first=None):
    q = SSD_CHUNK
    t = xbc_ref.shape[1]
    width = y_ref.shape[-1]
    n_pairs = width // LANES
    n_slabs = xpad_ref.shape[0]
    gstate = SSD_GROUPS * SSD_STATE
    gwidth = width // SSD_GROUPS
    if first is None:
        first = pl.program_id(1) == 0

    @pl.when(first)
    def _():
        xpad_ref[:, 0:SUBLANES, :] = jnp.zeros((n_slabs, SUBLANES, LANES), F32)
        h_ref[...] = jnp.zeros_like(h_ref)

    @pl.when(jnp.logical_not(first))
    def _():
        xpad_ref[:, 0:SUBLANES, :] = xpad_ref[:, t:t + SUBLANES, :]

    for c in range(n_slabs):
        lanes = slice(c * LANES, (c + 1) * LANES)
        xpad_ref[c, SUBLANES:SUBLANES + t, :] = xbc_ref[0, :, lanes]
        taps = [jnp.broadcast_to(cw_ref[k:k + 1, lanes], (SUBLANES, LANES)) for k in range(SSD_CONV)]
        bias = jnp.broadcast_to(cb_ref[:, lanes], (SUBLANES, LANES))
        for t0 in range(0, t, SUBLANES * CONV_ROW_STRIDE):
            for g in range(CONV_ROW_STRIDE):
                conv = bias
                for k in range(SSD_CONV):
                    start = SUBLANES + t0 + g - (SSD_CONV - 1 - k)
                    conv = conv + taps[k] * xpad_ref[c, pl.ds(start, SUBLANES, stride=CONV_ROW_STRIDE), :]
                act_ref[c, pl.ds(t0 + g, SUBLANES, stride=CONV_ROW_STRIDE), :] = _silu(conv)

    dt_all = _softplus(sm_ref[0] + dtb_ref[...])
    dta = dt_all * (-jnp.exp(alog_ref[...]) * LOG2_E)
    a_cs_all = _chunk_cumsum(dta, q)
    ri = lax.broadcasted_iota(jnp.int32, (q, q), 0)
    ci = lax.broadcasted_iota(jnp.int32, (q, q), 1)
    causal = ri >= ci
    lane = lax.broadcasted_iota(jnp.int32, (q, LANES), 1)
    lo = lane < HALF
    lo_row = lo[0:1, :]

    h_prev = h_ref[...]
    for rows in [slice(r0, r0 + q) for r0 in range(0, t, q)]:
        if side:
            side.pop(0)()
        bm = act_ref[n_pairs, rows, :]
        cm = act_ref[n_pairs + 1, rows, :]
        dt = dt_all[rows, :]
        a_cs = a_cs_all[rows, :]
        a_cs_t = a_cs.T
        a_last = a_cs[q - 1:q, :]
        bm_t = bm.T
        cm_g = [jnp.where(lo, cm, 0.0), jnp.where(lo, 0.0, cm)]
        cb = [_dot_nt(c, bm) for c in cm_g]
        y_off = [_dot(c, h_prev) for c in cm_g]

        a_last_pairs = []
        for p in range(n_pairs):
            if side and p % 2 == 1:
                side.pop(0)()
            e0, e1 = 2 * p, 2 * p + 1
            g = (p * LANES) // gwidth
            col = slice(p * LANES, (p + 1) * LANES)
            gcol = slice(p * LANES - g * gwidth, (p + 1) * LANES - g * gwidth)
            acs_pair = jnp.where(lo, a_cs[:, e0:e0 + 1], a_cs[:, e1:e1 + 1])
            dt_pair = jnp.where(lo, dt[:, e0:e0 + 1], dt[:, e1:e1 + 1])
            al_pair = jnp.where(lo_row, a_last[:, e0:e0 + 1], a_last[:, e1:e1 + 1])
            a_last_pairs.append(al_pair)
            xs2 = act_ref[p, rows, :]
            xdt = xs2 * dt_pair
            xdt_b = xdt.astype(BF16)
            m_pair = []
            for e in (e0, e1):
                seg = a_cs[:, e:e + 1] - a_cs_t[e:e + 1, :]
                decay = jnp.exp2(jnp.where(causal, seg, -jnp.inf))
                m_pair.append((cb[g] * decay).astype(BF16))
            zero_b = jnp.zeros_like(xdt_b)
            x_diag = jnp.concatenate([jnp.where(lo, xdt_b, zero_b), jnp.where(lo, zero_b, xdt_b)], axis=0)
            y2 = jnp.dot(jnp.concatenate(m_pair, axis=1), x_diag, preferred_element_type=F32)
            y2 = y2 + y_off[g][:, gcol] * jnp.exp2(acs_pair) + dskip_ref[:, col] * xs2
            ycat_ref[rows, col] = y2
            xw_ref[rows, col] = (xdt * jnp.exp2(al_pair - acs_pair)).astype(BF16)

        h_next = []
        for g in range(SSD_GROUPS):
            srows = slice(g * SSD_STATE, (g + 1) * SSD_STATE)
            ppg = n_pairs // SSD_GROUPS
            dec = jnp.exp2(jnp.concatenate(a_last_pairs[g * ppg:(g + 1) * ppg], axis=1))
            s_g = _dot(bm_t[srows, :], xw_ref[rows, g * gwidth:(g + 1) * gwidth])
            h_next.append(h_prev[srows, :] * dec + s_g)
        h_prev = jnp.concatenate(h_next, axis=0)
        y_ref[0, rows, :] = _rms(ycat_ref[rows, :] * _silu(z_ref[0, rows, :]), nw_ref[...]).astype(y_ref.dtype)
    h_ref[...] = h_prev


def _gla_body(q_ref, k_ref, v_ref, sm_ref, g_ref, w2_ref, gb_ref, nw_ref, o_ref, st_ref, *, mid_lane, side=None, first=None):
    c = GLA_CHUNK
    kw = q_ref.shape[-1]
    dk = kw // GLA_HEADS
    dv = v_ref.shape[-1] // GLA_HEADS
    n_pairs = kw // LANES

    if first is None:
        first = pl.program_id(1) == 0

    @pl.when(first)
    def _():
        st_ref[...] = jnp.zeros_like(st_ref)

    ri = lax.broadcasted_iota(jnp.int32, (c, c), 0)
    ci = lax.broadcasted_iota(jnp.int32, (c, c), 1)
    causal = ri >= ci
    lo = lax.broadcasted_iota(jnp.int32, (c, LANES), 1) < HALF

    sm_hi, sm_mid = _split_bf16(sm_ref[0], 2)
    sm_lane = lax.broadcasted_iota(jnp.int32, sm_hi.shape, 1)
    pre = jnp.dot(jnp.where(sm_lane < mid_lane, sm_hi, sm_mid), w2_ref[...],
                  preferred_element_type=F32) + gb_ref[...]
    gcs_all = _chunk_cumsum(_log_sigmoid(pre) * (LOG2_E / GLA_GATE_NORMALIZER), c)

    chunks = [slice(ch * c, (ch + 1) * c) for ch in range(q_ref.shape[1] // c)]
    pairs = [slice(p * LANES, (p + 1) * LANES) for p in range(n_pairs)]
    causal2 = jnp.concatenate([causal, causal], axis=0)

    def by_head(x):
        return jnp.concatenate([jnp.where(lo, x, 0.0), jnp.where(lo, 0.0, x)], axis=0).astype(BF16)

    q_st, k_st, g_last, scores = {}, {}, {}, {}

    def front(ch):
        rows = chunks[ch]
        gcs = gcs_all[rows, :]
        g_mid = gcs[c // 2:c // 2 + 1, :]
        g_last[ch] = gcs[c - 1:c, :]
        qs = q_ref[0, rows, :] * (dk ** -0.5)
        ks = k_ref[0, rows, :]
        q_mid = qs * jnp.exp2(gcs - g_mid)
        k_mid = ks * jnp.exp2(g_mid - gcs)
        k_in = k_mid.astype(BF16)
        q_st[ch] = q_mid * jnp.exp2(g_mid)
        k_st[ch] = (k_mid * jnp.exp2(g_last[ch] - g_mid)).astype(BF16)
        scores[ch] = [jnp.where(causal2, _dot_nt(by_head(q_mid[:, col]), k_in[:, col]), 0.0).astype(BF16)
                      for col in pairs]

    def back(ch, states):
        rows = chunks[ch]
        for p, col in enumerate(pairs):
            if side and p % 2 == 1:
                side.pop(0)()
            v_pair = [v_ref[0, rows, (2 * p + half) * dv:(2 * p + half + 1) * dv].astype(BF16) for half in range(2)]
            o_inter = _dot_nt(by_head(q_st[ch][:, col]), states[p])
            k_pair = k_st[ch][:, col]
            zero_k = jnp.zeros_like(k_pair)
            k_diag = jnp.concatenate([jnp.where(lo, k_pair, zero_k), jnp.where(lo, zero_k, k_pair)], axis=0)
            kv_t = _dot_tn(jnp.concatenate(v_pair, axis=0), k_diag)
            states[p] = states[p] * jnp.exp2(g_last[ch][:, col]) + kv_t
            for half in range(2):
                vcol = slice((2 * p + half) * dv, (2 * p + half + 1) * dv)
                o = (jnp.dot(scores[ch][p][half * c:(half + 1) * c, :], v_pair[half], preferred_element_type=F32)
                     + o_inter[half * c:(half + 1) * c, :])
                o_ref[0, rows, vcol] = (_rms(o, nw_ref[...]) * _silu(g_ref[0, rows, vcol])).astype(o_ref.dtype)
        del q_st[ch], k_st[ch], g_last[ch], scores[ch]

    states = [st_ref[:, col] for col in pairs]
    front(0)
    for ch in range(len(chunks)):
        if ch + 1 < len(chunks):
            front(ch + 1)
        if side:
            side.pop(0)()
        back(ch, states)
    for p, col in enumerate(pairs):
        st_ref[:, col] = states[p]


def _even_pipe_body(*refs, n_side, offsets, mid_lane, blocks_per_seq):
    it = iter(refs)
    take = lambda n: [next(it) for _ in range(n)]
    x0_ref, xa_ref, xb_ref, nw_ref, w_ref, wsm_ref = take(6)
    side_in = take(n_side)
    ssd_par = take(6)
    gla_par = take(3)
    y_ref, o_ref = take(2)
    side_out = take(n_side)
    (al_ref,) = take(1)
    u_refs = take(7)
    ssd_scr = take(5)
    gla_scr = take(1)
    z_s, xbc_s, q_s, k_s, v_s, g_s, sm_s = u_refs
    t = pl.program_id(0)
    rows_blk = xa_ref.shape[0]

    for s_in, s_out in zip(side_in, side_out):
        s_out[...] = s_in[...].astype(s_out.dtype)

    def project(x_ref, slot, defer, realign=False):
        outs = [u.at[slot] for u in u_refs]
        return _norm_proj_body(x_ref, nw_ref, w_ref, wsm_ref, *outs, al_ref, offsets=offsets, n_extra=1,
                               n_side=0, defer=defer, realign=realign)

    @pl.when(t == 0)
    def _():
        project(x0_ref, 0, defer=False, realign=True)

    for half, (x_ref, wslot) in enumerate(((xa_ref, 1), (xb_ref, 0))):
        rslot = 1 - wslot
        first = (2 * t + half) % blocks_per_seq == 0
        jobs = project(x_ref, wslot, defer=True)
        rd = lambda u: u.at[rslot:rslot + 1]
        out_rows = pl.ds(half * rows_blk, rows_blk)
        _ssd_body(rd(xbc_s), rd(sm_s), rd(z_s), *ssd_par, y_ref.at[:, out_rows], *ssd_scr, side=jobs, first=first)
        _gla_body(rd(q_s), rd(k_s), rd(v_s), rd(sm_s), rd(g_s), *gla_par, o_ref.at[:, out_rows], *gla_scr,
                  mid_lane=mid_lane, side=jobs, first=first)
        for job in jobs:
            job()


def _even_pipe(x, norm_w, w_in_bf, w_small, segments, conv_w, conv_b, a_log, dt_bias, d_skip, ssd_norm_w,
               gate_w2, gate_b, gla_norm_w, lr_lane, side_casts):
    b, l, d = x.shape
    t = EVEN_PIPE_ROWS
    width = ssd_norm_w.shape[0]
    cdim = conv_w.shape[1]
    n_heads = a_log.shape[0]
    rank, kw = gate_w2.shape
    vw = gla_norm_w.shape[0] * GLA_HEADS
    bps = l // t
    n = b * bps
    steps = n // 2
    assert bps % 2 == 0
    pad = lambda v: jnp.pad(v, (0, LANES - n_heads)).reshape(1, LANES)
    w2_hi, w2_mid = _split_bf16(gate_w2, 2)
    w2_pad = jnp.zeros((LANES, kw), BF16)
    for copy, part in enumerate((w2_hi, w2_mid, w2_hi)):
        w2_pad = w2_pad.at[lr_lane + copy * rank:lr_lane + (copy + 1) * rank, :].set(part)
    assert all(sc.shape[0] % (steps * BF16_SUBLANES) == 0 for sc in side_casts)
    shifted_cols = sum(w for off, w in segments if off % LANES)
    x_blk = lambda off: pl.BlockSpec((t, d), lambda s: (jnp.minimum(2 * s + off, n - 1), 0))
    side_specs = [pl.BlockSpec((sc.shape[0] // steps, sc.shape[1]), lambda s: (s, 0)) for sc in side_casts]
    scan_blk = lambda w: pl.BlockSpec((1, 2 * t, w), lambda s: (s // (bps // 2), s % (bps // 2), 0))
    widths = [w for _, w in segments] + [LANES]
    dtypes = [F32, F32, F32, F32, BF16, F32, F32]
    gwidth = width // SSD_GROUPS
    x2d = x.reshape(b * l, d)
    outs = pl.pallas_call(
        functools.partial(_even_pipe_body, n_side=len(side_casts), offsets=tuple(off for off, _ in segments),
                          mid_lane=lr_lane + 2 * rank, blocks_per_seq=bps),
        name="even_pipe",
        grid=(steps,),
        in_specs=[pl.BlockSpec((t, d), lambda s: (0, 0)), x_blk(1), x_blk(2), _resident((1, d)),
                  _resident(w_in_bf.shape), _resident(w_small.shape)] + side_specs
        + [_resident((SSD_CONV, cdim)), _resident((1, cdim)), _resident((1, LANES)), _resident((1, LANES)),
           _resident((1, width)), _resident((1, width)),
           _resident((LANES, kw)), _resident((1, kw)), _resident((1, vw // GLA_HEADS))],
        out_specs=[scan_blk(width), scan_blk(vw)] + side_specs,
        out_shape=[jax.ShapeDtypeStruct((b, l, width), BF16), jax.ShapeDtypeStruct((b, l, vw), BF16)]
        + [jax.ShapeDtypeStruct(sc.shape, BF16) for sc in side_casts],
        scratch_shapes=[pltpu.VMEM((d, shifted_cols), BF16)]
        + [pltpu.VMEM((2, t, w), dt) for w, dt in zip(widths, dtypes)]
        + [pltpu.VMEM((SSD_GROUPS * SSD_STATE, gwidth), F32),
           pltpu.VMEM((cdim // LANES, t + SUBLANES, LANES), F32),
           pltpu.VMEM((cdim // LANES, t, LANES), F32),
           pltpu.VMEM((t, width), BF16),
           pltpu.VMEM((t, width), F32),
           pltpu.VMEM((vw // GLA_HEADS, kw), F32)],
        compiler_params=_params("arbitrary"),
    )(x2d, x2d, x2d, norm_w.reshape(1, d), w_in_bf, w_small, *side_casts,
      conv_w, conv_b.reshape(1, cdim), pad(a_log), pad(dt_bias), jnp.repeat(d_skip, SSD_HEAD_DIM).reshape(1, width),
      ssd_norm_w.reshape(1, width), w2_pad, gate_b.reshape(1, kw), gla_norm_w.reshape(1, -1))
    return outs[0], outs[1], outs[2:]


def _out_proj_body(*refs, n_in, final_norm, next_widths):
    a_refs, x_ref, w_ref = refs[:n_in], refs[n_in], refs[n_in + 1]
    rest = refs[n_in + 2:]
    acc = x_ref[...]
    off = 0
    for a_ref in a_refs:
        width = a_ref.shape[1]
        acc = acc + jnp.dot(a_ref[...].astype(BF16), w_ref[off:off + width, :], preferred_element_type=F32)
        off += width
    if final_norm:
        fw_ref, o_ref = rest
        o_ref[...] = _rms(acc, fw_ref[...])
        return
    if not next_widths:
        (o_ref,) = rest
        o_ref[...] = acc
        return
    nw_ref, w2_ref, o_ref = rest[:3]
    o_ref[...] = acc
    h = _rms(acc, nw_ref[...]).astype(BF16)
    off = 0
    for p_ref, width in zip(rest[3:], next_widths):
        p_ref[...] = jnp.dot(h, w2_ref[:, off:off + width], preferred_element_type=F32).astype(p_ref.dtype)
        off += width


def _out_proj(acts, x2d, weight, final_w=None, next_proj=None):
    m, d = x2d.shape
    tm = OUT_PROJ_ROWS if next_proj is None else IN_PROJ_ROWS
    n_in = len(acts)
    assert sum(a.shape[1] for a in acts) == weight.shape[0]
    row = lambda w: pl.BlockSpec((tm, w), lambda i: (i, 0))
    in_specs = [row(a.shape[1]) for a in acts] + [row(d), _resident(weight.shape)]
    args = list(acts) + [x2d, weight]
    out_specs, out_shape, widths = row(d), jax.ShapeDtypeStruct((m, d), F32), ()
    if final_w is not None:
        in_specs.append(_resident((1, d)))
        args.append(final_w.reshape(1, d))
    elif next_proj is not None:
        norm_w, w2, widths, dtypes = next_proj
        assert sum(widths) == w2.shape[1] and all(w % LANES == 0 for w in widths)
        in_specs += [_resident((1, d)), _resident(w2.shape)]
        args += [norm_w.reshape(1, d), w2]
        out_specs = [out_specs] + [row(w) for w in widths]
        out_shape = [out_shape] + [jax.ShapeDtypeStruct((m, w), dt) for w, dt in zip(widths, dtypes)]
    outs = pl.pallas_call(
        functools.partial(_out_proj_body, n_in=n_in, final_norm=final_w is not None, next_widths=tuple(widths)),
        name="out_proj",
        grid=(m // tm,),
        in_specs=in_specs,
        out_specs=out_specs,
        out_shape=out_shape,
        compiler_params=_params("parallel"),
    )(*args)
    return outs if next_proj is None else (outs[0], outs[1:])


class _MobaQueryBlock:
    def __init__(self, n_past, q_ref, z_ref, o_ref, kb_ref, vt_ref, kmean_ref, s_ref):
        blk = MOBA_BLOCK
        self.n_past, self.z_ref, self.o_ref, self.kb_ref, self.vt_ref = n_past, z_ref, o_ref, kb_ref, vt_ref
        self.own = slice(n_past * blk, (n_past + 1) * blk)
        self.s_ref = s_ref.at[n_past % s_ref.shape[0]]
        q2 = q_ref[0, self.own, :]
        lo = lax.broadcasted_iota(jnp.int32, (blk, LANES), 1) < HALF
        q_cat = jnp.concatenate([jnp.where(lo, q2, 0.0), jnp.where(lo, 0.0, q2)], axis=0)
        scale = MOBA_HEAD_DIM ** -0.5
        key_i = lax.broadcasted_iota(jnp.int32, (blk, 2 * blk), 0)
        qry_i = lax.broadcasted_iota(jnp.int32, (blk, 2 * blk), 1)
        causal_t = key_i <= jnp.where(qry_i < blk, qry_i, qry_i - blk)

        self.masks = []
        if n_past:
            nb = kmean_ref.shape[0] // 2
            parts = [_dot_nt(kmean_ref[...], q_part) for q_part in _split_bf16(q_cat * scale, 2)]
            gate = (parts[0][0:nb] + parts[0][nb:]) + (parts[1][0:nb] + parts[1][nb:])
            g_rows = [gate[n:n + 1, :] for n in range(n_past)]
            for n in range(n_past):
                rank = jnp.zeros((1, 2 * blk), F32)
                for m in range(n_past):
                    if m != n:
                        ahead = (g_rows[m] >= g_rows[n]) if m < n else (g_rows[m] > g_rows[n])
                        rank = rank + jnp.where(ahead, 1.0, 0.0)
                self.masks.append(rank < MOBA_TOPK)
        self.masks.append(causal_t)
        self.q_s = (q_cat * (scale * LOG2_E)).astype(BF16)
        self.m_run = None
        self.p = []

    def logits_step(self, j):
        rows = slice(j * MOBA_BLOCK, (j + 1) * MOBA_BLOCK)
        sj = jnp.where(self.masks[j], _dot_nt(self.kb_ref[rows, :], self.q_s), -jnp.inf)
        self.s_ref[rows, :] = sj
        mj = jnp.max(sj, axis=0, keepdims=True)
        self.m_run = mj if self.m_run is None else jnp.maximum(self.m_run, mj)

    def value_step(self, j):
        rows = slice(j * MOBA_BLOCK, (j + 1) * MOBA_BLOCK)
        self.p.append(jnp.exp2(self.s_ref[rows, :] - self.m_run).astype(BF16))

    def finish(self):
        blk = MOBA_BLOCK
        nk = len(self.p) * blk
        p_all = jnp.concatenate(self.p, axis=0)
        outs = []
        for half in range(2):
            vt = self.vt_ref[half * VT_ROWS:(half + 1) * VT_ROWS, 0:nk]
            acc = jnp.dot(vt, p_all[:, half * blk:(half + 1) * blk], preferred_element_type=F32)
            outs.append(acc[0:HALF, :] / acc[HALF:HALF + 1, :])
        o_t = jnp.concatenate(outs, axis=0)
        self.o_ref[0, self.own, :] = (o_t.T * _silu(self.z_ref[0, self.own, :])).astype(self.o_ref.dtype)


def _moba_body(q_ref, k_ref, v_ref, z_ref, o_ref, vt_ref, kmean_ref, s_ref):
    blk = MOBA_BLOCK
    nb = k_ref.shape[1] // blk
    kmean = []
    for n in range(nb):
        rows = slice(n * blk, (n + 1) * blk)
        kmean.append(jnp.mean(k_ref[0, rows, :].astype(F32), axis=0, keepdims=True))
        v_t = v_ref[0, rows, :].astype(F32).T.astype(BF16)
        for half in range(2):
            vt_ref[half * VT_ROWS:half * VT_ROWS + HALF, rows] = v_t[half * HALF:(half + 1) * HALF, :]
    for half in range(2):
        vt_ref[half * VT_ROWS + HALF:(half + 1) * VT_ROWS, :] = jnp.ones((BF16_SUBLANES, vt_ref.shape[1]), BF16)
    kmean_ref[...] = jnp.concatenate(_split_bf16(jnp.concatenate(kmean, axis=0), 2), axis=0)
    make = functools.partial(_MobaQueryBlock, q_ref=q_ref, z_ref=z_ref, o_ref=o_ref, kb_ref=k_ref.at[0],
                             vt_ref=vt_ref, kmean_ref=kmean_ref, s_ref=s_ref)
    blocks = {0: make(0), 1: make(1)}
    blocks[0].logits_step(0)
    for i in range(nb):
        if i + 2 < nb:
            blocks[i + 2] = make(i + 2)
        cur, nxt = blocks.pop(i), blocks.get(i + 1)
        for j in range(i + 2):
            if j <= i:
                cur.value_step(j)
            if nxt is not None:
                nxt.logits_step(j)
        cur.finish()


def _moba(q, k, v, z):
    b, l, w = q.shape
    blk = MOBA_BLOCK
    nb = l // blk
    spec = pl.BlockSpec((1, l, LANES), lambda bi, hp: (bi, 0, hp))
    return pl.pallas_call(
        _moba_body,
        name="moba",
        grid=(b, w // LANES),
        in_specs=[spec, spec, spec, spec],
        out_specs=spec,
        out_shape=jax.ShapeDtypeStruct((b, l, w), BF16),
        scratch_shapes=[pltpu.VMEM((2 * VT_ROWS, l), BF16),
                        pltpu.VMEM((2 * nb, LANES), BF16),
                        pltpu.VMEM((2, l, 2 * blk), F32)],
        compiler_params=_params("parallel", "parallel"),
    )(q, k, v, z)


def _even_layer(x, norm_w, w_in_bf, conv_w, conv_b, a_log, dt_bias, d_skip, ssd_norm_w, gate_w2, gate_b,
                gla_norm_w, side_casts=()):
    b, l, d = x.shape
    width = ssd_norm_w.shape[0]
    cdim = conv_w.shape[1]
    n_heads = a_log.shape[0]
    rank, kw = gate_w2.shape
    vw = gla_norm_w.shape[0] * GLA_HEADS
    cuts = [0]
    for s in (width, cdim, n_heads, kw, kw, vw, rank, vw):
        cuts.append(cuts[-1] + s)
    seg = lambda j: w_in_bf[:, cuts[j]:cuts[j + 1]]
    pad = jnp.zeros((d, LANES - n_heads - GATE_COPIES * rank), BF16)
    w_small = jnp.concatenate([seg(2)] + [seg(6)] * GATE_COPIES + [pad], axis=1)
    segments = [(cuts[j], cuts[j + 1] - cuts[j]) for j in (0, 1, 3, 4, 5, 7)]
    y_a, o_b, cast = _even_pipe(x, norm_w, w_in_bf, w_small, segments, conv_w, conv_b, a_log, dt_bias, d_skip,
                                ssd_norm_w, gate_w2, gate_b, gla_norm_w, n_heads, side_casts)
    return [y_a.reshape(b * l, width), o_b.reshape(b * l, vw)], cast


ODD_PROJ_DTYPES = (F32, BF16, BF16, F32)


def _odd_layer(x, norm_w, w_in_bf, side_casts=(), projected=None):
    b, l, d = x.shape
    w = w_in_bf.shape[1] // 4
    outs = projected
    if outs is None:
        outs = _norm_proj(x.reshape(b * l, d), norm_w, w_in_bf, [(j * w, w) for j in range(4)],
                          list(ODD_PROJ_DTYPES), side_casts=side_casts)
    q, k, v, z = [u.reshape(b, l, w) for u in outs[:4]]
    return [_moba(q, k, v, z).reshape(b * l, w)], outs[4:]


def kernel(x, even_norm, even_w_in, even_conv_w, even_conv_b, even_a_log, even_dt_bias, even_d_skip, even_ssd_norm,
           even_gate_w2, even_gate_b, even_gla_norm, even_w_out, odd_norm, odd_w_in, odd_w_out, final_norm):
    b, l, d = x.shape
    depth = even_norm.shape[0] + odd_norm.shape[0]
    w_in = lambda layer: (even_w_in if layer % 2 == 0 else odd_w_in)[layer // 2]
    w_out = lambda layer: (even_w_out if layer % 2 == 0 else odd_w_out)[layer // 2]
    keys = [("out", 0)] + [(kind, layer) for layer in range(1, depth) for kind in ("in", "out")]
    pending = [w_in(layer) if kind == "in" else w_out(layer) for kind, layer in keys]
    bf = {("in", 0): w_in(0).astype(BF16)}
    projected = None
    for layer in range(depth):
        i = layer // 2
        side = pending if layer == 0 else ()
        if layer % 2 == 0:
            acts, cast = _even_layer(x, even_norm[i], bf["in", layer], even_conv_w[i], even_conv_b[i], even_a_log[i],
                                     even_dt_bias[i], even_d_skip[i], even_ssd_norm[i], even_gate_w2[i],
                                     even_gate_b[i], even_gla_norm[i], side_casts=side)
        else:
            acts, cast = _odd_layer(x, odd_norm[i], bf["in", layer], side_casts=side, projected=projected)
        bf.update(zip(keys, cast))
        projected = None
        x2d = x.reshape(b * l, d)
        if layer + 1 < depth and (layer + 1) % 2 == 1:
            nxt = layer + 1
            w2 = bf["in", nxt]
            x2d, projected = _out_proj(acts, x2d, bf["out", layer],
                                       next_proj=(odd_norm[nxt // 2], w2, [w2.shape[1] // 4] * 4, ODD_PROJ_DTYPES))
        else:
            x2d = _out_proj(acts, x2d, bf["out", layer], final_norm if layer == depth - 1 else None)
        x = x2d.reshape(b, l, d)
    return x
```
